```python
import math
import jax
import jax.numpy as jnp
from jax import lax
import numpy as np


D_MODEL = 1024
BATCH = 8
SEQ = 8192
DEPTH = 4

PLE_DIM = 256
N_EVEN = (DEPTH + 1) // 2
N_ODD = DEPTH // 2
D_FF = 2816
NORM_EPS = 1e-6
LN_EPS = 1e-5

GM_CHUNK = 128
GM_HEADS = 8
GM_HEAD_DIM = 128
GM_WIDTH = GM_HEADS * GM_HEAD_DIM

SSD_HEAD_DIM = 64
SSD_HEADS = 16
SSD_INNER = SSD_HEADS * SSD_HEAD_DIM
SSD_GROUPS = 2
SSD_STATE = 128
SSD_CONV = 4
SSD_CHUNK = 128
SSD_CONV_CH = SSD_INNER + 2 * SSD_GROUPS * SSD_STATE

HYB_IN = 2 * GM_WIDTH + SSD_INNER + SSD_CONV_CH + SSD_HEADS
HYB_OUT = GM_WIDTH + SSD_INNER

MLA_HEADS = 16
MLA_NOPE = 128
MLA_ROPE = 64
MLA_V = 128
MLA_Q_LORA = 256
MLA_KV_LORA = 128
MLA_QK = MLA_NOPE + MLA_ROPE
MLA_IN = MLA_Q_LORA + MLA_KV_LORA + MLA_ROPE
ROPE_BASE = 10000.0
ATTN_BLOCK = 128

kernel_name = "hybrid_gmlp_ssd_mla_macaron"


def rmsnorm(x, g):
    xf = x.astype(jnp.float32)
    y = xf * lax.rsqrt(jnp.mean(xf * xf, axis=-1, keepdims=True) + NORM_EPS)
    return (y * g.astype(jnp.float32)).astype(x.dtype)


def layernorm(x, g, b):
    xf = x.astype(jnp.float32)
    mu = jnp.mean(xf, axis=-1, keepdims=True)
    xc = xf - mu
    var = jnp.mean(xc * xc, axis=-1, keepdims=True)
    y = xc * lax.rsqrt(var + LN_EPS) * g.astype(jnp.float32) + b.astype(jnp.float32)
    return y.astype(x.dtype)


def swiglu(x, w_in, w_down):
    gate, up = jnp.split(x @ w_in, 2, axis=-1)
    return (jax.nn.silu(gate) * up) @ w_down


def rope_tables(positions):
    inv = 1.0 / (ROPE_BASE ** (jnp.arange(0, MLA_ROPE, 2, dtype=jnp.float32) / MLA_ROPE))
    ang = positions.astype(jnp.float32)[..., None] * inv
    return jnp.cos(ang), jnp.sin(ang)


def apply_rope(x, cos, sin):
    shape = cos.shape[:2] + (1,) * (x.ndim - 3) + cos.shape[2:]
    c = cos.reshape(shape)
    s = sin.reshape(shape)
    x1, x2 = jnp.split(x.astype(jnp.float32), 2, axis=-1)
    return jnp.concatenate([x1 * c - x2 * s, x2 * c + x1 * s], axis=-1).astype(x.dtype)


def gmlp_spatial_gating(uv, ln_g, ln_b, w_s, b_s):
    bsz, seq, _ = uv.shape
    u, v = jnp.split(jax.nn.gelu(uv), 2, axis=-1)
    v = v.reshape(bsz, seq // GM_CHUNK, GM_CHUNK, GM_HEADS, GM_HEAD_DIM)
    v = layernorm(v, ln_g.reshape(GM_HEADS, GM_HEAD_DIM), ln_b.reshape(GM_HEADS, GM_HEAD_DIM))
    causal = jnp.tril(jnp.ones((GM_CHUNK, GM_CHUNK), dtype=bool))
    w = jnp.where(causal[None], w_s, 0).astype(v.dtype)
    mixed = jnp.einsum('hts,bcshd->bcthd', w, v) + b_s.T[None, None, :, :, None]
    return u * mixed.reshape(bsz, seq, GM_WIDTH)


def ssd_chunked_scan(x, dt, a, bmat, cmat):
    bsz, seq, nh, hp = x.shape
    f32 = jnp.float32
    r = nh // SSD_GROUPS
    nc = seq // SSD_CHUNK
    L = SSD_CHUNK
    xd = (x.astype(f32) * dt[..., None]).reshape(bsz, nc, L, SSD_GROUPS, r, hp)
    da = (dt * a).reshape(bsz, nc, L, SSD_GROUPS, r)
    bc = bmat.astype(f32).reshape(bsz, nc, L, SSD_GROUPS, SSD_STATE)
    cc = cmat.astype(f32).reshape(bsz, nc, L, SSD_GROUPS, SSD_STATE)
    a_cs = jnp.cumsum(da, axis=2)
    seg = a_cs[:, :, :, None] - a_cs[:, :, None, :]
    causal = jnp.tril(jnp.ones((L, L), dtype=bool))[:, :, None, None]
    decay = jnp.exp(jnp.where(causal, seg, -jnp.inf))
    cb = jnp.einsum('bclgn,bcsgn->bclsg', cc, bc)
    y_diag = jnp.einsum('bclsgr,bcsgrp->bclgrp', cb[..., None] * decay, xd)
    decay_to_end = jnp.exp(a_cs[:, :, -1:] - a_cs)
    states = jnp.einsum('bclgn,bclgrp->bcgrpn', bc, xd * decay_to_end[..., None])
    chunk_decay = jnp.exp(a_cs[:, :, -1])

    def step(h, inp):
        st, dec = inp
        return h * dec[..., None, None] + st, h

    h0 = jnp.zeros((bsz, SSD_GROUPS, r, hp, SSD_STATE), f32)
    _, prev = lax.scan(step, h0, (jnp.moveaxis(states, 1, 0), jnp.moveaxis(chunk_decay, 1, 0)))
    prev = jnp.moveaxis(prev, 0, 1)
    y_off = jnp.einsum('bclgn,bcgrpn->bclgrp', cc, prev) * jnp.exp(a_cs)[..., None]
    return (y_diag + y_off).reshape(bsz, seq, nh, hp).astype(x.dtype)


def ssd_mixer(zxbcdt, conv_w, conv_b, dt_bias, a_log, d_skip, norm_g):
    bsz, seq, _ = zxbcdt.shape
    z, xbc, dt = jnp.split(zxbcdt, [SSD_INNER, SSD_INNER + SSD_CONV_CH], axis=-1)
    xbc = lax.conv_general_dilated(
        xbc, conv_w[:, None, :], window_strides=(1,), padding=[(SSD_CONV - 1, 0)],
        dimension_numbers=('NWC', 'WIO', 'NWC'), feature_group_count=SSD_CONV_CH) + conv_b
    xbc = jax.nn.silu(xbc)
    xs, bmat, cmat = jnp.split(xbc, [SSD_INNER, SSD_INNER + SSD_GROUPS * SSD_STATE], axis=-1)
    dt = jax.nn.softplus(dt.astype(jnp.float32) + dt_bias.astype(jnp.float32))
    a = -jnp.exp(a_log.astype(jnp.float32))
    xs = xs.reshape(bsz, seq, SSD_HEADS, SSD_HEAD_DIM)
    y = ssd_chunked_scan(xs, dt, a,
                         bmat.reshape(bsz, seq, SSD_GROUPS, SSD_STATE),
                         cmat.reshape(bsz, seq, SSD_GROUPS, SSD_STATE))
    y = (y + d_skip[:, None] * xs).reshape(bsz, seq, SSD_INNER)
    yg = (y * jax.nn.silu(z)).reshape(bsz, seq, SSD_GROUPS, SSD_INNER // SSD_GROUPS)
    yg = rmsnorm(yg, norm_g.reshape(SSD_GROUPS, SSD_INNER // SSD_GROUPS))
    return yg.reshape(bsz, seq, SSD_INNER)


def causal_attention_blocked(q, k, v):
    bsz, seq, nh, dk = q.shape
    nb = seq // ATTN_BLOCK
    scale = dk ** -0.5
    qb = jnp.moveaxis(q.reshape(bsz, nb, ATTN_BLOCK, nh, dk), 1, 0)
    k_pos = jnp.arange(seq)

    def one_block(args):
        qi, blk = args
        s = jnp.einsum('bthd,bshd->bhts', qi, k).astype(jnp.float32) * scale
        q_pos = blk * ATTN_BLOCK + jnp.arange(ATTN_BLOCK)
        s = jnp.where(k_pos[None, :] <= q_pos[:, None], s, -jnp.inf)
        pr = jax.nn.softmax(s, axis=-1).astype(v.dtype)
        return jnp.einsum('bhts,bshd->bthd', pr, v)

    out = lax.map(one_block, (qb, jnp.arange(nb)))
    return jnp.moveaxis(out, 0, 1).reshape(bsz, seq, nh, v.shape[-1])


def mla_attention(h, w_in, q_norm_g, kv_norm_g, w_uq, w_ukv, w_out, cos, sin):
    bsz, seq, _ = h.shape
    c_q, c_kv, k_rope = jnp.split(h @ w_in, [MLA_Q_LORA, MLA_Q_LORA + MLA_KV_LORA], axis=-1)
    q = (rmsnorm(c_q, q_norm_g) @ w_uq).reshape(bsz, seq, MLA_HEADS, MLA_QK)
    q_nope, q_rope = jnp.split(q, [MLA_NOPE], axis=-1)
    q = jnp.concatenate([q_nope, apply_rope(q_rope, cos, sin)], axis=-1)
    kv = (rmsnorm(c_kv, kv_norm_g) @ w_ukv).reshape(bsz, seq, MLA_HEADS, MLA_NOPE + MLA_V)
    k_nope, v = jnp.split(kv, [MLA_NOPE], axis=-1)
    k_rope = apply_rope(k_rope, cos, sin)
    k = jnp.concatenate(
        [k_nope, jnp.broadcast_to(k_rope[:, :, None, :], (bsz, seq, MLA_HEADS, MLA_ROPE))], axis=-1)
    o = causal_attention_blocked(q, k, v)
    return o.reshape(bsz, seq, MLA_HEADS * MLA_V) @ w_out


def _fwd_setup_inputs(seed: int = 0) -> dict:
    key = jax.random.key(seed)
    ks = iter(jax.random.split(key, 48))
    f32 = jnp.float32

    def nrm(shape, scale):
        return jax.random.normal(next(ks), shape, f32) * scale

    def gain(shape):
        return 1.0 + nrm(shape, 0.02)

    x = nrm((BATCH, SEQ, D_MODEL), 1.0)
    p = nrm((DEPTH, BATCH, SEQ, PLE_DIM), 1.0)
    start = jax.random.randint(next(ks), (BATCH, 1), 0, 4096, dtype=jnp.int32)
    positions = start + jnp.arange(SEQ, dtype=jnp.int32)[None, :]

    ffn1_pre_g = gain((DEPTH, D_MODEL))
    ffn1_w_in = nrm((DEPTH, D_MODEL, 2 * D_FF), D_MODEL ** -0.5)
    ffn1_w_down = nrm((DEPTH, D_FF, D_MODEL), D_FF ** -0.5)
    ffn1_post_g = gain((DEPTH, D_MODEL))
    mix_pre_g = gain((DEPTH, D_MODEL))
    mix_post_g = gain((DEPTH, D_MODEL))
    ffn2_pre_g = gain((DEPTH, D_MODEL))
    ffn2_w_in = nrm((DEPTH, D_MODEL, 2 * D_FF), D_MODEL ** -0.5)
    ffn2_w_down = nrm((DEPTH, D_FF, D_MODEL), D_FF ** -0.5)
    ffn2_post_g = gain((DEPTH, D_MODEL))
    ple_pre_g = gain((DEPTH, D_MODEL))
    ple_w_gate = nrm((DEPTH, D_MODEL, D_MODEL), D_MODEL ** -0.5)
    ple_w_proj = nrm((DEPTH, PLE_DIM, D_MODEL), PLE_DIM ** -0.5)
    ple_post_g = gain((DEPTH, D_MODEL))

    hyb_w_in = nrm((N_EVEN, D_MODEL, HYB_IN), D_MODEL ** -0.5)
    gm_ln_g = gain((N_EVEN, GM_WIDTH))
    gm_ln_b = nrm((N_EVEN, GM_WIDTH), 0.02)
    gm_w_s = nrm((N_EVEN, GM_HEADS, GM_CHUNK, GM_CHUNK), GM_CHUNK ** -0.5)
    gm_b_s = gain((N_EVEN, GM_HEADS, GM_CHUNK))
    ssd_conv_w = nrm((N_EVEN, SSD_CONV, SSD_CONV_CH), SSD_CONV ** -0.5)
    ssd_conv_b = nrm((N_EVEN, SSD_CONV_CH), 0.02)
    dt0 = jnp.exp(jax.random.uniform(next(ks), (N_EVEN, SSD_HEADS), f32,
                                     math.log(1e-3), math.log(1e-1)))
    ssd_dt_bias = dt0 + jnp.log(-jnp.expm1(-dt0))
    ssd_a_log = jnp.log(jax.random.uniform(next(ks), (N_EVEN, SSD_HEADS), f32, 1.0, 16.0))
    ssd_d = gain((N_EVEN, SSD_HEADS))
    ssd_norm_g = gain((N_EVEN, SSD_INNER))
    hyb_w_out = nrm((N_EVEN, HYB_OUT, D_MODEL), HYB_OUT ** -0.5)

    mla_w_in = nrm((N_ODD, D_MODEL, MLA_IN), D_MODEL ** -0.5)
    mla_q_norm_g = gain((N_ODD, MLA_Q_LORA))
    mla_kv_norm_g = gain((N_ODD, MLA_KV_LORA))
    mla_w_uq = nrm((N_ODD, MLA_Q_LORA, MLA_HEADS * MLA_QK), MLA_Q_LORA ** -0.5)
    mla_w_ukv = nrm((N_ODD, MLA_KV_LORA, MLA_HEADS * (MLA_NOPE + MLA_V)), MLA_KV_LORA ** -0.5)
    mla_w_out = nrm((N_ODD, MLA_HEADS * MLA_V, D_MODEL), (MLA_HEADS * MLA_V) ** -0.5)

    return {
        "x": x, "p": p, "positions": positions,
        "ffn1_pre_g": ffn1_pre_g, "ffn1_w_in": ffn1_w_in, "ffn1_w_down": ffn1_w_down,
        "ffn1_post_g": ffn1_post_g, "mix_pre_g": mix_pre_g, "mix_post_g": mix_post_g,
        "ffn2_pre_g": ffn2_pre_g, "ffn2_w_in": ffn2_w_in, "ffn2_w_down": ffn2_w_down,
        "ffn2_post_g": ffn2_post_g, "ple_pre_g": ple_pre_g, "ple_w_gate": ple_w_gate,
        "ple_w_proj": ple_w_proj, "ple_post_g": ple_post_g,
        "hyb_w_in": hyb_w_in, "gm_ln_g": gm_ln_g, "gm_ln_b": gm_ln_b, "gm_w_s": gm_w_s,
        "gm_b_s": gm_b_s, "ssd_conv_w": ssd_conv_w, "ssd_conv_b": ssd_conv_b,
        "ssd_dt_bias": ssd_dt_bias, "ssd_a_log": ssd_a_log, "ssd_d": ssd_d,
        "ssd_norm_g": ssd_norm_g, "hyb_w_out": hyb_w_out,
        "mla_w_in": mla_w_in, "mla_q_norm_g": mla_q_norm_g, "mla_kv_norm_g": mla_kv_norm_g,
        "mla_w_uq": mla_w_uq, "mla_w_ukv": mla_w_ukv, "mla_w_out": mla_w_out,
    }


def _fwd_reference(x, p, positions,
              ffn1_pre_g, ffn1_w_in, ffn1_w_down, ffn1_post_g, mix_pre_g, mix_post_g,
              ffn2_pre_g, ffn2_w_in, ffn2_w_down, ffn2_post_g,
              ple_pre_g, ple_w_gate, ple_w_proj, ple_post_g,
              hyb_w_in, gm_ln_g, gm_ln_b, gm_w_s, gm_b_s,
              ssd_conv_w, ssd_conv_b, ssd_dt_bias, ssd_a_log, ssd_d, ssd_norm_g, hyb_w_out,
              mla_w_in, mla_q_norm_g, mla_kv_norm_g, mla_w_uq, mla_w_ukv, mla_w_out):
    cos, sin = rope_tables(positions)
    h = x
    for i in range(DEPTH):
        j = i // 2
        f = swiglu(rmsnorm(h, ffn1_pre_g[i]), ffn1_w_in[i], ffn1_w_down[i])
        h = h + 0.5 * rmsnorm(f, ffn1_post_g[i])
        hn = rmsnorm(h, mix_pre_g[i])
        if i % 2 == 0:
            uv, zxbcdt = jnp.split(hn @ hyb_w_in[j], [2 * GM_WIDTH], axis=-1)
            ya = gmlp_spatial_gating(uv, gm_ln_g[j], gm_ln_b[j], gm_w_s[j], gm_b_s[j])
            yb = ssd_mixer(zxbcdt, ssd_conv_w[j], ssd_conv_b[j], ssd_dt_bias[j],
                           ssd_a_log[j], ssd_d[j], ssd_norm_g[j])
            mixed = jnp.concatenate([ya, yb], axis=-1) @ hyb_w_out[j]
        else:
            mixed = mla_attention(hn, mla_w_in[j], mla_q_norm_g[j], mla_kv_norm_g[j],
                                  mla_w_uq[j], mla_w_ukv[j], mla_w_out[j], cos, sin)
        h = h + rmsnorm(mixed, mix_post_g[i])
        f = swiglu(rmsnorm(h, ffn2_pre_g[i]), ffn2_w_in[i], ffn2_w_down[i])
        h = h + 0.5 * rmsnorm(f, ffn2_post_g[i])
        gate = jax.nn.sigmoid(rmsnorm(h, ple_pre_g[i]) @ ple_w_gate[i])
        h = h + rmsnorm(gate * (p[i] @ ple_w_proj[i]), ple_post_g[i])
    return h


import jax as _jax
import jax.numpy as _jnp

TWIN_FORMAT = 'train_step'
FWD_PARAMS = ['x', 'p', 'positions', 'ffn1_pre_g', 'ffn1_w_in', 'ffn1_w_down', 'ffn1_post_g', 'mix_pre_g', 'mix_post_g', 'ffn2_pre_g', 'ffn2_w_in', 'ffn2_w_down', 'ffn2_post_g', 'ple_pre_g', 'ple_w_gate', 'ple_w_proj', 'ple_post_g', 'hyb_w_in', 'gm_ln_g', 'gm_ln_b', 'gm_w_s', 'gm_b_s', 'ssd_conv_w', 'ssd_conv_b', 'ssd_dt_bias', 'ssd_a_log', 'ssd_d', 'ssd_norm_g', 'hyb_w_out', 'mla_w_in', 'mla_q_norm_g', 'mla_kv_norm_g', 'mla_w_uq', 'mla_w_ukv', 'mla_w_out']
TWIN_WEIGHTS = ['ffn1_pre_g', 'ffn1_w_in', 'ffn1_w_down', 'ffn1_post_g', 'mix_pre_g', 'mix_post_g', 'ffn2_pre_g', 'ffn2_w_in', 'ffn2_w_down', 'ffn2_post_g', 'ple_pre_g', 'ple_w_gate', 'ple_w_proj', 'ple_post_g', 'hyb_w_in', 'gm_ln_g', 'gm_ln_b', 'gm_w_s', 'gm_b_s', 'ssd_conv_w', 'ssd_conv_b', 'ssd_dt_bias', 'ssd_a_log', 'ssd_d', 'ssd_norm_g', 'hyb_w_out', 'mla_w_in', 'mla_q_norm_g', 'mla_kv_norm_g', 'mla_w_uq', 'mla_w_ukv', 'mla_w_out']
TWIN_DIFF_INPUT = 'x'
TWIN_INPUTS = ['x', 'p', 'positions', 'ffn1_pre_g', 'ffn1_w_in', 'ffn1_w_down', 'ffn1_post_g', 'mix_pre_g', 'mix_post_g', 'ffn2_pre_g', 'ffn2_w_in', 'ffn2_w_down', 'ffn2_post_g', 'ple_pre_g', 'ple_w_gate', 'ple_w_proj', 'ple_post_g', 'hyb_w_in', 'gm_ln_g', 'gm_ln_b', 'gm_w_s', 'gm_b_s', 'ssd_conv_w', 'ssd_conv_b', 'ssd_dt_bias', 'ssd_a_log', 'ssd_d', 'ssd_norm_g', 'hyb_w_out', 'mla_w_in', 'mla_q_norm_g', 'mla_kv_norm_g', 'mla_w_uq', 'mla_w_ukv', 'mla_w_out', 'loss_target', 'm_ffn1_pre_g', 'm_ffn1_w_in', 'm_ffn1_w_down', 'm_ffn1_post_g', 'm_mix_pre_g', 'm_mix_post_g', 'm_ffn2_pre_g', 'm_ffn2_w_in', 'm_ffn2_w_down', 'm_ffn2_post_g', 'm_ple_pre_g', 'm_ple_w_gate', 'm_ple_w_proj', 'm_ple_post_g', 'm_hyb_w_in', 'm_gm_ln_g', 'm_gm_ln_b', 'm_gm_w_s', 'm_gm_b_s', 'm_ssd_conv_w', 'm_ssd_conv_b', 'm_ssd_dt_bias', 'm_ssd_a_log', 'm_ssd_d', 'm_ssd_norm_g', 'm_hyb_w_out', 'm_mla_w_in', 'm_mla_q_norm_g', 'm_mla_kv_norm_g', 'm_mla_w_uq', 'm_mla_w_ukv', 'm_mla_w_out', 'v_ffn1_pre_g', 'v_ffn1_w_in', 'v_ffn1_w_down', 'v_ffn1_post_g', 'v_mix_pre_g', 'v_mix_post_g', 'v_ffn2_pre_g', 'v_ffn2_w_in', 'v_ffn2_w_down', 'v_ffn2_post_g', 'v_ple_pre_g', 'v_ple_w_gate', 'v_ple_w_proj', 'v_ple_post_g', 'v_hyb_w_in', 'v_gm_ln_g', 'v_gm_ln_b', 'v_gm_w_s', 'v_gm_b_s', 'v_ssd_conv_w', 'v_ssd_conv_b', 'v_ssd_dt_bias', 'v_ssd_a_log', 'v_ssd_d', 'v_ssd_norm_g', 'v_hyb_w_out', 'v_mla_w_in', 'v_mla_q_norm_g', 'v_mla_kv_norm_g', 'v_mla_w_uq', 'v_mla_w_ukv', 'v_mla_w_out']
TWIN_OUTPUTS = ['loss', 'grad_x', 'grad_ffn1_pre_g', 'grad_ffn1_w_in', 'grad_ffn1_w_down', 'grad_ffn1_post_g', 'grad_mix_pre_g', 'grad_mix_post_g', 'grad_ffn2_pre_g', 'grad_ffn2_w_in', 'grad_ffn2_w_down', 'grad_ffn2_post_g', 'grad_ple_pre_g', 'grad_ple_w_gate', 'grad_ple_w_proj', 'grad_ple_post_g', 'grad_hyb_w_in', 'grad_gm_ln_g', 'grad_gm_ln_b', 'grad_gm_w_s', 'grad_gm_b_s', 'grad_ssd_conv_w', 'grad_ssd_conv_b', 'grad_ssd_dt_bias', 'grad_ssd_a_log', 'grad_ssd_d', 'grad_ssd_norm_g', 'grad_hyb_w_out', 'grad_mla_w_in', 'grad_mla_q_norm_g', 'grad_mla_kv_norm_g', 'grad_mla_w_uq', 'grad_mla_w_ukv', 'grad_mla_w_out', 'delta_ffn1_pre_g', 'delta_ffn1_w_in', 'delta_ffn1_w_down', 'delta_ffn1_post_g', 'delta_mix_pre_g', 'delta_mix_post_g', 'delta_ffn2_pre_g', 'delta_ffn2_w_in', 'delta_ffn2_w_down', 'delta_ffn2_post_g', 'delta_ple_pre_g', 'delta_ple_w_gate', 'delta_ple_w_proj', 'delta_ple_post_g', 'delta_hyb_w_in', 'delta_gm_ln_g', 'delta_gm_ln_b', 'delta_gm_w_s', 'delta_gm_b_s', 'delta_ssd_conv_w', 'delta_ssd_conv_b', 'delta_ssd_dt_bias', 'delta_ssd_a_log', 'delta_ssd_d', 'delta_ssd_norm_g', 'delta_hyb_w_out', 'delta_mla_w_in', 'delta_mla_q_norm_g', 'delta_mla_kv_norm_g', 'delta_mla_w_uq', 'delta_mla_w_ukv', 'delta_mla_w_out', 'new_m_ffn1_pre_g', 'new_m_ffn1_w_in', 'new_m_ffn1_w_down', 'new_m_ffn1_post_g', 'new_m_mix_pre_g', 'new_m_mix_post_g', 'new_m_ffn2_pre_g', 'new_m_ffn2_w_in', 'new_m_ffn2_w_down', 'new_m_ffn2_post_g', 'new_m_ple_pre_g', 'new_m_ple_w_gate', 'new_m_ple_w_proj', 'new_m_ple_post_g', 'new_m_hyb_w_in', 'new_m_gm_ln_g', 'new_m_gm_ln_b', 'new_m_gm_w_s', 'new_m_gm_b_s', 'new_m_ssd_conv_w', 'new_m_ssd_conv_b', 'new_m_ssd_dt_bias', 'new_m_ssd_a_log', 'new_m_ssd_d', 'new_m_ssd_norm_g', 'new_m_hyb_w_out', 'new_m_mla_w_in', 'new_m_mla_q_norm_g', 'new_m_mla_kv_norm_g', 'new_m_mla_w_uq', 'new_m_mla_w_ukv', 'new_m_mla_w_out', 'new_v_ffn1_pre_g', 'new_v_ffn1_w_in', 'new_v_ffn1_w_down', 'new_v_ffn1_post_g', 'new_v_mix_pre_g', 'new_v_mix_post_g', 'new_v_ffn2_pre_g', 'new_v_ffn2_w_in', 'new_v_ffn2_w_down', 'new_v_ffn2_post_g', 'new_v_ple_pre_g', 'new_v_ple_w_gate', 'new_v_ple_w_proj', 'new_v_ple_post_g', 'new_v_hyb_w_in', 'new_v_gm_ln_g', 'new_v_gm_ln_b', 'new_v_gm_w_s', 'new_v_gm_b_s', 'new_v_ssd_conv_w', 'new_v_ssd_conv_b', 'new_v_ssd_dt_bias', 'new_v_ssd_a_log', 'new_v_ssd_d', 'new_v_ssd_norm_g', 'new_v_hyb_w_out', 'new_v_mla_w_in', 'new_v_mla_q_norm_g', 'new_v_mla_kv_norm_g', 'new_v_mla_w_uq', 'new_v_mla_w_ukv', 'new_v_mla_w_out']
TWIN_LEAF_KINDS = {'loss': 'loss', 'grad_x': 'grad_x', 'grad_ffn1_pre_g': 'grad_w', 'grad_ffn1_w_in': 'grad_w', 'grad_ffn1_w_down': 'grad_w', 'grad_ffn1_post_g': 'grad_w', 'grad_mix_pre_g': 'grad_w', 'grad_mix_post_g': 'grad_w', 'grad_ffn2_pre_g': 'grad_w', 'grad_ffn2_w_in': 'grad_w', 'grad_ffn2_w_down': 'grad_w', 'grad_ffn2_post_g': 'grad_w', 'grad_ple_pre_g': 'grad_w', 'grad_ple_w_gate': 'grad_w', 'grad_ple_w_proj': 'grad_w', 'grad_ple_post_g': 'grad_w', 'grad_hyb_w_in': 'grad_w', 'grad_gm_ln_g': 'grad_w', 'grad_gm_ln_b': 'grad_w', 'grad_gm_w_s': 'grad_w', 'grad_gm_b_s': 'grad_w', 'grad_ssd_conv_w': 'grad_w', 'grad_ssd_conv_b': 'grad_w', 'grad_ssd_dt_bias': 'grad_w', 'grad_ssd_a_log': 'grad_w', 'grad_ssd_d': 'grad_w', 'grad_ssd_norm_g': 'grad_w', 'grad_hyb_w_out': 'grad_w', 'grad_mla_w_in': 'grad_w', 'grad_mla_q_norm_g': 'grad_w', 'grad_mla_kv_norm_g': 'grad_w', 'grad_mla_w_uq': 'grad_w', 'grad_mla_w_ukv': 'grad_w', 'grad_mla_w_out': 'grad_w', 'delta_ffn1_pre_g': 'delta_w', 'delta_ffn1_w_in': 'delta_w', 'delta_ffn1_w_down': 'delta_w', 'delta_ffn1_post_g': 'delta_w', 'delta_mix_pre_g': 'delta_w', 'delta_mix_post_g': 'delta_w', 'delta_ffn2_pre_g': 'delta_w', 'delta_ffn2_w_in': 'delta_w', 'delta_ffn2_w_down': 'delta_w', 'delta_ffn2_post_g': 'delta_w', 'delta_ple_pre_g': 'delta_w', 'delta_ple_w_gate': 'delta_w', 'delta_ple_w_proj': 'delta_w', 'delta_ple_post_g': 'delta_w', 'delta_hyb_w_in': 'delta_w', 'delta_gm_ln_g': 'delta_w', 'delta_gm_ln_b': 'delta_w', 'delta_gm_w_s': 'delta_w', 'delta_gm_b_s': 'delta_w', 'delta_ssd_conv_w': 'delta_w', 'delta_ssd_conv_b': 'delta_w', 'delta_ssd_dt_bias': 'delta_w', 'delta_ssd_a_log': 'delta_w', 'delta_ssd_d': 'delta_w', 'delta_ssd_norm_g': 'delta_w', 'delta_hyb_w_out': 'delta_w', 'delta_mla_w_in': 'delta_w', 'delta_mla_q_norm_g': 'delta_w', 'delta_mla_kv_norm_g': 'delta_w', 'delta_mla_w_uq': 'delta_w', 'delta_mla_w_ukv': 'delta_w', 'delta_mla_w_out': 'delta_w', 'new_m_ffn1_pre_g': 'new_m', 'new_m_ffn1_w_in': 'new_m', 'new_m_ffn1_w_down': 'new_m', 'new_m_ffn1_post_g': 'new_m', 'new_m_mix_pre_g': 'new_m', 'new_m_mix_post_g': 'new_m', 'new_m_ffn2_pre_g': 'new_m', 'new_m_ffn2_w_in': 'new_m', 'new_m_ffn2_w_down': 'new_m', 'new_m_ffn2_post_g': 'new_m', 'new_m_ple_pre_g': 'new_m', 'new_m_ple_w_gate': 'new_m', 'new_m_ple_w_proj': 'new_m', 'new_m_ple_post_g': 'new_m', 'new_m_hyb_w_in': 'new_m', 'new_m_gm_ln_g': 'new_m', 'new_m_gm_ln_b': 'new_m', 'new_m_gm_w_s': 'new_m', 'new_m_gm_b_s': 'new_m', 'new_m_ssd_conv_w': 'new_m', 'new_m_ssd_conv_b': 'new_m', 'new_m_ssd_dt_bias': 'new_m', 'new_m_ssd_a_log': 'new_m', 'new_m_ssd_d': 'new_m', 'new_m_ssd_norm_g': 'new_m', 'new_m_hyb_w_out': 'new_m', 'new_m_mla_w_in': 'new_m', 'new_m_mla_q_norm_g': 'new_m', 'new_m_mla_kv_norm_g': 'new_m', 'new_m_mla_w_uq': 'new_m', 'new_m_mla_w_ukv': 'new_m', 'new_m_mla_w_out': 'new_m', 'new_v_ffn1_pre_g': 'new_v', 'new_v_ffn1_w_in': 'new_v', 'new_v_ffn1_w_down': 'new_v', 'new_v_ffn1_post_g': 'new_v', 'new_v_mix_pre_g': 'new_v', 'new_v_mix_post_g': 'new_v', 'new_v_ffn2_pre_g': 'new_v', 'new_v_ffn2_w_in': 'new_v', 'new_v_ffn2_w_down': 'new_v', 'new_v_ffn2_post_g': 'new_v', 'new_v_ple_pre_g': 'new_v', 'new_v_ple_w_gate': 'new_v', 'new_v_ple_w_proj': 'new_v', 'new_v_ple_post_g': 'new_v', 'new_v_hyb_w_in': 'new_v', 'new_v_gm_ln_g': 'new_v', 'new_v_gm_ln_b': 'new_v', 'new_v_gm_w_s': 'new_v', 'new_v_gm_b_s': 'new_v', 'new_v_ssd_conv_w': 'new_v', 'new_v_ssd_conv_b': 'new_v', 'new_v_ssd_dt_bias': 'new_v', 'new_v_ssd_a_log': 'new_v', 'new_v_ssd_d': 'new_v', 'new_v_ssd_norm_g': 'new_v', 'new_v_hyb_w_out': 'new_v', 'new_v_mla_w_in': 'new_v', 'new_v_mla_q_norm_g': 'new_v', 'new_v_mla_kv_norm_g': 'new_v', 'new_v_mla_w_uq': 'new_v', 'new_v_mla_w_ukv': 'new_v', 'new_v_mla_w_out': 'new_v'}


def _forward(args):
    return _fwd_reference(*[args[k] for k in FWD_PARAMS])


def _output_shape():
    def fwd():
        inp = _fwd_setup_inputs(0)
        return _fwd_reference(*[inp[k] for k in FWD_PARAMS])
    out = _jax.eval_shape(fwd)
    return out.shape, out.dtype

N_MICROBATCH = 1
ADAM_LR = 0.001
ADAM_B1 = 0.9
ADAM_B2 = 0.999
ADAM_EPS = 1e-08
ADAM_WD = 0.01
ADAM_STEP = 10
PER_EXAMPLE_BATCH_AXIS = {'x': 0, 'p': 1, 'positions': 0, 'loss_target': 0}
SHARED_INPUTS = []
_WEIGHT_DTYPES = {'ffn1_pre_g': _jnp.float32, 'ffn1_w_in': _jnp.float32, 'ffn1_w_down': _jnp.float32, 'ffn1_post_g': _jnp.float32, 'mix_pre_g': _jnp.float32, 'mix_post_g': _jnp.float32, 'ffn2_pre_g': _jnp.float32, 'ffn2_w_in': _jnp.float32, 'ffn2_w_down': _jnp.float32, 'ffn2_post_g': _jnp.float32, 'ple_pre_g': _jnp.float32, 'ple_w_gate': _jnp.float32, 'ple_w_proj': _jnp.float32, 'ple_post_g': _jnp.float32, 'hyb_w_in': _jnp.float32, 'gm_ln_g': _jnp.float32, 'gm_ln_b': _jnp.float32, 'gm_w_s': _jnp.float32, 'gm_b_s': _jnp.float32, 'ssd_conv_w': _jnp.float32, 'ssd_conv_b': _jnp.float32, 'ssd_dt_bias': _jnp.float32, 'ssd_a_log': _jnp.float32, 'ssd_d': _jnp.float32, 'ssd_norm_g': _jnp.float32, 'hyb_w_out': _jnp.float32, 'mla_w_in': _jnp.float32, 'mla_q_norm_g': _jnp.float32, 'mla_kv_norm_g': _jnp.float32, 'mla_w_uq': _jnp.float32, 'mla_w_ukv': _jnp.float32, 'mla_w_out': _jnp.float32}
MOMENT_SCALE = {'ffn1_pre_g': 9.186609e+00, 'ffn1_w_in': 3.241996e+00, 'ffn1_w_down': 6.847504e+00, 'ffn1_post_g': 1.735271e+01, 'mix_pre_g': 2.966101e+01, 'mix_post_g': 7.891343e+01, 'ffn2_pre_g': 8.811156e+00, 'ffn2_w_in': 3.704382e+00, 'ffn2_w_down': 6.754385e+00, 'ffn2_post_g': 1.767714e+01, 'ple_pre_g': 6.934898e-01, 'ple_w_gate': 6.951990e-01, 'ple_w_proj': 2.234093e+00, 'ple_post_g': 6.581057e+01, 'hyb_w_in': 7.170439e+00, 'gm_ln_g': 9.720316e-01, 'gm_ln_b': 3.433004e+00, 'gm_w_s': 8.682667e-01, 'gm_b_s': 2.477689e+00, 'ssd_conv_w': 1.435671e+01, 'ssd_conv_b': 3.910410e+01, 'ssd_dt_bias': 2.492471e+01, 'ssd_a_log': 8.802828e+01, 'ssd_d': 1.020390e+02, 'ssd_norm_g': 2.150927e+01, 'hyb_w_out': 4.008459e+01, 'mla_w_in': 6.087477e+01, 'mla_q_norm_g': 9.337762e+00, 'mla_kv_norm_g': 1.195704e+02, 'mla_w_uq': 2.836663e+00, 'mla_w_ukv': 2.008224e+01, 'mla_w_out': 3.875825e+01}


def _to_microbatches(a, axis):
    t = _jnp.moveaxis(a, axis, 0)
    t = t.reshape((N_MICROBATCH, t.shape[0] // N_MICROBATCH) + t.shape[1:])
    return _jnp.moveaxis(t, 1, axis + 1)


def setup_inputs(seed: int = 0) -> dict:
    inp = _fwd_setup_inputs(seed)
    key = _jax.random.fold_in(_jax.random.key(seed), 7919)
    shape, _ = _output_shape()
    out = dict(inp)
    out["loss_target"] = _jax.random.normal(_jax.random.fold_in(key, 0), shape, _jnp.float32)
    for i, name in enumerate(TWIN_WEIGHTS):
        w = inp[name].astype(_jnp.float32)
        if MOMENT_SCALE is None:
            s = _jnp.sqrt(_jnp.mean(_jnp.square(w)) + 1e-30)
        else:
            s = MOMENT_SCALE[name]
        km, kv = _jax.random.split(_jax.random.fold_in(key, i + 1))
        out[name] = w
        out["m_" + name] = s * _jax.random.normal(km, w.shape, _jnp.float32)
        out["v_" + name] = (s * s) * _jax.random.uniform(kv, w.shape, _jnp.float32, 0.5, 1.5)
    if N_MICROBATCH > 1:
        for name, axis in PER_EXAMPLE_BATCH_AXIS.items():
            out[name] = _to_microbatches(out[name], axis)
    return {'x': out['x'], 'p': out['p'], 'positions': out['positions'], 'ffn1_pre_g': out['ffn1_pre_g'], 'ffn1_w_in': out['ffn1_w_in'], 'ffn1_w_down': out['ffn1_w_down'], 'ffn1_post_g': out['ffn1_post_g'], 'mix_pre_g': out['mix_pre_g'], 'mix_post_g': out['mix_post_g'], 'ffn2_pre_g': out['ffn2_pre_g'], 'ffn2_w_in': out['ffn2_w_in'], 'ffn2_w_down': out['ffn2_w_down'], 'ffn2_post_g': out['ffn2_post_g'], 'ple_pre_g': out['ple_pre_g'], 'ple_w_gate': out['ple_w_gate'], 'ple_w_proj': out['ple_w_proj'], 'ple_post_g': out['ple_post_g'], 'hyb_w_in': out['hyb_w_in'], 'gm_ln_g': out['gm_ln_g'], 'gm_ln_b': out['gm_ln_b'], 'gm_w_s': out['gm_w_s'], 'gm_b_s': out['gm_b_s'], 'ssd_conv_w': out['ssd_conv_w'], 'ssd_conv_b': out['ssd_conv_b'], 'ssd_dt_bias': out['ssd_dt_bias'], 'ssd_a_log': out['ssd_a_log'], 'ssd_d': out['ssd_d'], 'ssd_norm_g': out['ssd_norm_g'], 'hyb_w_out': out['hyb_w_out'], 'mla_w_in': out['mla_w_in'], 'mla_q_norm_g': out['mla_q_norm_g'], 'mla_kv_norm_g': out['mla_kv_norm_g'], 'mla_w_uq': out['mla_w_uq'], 'mla_w_ukv': out['mla_w_ukv'], 'mla_w_out': out['mla_w_out'], 'loss_target': out['loss_target'], 'm_ffn1_pre_g': out['m_ffn1_pre_g'], 'm_ffn1_w_in': out['m_ffn1_w_in'], 'm_ffn1_w_down': out['m_ffn1_w_down'], 'm_ffn1_post_g': out['m_ffn1_post_g'], 'm_mix_pre_g': out['m_mix_pre_g'], 'm_mix_post_g': out['m_mix_post_g'], 'm_ffn2_pre_g': out['m_ffn2_pre_g'], 'm_ffn2_w_in': out['m_ffn2_w_in'], 'm_ffn2_w_down': out['m_ffn2_w_down'], 'm_ffn2_post_g': out['m_ffn2_post_g'], 'm_ple_pre_g': out['m_ple_pre_g'], 'm_ple_w_gate': out['m_ple_w_gate'], 'm_ple_w_proj': out['m_ple_w_proj'], 'm_ple_post_g': out['m_ple_post_g'], 'm_hyb_w_in': out['m_hyb_w_in'], 'm_gm_ln_g': out['m_gm_ln_g'], 'm_gm_ln_b': out['m_gm_ln_b'], 'm_gm_w_s': out['m_gm_w_s'], 'm_gm_b_s': out['m_gm_b_s'], 'm_ssd_conv_w': out['m_ssd_conv_w'], 'm_ssd_conv_b': out['m_ssd_conv_b'], 'm_ssd_dt_bias': out['m_ssd_dt_bias'], 'm_ssd_a_log': out['m_ssd_a_log'], 'm_ssd_d': out['m_ssd_d'], 'm_ssd_norm_g': out['m_ssd_norm_g'], 'm_hyb_w_out': out['m_hyb_w_out'], 'm_mla_w_in': out['m_mla_w_in'], 'm_mla_q_norm_g': out['m_mla_q_norm_g'], 'm_mla_kv_norm_g': out['m_mla_kv_norm_g'], 'm_mla_w_uq': out['m_mla_w_uq'], 'm_mla_w_ukv': out['m_mla_w_ukv'], 'm_mla_w_out': out['m_mla_w_out'], 'v_ffn1_pre_g': out['v_ffn1_pre_g'], 'v_ffn1_w_in': out['v_ffn1_w_in'], 'v_ffn1_w_down': out['v_ffn1_w_down'], 'v_ffn1_post_g': out['v_ffn1_post_g'], 'v_mix_pre_g': out['v_mix_pre_g'], 'v_mix_post_g': out['v_mix_post_g'], 'v_ffn2_pre_g': out['v_ffn2_pre_g'], 'v_ffn2_w_in': out['v_ffn2_w_in'], 'v_ffn2_w_down': out['v_ffn2_w_down'], 'v_ffn2_post_g': out['v_ffn2_post_g'], 'v_ple_pre_g': out['v_ple_pre_g'], 'v_ple_w_gate': out['v_ple_w_gate'], 'v_ple_w_proj': out['v_ple_w_proj'], 'v_ple_post_g': out['v_ple_post_g'], 'v_hyb_w_in': out['v_hyb_w_in'], 'v_gm_ln_g': out['v_gm_ln_g'], 'v_gm_ln_b': out['v_gm_ln_b'], 'v_gm_w_s': out['v_gm_w_s'], 'v_gm_b_s': out['v_gm_b_s'], 'v_ssd_conv_w': out['v_ssd_conv_w'], 'v_ssd_conv_b': out['v_ssd_conv_b'], 'v_ssd_dt_bias': out['v_ssd_dt_bias'], 'v_ssd_a_log': out['v_ssd_a_log'], 'v_ssd_d': out['v_ssd_d'], 'v_ssd_norm_g': out['v_ssd_norm_g'], 'v_hyb_w_out': out['v_hyb_w_out'], 'v_mla_w_in': out['v_mla_w_in'], 'v_mla_q_norm_g': out['v_mla_q_norm_g'], 'v_mla_kv_norm_g': out['v_mla_kv_norm_g'], 'v_mla_w_uq': out['v_mla_w_uq'], 'v_mla_w_ukv': out['v_mla_w_ukv'], 'v_mla_w_out': out['v_mla_w_out']}


def _loss(weights, diff, rest, loss_target):
    with _jax.named_scope("forward"):
        args = {**rest, TWIN_DIFF_INPUT: diff, **{k: w.astype(_WEIGHT_DTYPES[k]) for k, w in weights.items()}}
        y = _forward(args)
    with _jax.named_scope("loss_head"):
        err = _jnp.square(y.astype(_jnp.float32) - loss_target)
        return 0.5 * _jnp.sum(_jnp.mean(err, axis=-1)) if err.ndim else 0.5 * err


def _adamw(w, g, m, v):
    m = ADAM_B1 * m + (1.0 - ADAM_B1) * g
    v = ADAM_B2 * v + (1.0 - ADAM_B2) * _jnp.square(g)
    m_hat = m / (1.0 - ADAM_B1 ** ADAM_STEP)
    v_hat = v / (1.0 - ADAM_B2 ** ADAM_STEP)
    delta = -ADAM_LR * (m_hat / (_jnp.sqrt(v_hat) + ADAM_EPS) + ADAM_WD * w)
    return delta, m, v


def reference(x, p, positions, ffn1_pre_g, ffn1_w_in, ffn1_w_down, ffn1_post_g, mix_pre_g, mix_post_g, ffn2_pre_g, ffn2_w_in, ffn2_w_down, ffn2_post_g, ple_pre_g, ple_w_gate, ple_w_proj, ple_post_g, hyb_w_in, gm_ln_g, gm_ln_b, gm_w_s, gm_b_s, ssd_conv_w, ssd_conv_b, ssd_dt_bias, ssd_a_log, ssd_d, ssd_norm_g, hyb_w_out, mla_w_in, mla_q_norm_g, mla_kv_norm_g, mla_w_uq, mla_w_ukv, mla_w_out, loss_target, m_ffn1_pre_g, m_ffn1_w_in, m_ffn1_w_down, m_ffn1_post_g, m_mix_pre_g, m_mix_post_g, m_ffn2_pre_g, m_ffn2_w_in, m_ffn2_w_down, m_ffn2_post_g, m_ple_pre_g, m_ple_w_gate, m_ple_w_proj, m_ple_post_g, m_hyb_w_in, m_gm_ln_g, m_gm_ln_b, m_gm_w_s, m_gm_b_s, m_ssd_conv_w, m_ssd_conv_b, m_ssd_dt_bias, m_ssd_a_log, m_ssd_d, m_ssd_norm_g, m_hyb_w_out, m_mla_w_in, m_mla_q_norm_g, m_mla_kv_norm_g, m_mla_w_uq, m_mla_w_ukv, m_mla_w_out, v_ffn1_pre_g, v_ffn1_w_in, v_ffn1_w_down, v_ffn1_post_g, v_mix_pre_g, v_mix_post_g, v_ffn2_pre_g, v_ffn2_w_in, v_ffn2_w_down, v_ffn2_post_g, v_ple_pre_g, v_ple_w_gate, v_ple_w_proj, v_ple_post_g, v_hyb_w_in, v_gm_ln_g, v_gm_ln_b, v_gm_w_s, v_gm_b_s, v_ssd_conv_w, v_ssd_conv_b, v_ssd_dt_bias, v_ssd_a_log, v_ssd_d, v_ssd_norm_g, v_hyb_w_out, v_mla_w_in, v_mla_q_norm_g, v_mla_kv_norm_g, v_mla_w_uq, v_mla_w_ukv, v_mla_w_out):
    given = dict(x=x, p=p, positions=positions, ffn1_pre_g=ffn1_pre_g, ffn1_w_in=ffn1_w_in, ffn1_w_down=ffn1_w_down, ffn1_post_g=ffn1_post_g, mix_pre_g=mix_pre_g, mix_post_g=mix_post_g, ffn2_pre_g=ffn2_pre_g, ffn2_w_in=ffn2_w_in, ffn2_w_down=ffn2_w_down, ffn2_post_g=ffn2_post_g, ple_pre_g=ple_pre_g, ple_w_gate=ple_w_gate, ple_w_proj=ple_w_proj, ple_post_g=ple_post_g, hyb_w_in=hyb_w_in, gm_ln_g=gm_ln_g, gm_ln_b=gm_ln_b, gm_w_s=gm_w_s, gm_b_s=gm_b_s, ssd_conv_w=ssd_conv_w, ssd_conv_b=ssd_conv_b, ssd_dt_bias=ssd_dt_bias, ssd_a_log=ssd_a_log, ssd_d=ssd_d, ssd_norm_g=ssd_norm_g, hyb_w_out=hyb_w_out, mla_w_in=mla_w_in, mla_q_norm_g=mla_q_norm_g, mla_kv_norm_g=mla_kv_norm_g, mla_w_uq=mla_w_uq, mla_w_ukv=mla_w_ukv, mla_w_out=mla_w_out, loss_target=loss_target, m_ffn1_pre_g=m_ffn1_pre_g, m_ffn1_w_in=m_ffn1_w_in, m_ffn1_w_down=m_ffn1_w_down, m_ffn1_post_g=m_ffn1_post_g, m_mix_pre_g=m_mix_pre_g, m_mix_post_g=m_mix_post_g, m_ffn2_pre_g=m_ffn2_pre_g, m_ffn2_w_in=m_ffn2_w_in, m_ffn2_w_down=m_ffn2_w_down, m_ffn2_post_g=m_ffn2_post_g, m_ple_pre_g=m_ple_pre_g, m_ple_w_gate=m_ple_w_gate, m_ple_w_proj=m_ple_w_proj, m_ple_post_g=m_ple_post_g, m_hyb_w_in=m_hyb_w_in, m_gm_ln_g=m_gm_ln_g, m_gm_ln_b=m_gm_ln_b, m_gm_w_s=m_gm_w_s, m_gm_b_s=m_gm_b_s, m_ssd_conv_w=m_ssd_conv_w, m_ssd_conv_b=m_ssd_conv_b, m_ssd_dt_bias=m_ssd_dt_bias, m_ssd_a_log=m_ssd_a_log, m_ssd_d=m_ssd_d, m_ssd_norm_g=m_ssd_norm_g, m_hyb_w_out=m_hyb_w_out, m_mla_w_in=m_mla_w_in, m_mla_q_norm_g=m_mla_q_norm_g, m_mla_kv_norm_g=m_mla_kv_norm_g, m_mla_w_uq=m_mla_w_uq, m_mla_w_ukv=m_mla_w_ukv, m_mla_w_out=m_mla_w_out, v_ffn1_pre_g=v_ffn1_pre_g, v_ffn1_w_in=v_ffn1_w_in, v_ffn1_w_down=v_ffn1_w_down, v_ffn1_post_g=v_ffn1_post_g, v_mix_pre_g=v_mix_pre_g, v_mix_post_g=v_mix_post_g, v_ffn2_pre_g=v_ffn2_pre_g, v_ffn2_w_in=v_ffn2_w_in, v_ffn2_w_down=v_ffn2_w_down, v_ffn2_post_g=v_ffn2_post_g, v_ple_pre_g=v_ple_pre_g, v_ple_w_gate=v_ple_w_gate, v_ple_w_proj=v_ple_w_proj, v_ple_post_g=v_ple_post_g, v_hyb_w_in=v_hyb_w_in, v_gm_ln_g=v_gm_ln_g, v_gm_ln_b=v_gm_ln_b, v_gm_w_s=v_gm_w_s, v_gm_b_s=v_gm_b_s, v_ssd_conv_w=v_ssd_conv_w, v_ssd_conv_b=v_ssd_conv_b, v_ssd_dt_bias=v_ssd_dt_bias, v_ssd_a_log=v_ssd_a_log, v_ssd_d=v_ssd_d, v_ssd_norm_g=v_ssd_norm_g, v_hyb_w_out=v_hyb_w_out, v_mla_w_in=v_mla_w_in, v_mla_q_norm_g=v_mla_q_norm_g, v_mla_kv_norm_g=v_mla_kv_norm_g, v_mla_w_uq=v_mla_w_uq, v_mla_w_ukv=v_mla_w_ukv, v_mla_w_out=v_mla_w_out)
    weights = {n: given[n] for n in TWIN_WEIGHTS}
    shared = {n: given[n] for n in SHARED_INPUTS}
    per_example = {n: given[n] for n in ['x', 'p', 'positions']}
    grad_fn = _jax.value_and_grad(_loss, argnums=(0, 1))

    def one_microbatch(ex, loss_target):
        ex = dict(ex)
        diff = ex.pop(TWIN_DIFF_INPUT)
        return grad_fn(weights, diff, {**shared, **ex}, loss_target)

    if N_MICROBATCH == 1:
        loss, (grad_w, grad_x) = one_microbatch(per_example, given["loss_target"])
    else:
        def body(carry, xs):
            loss_sum, grad_sum = carry
            l_k, (gw_k, gx_k) = one_microbatch(xs[0], xs[1])
            with _jax.named_scope("update"):
                return (loss_sum + l_k, _jax.tree.map(_jnp.add, grad_sum, gw_k)), gx_k

        init = (_jnp.zeros((), _jnp.float32), _jax.tree.map(_jnp.zeros_like, weights))
        (loss, grad_w), grad_x = _jax.lax.scan(body, init, (per_example, given["loss_target"]))
    with _jax.named_scope("update"):
        delta_w, new_m, new_v = {}, {}, {}
        for n in TWIN_WEIGHTS:
            delta_w[n], new_m[n], new_v[n] = _adamw(weights[n], grad_w[n], given["m_" + n], given["v_" + n])
    return (loss, grad_x, *[grad_w[n] for n in TWIN_WEIGHTS], *[delta_w[n] for n in TWIN_WEIGHTS],
            *[new_m[n] for n in TWIN_WEIGHTS], *[new_v[n] for n in TWIN_WEIGHTS])
```

```python
import functools
import math

import jax
import jax.numpy as jnp
import numpy as np
from jax import lax
from jax.experimental import pallas as pl
from jax.experimental.pallas import tpu as pltpu

F32 = jnp.float32
BF16 = jnp.bfloat16
SDS = jax.ShapeDtypeStruct
MESH = pl.DeviceIdType.MESH
HIGHEST = lax.Precision.HIGHEST

D_MODEL = 1024
DEPTH = 4
D_FF = 2816
NORM_EPS = 1e-6
LN_EPS = 1e-5
GM_HEADS = 8
CHUNK = 128
SSD_HEADS = 16
SSD_HEAD_DIM = 64
SSD_INNER = 1024
SSD_STATE = 128
SSD_CONV = 4
SSD_CONV_CH = 1536
HYB_MAIN = 4608
HYB_IN = 4624
HYB_PAD = 5120
MLA_HEADS = 16
MLA_NOPE = 128
MLA_ROPE = 64
MLA_QK = 192
MLA_QPAD = 256
MLA_Q_LORA = 256
MLA_KV_LORA = 128
ROPE_BASE = 10000.0
ADAM_LR = 0.001
ADAM_B1 = 0.9
ADAM_B2 = 0.999
ADAM_EPS = 1e-08
ADAM_WD = 0.01
ADAM_STEP = 10

N_CHIPS = 4
LANES = 128
VMEM_LIMIT = 56 * 1024 * 1024
PACK_ROWS = 2048

WEIGHTS = [
    ("ffn1_pre_g", (4, 1024), None, False),
    ("ffn1_w_in", (4, 1024, 5632), 2, True),
    ("ffn1_w_down", (4, 2816, 1024), 1, True),
    ("ffn1_post_g", (4, 1024), None, False),
    ("mix_pre_g", (4, 1024), None, False),
    ("mix_post_g", (4, 1024), None, False),
    ("ffn2_pre_g", (4, 1024), None, False),
    ("ffn2_w_in", (4, 1024, 5632), 2, True),
    ("ffn2_w_down", (4, 2816, 1024), 1, True),
    ("ffn2_post_g", (4, 1024), None, False),
    ("ple_pre_g", (4, 1024), None, False),
    ("ple_w_gate", (4, 1024, 1024), 1, True),
    ("ple_w_proj", (4, 256, 1024), 2, True),
    ("ple_post_g", (4, 1024), None, False),
    ("hyb_w_in", (2, 1024, 4624), 2, True),
    ("gm_ln_g", (2, 1024), None, False),
    ("gm_ln_b", (2, 1024), None, False),
    ("gm_w_s", (2, 8, 128, 128), None, False),
    ("gm_b_s", (2, 8, 128), None, False),
    ("ssd_conv_w", (2, 4, 1536), 2, False),
    ("ssd_conv_b", (2, 1536), None, False),
    ("ssd_dt_bias", (2, 16), None, False),
    ("ssd_a_log", (2, 16), None, False),
    ("ssd_d", (2, 16), None, False),
    ("ssd_norm_g", (2, 1024), None, False),
    ("hyb_w_out", (2, 2048, 1024), 1, True),
    ("mla_w_in", (2, 1024, 448), 1, True),
    ("mla_q_norm_g", (2, 256), 1, False),
    ("mla_kv_norm_g", (2, 128), None, False),
    ("mla_w_uq", (2, 256, 3072), 2, True),
    ("mla_w_ukv", (2, 128, 4096), 2, True),
    ("mla_w_out", (2, 2048, 1024), 1, True),
]
WNAMES = [w[0] for w in WEIGHTS]


def _pick(dim, target):
    if dim <= target:
        return dim
    t = (target // LANES) * LANES
    while t >= LANES:
        if dim % t == 0:
            return t
        t -= LANES
    return dim


def mm(a, b, *, ta=False, tb=False, out_dtype=F32, name, tm=512, tn=1024, tk=1024):
    if ta:
        K, M = a.shape
    else:
        M, K = a.shape
    if tb:
        N, K2 = b.shape
    else:
        K2, N = b.shape
    assert K == K2, (a.shape, b.shape, ta, tb)
    bm, bn, bk = _pick(M, tm), _pick(N, tn), _pick(K, tk)
    nk = K // bk
    a_spec = pl.BlockSpec((bk, bm), lambda i, j, k: (k, i)) if ta else pl.BlockSpec((bm, bk), lambda i, j, k: (i, k))
    b_spec = pl.BlockSpec((bn, bk), lambda i, j, k: (j, k)) if tb else pl.BlockSpec((bk, bn), lambda i, j, k: (k, j))
    dn = (((0 if ta else 1,), (1 if tb else 0,)), ((), ()))

    def body(a_ref, b_ref, o_ref, acc_ref):
        k = pl.program_id(2)

        @pl.when(k == 0)
        def _():
            acc_ref[...] = jnp.zeros_like(acc_ref)

        acc_ref[...] += lax.dot_general(a_ref[...].astype(BF16), b_ref[...].astype(BF16), dn,
                                        preferred_element_type=F32)

        @pl.when(k == nk - 1)
        def _():
            o_ref[...] = acc_ref[...].astype(o_ref.dtype)

    return pl.pallas_call(
        body, name=name, grid=(M // bm, N // bn, nk),
        in_specs=[a_spec, b_spec], out_specs=pl.BlockSpec((bm, bn), lambda i, j, k: (i, j)),
        out_shape=SDS((M, N), out_dtype), scratch_shapes=[pltpu.VMEM((bm, bn), F32)],
        compiler_params=pltpu.CompilerParams(dimension_semantics=("parallel", "parallel", "arbitrary"),
                                             vmem_limit_bytes=VMEM_LIMIT),
    )(a, b)


def row_call(fn, xs, ps, outs, accs=(), *, tb, name, reverse=False):
    xs = [x if isinstance(x, tuple) else (x, x.shape[1], 0) for x in xs]
    T = xs[0][0].shape[0]
    tb = min(tb, T)
    n = T // tb
    assert n * tb == T
    nx, npar, no, na = len(xs), len(ps), len(outs), len(accs)

    def ridx(i):
        return n - 1 - i if reverse else i

    in_specs = [pl.BlockSpec((tb, w), functools.partial(lambda i, cb: (ridx(i), cb), cb=cb)) for (_, w, cb) in xs]
    in_specs += [pl.BlockSpec(p.shape, functools.partial(lambda i, nd: (0,) * nd, nd=p.ndim)) for p in ps]
    out_specs = [pl.BlockSpec((tb, c), lambda i: (ridx(i), 0)) for (c, _) in outs]
    out_specs += [pl.BlockSpec(s, functools.partial(lambda i, nd: (0,) * nd, nd=len(s))) for s in accs]
    out_shape = [SDS((T, c), dt) for (c, dt) in outs] + [SDS(s, F32) for s in accs]

    def body(*refs):
        xr, pr = refs[:nx], refs[nx:nx + npar]
        orf, ar = refs[nx + npar:nx + npar + no], refs[nx + npar + no:]
        res = fn(*[r[...] for r in xr], *[r[...] for r in pr])
        for r, v in zip(orf, res[:no]):
            r[...] = v.astype(r.dtype)
        if na:
            @pl.when(pl.program_id(0) == 0)
            def _():
                for r in ar:
                    r[...] = jnp.zeros_like(r)

            for r, v in zip(ar, res[no:]):
                r[...] += v.astype(F32)

    res = pl.pallas_call(
        body, name=name, grid=(n,), in_specs=in_specs, out_specs=out_specs, out_shape=out_shape,
        compiler_params=pltpu.CompilerParams(dimension_semantics=("arbitrary",), vmem_limit_bytes=VMEM_LIMIT),
    )(*[x[0] for x in xs], *ps)
    return res


def _f32(*a):
    return [v.astype(F32) for v in a]


def t_rms(x, g):
    return x * lax.rsqrt(jnp.mean(x * x, axis=-1, keepdims=True) + NORM_EPS) * g


def t_swiglu(gu):
    return jax.nn.silu(gu[:, :D_FF]) * gu[:, D_FF:]


def t_ple(gl, pp, g):
    return t_rms(jax.nn.sigmoid(gl) * pp, g)


def _iota(shape, d):
    return lax.broadcasted_iota(jnp.int32, shape, d)


def _bdot(a, b, dn=(((1,), (0,)), ((), ()))):
    return lax.dot_general(a.astype(BF16), b.astype(BF16), dn, preferred_element_type=F32)


def _hdot(a, b):
    return jnp.dot(a, b, precision=HIGHEST, preferred_element_type=F32)


NT = (((1,), (1,)), ((), ()))
TN = (((0,), (0,)), ((), ()))


def t_gmlp(uv, ln_g, ln_b, w_s, b_st):
    tb = uv.shape[0]
    guv = jax.nn.gelu(uv)
    u, v = guv[:, :1024], guv[:, 1024:]
    tri = _iota((CHUNK, CHUNK), 1) <= _iota((CHUNK, CHUNK), 0)
    rows = []
    for c in range(tb // CHUNK):
        vc = v[c * CHUNK:(c + 1) * CHUNK]
        heads = []
        for h in range(GM_HEADS):
            sl = slice(h * 128, (h + 1) * 128)
            vh = vc[:, sl]
            xc = vh - jnp.mean(vh, axis=-1, keepdims=True)
            var = jnp.mean(xc * xc, axis=-1, keepdims=True)
            y = xc * lax.rsqrt(var + LN_EPS) * ln_g[:, sl] + ln_b[:, sl]
            wm = jnp.where(tri, w_s[sl, :], 0.0)
            heads.append(_bdot(wm, y) + b_st[:, h:h + 1])
        rows.append(jnp.concatenate(heads, axis=1))
    mixed = rows[0] if len(rows) == 1 else jnp.concatenate(rows, axis=0)
    return u * mixed


def t_ssd(pre, dtr, z, st, dt_bias, a_log, d_exp, norm_g):
    L = CHUNK
    xbc = jax.nn.silu(pre)
    xs, bm, cm = xbc[:, :1024], xbc[:, 1024:1280], xbc[:, 1280:1536]
    valid = _iota((1, LANES), 1) < SSD_HEADS
    dt16 = jnp.where(valid, jax.nn.softplus(dtr + dt_bias), 0.0)
    a16 = jnp.where(valid, -jnp.exp(a_log), 0.0)
    da16 = dt16 * a16
    tri = _iota((L, L), 1) <= _iota((L, L), 0)
    acs16 = _hdot(tri.astype(F32), da16)
    hh, cc = _iota((LANES, 1024), 0), _iota((LANES, 1024), 1)
    expand = ((cc >= hh * SSD_HEAD_DIM) & (cc < (hh + 1) * SSD_HEAD_DIM)).astype(F32)
    acs = _hdot(acs16, expand)
    dte = _hdot(dt16, expand)
    alast = jnp.sum(jnp.where(_iota((L, 1024), 0) == L - 1, acs, 0.0), axis=0, keepdims=True)
    xd = xs * dte
    groups = [slice(0, 512), slice(512, 1024)]
    bg = [bm[:, :128], bm[:, 128:]]
    cg = [cm[:, :128], cm[:, 128:]]
    yoff = jnp.concatenate([_bdot(cg[g], st[:, groups[g]]) for g in range(2)], axis=1) * jnp.exp(acs)
    xdw = xd * jnp.exp(alast - acs)
    s_t = jnp.concatenate([_bdot(bg[g], xdw[:, groups[g]], TN) for g in range(2)], axis=1)
    st_new = st * jnp.exp(alast) + s_t
    cb = [_bdot(cg[g], bg[g], NT) for g in range(2)]
    acs16_t = acs16.T
    lo = _iota((1, LANES), 1) < SSD_HEAD_DIM
    slabs = []
    for j in range(SSD_HEADS // 2):
        g = j // 4
        xslab = xd[:, j * 128:(j + 1) * 128]
        acc = None
        for half in range(2):
            h = 2 * j + half
            seg = acs16[:, h:h + 1] - acs16_t[h:h + 1, :]
            mmat = cb[g] * jnp.exp(jnp.where(tri, seg, -1e30))
            xm = jnp.where(lo if half == 0 else jnp.logical_not(lo), xslab, 0.0)
            term = _bdot(mmat, xm)
            acc = term if acc is None else acc + term
        slabs.append(acc)
    y = jnp.concatenate(slabs, axis=1) + yoff + d_exp * xs
    yg = y * jax.nn.silu(z)
    outs = []
    for g in range(2):
        t = yg[:, groups[g]]
        outs.append(t * lax.rsqrt(jnp.mean(t * t, axis=-1, keepdims=True) + NORM_EPS) * norm_g[:, groups[g]])
    return jnp.concatenate(outs, axis=1), st_new


def t_kprep(c_all, cs, sn, qg, kvg):
    cqn = t_rms(c_all[:, :256], qg)
    ckvn = t_rms(c_all[:, 256:384], kvg)
    kr = c_all[:, 384:512] * cs + c_all[:, 512:640] * sn
    return cqn, ckvn, kr


def t_qrope(qb, c256, s256):
    scale = MLA_QK ** -0.5
    half = MLA_HEADS * MLA_QPAD
    outs = []
    for h in range(MLA_HEADS):
        a = qb[:, h * MLA_QPAD:(h + 1) * MLA_QPAD]
        b = qb[:, half + h * MLA_QPAD:half + (h + 1) * MLA_QPAD]
        outs.append((a * c256 + b * s256) * scale)
    return jnp.concatenate(outs, axis=1)


def rms_fwd(h, g, *, name, tb=512):
    def fn(h, g):
        return (t_rms(h.astype(F32), g),)
    return row_call(fn, [h], [g], [(h.shape[1], BF16)], tb=tb, name=name)[0]


def rms_bwd(h, dhn, dres, g, *, name, tb=256):
    def fn(h, dhn, dres, g):
        h, dhn, dres = _f32(h, dhn, dres)
        _, vjp = jax.vjp(t_rms, h, g)
        dh, dg = vjp(dhn)
        return dres + dh, dg
    return row_call(fn, [h, dhn, dres], [g], [(h.shape[1], F32)], [g.shape], tb=tb, name=name)


def post_fwd(h, f, g, scale, *, name, tb=512):
    def fn(h, f, g):
        return (h + scale * t_rms(f.astype(F32), g),)
    return row_call(fn, [h, f], [g], [(h.shape[1], F32)], tb=tb, name=name)[0]


def post_bwd(f, dout, g, scale, *, name, tb=256):
    def fn(f, dout, g):
        f, dout = _f32(f, dout)
        _, vjp = jax.vjp(lambda f, g: scale * t_rms(f, g), f, g)
        return vjp(dout)
    return row_call(fn, [f, dout], [g], [(f.shape[1], BF16)], [g.shape], tb=tb, name=name)


def swiglu_fwd(gu, *, name, tb=256):
    def fn(gu):
        return (t_swiglu(gu.astype(F32)),)
    return row_call(fn, [gu], [], [(D_FF, BF16)], tb=tb, name=name)[0]


def swiglu_bwd(gu, da, *, name, tb=256):
    def fn(gu, da):
        gu, da = _f32(gu, da)
        _, vjp = jax.vjp(t_swiglu, gu)
        return vjp(da)
    return row_call(fn, [gu, da], [], [(2 * D_FF, BF16)], tb=tb, name=name)[0]


def ple_fwd(h, gl, pp, g, *, name, tb=512):
    def fn(h, gl, pp, g):
        return (h + t_ple(gl, pp, g),)
    return row_call(fn, [h, gl, pp], [g], [(D_MODEL, F32)], tb=tb, name=name)[0]


def ple_bwd(gl, pp, dout, g, *, name, tb=256):
    def fn(gl, pp, dout, g):
        _, vjp = jax.vjp(t_ple, gl, pp, g)
        return vjp(dout)
    return row_call(fn, [gl, pp, dout], [g], [(D_MODEL, BF16), (D_MODEL, BF16)], [g.shape], tb=tb, name=name)


def gmlp_fwd(proj, ln_g, ln_b, w_s, b_st, *, name, tb=256):
    def fn(uv, ln_g, ln_b, w_s, b_st):
        return (t_gmlp(uv, ln_g, ln_b, w_s, b_st),)
    return row_call(fn, [(proj, 2048, 0)], [ln_g, ln_b, w_s, b_st], [(1024, BF16)], tb=tb, name=name)[0]


def gmlp_bwd(proj, dya, ln_g, ln_b, w_s, b_st, *, name, tb=128):
    def fn(uv, dya, ln_g, ln_b, w_s, b_st):
        _, vjp = jax.vjp(t_gmlp, uv, ln_g, ln_b, w_s, b_st)
        return vjp(dya.astype(F32))
    return row_call(fn, [(proj, 2048, 0), (dya, 1024, 0)], [ln_g, ln_b, w_s, b_st], [(2048, BF16)],
                    [ln_g.shape, ln_b.shape, w_s.shape, b_st.shape], tb=tb, name=name)


def kprep_fwd(c_all, cs, sn, qg, kvg, *, name, tb=512):
    return row_call(t_kprep, [c_all, cs, sn], [qg, kvg], [(256, BF16), (128, BF16), (128, BF16)], tb=tb, name=name)


def kprep_bwd(c_all, cs, sn, dcqn, dckvn, dkr, qg, kvg, *, name, tb=256):
    def fn(c_all, cs, sn, dcqn, dckvn, dkr, qg, kvg):
        dcqn, dckvn, dkr = _f32(dcqn, dckvn, dkr)
        _, vjp = jax.vjp(lambda c, qg, kvg: t_kprep(c, cs, sn, qg, kvg), c_all, qg, kvg)
        return vjp((dcqn, dckvn, dkr))
    return row_call(fn, [c_all, cs, sn, dcqn, dckvn, dkr], [qg, kvg], [(640, BF16)], [qg.shape, kvg.shape],
                    tb=tb, name=name)


def qrope_fwd(qb, c256, s256, *, name, tb=256):
    def fn(qb, c256, s256):
        return (t_qrope(qb, c256, s256),)
    return row_call(fn, [qb, c256, s256], [], [(MLA_HEADS * MLA_QPAD, BF16)], tb=tb, name=name)[0]


def qrope_bwd(dq, c256, s256, *, name, tb=256):
    def fn(dq, c256, s256):
        scale = MLA_QK ** -0.5
        a, b = [], []
        for h in range(MLA_HEADS):
            d = dq[:, h * MLA_QPAD:(h + 1) * MLA_QPAD] * scale
            a.append(d * c256)
            b.append(d * s256)
        return (jnp.concatenate(a + b, axis=1),)
    return row_call(fn, [dq, c256, s256], [], [(2 * MLA_HEADS * MLA_QPAD, BF16)], tb=tb, name=name)[0]


def delta_fwd(do, o, *, name, tb=512):
    def fn(do, o):
        do, o = _f32(do, o)
        outs = []
        for h in range(MLA_HEADS):
            sl = slice(h * 128, (h + 1) * 128)
            outs.append(jnp.broadcast_to(jnp.sum(do[:, sl] * o[:, sl], axis=-1, keepdims=True), (do.shape[0], 128)))
        return (jnp.concatenate(outs, axis=1),)
    return row_call(fn, [do, o], [], [(2048, F32)], tb=tb, name=name)[0]


def headsum(dkr_h, *, name, tb=512):
    def fn(d):
        acc = d[:, :128]
        for h in range(1, MLA_HEADS):
            acc = acc + d[:, h * 128:(h + 1) * 128]
        return (acc,)
    return row_call(fn, [dkr_h], [], [(128, F32)], tb=tb, name=name)[0]


def loss_fwd(y, t, *, name, tb=512):
    def fn(y, t):
        e = y - t
        return e * (1.0 / D_MODEL), jnp.sum(e * e, axis=0, keepdims=True)
    return row_call(fn, [y, t], [], [(D_MODEL, F32)], [(1, D_MODEL)], tb=tb, name=name)


def conv_fwd(proj, w, b, *, name, tb=256):
    T = proj.shape[0]
    n = T // tb
    hb = tb // CHUNK
    C = SSD_CONV_CH

    def body(cur, prev, w_ref, b_ref, o_ref, scr):
        i = pl.program_id(0)
        scr[pl.ds(0, CHUNK), :] = jnp.where(i > 0, prev[...], 0.0)
        scr[pl.ds(CHUNK, tb), :] = cur[...]
        y = b_ref[...] + w_ref[3:4, :] * cur[...]
        for k in range(SSD_CONV - 1):
            y = y + w_ref[k:k + 1, :] * scr[pl.ds(CHUNK - (SSD_CONV - 1) + k, tb), :]
        o_ref[...] = y

    return pl.pallas_call(
        body, name=name, grid=(n,),
        in_specs=[pl.BlockSpec((tb, C), lambda i: (i, 2)),
                  pl.BlockSpec((CHUNK, C), lambda i: (jnp.maximum(i * hb - 1, 0), 2)),
                  pl.BlockSpec((SSD_CONV, C), lambda i: (0, 0)), pl.BlockSpec((1, C), lambda i: (0, 0))],
        out_specs=pl.BlockSpec((tb, C), lambda i: (i, 0)), out_shape=SDS((T, C), F32),
        scratch_shapes=[pltpu.VMEM((CHUNK + tb, C), F32)],
        compiler_params=pltpu.CompilerParams(dimension_semantics=("arbitrary",), vmem_limit_bytes=VMEM_LIMIT),
    )(proj, proj, w, b)


def conv_bwd(dpre, proj, w, *, name, tb=256):
    T = proj.shape[0]
    n = T // tb
    hb = tb // CHUNK
    nh = T // CHUNK
    C = SSD_CONV_CH

    def body(dcur, dnext, xcur, xprev, w_ref, dx_ref, dw_ref, db_ref, dscr, xscr):
        i = pl.program_id(0)

        @pl.when(i == 0)
        def _():
            dw_ref[...] = jnp.zeros_like(dw_ref)
            db_ref[...] = jnp.zeros_like(db_ref)

        d = dcur[...]
        dscr[pl.ds(0, tb), :] = d
        dscr[pl.ds(tb, CHUNK), :] = jnp.where(i < n - 1, dnext[...], 0.0)
        xscr[pl.ds(0, CHUNK), :] = jnp.where(i > 0, xprev[...], 0.0)
        xscr[pl.ds(CHUNK, tb), :] = xcur[...]
        dx = w_ref[3:4, :] * d
        for k in range(SSD_CONV - 1):
            dx = dx + w_ref[k:k + 1, :] * dscr[pl.ds(SSD_CONV - 1 - k, tb), :]
        dx_ref[...] = dx.astype(dx_ref.dtype)
        for k in range(SSD_CONV):
            xk = xscr[pl.ds(CHUNK - (SSD_CONV - 1) + k, tb), :]
            dw_ref[k:k + 1, :] += jnp.sum(d * xk, axis=0, keepdims=True)
        db_ref[...] += jnp.sum(d, axis=0, keepdims=True)

    return pl.pallas_call(
        body, name=name, grid=(n,),
        in_specs=[pl.BlockSpec((tb, C), lambda i: (i, 0)),
                  pl.BlockSpec((CHUNK, C), lambda i: (jnp.minimum((i + 1) * hb, nh - 1), 0)),
                  pl.BlockSpec((tb, C), lambda i: (i, 2)),
                  pl.BlockSpec((CHUNK, C), lambda i: (jnp.maximum(i * hb - 1, 0), 2)),
                  pl.BlockSpec((SSD_CONV, C), lambda i: (0, 0))],
        out_specs=[pl.BlockSpec((tb, C), lambda i: (i, 0)), pl.BlockSpec((SSD_CONV, C), lambda i: (0, 0)),
                   pl.BlockSpec((1, C), lambda i: (0, 0))],
        out_shape=[SDS((T, C), BF16), SDS((SSD_CONV, C), F32), SDS((1, C), F32)],
        scratch_shapes=[pltpu.VMEM((tb + CHUNK, C), F32), pltpu.VMEM((CHUNK + tb, C), F32)],
        compiler_params=pltpu.CompilerParams(dimension_semantics=("arbitrary",), vmem_limit_bytes=VMEM_LIMIT),
    )(dpre, dpre, proj, proj, w)


def _ssd_specs(nc, rev):
    def r(c):
        return nc - 1 - c if rev else c
    pre = pl.BlockSpec((CHUNK, SSD_CONV_CH), lambda c: (r(c), 0))
    dtr = pl.BlockSpec((CHUNK, LANES), lambda c: (r(c), HYB_MAIN // LANES))
    z = pl.BlockSpec((CHUNK, 1024), lambda c: (r(c), 2))
    row = pl.BlockSpec((CHUNK, 1024), lambda c: (r(c), 0))
    return pre, dtr, z, row


def _pspec(shape):
    return pl.BlockSpec(shape, lambda c: (0,) * len(shape))


def ssd_fwd(pre, proj, dt_bias, a_log, d_exp, norm_g, *, name):
    T = pre.shape[0]
    nc = T // CHUNK
    s_pre, s_dt, s_z, s_row = _ssd_specs(nc, False)

    def body(pre_r, dt_r, z_r, b_r, a_r, d_r, g_r, y_r, sv_r, st):
        @pl.when(pl.program_id(0) == 0)
        def _():
            st[...] = jnp.zeros_like(st)

        s0 = st[...]
        sv_r[...] = s0
        y, s1 = t_ssd(pre_r[...], dt_r[...], z_r[...], s0, b_r[...], a_r[...], d_r[...], g_r[...])
        y_r[...] = y.astype(y_r.dtype)
        st[...] = s1

    return pl.pallas_call(
        body, name=name, grid=(nc,),
        in_specs=[s_pre, s_dt, s_z, _pspec((1, LANES)), _pspec((1, LANES)), _pspec((1, 1024)), _pspec((1, 1024))],
        out_specs=[s_row, s_row], out_shape=[SDS((T, 1024), BF16), SDS((T, 1024), F32)],
        scratch_shapes=[pltpu.VMEM((SSD_STATE, 1024), F32)],
        compiler_params=pltpu.CompilerParams(dimension_semantics=("arbitrary",), vmem_limit_bytes=VMEM_LIMIT),
    )(pre, proj, proj, dt_bias, a_log, d_exp, norm_g)


def ssd_bwd(pre, proj, states, dyab, dt_bias, a_log, d_exp, norm_g, *, name):
    T = pre.shape[0]
    nc = T // CHUNK
    s_pre, s_dt, s_z, s_row = _ssd_specs(nc, True)
    s_dtout = pl.BlockSpec((CHUNK, LANES), lambda c: (nc - 1 - c, 0))
    s_dy = pl.BlockSpec((CHUNK, 1024), lambda c: (nc - 1 - c, 1))

    def body(pre_r, dt_r, z_r, sv_r, dy_r, b_r, a_r, d_r, g_r, dpre_r, ddt_r, dz_r, db_r, da_r, dd_r, dg_r, dst):
        @pl.when(pl.program_id(0) == 0)
        def _():
            dst[...] = jnp.zeros_like(dst)
            for r in (db_r, da_r, dd_r, dg_r):
                r[...] = jnp.zeros_like(r)

        _, vjp = jax.vjp(t_ssd, pre_r[...], dt_r[...], z_r[...], sv_r[...], b_r[...], a_r[...], d_r[...], g_r[...])
        dpre, ddt, dz, ds0, db, da, dd, dg = vjp((dy_r[...].astype(F32), dst[...]))
        dpre_r[...] = dpre
        ddt_r[...] = ddt.astype(ddt_r.dtype)
        dz_r[...] = dz.astype(dz_r.dtype)
        dst[...] = ds0
        db_r[...] += db
        da_r[...] += da
        dd_r[...] += dd
        dg_r[...] += dg

    return pl.pallas_call(
        body, name=name, grid=(nc,),
        in_specs=[s_pre, s_dt, s_z, s_row, s_dy, _pspec((1, LANES)), _pspec((1, LANES)), _pspec((1, 1024)),
                  _pspec((1, 1024))],
        out_specs=[s_pre, s_dtout, s_row, _pspec((1, LANES)), _pspec((1, LANES)), _pspec((1, 1024)), _pspec((1, 1024))],
        out_shape=[SDS((T, SSD_CONV_CH), F32), SDS((T, LANES), BF16), SDS((T, 1024), BF16),
                   SDS((1, LANES), F32), SDS((1, LANES), F32), SDS((1, 1024), F32), SDS((1, 1024), F32)],
        scratch_shapes=[pltpu.VMEM((SSD_STATE, 1024), F32)],
        compiler_params=pltpu.CompilerParams(dimension_semantics=("arbitrary",), vmem_limit_bytes=VMEM_LIMIT),
    )(pre, proj, proj, states, dyab, dt_bias, a_log, d_exp, norm_g)


def _attn_tile(T):
    return min(512, T // 2)


def _causal(tq):
    return _iota((tq, tq), 1) <= _iota((tq, tq), 0)


def attn_fwd(q, kv, kr, *, name):
    T = q.shape[0]
    tq = _attn_tile(T)
    nq = T // tq

    def body(q_ref, kn_ref, v_ref, kr_ref, o_ref, lse_ref):
        qi = pl.program_id(1)
        qv = q_ref[...]

        def blk(ki, masked, carry):
            m, l, acc = carry
            off = pl.multiple_of(ki * tq, tq)
            k = jnp.concatenate([kn_ref[pl.ds(off, tq), :], kr_ref[pl.ds(off, tq), :]], axis=1)
            s = lax.dot_general(qv, k, NT, preferred_element_type=F32)
            if masked:
                s = jnp.where(_causal(tq), s, -1e30)
            m_new = jnp.maximum(m, jnp.max(s, axis=-1, keepdims=True))
            p = jnp.exp(s - m_new)
            alpha = jnp.exp(m - m_new)
            l = alpha * l + jnp.sum(p, axis=-1, keepdims=True)
            acc = alpha * acc + jnp.dot(p.astype(BF16), v_ref[pl.ds(off, tq), :], preferred_element_type=F32)
            return m_new, l, acc

        init = (jnp.full((tq, 1), -1e30, F32), jnp.zeros((tq, 1), F32), jnp.zeros((tq, 128), F32))
        carry = lax.fori_loop(0, qi, lambda ki, c: blk(ki, False, c), init)
        m, l, acc = blk(qi, True, carry)
        o_ref[...] = (acc / l).astype(o_ref.dtype)
        lse_ref[...] = jnp.broadcast_to(m + jnp.log(l), (tq, 128))

    return pl.pallas_call(
        body, name=name, grid=(MLA_HEADS, nq),
        in_specs=[pl.BlockSpec((tq, MLA_QPAD), lambda h, i: (i, h)),
                  pl.BlockSpec((T, 128), lambda h, i: (0, h)),
                  pl.BlockSpec((T, 128), lambda h, i: (0, MLA_HEADS + h)),
                  pl.BlockSpec((T, 128), lambda h, i: (0, 0))],
        out_specs=[pl.BlockSpec((tq, 128), lambda h, i: (i, h)), pl.BlockSpec((tq, 128), lambda h, i: (i, h))],
        out_shape=[SDS((T, 2048), BF16), SDS((T, 2048), F32)],
        compiler_params=pltpu.CompilerParams(dimension_semantics=("parallel", "arbitrary"),
                                             vmem_limit_bytes=VMEM_LIMIT),
    )(q, kv, kv, kr)


def attn_bwd_dq(q, kv, kr, do, lse, dl, *, name):
    T = q.shape[0]
    tq = _attn_tile(T)
    nq = T // tq

    def body(q_ref, do_ref, lse_ref, dl_ref, kn_ref, v_ref, kr_ref, dq_ref):
        qi = pl.program_id(1)
        qv, dov = q_ref[...], do_ref[...]
        lse, dl = lse_ref[:, 0:1], dl_ref[:, 0:1]

        def blk(ki, masked, dq):
            off = pl.multiple_of(ki * tq, tq)
            k = jnp.concatenate([kn_ref[pl.ds(off, tq), :], kr_ref[pl.ds(off, tq), :]], axis=1)
            s = lax.dot_general(qv, k, NT, preferred_element_type=F32)
            if masked:
                s = jnp.where(_causal(tq), s, -1e30)
            p = jnp.exp(s - lse)
            dp = lax.dot_general(dov, v_ref[pl.ds(off, tq), :], NT, preferred_element_type=F32)
            ds = p * (dp - dl)
            return dq + jnp.dot(ds.astype(BF16), k, preferred_element_type=F32)

        dq = lax.fori_loop(0, qi, lambda ki, c: blk(ki, False, c), jnp.zeros((tq, MLA_QPAD), F32))
        dq_ref[...] = blk(qi, True, dq)

    return pl.pallas_call(
        body, name=name, grid=(MLA_HEADS, nq),
        in_specs=[pl.BlockSpec((tq, MLA_QPAD), lambda h, i: (i, h)),
                  pl.BlockSpec((tq, 128), lambda h, i: (i, h)),
                  pl.BlockSpec((tq, 128), lambda h, i: (i, h)),
                  pl.BlockSpec((tq, 128), lambda h, i: (i, h)),
                  pl.BlockSpec((T, 128), lambda h, i: (0, h)),
                  pl.BlockSpec((T, 128), lambda h, i: (0, MLA_HEADS + h)),
                  pl.BlockSpec((T, 128), lambda h, i: (0, 0))],
        out_specs=pl.BlockSpec((tq, MLA_QPAD), lambda h, i: (i, h)),
        out_shape=SDS((T, MLA_HEADS * MLA_QPAD), F32),
        compiler_params=pltpu.CompilerParams(dimension_semantics=("parallel", "arbitrary"),
                                             vmem_limit_bytes=VMEM_LIMIT),
    )(q, do, lse, dl, kv, kv, kr)


def attn_bwd_dkv(q, kv, kr, do, lse, dl, *, name):
    T = q.shape[0]
    tq = _attn_tile(T)
    nq = T // tq

    def body(q_ref, do_ref, lse_ref, dl_ref, kn_ref, v_ref, kr_ref, dkn_ref, dv_ref, dkr_ref):
        ki = pl.program_id(1)
        k = jnp.concatenate([kn_ref[...], kr_ref[...]], axis=1)
        v = v_ref[...]

        def blk(qi, masked, carry):
            dk, dv = carry
            off = pl.multiple_of(qi * tq, tq)
            qv, dov = q_ref[pl.ds(off, tq), :], do_ref[pl.ds(off, tq), :]
            lse, dl = lse_ref[pl.ds(off, tq), 0:1], dl_ref[pl.ds(off, tq), 0:1]
            s = lax.dot_general(qv, k, NT, preferred_element_type=F32)
            if masked:
                s = jnp.where(_causal(tq), s, -1e30)
            p = jnp.exp(s - lse)
            dv = dv + lax.dot_general(p.astype(BF16), dov, TN, preferred_element_type=F32)
            dp = lax.dot_general(dov, v, NT, preferred_element_type=F32)
            ds = p * (dp - dl)
            dk = dk + lax.dot_general(ds.astype(BF16), qv, TN, preferred_element_type=F32)
            return dk, dv

        carry = blk(ki, True, (jnp.zeros((tq, MLA_QPAD), F32), jnp.zeros((tq, 128), F32)))
        dk, dv = lax.fori_loop(ki + 1, nq, lambda qi, c: blk(qi, False, c), carry)
        dkn_ref[...] = dk[:, :128].astype(dkn_ref.dtype)
        dkr_ref[...] = dk[:, 128:]
        dv_ref[...] = dv.astype(dv_ref.dtype)

    return pl.pallas_call(
        body, name=name, grid=(MLA_HEADS, nq),
        in_specs=[pl.BlockSpec((T, MLA_QPAD), lambda h, i: (0, h)),
                  pl.BlockSpec((T, 128), lambda h, i: (0, h)),
                  pl.BlockSpec((T, 128), lambda h, i: (0, h)),
                  pl.BlockSpec((T, 128), lambda h, i: (0, h)),
                  pl.BlockSpec((tq, 128), lambda h, i: (i, h)),
                  pl.BlockSpec((tq, 128), lambda h, i: (i, MLA_HEADS + h)),
                  pl.BlockSpec((tq, 128), lambda h, i: (i, 0))],
        out_specs=[pl.BlockSpec((tq, 128), lambda h, i: (i, h)), pl.BlockSpec((tq, 128), lambda h, i: (i, h)),
                   pl.BlockSpec((tq, 128), lambda h, i: (i, h))],
        out_shape=[SDS((T, 2048), BF16), SDS((T, 2048), BF16), SDS((T, 2048), F32)],
        compiler_params=pltpu.CompilerParams(dimension_semantics=("parallel", "arbitrary"),
                                             vmem_limit_bytes=VMEM_LIMIT),
    )(q, do, lse, dl, kv, kv, kr)


def _ffn_fwd(h, pre_g, w_in, w_down, post_g, tag):
    hn = rms_fwd(h, pre_g, name=f"{tag}_pre")
    gu = mm(hn, w_in, out_dtype=BF16, name=f"{tag}_in", tn=512)
    a = swiglu_fwd(gu, name=f"{tag}_act")
    f = mm(a, w_down, name=f"{tag}_down", tk=1408)
    h2 = post_fwd(h, f, post_g, 0.5, name=f"{tag}_post")
    return h2, (h, hn, gu, a, f)


def _ffn_bwd(dh2, saved, pre_g, w_in, w_down, post_g, tag):
    h, hn, gu, a, f = saved
    df, dpost = post_bwd(f, dh2, post_g, 0.5, name=f"{tag}_post_b")
    da = mm(df, w_down, tb=True, name=f"{tag}_down_bx", tn=1408)
    dw_down = mm(a, df, ta=True, name=f"{tag}_down_bw", tm=1408)
    dgu = swiglu_bwd(gu, da, name=f"{tag}_act_b")
    dhn = mm(dgu, w_in, tb=True, name=f"{tag}_in_bx", tk=1408)
    dw_in = mm(hn, dgu, ta=True, name=f"{tag}_in_bw", tn=512)
    dh, dpre = rms_bwd(h, dhn, dh2, pre_g, name=f"{tag}_pre_b")
    return dh, dpre, dw_in, dw_down, dpost


def _hyb_fwd(hn, w, tag):
    proj = mm(hn, w["hyb_in"], name=f"{tag}_in", tn=512)
    ya = gmlp_fwd(proj, w["ln_g"], w["ln_b"], w["w_s"], w["b_st"], name=f"{tag}_gmlp")
    pre = conv_fwd(proj, w["conv_w"], w["conv_b"], name=f"{tag}_conv")
    yb, states = ssd_fwd(pre, proj, w["dt_bias"], w["a_log"], w["d_exp"], w["norm_g"], name=f"{tag}_ssd")
    yab = jnp.concatenate([ya, yb], axis=1)
    mixed = mm(yab, w["hyb_out"], name=f"{tag}_out")
    return mixed, (proj, pre, states, yab)


def _hyb_bwd(dmixed, hn, saved, w, tag):
    proj, pre, states, yab = saved
    dyab = mm(dmixed, w["hyb_out"], tb=True, name=f"{tag}_out_bx")
    dw_out = mm(yab, dmixed, ta=True, name=f"{tag}_out_bw")
    duv, dln_g, dln_b, dw_s, db_st = gmlp_bwd(proj, dyab, w["ln_g"], w["ln_b"], w["w_s"], w["b_st"],
                                              name=f"{tag}_gmlp_b")
    dpre, ddt, dz, ddt_bias, da_log, dd_exp, dnorm_g = ssd_bwd(
        pre, proj, states, dyab, w["dt_bias"], w["a_log"], w["d_exp"], w["norm_g"], name=f"{tag}_ssd_b")
    dxbc, dconv_w, dconv_b = conv_bwd(dpre, proj, w["conv_w"], name=f"{tag}_conv_b")
    pad = jnp.zeros((duv.shape[0], HYB_PAD - HYB_MAIN - LANES), BF16)
    dproj = jnp.concatenate([duv, dz, dxbc, ddt, pad], axis=1)
    dhn = mm(dproj, w["hyb_in"], tb=True, name=f"{tag}_in_bx")
    dw_in = mm(hn, dproj, ta=True, name=f"{tag}_in_bw", tn=512)
    g = dict(hyb_in=dw_in, hyb_out=dw_out, ln_g=dln_g, ln_b=dln_b, w_s=dw_s, b_st=db_st, conv_w=dconv_w,
             conv_b=dconv_b, dt_bias=ddt_bias, a_log=da_log, d_exp=dd_exp, norm_g=dnorm_g)
    return dhn, g


def _mla_fwd(hn, w, rope, tag):
    cs, sn, c256, s256 = rope
    c_all = mm(hn, w["mla_in"], name=f"{tag}_in")
    cqn, ckvn, kr = kprep_fwd(c_all, cs, sn, w["q_g"], w["kv_g"], name=f"{tag}_kprep")
    qb = mm(cqn, w["uq"], name=f"{tag}_uq")
    q = qrope_fwd(qb, c256, s256, name=f"{tag}_qrope")
    kv = mm(ckvn, w["ukv"], out_dtype=BF16, name=f"{tag}_ukv")
    o, lse = attn_fwd(q, kv, kr, name=f"{tag}_attn")
    mixed = mm(o, w["mla_out"], name=f"{tag}_out")
    return mixed, (c_all, cqn, ckvn, kr, q, kv, o, lse)


def _mla_bwd(dmixed, hn, saved, w, rope, tag):
    cs, sn, c256, s256 = rope
    c_all, cqn, ckvn, kr, q, kv, o, lse = saved
    do = mm(dmixed, w["mla_out"], tb=True, out_dtype=BF16, name=f"{tag}_out_bx")
    dw_out = mm(o, dmixed, ta=True, name=f"{tag}_out_bw")
    dl = delta_fwd(do, o, name=f"{tag}_delta")
    dq = attn_bwd_dq(q, kv, kr, do, lse, dl, name=f"{tag}_attn_bq")
    dkn, dv, dkr_h = attn_bwd_dkv(q, kv, kr, do, lse, dl, name=f"{tag}_attn_bkv")
    dkr = headsum(dkr_h, name=f"{tag}_dkr")
    dkv = jnp.concatenate([dkn, dv], axis=1)
    dckvn = mm(dkv, w["ukv"], tb=True, name=f"{tag}_ukv_bx")
    dw_ukv = mm(ckvn, dkv, ta=True, name=f"{tag}_ukv_bw")
    dqb = qrope_bwd(dq, c256, s256, name=f"{tag}_qrope_b")
    dcqn = mm(dqb, w["uq"], tb=True, name=f"{tag}_uq_bx")
    dw_uq = mm(cqn, dqb, ta=True, name=f"{tag}_uq_bw")
    dc_all, dq_g, dkv_g = kprep_bwd(c_all, cs, sn, dcqn, dckvn, dkr, w["q_g"], w["kv_g"], name=f"{tag}_kprep_b")
    dhn = mm(dc_all, w["mla_in"], tb=True, name=f"{tag}_in_bx")
    dw_in = mm(hn, dc_all, ta=True, name=f"{tag}_in_bw")
    g = dict(mla_in=dw_in, mla_out=dw_out, uq=dw_uq, ukv=dw_ukv, q_g=dq_g, kv_g=dkv_g)
    return dhn, g


def local_step(x, p, rope, target, lw):
    h = x
    saved = []
    for i in range(DEPTH):
        w = lw[i]
        t = f"l{i}"
        h, s1 = _ffn_fwd(h, w["ffn1_pre_g"], w["ffn1_w_in"], w["ffn1_w_down"], w["ffn1_post_g"], f"{t}_f1")
        h1 = h
        hn = rms_fwd(h1, w["mix_pre_g"], name=f"{t}_mixpre")
        if i % 2 == 0:
            mixed, sm = _hyb_fwd(hn, w, f"{t}_hyb")
        else:
            mixed, sm = _mla_fwd(hn, w, rope, f"{t}_mla")
        h = post_fwd(h1, mixed, w["mix_post_g"], 1.0, name=f"{t}_mixpost")
        h, s2 = _ffn_fwd(h, w["ffn2_pre_g"], w["ffn2_w_in"], w["ffn2_w_down"], w["ffn2_post_g"], f"{t}_f2")
        h3 = h
        hn3 = rms_fwd(h3, w["ple_pre_g"], name=f"{t}_plepre")
        gl = mm(hn3, w["ple_w_gate"], name=f"{t}_plegate")
        pp = mm(p[i], w["ple_w_proj"], name=f"{t}_pleproj")
        h = ple_fwd(h3, gl, pp, w["ple_post_g"], name=f"{t}_plepost")
        saved.append((s1, h1, hn, sm, mixed, s2, h3, hn3, gl, pp))

    dh, sq = loss_fwd(h, target, name="loss")
    grads = [None] * DEPTH
    for i in reversed(range(DEPTH)):
        w = lw[i]
        t = f"l{i}"
        s1, h1, hn, sm, mixed, s2, h3, hn3, gl, pp = saved[i]
        g = {}
        dgl, dpp, g["ple_post_g"] = ple_bwd(gl, pp, dh, w["ple_post_g"], name=f"{t}_plepost_b")
        dhn3 = mm(dgl, w["ple_w_gate"], tb=True, name=f"{t}_plegate_bx")
        g["ple_w_gate"] = mm(hn3, dgl, ta=True, name=f"{t}_plegate_bw")
        g["ple_w_proj"] = mm(p[i], dpp, ta=True, name=f"{t}_pleproj_bw")
        dh, g["ple_pre_g"] = rms_bwd(h3, dhn3, dh, w["ple_pre_g"], name=f"{t}_plepre_b")
        dh, g["ffn2_pre_g"], g["ffn2_w_in"], g["ffn2_w_down"], g["ffn2_post_g"] = _ffn_bwd(
            dh, s2, w["ffn2_pre_g"], w["ffn2_w_in"], w["ffn2_w_down"], w["ffn2_post_g"], f"{t}_f2")
        dmixed, g["mix_post_g"] = post_bwd(mixed, dh, w["mix_post_g"], 1.0, name=f"{t}_mixpost_b")
        if i % 2 == 0:
            dhn, gm = _hyb_bwd(dmixed, hn, sm, w, f"{t}_hyb")
        else:
            dhn, gm = _mla_bwd(dmixed, hn, sm, w, rope, f"{t}_mla")
        g.update(gm)
        dh, g["mix_pre_g"] = rms_bwd(h1, dhn, dh, w["mix_pre_g"], name=f"{t}_mixpre_b")
        dh, g["ffn1_pre_g"], g["ffn1_w_in"], g["ffn1_w_down"], g["ffn1_post_g"] = _ffn_bwd(
            dh, s1, w["ffn1_pre_g"], w["ffn1_w_in"], w["ffn1_w_down"], w["ffn1_post_g"], f"{t}_f1")
        grads[i] = g
    return sq, dh, grads


def _zeros_like_cols(a, n):
    return jnp.zeros(a.shape[:-1] + (n,), a.dtype)


def layer_weights(full, i):
    j = i // 2
    row = lambda v: v.reshape(1, -1)
    w = {k: row(full[k][i]) for k in ("ffn1_pre_g", "ffn1_post_g", "mix_pre_g", "mix_post_g", "ffn2_pre_g",
                                      "ffn2_post_g", "ple_pre_g", "ple_post_g")}
    for k in ("ffn1_w_in", "ffn1_w_down", "ffn2_w_in", "ffn2_w_down", "ple_w_gate", "ple_w_proj"):
        w[k] = full[k][i]
    if i % 2 == 0:
        hw = full["hyb_w_in"][j]
        w["hyb_in"] = jnp.concatenate([hw, _zeros_like_cols(hw, HYB_PAD - HYB_IN)], axis=1)
        w["hyb_out"] = full["hyb_w_out"][j]
        w["ln_g"], w["ln_b"] = row(full["gm_ln_g"][j]), row(full["gm_ln_b"][j])
        w["w_s"] = full["gm_w_s"][j].reshape(GM_HEADS * CHUNK, CHUNK)
        w["b_st"] = jnp.pad(full["gm_b_s"][j].T, ((0, 0), (0, LANES - GM_HEADS)))
        w["conv_w"], w["conv_b"] = full["ssd_conv_w"][j], row(full["ssd_conv_b"][j])
        pad16 = lambda v: jnp.pad(v.reshape(1, -1), ((0, 0), (0, LANES - SSD_HEADS)))
        w["dt_bias"], w["a_log"] = pad16(full["ssd_dt_bias"][j]), pad16(full["ssd_a_log"][j])
        w["d_exp"] = row(jnp.repeat(full["ssd_d"][j], SSD_HEAD_DIM))
        w["norm_g"] = row(full["ssd_norm_g"][j])
    else:
        wi = full["mla_w_in"][j]
        z64 = _zeros_like_cols(wi, 64)
        w["mla_in"] = jnp.concatenate([wi[:, :384], wi[:, 384:448], z64, -wi[:, 416:448], wi[:, 384:416], z64], axis=1)
        uq = full["mla_w_uq"][j].reshape(MLA_Q_LORA, MLA_HEADS, MLA_QK)
        zq = jnp.zeros((MLA_Q_LORA, MLA_HEADS, 64), uq.dtype)
        pad_part = jnp.concatenate([uq, zq], axis=2)
        swp_part = jnp.concatenate([jnp.zeros_like(uq[:, :, :128]), -uq[:, :, 160:192], uq[:, :, 128:160], zq], axis=2)
        w["uq"] = jnp.concatenate([pad_part.reshape(MLA_Q_LORA, -1), swp_part.reshape(MLA_Q_LORA, -1)], axis=1)
        ukv = full["mla_w_ukv"][j].reshape(MLA_KV_LORA, MLA_HEADS, 256)
        w["ukv"] = jnp.concatenate([ukv[:, :, :128].reshape(MLA_KV_LORA, -1), ukv[:, :, 128:].reshape(MLA_KV_LORA, -1)],
                                   axis=1)
        w["mla_out"] = full["mla_w_out"][j]
        w["q_g"], w["kv_g"] = row(full["mla_q_norm_g"][j]), row(full["mla_kv_norm_g"][j])
    return w


def full_grads(grads):
    out = {}
    stack = lambda k, idx: jnp.stack([grads[i][k] for i in idx])
    every, even, odd = range(DEPTH), range(0, DEPTH, 2), range(1, DEPTH, 2)
    for k in ("ffn1_pre_g", "ffn1_post_g", "mix_pre_g", "mix_post_g", "ffn2_pre_g", "ffn2_post_g", "ple_pre_g",
              "ple_post_g"):
        out[k] = stack(k, every).reshape(DEPTH, D_MODEL)
    for k in ("ffn1_w_in", "ffn1_w_down", "ffn2_w_in", "ffn2_w_down", "ple_w_gate", "ple_w_proj"):
        out[k] = stack(k, every)
    out["hyb_w_in"] = stack("hyb_in", even)[:, :, :HYB_IN]
    out["hyb_w_out"] = stack("hyb_out", even)
    out["gm_ln_g"] = stack("ln_g", even).reshape(2, 1024)
    out["gm_ln_b"] = stack("ln_b", even).reshape(2, 1024)
    out["gm_w_s"] = stack("w_s", even).reshape(2, GM_HEADS, CHUNK, CHUNK)
    out["gm_b_s"] = jnp.swapaxes(stack("b_st", even)[:, :, :GM_HEADS], 1, 2)
    out["ssd_conv_w"] = stack("conv_w", even)
    out["ssd_conv_b"] = stack("conv_b", even).reshape(2, SSD_CONV_CH)
    out["ssd_dt_bias"] = stack("dt_bias", even)[:, 0, :SSD_HEADS]
    out["ssd_a_log"] = stack("a_log", even)[:, 0, :SSD_HEADS]
    out["ssd_d"] = stack("d_exp", even).reshape(2, SSD_HEADS, SSD_HEAD_DIM).sum(axis=-1)
    out["ssd_norm_g"] = stack("norm_g", even).reshape(2, 1024)
    dwi = stack("mla_in", odd)
    out["mla_w_in"] = jnp.concatenate([dwi[:, :, :384], dwi[:, :, 384:416] + dwi[:, :, 544:576],
                                       dwi[:, :, 416:448] - dwi[:, :, 512:544]], axis=2)
    duq = stack("uq", odd)
    half = MLA_HEADS * MLA_QPAD
    dp = duq[:, :, :half].reshape(2, MLA_Q_LORA, MLA_HEADS, MLA_QPAD)
    ds = duq[:, :, half:].reshape(2, MLA_Q_LORA, MLA_HEADS, MLA_QPAD)
    out["mla_w_uq"] = jnp.concatenate([dp[..., :128], dp[..., 128:160] + ds[..., 160:192],
                                       dp[..., 160:192] - ds[..., 128:160]], axis=-1).reshape(2, MLA_Q_LORA, -1)
    dukv = stack("ukv", odd)
    dk = dukv[:, :, :2048].reshape(2, MLA_KV_LORA, MLA_HEADS, 128)
    dv = dukv[:, :, 2048:].reshape(2, MLA_KV_LORA, MLA_HEADS, 128)
    out["mla_w_ukv"] = jnp.concatenate([dk, dv], axis=-1).reshape(2, MLA_KV_LORA, -1)
    out["mla_w_out"] = stack("mla_out", odd)
    out["mla_q_norm_g"] = stack("q_g", odd).reshape(2, MLA_Q_LORA)
    out["mla_kv_norm_g"] = stack("kv_g", odd).reshape(2, MLA_KV_LORA)
    return out


def rope_tables(positions):
    T = positions.shape[0]
    inv = 1.0 / (ROPE_BASE ** (jnp.arange(0, MLA_ROPE, 2, dtype=F32) / MLA_ROPE))
    ang = positions.astype(F32)[:, None] * inv
    cos, sin = jnp.cos(ang), jnp.sin(ang)
    z64 = jnp.zeros((T, 64), F32)
    cs = jnp.concatenate([cos, cos, z64], axis=1)
    sn = jnp.concatenate([sin, sin, z64], axis=1)
    c256 = jnp.concatenate([jnp.ones((T, 128), F32), cs], axis=1)
    s256 = jnp.concatenate([jnp.zeros((T, 128), F32), sn], axis=1)
    return cs, sn, c256, s256


def _rows(n):
    return -(-n // LANES)


def _pack(pieces, dtype, row_multiple):
    flat = []
    total = 0
    for a in pieces:
        v = a.reshape(-1).astype(dtype)
        padn = _rows(v.shape[0]) * LANES - v.shape[0]
        if padn:
            v = jnp.concatenate([v, jnp.zeros((padn,), dtype)])
        flat.append(v)
        total += v.shape[0] // LANES
    tail = -total % row_multiple
    if tail:
        flat.append(jnp.zeros((tail * LANES,), dtype))
    return jnp.concatenate(flat).reshape(-1, LANES)


def _unpack(slab, shapes):
    out = []
    r = 0
    for s in shapes:
        n = int(np.prod(s))
        nr = _rows(n)
        out.append(slab[r:r + nr].reshape(-1)[:n].reshape(s))
        r += nr
    return out


def _shard_shape(shape, ax):
    if ax is None:
        return tuple(shape)
    s = list(shape)
    s[ax] //= N_CHIPS
    return tuple(s)


def _chip_slice(a, ax, k):
    if ax is None:
        return a
    n = a.shape[ax] // N_CHIPS
    return lax.slice_in_dim(a, k * n, (k + 1) * n, axis=ax)


def _plane_peers():
    x, y, c = lax.axis_index("x"), lax.axis_index("y"), lax.axis_index("c")
    return (x, y, c), [(1 - x, y, c), (x, 1 - y, c), (1 - x, 1 - y, c)]


ANY = pl.BlockSpec(memory_space=pl.ANY)


def plane_allgather(slab):
    R = slab.shape[0]

    def body(src, out, send_sems, recv_sems, local_sem):
        (x, y, c), peers = _plane_peers()
        me = 2 * x + y
        local = pltpu.make_async_copy(src, out.at[me], local_sem)
        local.start()
        copies = []
        for j, peer in enumerate(peers):
            cp = pltpu.make_async_remote_copy(src_ref=src, dst_ref=out.at[me], send_sem=send_sems.at[j],
                                              recv_sem=recv_sems.at[j], device_id=peer, device_id_type=MESH)
            cp.start()
            copies.append(cp)
        for cp in copies:
            cp.wait()
        local.wait()

    return pl.pallas_call(
        body, name="plane_allgather", out_shape=SDS((N_CHIPS, R, LANES), slab.dtype),
        in_specs=[ANY], out_specs=ANY,
        scratch_shapes=[pltpu.SemaphoreType.DMA((3,)), pltpu.SemaphoreType.DMA((3,)), pltpu.SemaphoreType.DMA],
    )(slab)


def plane_alltoall(buf):
    R = buf.shape[1]

    def body(src, out, send_sems, recv_sems, local_sem):
        (x, y, c), peers = _plane_peers()
        me = 2 * x + y
        local = pltpu.make_async_copy(src.at[me], out.at[me], local_sem)
        local.start()
        copies = []
        for j, peer in enumerate(peers):
            cp = pltpu.make_async_remote_copy(src_ref=src.at[2 * peer[0] + peer[1]], dst_ref=out.at[me],
                                              send_sem=send_sems.at[j], recv_sem=recv_sems.at[j], device_id=peer,
                                              device_id_type=MESH)
            cp.start()
            copies.append(cp)
        for cp in copies:
            cp.wait()
        local.wait()

    return pl.pallas_call(
        body, name="plane_alltoall", out_shape=SDS((N_CHIPS, R, LANES), buf.dtype),
        in_specs=[ANY], out_specs=ANY,
        scratch_shapes=[pltpu.SemaphoreType.DMA((3,)), pltpu.SemaphoreType.DMA((3,)), pltpu.SemaphoreType.DMA],
    )(buf)


def sibling_swap(buf):
    def body(src, out, send_sem, recv_sem):
        x, y, c = lax.axis_index("x"), lax.axis_index("y"), lax.axis_index("c")
        cp = pltpu.make_async_remote_copy(src_ref=src, dst_ref=out, send_sem=send_sem, recv_sem=recv_sem,
                                          device_id=(x, y, 1 - c), device_id_type=MESH)
        cp.start()
        cp.wait()

    return pl.pallas_call(
        body, name="sibling_swap", out_shape=SDS(buf.shape, buf.dtype), in_specs=[ANY], out_specs=ANY,
        scratch_shapes=[pltpu.SemaphoreType.DMA, pltpu.SemaphoreType.DMA],
    )(buf)


def plane_sum(r4):
    R = r4.shape[1]

    def body(r_ref, o_ref):
        acc = r_ref[0].astype(F32)
        for k in range(1, N_CHIPS):
            acc = acc + r_ref[k].astype(F32)
        o_ref[...] = acc

    return pl.pallas_call(
        body, name="plane_sum", grid=(R // PACK_ROWS,),
        in_specs=[pl.BlockSpec((N_CHIPS, PACK_ROWS, LANES), lambda i: (0, i, 0))],
        out_specs=pl.BlockSpec((PACK_ROWS, LANES), lambda i: (i, 0)), out_shape=SDS((R, LANES), F32),
        compiler_params=pltpu.CompilerParams(dimension_semantics=("parallel",)),
    )(r4)


def adamw(pa, pb, w, m, v):
    R = w.shape[0]

    def body(pa_r, pb_r, w_r, m_r, v_r, g_o, d_o, m_o, v_o):
        g = pa_r[...] + pb_r[...]
        mn = ADAM_B1 * m_r[...] + (1.0 - ADAM_B1) * g
        vn = ADAM_B2 * v_r[...] + (1.0 - ADAM_B2) * jnp.square(g)
        m_hat = mn / (1.0 - ADAM_B1 ** ADAM_STEP)
        v_hat = vn / (1.0 - ADAM_B2 ** ADAM_STEP)
        g_o[...] = g
        d_o[...] = -ADAM_LR * (m_hat / (jnp.sqrt(v_hat) + ADAM_EPS) + ADAM_WD * w_r[...])
        m_o[...] = mn
        v_o[...] = vn

    spec = pl.BlockSpec((PACK_ROWS, LANES), lambda i: (i, 0))
    return pl.pallas_call(
        body, name="adamw", grid=(R // PACK_ROWS,), in_specs=[spec] * 5, out_specs=[spec] * 4,
        out_shape=[SDS((R, LANES), F32)] * 4,
        compiler_params=pltpu.CompilerParams(dimension_semantics=("parallel",)),
    )(pa, pb, w, m, v)


def gather_weights(wl):
    pieces = []
    for name, shape, ax, as_bf16 in WEIGHTS:
        if ax is None:
            continue
        a = wl[name]
        pieces.append(a.astype(BF16) if as_bf16 else lax.bitcast_convert_type(a, BF16))
    slab = _pack(pieces, BF16, 16)
    got = plane_allgather(slab)
    shapes = [v.shape for v in pieces]
    per_chip = [_unpack(got[k], shapes) for k in range(N_CHIPS)]
    full = {}
    idx = 0
    for name, shape, ax, as_bf16 in WEIGHTS:
        if ax is None:
            full[name] = wl[name]
            continue
        parts = [per_chip[k][idx] for k in range(N_CHIPS)]
        if not as_bf16:
            parts = [lax.bitcast_convert_type(v, F32) for v in parts]
        full[name] = jnp.concatenate(parts, axis=ax)
        idx += 1
    return full


def kernel(x, p, positions, ffn1_pre_g, ffn1_w_in, ffn1_w_down, ffn1_post_g, mix_pre_g, mix_post_g, ffn2_pre_g, ffn2_w_in, ffn2_w_down, ffn2_post_g, ple_pre_g, ple_w_gate, ple_w_proj, ple_post_g, hyb_w_in, gm_ln_g, gm_ln_b, gm_w_s, gm_b_s, ssd_conv_w, ssd_conv_b, ssd_dt_bias, ssd_a_log, ssd_d, ssd_norm_g, hyb_w_out, mla_w_in, mla_q_norm_g, mla_kv_norm_g, mla_w_uq, mla_w_ukv, mla_w_out, loss_target, m_ffn1_pre_g, m_ffn1_w_in, m_ffn1_w_down, m_ffn1_post_g, m_mix_pre_g, m_mix_post_g, m_ffn2_pre_g, m_ffn2_w_in, m_ffn2_w_down, m_ffn2_post_g, m_ple_pre_g, m_ple_w_gate, m_ple_w_proj, m_ple_post_g, m_hyb_w_in, m_gm_ln_g, m_gm_ln_b, m_gm_w_s, m_gm_b_s, m_ssd_conv_w, m_ssd_conv_b, m_ssd_dt_bias, m_ssd_a_log, m_ssd_d, m_ssd_norm_g, m_hyb_w_out, m_mla_w_in, m_mla_q_norm_g, m_mla_kv_norm_g, m_mla_w_uq, m_mla_w_ukv, m_mla_w_out, v_ffn1_pre_g, v_ffn1_w_in, v_ffn1_w_down, v_ffn1_post_g, v_mix_pre_g, v_mix_post_g, v_ffn2_pre_g, v_ffn2_w_in, v_ffn2_w_down, v_ffn2_post_g, v_ple_pre_g, v_ple_w_gate, v_ple_w_proj, v_ple_post_g, v_hyb_w_in, v_gm_ln_g, v_gm_ln_b, v_gm_w_s, v_gm_b_s, v_ssd_conv_w, v_ssd_conv_b, v_ssd_dt_bias, v_ssd_a_log, v_ssd_d, v_ssd_norm_g, v_hyb_w_out, v_mla_w_in, v_mla_q_norm_g, v_mla_kv_norm_g, v_mla_w_uq, v_mla_w_ukv, v_mla_w_out):
    args = locals()
    wl = {n: args[n] for n in WNAMES}
    ml = {n: args["m_" + n] for n in WNAMES}
    vl = {n: args["v_" + n] for n in WNAMES}

    full = gather_weights(wl)
    lw = [layer_weights(full, i) for i in range(DEPTH)]
    rope = rope_tables(positions[0])
    sq, dx, grads = local_step(x[0], p[:, 0], rope, loss_target[0], lw)
    loss = lax.psum(0.5 * jnp.sum(sq) / D_MODEL, ("x", "y", "c"))

    fg = full_grads(grads)
    dest = [_pack([_chip_slice(fg[n], ax, k) for n, _, ax, _ in WEIGHTS], BF16, PACK_ROWS) for k in range(N_CHIPS)]
    got = plane_alltoall(jnp.stack(dest))
    mine = plane_sum(got)
    other = sibling_swap(mine)
    order = [wl[n] for n in WNAMES]
    g_s, d_s, m_s, v_s = adamw(mine, other, _pack(order, F32, PACK_ROWS), _pack([ml[n] for n in WNAMES], F32, PACK_ROWS),
                               _pack([vl[n] for n in WNAMES], F32, PACK_ROWS))
    shapes = [a.shape for a in order]
    return (loss, dx[None], *_unpack(g_s, shapes), *_unpack(d_s, shapes), *_unpack(m_s, shapes), *_unpack(v_s, shapes))
```

```python
import functools
import math

import jax
import jax.numpy as jnp
import numpy as np
from jax import lax
from jax.experimental import pallas as pl
from jax.experimental.pallas import tpu as pltpu

F32 = jnp.float32
BF16 = jnp.bfloat16
SDS = jax.ShapeDtypeStruct
MESH = pl.DeviceIdType.MESH
HIGHEST = lax.Precision.HIGHEST

D_MODEL = 1024
DEPTH = 4
D_FF = 2816
NORM_EPS = 1e-6
LN_EPS = 1e-5
GM_HEADS = 8
CHUNK = 128
SSD_HEADS = 16
SSD_HEAD_DIM = 64
SSD_INNER = 1024
SSD_STATE = 128
SSD_CONV = 4
SSD_CONV_CH = 1536
HYB_MAIN = 4608
HYB_IN = 4624
HYB_PAD = 5120
MLA_HEADS = 16
MLA_NOPE = 128
MLA_ROPE = 64
MLA_QK = 192
MLA_QPAD = 256
MLA_Q_LORA = 256
MLA_KV_LORA = 128
ROPE_BASE = 10000.0
ADAM_LR = 0.001
ADAM_B1 = 0.9
ADAM_B2 = 0.999
ADAM_EPS = 1e-08
ADAM_WD = 0.01
ADAM_STEP = 10

N_CHIPS = 4
LANES = 128
VMEM_LIMIT = 56 * 1024 * 1024
PACK_ROWS = 2048

WEIGHTS = [
    ("ffn1_pre_g", (4, 1024), None, False),
    ("ffn1_w_in", (4, 1024, 5632), 2, True),
    ("ffn1_w_down", (4, 2816, 1024), 1, True),
    ("ffn1_post_g", (4, 1024), None, False),
    ("mix_pre_g", (4, 1024), None, False),
    ("mix_post_g", (4, 1024), None, False),
    ("ffn2_pre_g", (4, 1024), None, False),
    ("ffn2_w_in", (4, 1024, 5632), 2, True),
    ("ffn2_w_down", (4, 2816, 1024), 1, True),
    ("ffn2_post_g", (4, 1024), None, False),
    ("ple_pre_g", (4, 1024), None, False),
    ("ple_w_gate", (4, 1024, 1024), 1, True),
    ("ple_w_proj", (4, 256, 1024), 2, True),
    ("ple_post_g", (4, 1024), None, False),
    ("hyb_w_in", (2, 1024, 4624), 2, True),
    ("gm_ln_g", (2, 1024), None, False),
    ("gm_ln_b", (2, 1024), None, False),
    ("gm_w_s", (2, 8, 128, 128), None, False),
    ("gm_b_s", (2, 8, 128), None, False),
    ("ssd_conv_w", (2, 4, 1536), 2, False),
    ("ssd_conv_b", (2, 1536), None, False),
    ("ssd_dt_bias", (2, 16), None, False),
    ("ssd_a_log", (2, 16), None, False),
    ("ssd_d", (2, 16), None, False),
    ("ssd_norm_g", (2, 1024), None, False),
    ("hyb_w_out", (2, 2048, 1024), 1, True),
    ("mla_w_in", (2, 1024, 448), 1, True),
    ("mla_q_norm_g", (2, 256), 1, False),
    ("mla_kv_norm_g", (2, 128), None, False),
    ("mla_w_uq", (2, 256, 3072), 2, True),
    ("mla_w_ukv", (2, 128, 4096), 2, True),
    ("mla_w_out", (2, 2048, 1024), 1, True),
]
WNAMES = [w[0] for w in WEIGHTS]
WSPEC = {w[0]: w for w in WEIGHTS}
REG = ["ffn1_w_in", "ffn1_w_down", "ffn2_w_in", "ffn2_w_down", "ple_w_gate", "ple_w_proj", "hyb_w_out", "mla_w_out"]
MISC = [n for n in WNAMES if n not in REG]


def _pick(dim, target):
    if dim <= target:
        return dim
    t = (target // LANES) * LANES
    while t >= LANES:
        if dim % t == 0:
            return t
        t -= LANES
    return dim


def mm(a, b, *, ta=False, tb=False, out_dtype=F32, name, tm=512, tn=1024, tk=1024):
    a, la = a if isinstance(a, tuple) else (a, None)
    b, lb = b if isinstance(b, tuple) else (b, None)
    if ta:
        K, M = a.shape[-2:]
    else:
        M, K = a.shape[-2:]
    if tb:
        N, K2 = b.shape[-2:]
    else:
        K2, N = b.shape[-2:]
    assert K == K2, (a.shape, b.shape, ta, tb)
    bm, bn, bk = _pick(M, tm), _pick(N, tn), _pick(K, tk)
    nk = K // bk

    def spec(shape, idx, layer):
        if layer is None:
            return pl.BlockSpec(shape, idx)
        return pl.BlockSpec((None,) + shape, lambda i, j, k: (layer,) + idx(i, j, k))

    a_spec = spec((bk, bm), lambda i, j, k: (k, i), la) if ta else spec((bm, bk), lambda i, j, k: (i, k), la)
    b_spec = spec((bn, bk), lambda i, j, k: (j, k), lb) if tb else spec((bk, bn), lambda i, j, k: (k, j), lb)
    dn = (((0 if ta else 1,), (1 if tb else 0,)), ((), ()))

    def body(a_ref, b_ref, o_ref, acc_ref):
        k = pl.program_id(2)

        @pl.when(k == 0)
        def _():
            acc_ref[...] = jnp.zeros_like(acc_ref)

        acc_ref[...] += lax.dot_general(a_ref[...].astype(BF16), b_ref[...].astype(BF16), dn,
                                        preferred_element_type=F32)

        @pl.when(k == nk - 1)
        def _():
            o_ref[...] = acc_ref[...].astype(o_ref.dtype)

    return pl.pallas_call(
        body, name=name, grid=(M // bm, N // bn, nk),
        in_specs=[a_spec, b_spec], out_specs=pl.BlockSpec((bm, bn), lambda i, j, k: (i, j)),
        out_shape=SDS((M, N), out_dtype), scratch_shapes=[pltpu.VMEM((bm, bn), F32)],
        compiler_params=pltpu.CompilerParams(dimension_semantics=("parallel", "parallel", "arbitrary"),
                                             vmem_limit_bytes=VMEM_LIMIT),
    )(a, b)


def row_call(fn, xs, ps, outs, accs=(), *, tb, name, reverse=False):
    xs = [x if isinstance(x, tuple) else (x, x.shape[1], 0) for x in xs]
    T = xs[0][0].shape[0]
    tb = min(tb, T)
    n = T // tb
    assert n * tb == T
    nx, npar, no, na = len(xs), len(ps), len(outs), len(accs)

    def ridx(i):
        return n - 1 - i if reverse else i

    in_specs = [pl.BlockSpec((tb, w), functools.partial(lambda i, cb: (ridx(i), cb), cb=cb)) for (_, w, cb) in xs]
    in_specs += [pl.BlockSpec(p.shape, functools.partial(lambda i, nd: (0,) * nd, nd=p.ndim)) for p in ps]
    out_specs = [pl.BlockSpec((tb, c), lambda i: (ridx(i), 0)) for (c, _) in outs]
    out_specs += [pl.BlockSpec(s, functools.partial(lambda i, nd: (0,) * nd, nd=len(s))) for s in accs]
    out_shape = [SDS((T, c), dt) for (c, dt) in outs] + [SDS(s, F32) for s in accs]

    def body(*refs):
        xr, pr = refs[:nx], refs[nx:nx + npar]
        orf, ar = refs[nx + npar:nx + npar + no], refs[nx + npar + no:]
        res = fn(*[r[...] for r in xr], *[r[...] for r in pr])
        for r, v in zip(orf, res[:no]):
            r[...] = v.astype(r.dtype)
        if na:
            @pl.when(pl.program_id(0) == 0)
            def _():
                for r in ar:
                    r[...] = jnp.zeros_like(r)

            for r, v in zip(ar, res[no:]):
                r[...] += v.astype(F32)

    res = pl.pallas_call(
        body, name=name, grid=(n,), in_specs=in_specs, out_specs=out_specs, out_shape=out_shape,
        compiler_params=pltpu.CompilerParams(dimension_semantics=("arbitrary",), vmem_limit_bytes=VMEM_LIMIT),
    )(*[x[0] for x in xs], *ps)
    return res


def _f32(*a):
    return [v.astype(F32) for v in a]


def t_rms(x, g):
    return x * lax.rsqrt(jnp.mean(x * x, axis=-1, keepdims=True) + NORM_EPS) * g


def t_swiglu(gu):
    return jax.nn.silu(gu[:, :D_FF]) * gu[:, D_FF:]


def t_ple(gl, pp, g):
    return t_rms(jax.nn.sigmoid(gl) * pp, g)


def _iota(shape, d):
    return lax.broadcasted_iota(jnp.int32, shape, d)


def _bdot(a, b, dn=(((1,), (0,)), ((), ()))):
    return lax.dot_general(a.astype(BF16), b.astype(BF16), dn, preferred_element_type=F32)


def _hdot(a, b):
    return jnp.dot(a, b, precision=HIGHEST, preferred_element_type=F32)


NT = (((1,), (1,)), ((), ()))
TN = (((0,), (0,)), ((), ()))


def t_gmlp(uv, ln_g, ln_b, w_s, b_st):
    tb = uv.shape[0]
    guv = jax.nn.gelu(uv)
    u, v = guv[:, :1024], guv[:, 1024:]
    tri = _iota((CHUNK, CHUNK), 1) <= _iota((CHUNK, CHUNK), 0)
    rows = []
    for c in range(tb // CHUNK):
        vc = v[c * CHUNK:(c + 1) * CHUNK]
        heads = []
        for h in range(GM_HEADS):
            sl = slice(h * 128, (h + 1) * 128)
            vh = vc[:, sl]
            xc = vh - jnp.mean(vh, axis=-1, keepdims=True)
            var = jnp.mean(xc * xc, axis=-1, keepdims=True)
            y = xc * lax.rsqrt(var + LN_EPS) * ln_g[:, sl] + ln_b[:, sl]
            wm = jnp.where(tri, w_s[sl, :], 0.0)
            heads.append(_bdot(wm, y) + b_st[:, h:h + 1])
        rows.append(jnp.concatenate(heads, axis=1))
    mixed = rows[0] if len(rows) == 1 else jnp.concatenate(rows, axis=0)
    return u * mixed


def t_ssd(pre, dtr, z, st, dt_bias, a_log, d_exp, norm_g):
    L = CHUNK
    xbc = jax.nn.silu(pre)
    xs, bm, cm = xbc[:, :1024], xbc[:, 1024:1280], xbc[:, 1280:1536]
    valid = _iota((1, LANES), 1) < SSD_HEADS
    dt16 = jnp.where(valid, jax.nn.softplus(dtr + dt_bias), 0.0)
    a16 = jnp.where(valid, -jnp.exp(a_log), 0.0)
    da16 = dt16 * a16
    tri = _iota((L, L), 1) <= _iota((L, L), 0)
    acs16 = _hdot(tri.astype(F32), da16)
    hh, cc = _iota((LANES, 1024), 0), _iota((LANES, 1024), 1)
    expand = ((cc >= hh * SSD_HEAD_DIM) & (cc < (hh + 1) * SSD_HEAD_DIM)).astype(F32)
    acs = _hdot(acs16, expand)
    dte = _hdot(dt16, expand)
    alast = jnp.sum(jnp.where(_iota((L, 1024), 0) == L - 1, acs, 0.0), axis=0, keepdims=True)
    xd = xs * dte
    groups = [slice(0, 512), slice(512, 1024)]
    bg = [bm[:, :128], bm[:, 128:]]
    cg = [cm[:, :128], cm[:, 128:]]
    yoff = jnp.concatenate([_bdot(cg[g], st[:, groups[g]]) for g in range(2)], axis=1) * jnp.exp(acs)
    xdw = xd * jnp.exp(alast - acs)
    s_t = jnp.concatenate([_bdot(bg[g], xdw[:, groups[g]], TN) for g in range(2)], axis=1)
    st_new = st * jnp.exp(alast) + s_t
    cb = [_bdot(cg[g], bg[g], NT) for g in range(2)]
    acs16_t = acs16.T
    lo = _iota((1, LANES), 1) < SSD_HEAD_DIM
    slabs = []
    for j in range(SSD_HEADS // 2):
        g = j // 4
        xslab = xd[:, j * 128:(j + 1) * 128]
        acc = None
        for half in range(2):
            h = 2 * j + half
            seg = acs16[:, h:h + 1] - acs16_t[h:h + 1, :]
            mmat = cb[g] * jnp.exp(jnp.where(tri, seg, -1e30))
            xm = jnp.where(lo if half == 0 else jnp.logical_not(lo), xslab, 0.0)
            term = _bdot(mmat, xm)
            acc = term if acc is None else acc + term
        slabs.append(acc)
    y = jnp.concatenate(slabs, axis=1) + yoff + d_exp * xs
    yg = y * jax.nn.silu(z)
    outs = []
    for g in range(2):
        t = yg[:, groups[g]]
        outs.append(t * lax.rsqrt(jnp.mean(t * t, axis=-1, keepdims=True) + NORM_EPS) * norm_g[:, groups[g]])
    return jnp.concatenate(outs, axis=1), st_new


def t_kprep(c_all, cs, sn, qg, kvg):
    cqn = t_rms(c_all[:, :256], qg)
    ckvn = t_rms(c_all[:, 256:384], kvg)
    kr = c_all[:, 384:512] * cs + c_all[:, 512:640] * sn
    return cqn, ckvn, kr


def t_qrope(qb, c256, s256):
    scale = MLA_QK ** -0.5
    half = MLA_HEADS * MLA_QPAD
    outs = []
    for h in range(MLA_HEADS):
        a = qb[:, h * MLA_QPAD:(h + 1) * MLA_QPAD]
        b = qb[:, half + h * MLA_QPAD:half + (h + 1) * MLA_QPAD]
        outs.append((a * c256 + b * s256) * scale)
    return jnp.concatenate(outs, axis=1)


def rms_fwd(h, g, *, name, tb=512):
    def fn(h, g):
        return (t_rms(h.astype(F32), g),)
    return row_call(fn, [h], [g], [(h.shape[1], BF16)], tb=tb, name=name)[0]


def rms_bwd(h, dhn, dres, g, *, name, tb=256):
    def fn(h, dhn, dres, g):
        h, dhn, dres = _f32(h, dhn, dres)
        _, vjp = jax.vjp(t_rms, h, g)
        dh, dg = vjp(dhn)
        return dres + dh, dg
    return row_call(fn, [h, dhn, dres], [g], [(h.shape[1], F32)], [g.shape], tb=tb, name=name)


def post_fwd(h, f, g, scale, *, name, tb=512):
    def fn(h, f, g):
        return (h + scale * t_rms(f.astype(F32), g),)
    return row_call(fn, [h, f], [g], [(h.shape[1], F32)], tb=tb, name=name)[0]


def post_bwd(f, dout, g, scale, *, name, tb=256):
    def fn(f, dout, g):
        f, dout = _f32(f, dout)
        _, vjp = jax.vjp(lambda f, g: scale * t_rms(f, g), f, g)
        return vjp(dout)
    return row_call(fn, [f, dout], [g], [(f.shape[1], BF16)], [g.shape], tb=tb, name=name)


def swiglu_fwd(gu, *, name, tb=256):
    def fn(gu):
        return (t_swiglu(gu.astype(F32)),)
    return row_call(fn, [gu], [], [(D_FF, BF16)], tb=tb, name=name)[0]


def swiglu_bwd(gu, da, *, name, tb=256):
    def fn(gu, da):
        gu, da = _f32(gu, da)
        _, vjp = jax.vjp(t_swiglu, gu)
        return vjp(da)
    return row_call(fn, [gu, da], [], [(2 * D_FF, BF16)], tb=tb, name=name)[0]


def ple_fwd(h, gl, pp, g, *, name, tb=512):
    def fn(h, gl, pp, g):
        return (h + t_ple(gl, pp, g),)
    return row_call(fn, [h, gl, pp], [g], [(D_MODEL, F32)], tb=tb, name=name)[0]


def ple_bwd(gl, pp, dout, g, *, name, tb=256):
    def fn(gl, pp, dout, g):
        _, vjp = jax.vjp(t_ple, gl, pp, g)
        return vjp(dout)
    return row_call(fn, [gl, pp, dout], [g], [(D_MODEL, BF16), (D_MODEL, BF16)], [g.shape], tb=tb, name=name)


def gmlp_fwd(proj, ln_g, ln_b, w_s, b_st, *, name, tb=256):
    def fn(uv, ln_g, ln_b, w_s, b_st):
        return (t_gmlp(uv, ln_g, ln_b, w_s, b_st),)
    return row_call(fn, [(proj, 2048, 0)], [ln_g, ln_b, w_s, b_st], [(1024, BF16)], tb=tb, name=name)[0]


def gmlp_bwd(proj, dya, ln_g, ln_b, w_s, b_st, *, name, tb=128):
    def fn(uv, dya, ln_g, ln_b, w_s, b_st):
        _, vjp = jax.vjp(t_gmlp, uv, ln_g, ln_b, w_s, b_st)
        return vjp(dya.astype(F32))
    return row_call(fn, [(proj, 2048, 0), (dya, 1024, 0)], [ln_g, ln_b, w_s, b_st], [(2048, BF16)],
                    [ln_g.shape, ln_b.shape, w_s.shape, b_st.shape], tb=tb, name=name)


def kprep_fwd(c_all, cs, sn, qg, kvg, *, name, tb=512):
    return row_call(t_kprep, [c_all, cs, sn], [qg, kvg], [(256, BF16), (128, BF16), (128, BF16)], tb=tb, name=name)


def kprep_bwd(c_all, cs, sn, dcqn, dckvn, dkr, qg, kvg, *, name, tb=256):
    def fn(c_all, cs, sn, dcqn, dckvn, dkr, qg, kvg):
        dcqn, dckvn, dkr = _f32(dcqn, dckvn, dkr)
        _, vjp = jax.vjp(lambda c, qg, kvg: t_kprep(c, cs, sn, qg, kvg), c_all, qg, kvg)
        return vjp((dcqn, dckvn, dkr))
    return row_call(fn, [c_all, cs, sn, dcqn, dckvn, dkr], [qg, kvg], [(640, BF16)], [qg.shape, kvg.shape],
                    tb=tb, name=name)


def qrope_fwd(qb, c256, s256, *, name, tb=256):
    def fn(qb, c256, s256):
        return (t_qrope(qb, c256, s256),)
    return row_call(fn, [qb, c256, s256], [], [(MLA_HEADS * MLA_QPAD, BF16)], tb=tb, name=name)[0]


def qrope_bwd(dq, c256, s256, *, name, tb=256):
    def fn(dq, c256, s256):
        scale = MLA_QK ** -0.5
        a, b = [], []
        for h in range(MLA_HEADS):
            d = dq[:, h * MLA_QPAD:(h + 1) * MLA_QPAD] * scale
            a.append(d * c256)
            b.append(d * s256)
        return (jnp.concatenate(a + b, axis=1),)
    return row_call(fn, [dq, c256, s256], [], [(2 * MLA_HEADS * MLA_QPAD, BF16)], tb=tb, name=name)[0]


def delta_fwd(do, o, *, name, tb=512):
    def fn(do, o):
        do, o = _f32(do, o)
        outs = []
        for h in range(MLA_HEADS):
            sl = slice(h * 128, (h + 1) * 128)
            outs.append(jnp.broadcast_to(jnp.sum(do[:, sl] * o[:, sl], axis=-1, keepdims=True), (do.shape[0], 128)))
        return (jnp.concatenate(outs, axis=1),)
    return row_call(fn, [do, o], [], [(2048, F32)], tb=tb, name=name)[0]


def headsum(dkr_h, *, name, tb=512):
    def fn(d):
        acc = d[:, :128]
        for h in range(1, MLA_HEADS):
            acc = acc + d[:, h * 128:(h + 1) * 128]
        return (acc,)
    return row_call(fn, [dkr_h], [], [(128, F32)], tb=tb, name=name)[0]


def loss_fwd(y, t, *, name, tb=512):
    def fn(y, t):
        e = y - t
        return e * (1.0 / D_MODEL), jnp.sum(e * e, axis=0, keepdims=True)
    return row_call(fn, [y, t], [], [(D_MODEL, F32)], [(1, D_MODEL)], tb=tb, name=name)


def conv_fwd(proj, w, b, *, name, tb=256):
    T = proj.shape[0]
    n = T // tb
    hb = tb // CHUNK
    C = SSD_CONV_CH

    def body(cur, prev, w_ref, b_ref, o_ref, scr):
        i = pl.program_id(0)
        scr[pl.ds(0, CHUNK), :] = jnp.where(i > 0, prev[...], 0.0)
        scr[pl.ds(CHUNK, tb), :] = cur[...]
        y = b_ref[...] + w_ref[3:4, :] * cur[...]
        for k in range(SSD_CONV - 1):
            y = y + w_ref[k:k + 1, :] * scr[pl.ds(CHUNK - (SSD_CONV - 1) + k, tb), :]
        o_ref[...] = y

    return pl.pallas_call(
        body, name=name, grid=(n,),
        in_specs=[pl.BlockSpec((tb, C), lambda i: (i, 2)),
                  pl.BlockSpec((CHUNK, C), lambda i: (jnp.maximum(i * hb - 1, 0), 2)),
                  pl.BlockSpec((SSD_CONV, C), lambda i: (0, 0)), pl.BlockSpec((1, C), lambda i: (0, 0))],
        out_specs=pl.BlockSpec((tb, C), lambda i: (i, 0)), out_shape=SDS((T, C), F32),
        scratch_shapes=[pltpu.VMEM((CHUNK + tb, C), F32)],
        compiler_params=pltpu.CompilerParams(dimension_semantics=("arbitrary",), vmem_limit_bytes=VMEM_LIMIT),
    )(proj, proj, w, b)


def conv_bwd(dpre, proj, w, *, name, tb=256):
    T = proj.shape[0]
    n = T // tb
    hb = tb // CHUNK
    nh = T // CHUNK
    C = SSD_CONV_CH

    def body(dcur, dnext, xcur, xprev, w_ref, dx_ref, dw_ref, db_ref, dscr, xscr):
        i = pl.program_id(0)

        @pl.when(i == 0)
        def _():
            dw_ref[...] = jnp.zeros_like(dw_ref)
            db_ref[...] = jnp.zeros_like(db_ref)

        d = dcur[...]
        dscr[pl.ds(0, tb), :] = d
        dscr[pl.ds(tb, CHUNK), :] = jnp.where(i < n - 1, dnext[...], 0.0)
        xscr[pl.ds(0, CHUNK), :] = jnp.where(i > 0, xprev[...], 0.0)
        xscr[pl.ds(CHUNK, tb), :] = xcur[...]
        dx = w_ref[3:4, :] * d
        for k in range(SSD_CONV - 1):
            dx = dx + w_ref[k:k + 1, :] * dscr[pl.ds(SSD_CONV - 1 - k, tb), :]
        dx_ref[...] = dx.astype(dx_ref.dtype)
        for k in range(SSD_CONV):
            xk = xscr[pl.ds(CHUNK - (SSD_CONV - 1) + k, tb), :]
            dw_ref[k:k + 1, :] += jnp.sum(d * xk, axis=0, keepdims=True)
        db_ref[...] += jnp.sum(d, axis=0, keepdims=True)

    return pl.pallas_call(
        body, name=name, grid=(n,),
        in_specs=[pl.BlockSpec((tb, C), lambda i: (i, 0)),
                  pl.BlockSpec((CHUNK, C), lambda i: (jnp.minimum((i + 1) * hb, nh - 1), 0)),
                  pl.BlockSpec((tb, C), lambda i: (i, 2)),
                  pl.BlockSpec((CHUNK, C), lambda i: (jnp.maximum(i * hb - 1, 0), 2)),
                  pl.BlockSpec((SSD_CONV, C), lambda i: (0, 0))],
        out_specs=[pl.BlockSpec((tb, C), lambda i: (i, 0)), pl.BlockSpec((SSD_CONV, C), lambda i: (0, 0)),
                   pl.BlockSpec((1, C), lambda i: (0, 0))],
        out_shape=[SDS((T, C), BF16), SDS((SSD_CONV, C), F32), SDS((1, C), F32)],
        scratch_shapes=[pltpu.VMEM((tb + CHUNK, C), F32), pltpu.VMEM((CHUNK + tb, C), F32)],
        compiler_params=pltpu.CompilerParams(dimension_semantics=("arbitrary",), vmem_limit_bytes=VMEM_LIMIT),
    )(dpre, dpre, proj, proj, w)


def _ssd_specs(nc, rev):
    def r(c):
        return nc - 1 - c if rev else c
    pre = pl.BlockSpec((CHUNK, SSD_CONV_CH), lambda c: (r(c), 0))
    dtr = pl.BlockSpec((CHUNK, LANES), lambda c: (r(c), HYB_MAIN // LANES))
    z = pl.BlockSpec((CHUNK, 1024), lambda c: (r(c), 2))
    row = pl.BlockSpec((CHUNK, 1024), lambda c: (r(c), 0))
    return pre, dtr, z, row


def _pspec(shape):
    return pl.BlockSpec(shape, lambda c: (0,) * len(shape))


def ssd_fwd(pre, proj, dt_bias, a_log, d_exp, norm_g, *, name):
    T = pre.shape[0]
    nc = T // CHUNK
    s_pre, s_dt, s_z, s_row = _ssd_specs(nc, False)

    def body(pre_r, dt_r, z_r, b_r, a_r, d_r, g_r, y_r, sv_r, st):
        @pl.when(pl.program_id(0) == 0)
        def _():
            st[...] = jnp.zeros_like(st)

        s0 = st[...]
        sv_r[...] = s0
        y, s1 = t_ssd(pre_r[...], dt_r[...], z_r[...], s0, b_r[...], a_r[...], d_r[...], g_r[...])
        y_r[...] = y.astype(y_r.dtype)
        st[...] = s1

    return pl.pallas_call(
        body, name=name, grid=(nc,),
        in_specs=[s_pre, s_dt, s_z, _pspec((1, LANES)), _pspec((1, LANES)), _pspec((1, 1024)), _pspec((1, 1024))],
        out_specs=[s_row, s_row], out_shape=[SDS((T, 1024), BF16), SDS((T, 1024), F32)],
        scratch_shapes=[pltpu.VMEM((SSD_STATE, 1024), F32)],
        compiler_params=pltpu.CompilerParams(dimension_semantics=("arbitrary",), vmem_limit_bytes=VMEM_LIMIT),
    )(pre, proj, proj, dt_bias, a_log, d_exp, norm_g)


def ssd_bwd(pre, proj, states, dyab, dt_bias, a_log, d_exp, norm_g, *, name):
    T = pre.shape[0]
    nc = T // CHUNK
    s_pre, s_dt, s_z, s_row = _ssd_specs(nc, True)
    s_dtout = pl.BlockSpec((CHUNK, LANES), lambda c: (nc - 1 - c, 0))
    s_dy = pl.BlockSpec((CHUNK, 1024), lambda c: (nc - 1 - c, 1))

    def body(pre_r, dt_r, z_r, sv_r, dy_r, b_r, a_r, d_r, g_r, dpre_r, ddt_r, dz_r, db_r, da_r, dd_r, dg_r, dst):
        @pl.when(pl.program_id(0) == 0)
        def _():
            dst[...] = jnp.zeros_like(dst)
            for r in (db_r, da_r, dd_r, dg_r):
                r[...] = jnp.zeros_like(r)

        _, vjp = jax.vjp(t_ssd, pre_r[...], dt_r[...], z_r[...], sv_r[...], b_r[...], a_r[...], d_r[...], g_r[...])
        dpre, ddt, dz, ds0, db, da, dd, dg = vjp((dy_r[...].astype(F32), dst[...]))
        dpre_r[...] = dpre
        ddt_r[...] = ddt.astype(ddt_r.dtype)
        dz_r[...] = dz.astype(dz_r.dtype)
        dst[...] = ds0
        db_r[...] += db
        da_r[...] += da
        dd_r[...] += dd
        dg_r[...] += dg

    return pl.pallas_call(
        body, name=name, grid=(nc,),
        in_specs=[s_pre, s_dt, s_z, s_row, s_dy, _pspec((1, LANES)), _pspec((1, LANES)), _pspec((1, 1024)),
                  _pspec((1, 1024))],
        out_specs=[s_pre, s_dtout, s_row, _pspec((1, LANES)), _pspec((1, LANES)), _pspec((1, 1024)), _pspec((1, 1024))],
        out_shape=[SDS((T, SSD_CONV_CH), F32), SDS((T, LANES), BF16), SDS((T, 1024), BF16),
                   SDS((1, LANES), F32), SDS((1, LANES), F32), SDS((1, 1024), F32), SDS((1, 1024), F32)],
        scratch_shapes=[pltpu.VMEM((SSD_STATE, 1024), F32)],
        compiler_params=pltpu.CompilerParams(dimension_semantics=("arbitrary",), vmem_limit_bytes=VMEM_LIMIT),
    )(pre, proj, proj, states, dyab, dt_bias, a_log, d_exp, norm_g)


def _attn_tile(T):
    return min(512, T // 2)


def _causal(tq):
    return _iota((tq, tq), 1) <= _iota((tq, tq), 0)


def attn_fwd(q, kv, kr, *, name):
    T = q.shape[0]
    tq = _attn_tile(T)
    nq = T // tq

    def body(q_ref, kn_ref, v_ref, kr_ref, o_ref, lse_ref):
        qi = pl.program_id(1)
        qv = q_ref[...]

        def blk(ki, masked, carry):
            m, l, acc = carry
            off = pl.multiple_of(ki * tq, tq)
            k = jnp.concatenate([kn_ref[pl.ds(off, tq), :], kr_ref[pl.ds(off, tq), :]], axis=1)
            s = lax.dot_general(qv, k, NT, preferred_element_type=F32)
            if masked:
                s = jnp.where(_causal(tq), s, -1e30)
            m_new = jnp.maximum(m, jnp.max(s, axis=-1, keepdims=True))
            p = jnp.exp(s - m_new)
            alpha = jnp.exp(m - m_new)
            l = alpha * l + jnp.sum(p, axis=-1, keepdims=True)
            acc = alpha * acc + jnp.dot(p.astype(BF16), v_ref[pl.ds(off, tq), :], preferred_element_type=F32)
            return m_new, l, acc

        init = (jnp.full((tq, 1), -1e30, F32), jnp.zeros((tq, 1), F32), jnp.zeros((tq, 128), F32))
        carry = lax.fori_loop(0, qi, lambda ki, c: blk(ki, False, c), init)
        m, l, acc = blk(qi, True, carry)
        o_ref[...] = (acc / l).astype(o_ref.dtype)
        lse_ref[...] = jnp.broadcast_to(m + jnp.log(l), (tq, 128))

    return pl.pallas_call(
        body, name=name, grid=(MLA_HEADS, nq),
        in_specs=[pl.BlockSpec((tq, MLA_QPAD), lambda h, i: (i, h)),
                  pl.BlockSpec((T, 128), lambda h, i: (0, h)),
                  pl.BlockSpec((T, 128), lambda h, i: (0, MLA_HEADS + h)),
                  pl.BlockSpec((T, 128), lambda h, i: (0, 0))],
        out_specs=[pl.BlockSpec((tq, 128), lambda h, i: (i, h)), pl.BlockSpec((tq, 128), lambda h, i: (i, h))],
        out_shape=[SDS((T, 2048), BF16), SDS((T, 2048), F32)],
        compiler_params=pltpu.CompilerParams(dimension_semantics=("parallel", "arbitrary"),
                                             vmem_limit_bytes=VMEM_LIMIT),
    )(q, kv, kv, kr)


def attn_bwd_dq(q, kv, kr, do, lse, dl, *, name):
    T = q.shape[0]
    tq = _attn_tile(T)
    nq = T // tq

    def body(q_ref, do_ref, lse_ref, dl_ref, kn_ref, v_ref, kr_ref, dq_ref):
        qi = pl.program_id(1)
        qv, dov = q_ref[...], do_ref[...]
        lse, dl = lse_ref[:, 0:1], dl_ref[:, 0:1]

        def blk(ki, masked, dq):
            off = pl.multiple_of(ki * tq, tq)
            k = jnp.concatenate([kn_ref[pl.ds(off, tq), :], kr_ref[pl.ds(off, tq), :]], axis=1)
            s = lax.dot_general(qv, k, NT, preferred_element_type=F32)
            if masked:
                s = jnp.where(_causal(tq), s, -1e30)
            p = jnp.exp(s - lse)
            dp = lax.dot_general(dov, v_ref[pl.ds(off, tq), :], NT, preferred_element_type=F32)
            ds = p * (dp - dl)
            return dq + jnp.dot(ds.astype(BF16), k, preferred_element_type=F32)

        dq = lax.fori_loop(0, qi, lambda ki, c: blk(ki, False, c), jnp.zeros((tq, MLA_QPAD), F32))
        dq_ref[...] = blk(qi, True, dq)

    return pl.pallas_call(
        body, name=name, grid=(MLA_HEADS, nq),
        in_specs=[pl.BlockSpec((tq, MLA_QPAD), lambda h, i: (i, h)),
                  pl.BlockSpec((tq, 128), lambda h, i: (i, h)),
                  pl.BlockSpec((tq, 128), lambda h, i: (i, h)),
                  pl.BlockSpec((tq, 128), lambda h, i: (i, h)),
                  pl.BlockSpec((T, 128), lambda h, i: (0, h)),
                  pl.BlockSpec((T, 128), lambda h, i: (0, MLA_HEADS + h)),
                  pl.BlockSpec((T, 128), lambda h, i: (0, 0))],
        out_specs=pl.BlockSpec((tq, MLA_QPAD), lambda h, i: (i, h)),
        out_shape=SDS((T, MLA_HEADS * MLA_QPAD), F32),
        compiler_params=pltpu.CompilerParams(dimension_semantics=("parallel", "arbitrary"),
                                             vmem_limit_bytes=VMEM_LIMIT),
    )(q, do, lse, dl, kv, kv, kr)


def attn_bwd_dkv(q, kv, kr, do, lse, dl, *, name):
    T = q.shape[0]
    tq = _attn_tile(T)
    nq = T // tq

    def body(q_ref, do_ref, lse_ref, dl_ref, kn_ref, v_ref, kr_ref, dkn_ref, dv_ref, dkr_ref):
        ki = pl.program_id(1)
        k = jnp.concatenate([kn_ref[...], kr_ref[...]], axis=1)
        v = v_ref[...]

        def blk(qi, masked, carry):
            dk, dv = carry
            off = pl.multiple_of(qi * tq, tq)
            qv, dov = q_ref[pl.ds(off, tq), :], do_ref[pl.ds(off, tq), :]
            lse, dl = lse_ref[pl.ds(off, tq), 0:1], dl_ref[pl.ds(off, tq), 0:1]
            s = lax.dot_general(qv, k, NT, preferred_element_type=F32)
            if masked:
                s = jnp.where(_causal(tq), s, -1e30)
            p = jnp.exp(s - lse)
            dv = dv + lax.dot_general(p.astype(BF16), dov, TN, preferred_element_type=F32)
            dp = lax.dot_general(dov, v, NT, preferred_element_type=F32)
            ds = p * (dp - dl)
            dk = dk + lax.dot_general(ds.astype(BF16), qv, TN, preferred_element_type=F32)
            return dk, dv

        carry = blk(ki, True, (jnp.zeros((tq, MLA_QPAD), F32), jnp.zeros((tq, 128), F32)))
        dk, dv = lax.fori_loop(ki + 1, nq, lambda qi, c: blk(qi, False, c), carry)
        dkn_ref[...] = dk[:, :128].astype(dkn_ref.dtype)
        dkr_ref[...] = dk[:, 128:]
        dv_ref[...] = dv.astype(dv_ref.dtype)

    return pl.pallas_call(
        body, name=name, grid=(MLA_HEADS, nq),
        in_specs=[pl.BlockSpec((T, MLA_QPAD), lambda h, i: (0, h)),
                  pl.BlockSpec((T, 128), lambda h, i: (0, h)),
                  pl.BlockSpec((T, 128), lambda h, i: (0, h)),
                  pl.BlockSpec((T, 128), lambda h, i: (0, h)),
                  pl.BlockSpec((tq, 128), lambda h, i: (i, h)),
                  pl.BlockSpec((tq, 128), lambda h, i: (i, MLA_HEADS + h)),
                  pl.BlockSpec((tq, 128), lambda h, i: (i, 0))],
        out_specs=[pl.BlockSpec((tq, 128), lambda h, i: (i, h)), pl.BlockSpec((tq, 128), lambda h, i: (i, h)),
                   pl.BlockSpec((tq, 128), lambda h, i: (i, h))],
        out_shape=[SDS((T, 2048), BF16), SDS((T, 2048), BF16), SDS((T, 2048), F32)],
        compiler_params=pltpu.CompilerParams(dimension_semantics=("parallel", "arbitrary"),
                                             vmem_limit_bytes=VMEM_LIMIT),
    )(q, do, lse, dl, kv, kv, kr)


def _ffn_fwd(h, pre_g, w_in, w_down, post_g, tag):
    hn = rms_fwd(h, pre_g, name=f"{tag}_pre")
    gu = mm(hn, w_in, out_dtype=BF16, name=f"{tag}_in", tn=512)
    a = swiglu_fwd(gu, name=f"{tag}_act")
    f = mm(a, w_down, name=f"{tag}_down", tk=1408)
    h2 = post_fwd(h, f, post_g, 0.5, name=f"{tag}_post")
    return h2, (h, hn, gu, a, f)


def _ffn_bwd(dh2, saved, pre_g, w_in, w_down, post_g, tag):
    h, hn, gu, a, f = saved
    df, dpost = post_bwd(f, dh2, post_g, 0.5, name=f"{tag}_post_b")
    da = mm(df, w_down, tb=True, name=f"{tag}_down_bx", tn=1408)
    dw_down = mm(a, df, ta=True, out_dtype=BF16, name=f"{tag}_down_bw", tm=1408)
    dgu = swiglu_bwd(gu, da, name=f"{tag}_act_b")
    dhn = mm(dgu, w_in, tb=True, name=f"{tag}_in_bx", tk=1408)
    dw_in = mm(hn, dgu, ta=True, out_dtype=BF16, name=f"{tag}_in_bw", tn=512)
    dh, dpre = rms_bwd(h, dhn, dh2, pre_g, name=f"{tag}_pre_b")
    return dh, dpre, dw_in, dw_down, dpost


def _hyb_fwd(hn, w, tag):
    proj = mm(hn, w["hyb_in"], name=f"{tag}_in", tn=512)
    ya = gmlp_fwd(proj, w["ln_g"], w["ln_b"], w["w_s"], w["b_st"], name=f"{tag}_gmlp")
    pre = conv_fwd(proj, w["conv_w"], w["conv_b"], name=f"{tag}_conv")
    yb, states = ssd_fwd(pre, proj, w["dt_bias"], w["a_log"], w["d_exp"], w["norm_g"], name=f"{tag}_ssd")
    yab = jnp.concatenate([ya, yb], axis=1)
    mixed = mm(yab, w["hyb_out"], name=f"{tag}_out")
    return mixed, (proj, pre, states, yab)


def _hyb_bwd(dmixed, hn, saved, w, tag):
    proj, pre, states, yab = saved
    dyab = mm(dmixed, w["hyb_out"], tb=True, name=f"{tag}_out_bx")
    dw_out = mm(yab, dmixed, ta=True, out_dtype=BF16, name=f"{tag}_out_bw")
    duv, dln_g, dln_b, dw_s, db_st = gmlp_bwd(proj, dyab, w["ln_g"], w["ln_b"], w["w_s"], w["b_st"],
                                              name=f"{tag}_gmlp_b")
    dpre, ddt, dz, ddt_bias, da_log, dd_exp, dnorm_g = ssd_bwd(
        pre, proj, states, dyab, w["dt_bias"], w["a_log"], w["d_exp"], w["norm_g"], name=f"{tag}_ssd_b")
    dxbc, dconv_w, dconv_b = conv_bwd(dpre, proj, w["conv_w"], name=f"{tag}_conv_b")
    pad = jnp.zeros((duv.shape[0], HYB_PAD - HYB_MAIN - LANES), BF16)
    dproj = jnp.concatenate([duv, dz, dxbc, ddt, pad], axis=1)
    dhn = mm(dproj, w["hyb_in"], tb=True, name=f"{tag}_in_bx")
    dw_in = mm(hn, dproj, ta=True, name=f"{tag}_in_bw", tn=512)
    g = dict(hyb_in=dw_in, hyb_out=dw_out, ln_g=dln_g, ln_b=dln_b, w_s=dw_s, b_st=db_st, conv_w=dconv_w,
             conv_b=dconv_b, dt_bias=ddt_bias, a_log=da_log, d_exp=dd_exp, norm_g=dnorm_g)
    return dhn, g


def _mla_fwd(hn, w, rope, tag):
    cs, sn, c256, s256 = rope
    c_all = mm(hn, w["mla_in"], name=f"{tag}_in")
    cqn, ckvn, kr = kprep_fwd(c_all, cs, sn, w["q_g"], w["kv_g"], name=f"{tag}_kprep")
    qb = mm(cqn, w["uq"], name=f"{tag}_uq")
    q = qrope_fwd(qb, c256, s256, name=f"{tag}_qrope")
    kv = mm(ckvn, w["ukv"], out_dtype=BF16, name=f"{tag}_ukv")
    o, lse = attn_fwd(q, kv, kr, name=f"{tag}_attn")
    mixed = mm(o, w["mla_out"], name=f"{tag}_out")
    return mixed, (c_all, cqn, ckvn, kr, q, kv, o, lse)


def _mla_bwd(dmixed, hn, saved, w, rope, tag):
    cs, sn, c256, s256 = rope
    c_all, cqn, ckvn, kr, q, kv, o, lse = saved
    do = mm(dmixed, w["mla_out"], tb=True, out_dtype=BF16, name=f"{tag}_out_bx")
    dw_out = mm(o, dmixed, ta=True, out_dtype=BF16, name=f"{tag}_out_bw")
    dl = delta_fwd(do, o, name=f"{tag}_delta")
    dq = attn_bwd_dq(q, kv, kr, do, lse, dl, name=f"{tag}_attn_bq")
    dkn, dv, dkr_h = attn_bwd_dkv(q, kv, kr, do, lse, dl, name=f"{tag}_attn_bkv")
    dkr = headsum(dkr_h, name=f"{tag}_dkr")
    dkv = jnp.concatenate([dkn, dv], axis=1)
    dckvn = mm(dkv, w["ukv"], tb=True, name=f"{tag}_ukv_bx")
    dw_ukv = mm(ckvn, dkv, ta=True, name=f"{tag}_ukv_bw")
    dqb = qrope_bwd(dq, c256, s256, name=f"{tag}_qrope_b")
    dcqn = mm(dqb, w["uq"], tb=True, name=f"{tag}_uq_bx")
    dw_uq = mm(cqn, dqb, ta=True, name=f"{tag}_uq_bw")
    dc_all, dq_g, dkv_g = kprep_bwd(c_all, cs, sn, dcqn, dckvn, dkr, w["q_g"], w["kv_g"], name=f"{tag}_kprep_b")
    dhn = mm(dc_all, w["mla_in"], tb=True, name=f"{tag}_in_bx")
    dw_in = mm(hn, dc_all, ta=True, name=f"{tag}_in_bw")
    g = dict(mla_in=dw_in, mla_out=dw_out, uq=dw_uq, ukv=dw_ukv, q_g=dq_g, kv_g=dkv_g)
    return dhn, g


def local_step(x, p, rope, target, lw):
    h = x
    saved = []
    for i in range(DEPTH):
        w = lw[i]
        t = f"l{i}"
        h, s1 = _ffn_fwd(h, w["ffn1_pre_g"], w["ffn1_w_in"], w["ffn1_w_down"], w["ffn1_post_g"], f"{t}_f1")
        h1 = h
        hn = rms_fwd(h1, w["mix_pre_g"], name=f"{t}_mixpre")
        if i % 2 == 0:
            mixed, sm = _hyb_fwd(hn, w, f"{t}_hyb")
        else:
            mixed, sm = _mla_fwd(hn, w, rope, f"{t}_mla")
        h = post_fwd(h1, mixed, w["mix_post_g"], 1.0, name=f"{t}_mixpost")
        h, s2 = _ffn_fwd(h, w["ffn2_pre_g"], w["ffn2_w_in"], w["ffn2_w_down"], w["ffn2_post_g"], f"{t}_f2")
        h3 = h
        hn3 = rms_fwd(h3, w["ple_pre_g"], name=f"{t}_plepre")
        gl = mm(hn3, w["ple_w_gate"], name=f"{t}_plegate")
        pp = mm((p, i), w["ple_w_proj"], name=f"{t}_pleproj")
        h = ple_fwd(h3, gl, pp, w["ple_post_g"], name=f"{t}_plepost")
        saved.append((s1, h1, hn, sm, mixed, s2, h3, hn3, gl, pp))

    dh, sq = loss_fwd(h, target, name="loss")
    grads = [None] * DEPTH
    for i in reversed(range(DEPTH)):
        w = lw[i]
        t = f"l{i}"
        s1, h1, hn, sm, mixed, s2, h3, hn3, gl, pp = saved[i]
        g = {}
        dgl, dpp, g["ple_post_g"] = ple_bwd(gl, pp, dh, w["ple_post_g"], name=f"{t}_plepost_b")
        dhn3 = mm(dgl, w["ple_w_gate"], tb=True, name=f"{t}_plegate_bx")
        g["ple_w_gate"] = mm(hn3, dgl, ta=True, out_dtype=BF16, name=f"{t}_plegate_bw")
        g["ple_w_proj"] = mm((p, i), dpp, ta=True, out_dtype=BF16, name=f"{t}_pleproj_bw")
        dh, g["ple_pre_g"] = rms_bwd(h3, dhn3, dh, w["ple_pre_g"], name=f"{t}_plepre_b")
        dh, g["ffn2_pre_g"], g["ffn2_w_in"], g["ffn2_w_down"], g["ffn2_post_g"] = _ffn_bwd(
            dh, s2, w["ffn2_pre_g"], w["ffn2_w_in"], w["ffn2_w_down"], w["ffn2_post_g"], f"{t}_f2")
        dmixed, g["mix_post_g"] = post_bwd(mixed, dh, w["mix_post_g"], 1.0, name=f"{t}_mixpost_b")
        if i % 2 == 0:
            dhn, gm = _hyb_bwd(dmixed, hn, sm, w, f"{t}_hyb")
        else:
            dhn, gm = _mla_bwd(dmixed, hn, sm, w, rope, f"{t}_mla")
        g.update(gm)
        dh, g["mix_pre_g"] = rms_bwd(h1, dhn, dh, w["mix_pre_g"], name=f"{t}_mixpre_b")
        dh, g["ffn1_pre_g"], g["ffn1_w_in"], g["ffn1_w_down"], g["ffn1_post_g"] = _ffn_bwd(
            dh, s1, w["ffn1_pre_g"], w["ffn1_w_in"], w["ffn1_w_down"], w["ffn1_post_g"], f"{t}_f1")
        grads[i] = g
    return sq, dh, grads


def _zeros_like_cols(a, n):
    return jnp.zeros(a.shape[:-1] + (n,), a.dtype)


def layer_weights(full, i):
    j = i // 2
    row = lambda v: v.reshape(1, -1)
    w = {k: row(full[k][i]) for k in ("ffn1_pre_g", "ffn1_post_g", "mix_pre_g", "mix_post_g", "ffn2_pre_g",
                                      "ffn2_post_g", "ple_pre_g", "ple_post_g")}
    for k in ("ffn1_w_in", "ffn1_w_down", "ffn2_w_in", "ffn2_w_down", "ple_w_gate", "ple_w_proj"):
        w[k] = (full[k], i)
    if i % 2 == 0:
        hw = full["hyb_w_in"][j]
        w["hyb_in"] = jnp.concatenate([hw, _zeros_like_cols(hw, HYB_PAD - HYB_IN)], axis=1)
        w["hyb_out"] = (full["hyb_w_out"], j)
        w["ln_g"], w["ln_b"] = row(full["gm_ln_g"][j]), row(full["gm_ln_b"][j])
        w["w_s"] = full["gm_w_s"][j].reshape(GM_HEADS * CHUNK, CHUNK)
        w["b_st"] = jnp.pad(full["gm_b_s"][j].T, ((0, 0), (0, LANES - GM_HEADS)))
        w["conv_w"], w["conv_b"] = full["ssd_conv_w"][j], row(full["ssd_conv_b"][j])
        pad16 = lambda v: jnp.pad(v.reshape(1, -1), ((0, 0), (0, LANES - SSD_HEADS)))
        w["dt_bias"], w["a_log"] = pad16(full["ssd_dt_bias"][j]), pad16(full["ssd_a_log"][j])
        w["d_exp"] = row(jnp.repeat(full["ssd_d"][j], SSD_HEAD_DIM))
        w["norm_g"] = row(full["ssd_norm_g"][j])
    else:
        wi = full["mla_w_in"][j]
        z64 = _zeros_like_cols(wi, 64)
        w["mla_in"] = jnp.concatenate([wi[:, :384], wi[:, 384:448], z64, -wi[:, 416:448], wi[:, 384:416], z64], axis=1)
        uq = full["mla_w_uq"][j].reshape(MLA_Q_LORA, MLA_HEADS, MLA_QK)
        zq = jnp.zeros((MLA_Q_LORA, MLA_HEADS, 64), uq.dtype)
        pad_part = jnp.concatenate([uq, zq], axis=2)
        swp_part = jnp.concatenate([jnp.zeros_like(uq[:, :, :128]), -uq[:, :, 160:192], uq[:, :, 128:160], zq], axis=2)
        w["uq"] = jnp.concatenate([pad_part.reshape(MLA_Q_LORA, -1), swp_part.reshape(MLA_Q_LORA, -1)], axis=1)
        ukv = full["mla_w_ukv"][j].reshape(MLA_KV_LORA, MLA_HEADS, 256)
        w["ukv"] = jnp.concatenate([ukv[:, :, :128].reshape(MLA_KV_LORA, -1), ukv[:, :, 128:].reshape(MLA_KV_LORA, -1)],
                                   axis=1)
        w["mla_out"] = (full["mla_w_out"], j)
        w["q_g"], w["kv_g"] = row(full["mla_q_norm_g"][j]), row(full["mla_kv_norm_g"][j])
    return w


def full_grads(grads):
    out = {}
    stack = lambda k, idx: jnp.stack([grads[i][k] for i in idx])
    every, even, odd = range(DEPTH), range(0, DEPTH, 2), range(1, DEPTH, 2)
    for k in ("ffn1_pre_g", "ffn1_post_g", "mix_pre_g", "mix_post_g", "ffn2_pre_g", "ffn2_post_g", "ple_pre_g",
              "ple_post_g"):
        out[k] = stack(k, every).reshape(DEPTH, D_MODEL)
    for k in ("ffn1_w_in", "ffn1_w_down", "ffn2_w_in", "ffn2_w_down", "ple_w_gate", "ple_w_proj"):
        out[k] = stack(k, every)
    out["hyb_w_in"] = stack("hyb_in", even)[:, :, :HYB_IN]
    out["hyb_w_out"] = stack("hyb_out", even)
    out["gm_ln_g"] = stack("ln_g", even).reshape(2, 1024)
    out["gm_ln_b"] = stack("ln_b", even).reshape(2, 1024)
    out["gm_w_s"] = stack("w_s", even).reshape(2, GM_HEADS, CHUNK, CHUNK)
    out["gm_b_s"] = jnp.swapaxes(stack("b_st", even)[:, :, :GM_HEADS], 1, 2)
    out["ssd_conv_w"] = stack("conv_w", even)
    out["ssd_conv_b"] = stack("conv_b", even).reshape(2, SSD_CONV_CH)
    out["ssd_dt_bias"] = stack("dt_bias", even)[:, 0, :SSD_HEADS]
    out["ssd_a_log"] = stack("a_log", even)[:, 0, :SSD_HEADS]
    out["ssd_d"] = stack("d_exp", even).reshape(2, SSD_HEADS, SSD_HEAD_DIM).sum(axis=-1)
    out["ssd_norm_g"] = stack("norm_g", even).reshape(2, 1024)
    dwi = stack("mla_in", odd)
    out["mla_w_in"] = jnp.concatenate([dwi[:, :, :384], dwi[:, :, 384:416] + dwi[:, :, 544:576],
                                       dwi[:, :, 416:448] - dwi[:, :, 512:544]], axis=2)
    duq = stack("uq", odd)
    half = MLA_HEADS * MLA_QPAD
    dp = duq[:, :, :half].reshape(2, MLA_Q_LORA, MLA_HEADS, MLA_QPAD)
    ds = duq[:, :, half:].reshape(2, MLA_Q_LORA, MLA_HEADS, MLA_QPAD)
    out["mla_w_uq"] = jnp.concatenate([dp[..., :128], dp[..., 128:160] + ds[..., 160:192],
                                       dp[..., 160:192] - ds[..., 128:160]], axis=-1).reshape(2, MLA_Q_LORA, -1)
    dukv = stack("ukv", odd)
    dk = dukv[:, :, :2048].reshape(2, MLA_KV_LORA, MLA_HEADS, 128)
    dv = dukv[:, :, 2048:].reshape(2, MLA_KV_LORA, MLA_HEADS, 128)
    out["mla_w_ukv"] = jnp.concatenate([dk, dv], axis=-1).reshape(2, MLA_KV_LORA, -1)
    out["mla_w_out"] = stack("mla_out", odd)
    out["mla_q_norm_g"] = stack("q_g", odd).reshape(2, MLA_Q_LORA)
    out["mla_kv_norm_g"] = stack("kv_g", odd).reshape(2, MLA_KV_LORA)
    return out


def rope_tables(positions):
    T = positions.shape[0]
    inv = 1.0 / (ROPE_BASE ** (jnp.arange(0, MLA_ROPE, 2, dtype=F32) / MLA_ROPE))
    ang = positions.astype(F32)[:, None] * inv
    cos, sin = jnp.cos(ang), jnp.sin(ang)
    z64 = jnp.zeros((T, 64), F32)
    cs = jnp.concatenate([cos, cos, z64], axis=1)
    sn = jnp.concatenate([sin, sin, z64], axis=1)
    c256 = jnp.concatenate([jnp.ones((T, 128), F32), cs], axis=1)
    s256 = jnp.concatenate([jnp.zeros((T, 128), F32), sn], axis=1)
    return cs, sn, c256, s256


def _rows(n):
    return -(-n // LANES)


def _pack(pieces, dtype, row_multiple):
    flat = []
    total = 0
    for a in pieces:
        v = a.reshape(-1).astype(dtype)
        padn = _rows(v.shape[0]) * LANES - v.shape[0]
        if padn:
            v = jnp.concatenate([v, jnp.zeros((padn,), dtype)])
        flat.append(v)
        total += v.shape[0] // LANES
    tail = -total % row_multiple
    if tail:
        flat.append(jnp.zeros((tail * LANES,), dtype))
    return jnp.concatenate(flat).reshape(-1, LANES)


def _unpack(slab, shapes):
    out = []
    r = 0
    for s in shapes:
        n = int(np.prod(s))
        nr = _rows(n)
        out.append(slab[r:r + nr].reshape(-1)[:n].reshape(s))
        r += nr
    return out


def _shard_shape(shape, ax):
    if ax is None:
        return tuple(shape)
    s = list(shape)
    s[ax] //= N_CHIPS
    return tuple(s)


def _chip_slice(a, ax, k):
    if ax is None:
        return a
    n = a.shape[ax] // N_CHIPS
    return lax.slice_in_dim(a, k * n, (k + 1) * n, axis=ax)


def _plane_peers():
    x, y, c = lax.axis_index("x"), lax.axis_index("y"), lax.axis_index("c")
    return (x, y, c), [(1 - x, y, c), (x, 1 - y, c), (1 - x, 1 - y, c)]


ANY = pl.BlockSpec(memory_space=pl.ANY)


def plane_allgather(slab):
    R = slab.shape[0]

    def body(src, out, send_sems, recv_sems, local_sem):
        (x, y, c), peers = _plane_peers()
        me = 2 * x + y
        local = pltpu.make_async_copy(src, out.at[me], local_sem)
        local.start()
        copies = []
        for j, peer in enumerate(peers):
            cp = pltpu.make_async_remote_copy(src_ref=src, dst_ref=out.at[me], send_sem=send_sems.at[j],
                                              recv_sem=recv_sems.at[j], device_id=peer, device_id_type=MESH)
            cp.start()
            copies.append(cp)
        for cp in copies:
            cp.wait()
        local.wait()

    return pl.pallas_call(
        body, name="plane_allgather", out_shape=SDS((N_CHIPS, R, LANES), slab.dtype),
        in_specs=[ANY], out_specs=ANY,
        scratch_shapes=[pltpu.SemaphoreType.DMA((3,)), pltpu.SemaphoreType.DMA((3,)), pltpu.SemaphoreType.DMA],
    )(slab)


def plane_alltoall(buf):
    R = buf.shape[1]

    def body(src, out, send_sems, recv_sems, local_sem):
        (x, y, c), peers = _plane_peers()
        me = 2 * x + y
        local = pltpu.make_async_copy(src.at[me], out.at[me], local_sem)
        local.start()
        copies = []
        for j, peer in enumerate(peers):
            cp = pltpu.make_async_remote_copy(src_ref=src.at[2 * peer[0] + peer[1]], dst_ref=out.at[me],
                                              send_sem=send_sems.at[j], recv_sem=recv_sems.at[j], device_id=peer,
                                              device_id_type=MESH)
            cp.start()
            copies.append(cp)
        for cp in copies:
            cp.wait()
        local.wait()

    return pl.pallas_call(
        body, name="plane_alltoall", out_shape=SDS((N_CHIPS, R, LANES), buf.dtype),
        in_specs=[ANY], out_specs=ANY,
        scratch_shapes=[pltpu.SemaphoreType.DMA((3,)), pltpu.SemaphoreType.DMA((3,)), pltpu.SemaphoreType.DMA],
    )(buf)


def sibling_swap(buf):
    def body(src, out, send_sem, recv_sem):
        x, y, c = lax.axis_index("x"), lax.axis_index("y"), lax.axis_index("c")
        cp = pltpu.make_async_remote_copy(src_ref=src, dst_ref=out, send_sem=send_sem, recv_sem=recv_sem,
                                          device_id=(x, y, 1 - c), device_id_type=MESH)
        cp.start()
        cp.wait()

    return pl.pallas_call(
        body, name="sibling_swap", out_shape=SDS(buf.shape, buf.dtype), in_specs=[ANY], out_specs=ANY,
        scratch_shapes=[pltpu.SemaphoreType.DMA, pltpu.SemaphoreType.DMA],
    )(buf)


def _chip_block(ref, ax, k, n):
    start = pl.multiple_of(k * n, n)
    return ref.at[:, pl.ds(start, n), :] if ax == 1 else ref.at[:, :, pl.ds(start, n)]


def gather_reg(shards, axes):
    n = len(shards)
    fulls = []
    for s, ax in zip(shards, axes):
        shape = list(s.shape)
        shape[ax] *= N_CHIPS
        fulls.append(SDS(tuple(shape), s.dtype))

    def body(*refs):
        srcs, outs = refs[:n], refs[n:2 * n]
        send_sems, recv_sems, local_sems = refs[2 * n:]
        (x, y, c), peers = _plane_peers()
        me = 2 * x + y
        copies = []
        for t in range(n):
            dst = _chip_block(outs[t], axes[t], me, srcs[t].shape[axes[t]])
            local = pltpu.make_async_copy(srcs[t], dst, local_sems.at[t])
            local.start()
            copies.append(local)
            for j, peer in enumerate(peers):
                cp = pltpu.make_async_remote_copy(src_ref=srcs[t], dst_ref=dst, send_sem=send_sems.at[3 * t + j],
                                                  recv_sem=recv_sems.at[3 * t + j], device_id=peer, device_id_type=MESH)
                cp.start()
                copies.append(cp)
        for cp in copies:
            cp.wait()

    return pl.pallas_call(
        body, name="gather_reg", out_shape=fulls, in_specs=[ANY] * n, out_specs=[ANY] * n,
        scratch_shapes=[pltpu.SemaphoreType.DMA((3 * n,)), pltpu.SemaphoreType.DMA((3 * n,)),
                        pltpu.SemaphoreType.DMA((n,))],
    )(*shards)


def exchange_reg(grads, axes):
    n = len(grads)
    outs = []
    for g, ax in zip(grads, axes):
        shape = list(g.shape)
        shape[ax] //= N_CHIPS
        outs.append(SDS((N_CHIPS,) + tuple(shape), g.dtype))

    def body(*refs):
        srcs, mine, theirs = refs[:n], refs[n:2 * n], refs[2 * n:3 * n]
        send_sems, recv_sems, local_sems, fsend_sems, frecv_sems = refs[3 * n:]
        (x, y, c), peers = _plane_peers()
        sibling = (x, y, 1 - c)
        me = 2 * x + y
        pidx = [2 * px + py for (px, py, _) in peers]
        sends, locals_ = [], []
        for t in range(n):
            ns = mine[t].shape[axes[t] + 1]
            local = pltpu.make_async_copy(_chip_block(srcs[t], axes[t], me, ns), mine[t].at[me], local_sems.at[t])
            local.start()
            locals_.append(local)
            for j, peer in enumerate(peers):
                cp = pltpu.make_async_remote_copy(src_ref=_chip_block(srcs[t], axes[t], pidx[j], ns),
                                                  dst_ref=mine[t].at[me], send_sem=send_sems.at[3 * t + j],
                                                  recv_sem=recv_sems.at[3 * t + j], device_id=peer, device_id_type=MESH)
                cp.start()
                sends.append(cp)
        forwards = []
        for t in range(n):
            locals_[t].wait()
            blocks = [me] + pidx
            for q, blk in enumerate(blocks):
                if q > 0:
                    sends[3 * t + q - 1].wait_recv()
                fw = pltpu.make_async_remote_copy(src_ref=mine[t].at[blk], dst_ref=theirs[t].at[blk],
                                                  send_sem=fsend_sems.at[4 * t + q], recv_sem=frecv_sems.at[4 * t + q],
                                                  device_id=sibling, device_id_type=MESH)
                fw.start()
                forwards.append(fw)
        for cp in sends:
            cp.wait_send()
        for fw in forwards:
            fw.wait()

    return pl.pallas_call(
        body, name="exchange_reg", out_shape=outs + outs, in_specs=[ANY] * n, out_specs=[ANY] * (2 * n),
        scratch_shapes=[pltpu.SemaphoreType.DMA((3 * n,)), pltpu.SemaphoreType.DMA((3 * n,)),
                        pltpu.SemaphoreType.DMA((n,)), pltpu.SemaphoreType.DMA((4 * n,)),
                        pltpu.SemaphoreType.DMA((4 * n,))],
    )(*grads)


def _adam_update(g, w, m, v):
    mn = ADAM_B1 * m + (1.0 - ADAM_B1) * g
    vn = ADAM_B2 * v + (1.0 - ADAM_B2) * jnp.square(g)
    m_hat = mn / (1.0 - ADAM_B1 ** ADAM_STEP)
    v_hat = vn / (1.0 - ADAM_B2 ** ADAM_STEP)
    return -ADAM_LR * (m_hat / (jnp.sqrt(v_hat) + ADAM_EPS) + ADAM_WD * w), mn, vn


def adamw_reg(mine, theirs, w, m, v, *, name):
    L, rs, cs = w.shape
    tr = rs
    for cand in range(256, 15, -16):
        if rs % cand == 0:
            tr = cand
            break

    def body(a_r, b_r, w_r, m_r, v_r, g_o, d_o, m_o, v_o):
        pa = a_r[0].astype(F32)
        pb = b_r[0].astype(F32)
        for k in range(1, N_CHIPS):
            pa = pa + a_r[k].astype(F32)
            pb = pb + b_r[k].astype(F32)
        g = pa + pb
        d, mn, vn = _adam_update(g, w_r[...], m_r[...], v_r[...])
        g_o[...] = g
        d_o[...] = d
        m_o[...] = mn
        v_o[...] = vn

    s4 = pl.BlockSpec((N_CHIPS, None, tr, cs), lambda l, i: (0, l, i, 0))
    s1 = pl.BlockSpec((None, tr, cs), lambda l, i: (l, i, 0))
    return pl.pallas_call(
        body, name=name, grid=(L, rs // tr), in_specs=[s4, s4, s1, s1, s1], out_specs=[s1] * 4,
        out_shape=[SDS((L, rs, cs), F32)] * 4,
        compiler_params=pltpu.CompilerParams(dimension_semantics=("parallel", "parallel"),
                                             vmem_limit_bytes=VMEM_LIMIT),
    )(mine, theirs, w, m, v)


def plane_sum(r4):
    R = r4.shape[1]

    def body(r_ref, o_ref):
        acc = r_ref[0].astype(F32)
        for k in range(1, N_CHIPS):
            acc = acc + r_ref[k].astype(F32)
        o_ref[...] = acc

    return pl.pallas_call(
        body, name="plane_sum", grid=(R // PACK_ROWS,),
        in_specs=[pl.BlockSpec((N_CHIPS, PACK_ROWS, LANES), lambda i: (0, i, 0))],
        out_specs=pl.BlockSpec((PACK_ROWS, LANES), lambda i: (i, 0)), out_shape=SDS((R, LANES), F32),
        compiler_params=pltpu.CompilerParams(dimension_semantics=("parallel",)),
    )(r4)


def adamw(pa, pb, w, m, v):
    R = w.shape[0]

    def body(pa_r, pb_r, w_r, m_r, v_r, g_o, d_o, m_o, v_o):
        g = pa_r[...] + pb_r[...]
        d, mn, vn = _adam_update(g, w_r[...], m_r[...], v_r[...])
        g_o[...] = g
        d_o[...] = d
        m_o[...] = mn
        v_o[...] = vn

    spec = pl.BlockSpec((PACK_ROWS, LANES), lambda i: (i, 0))
    return pl.pallas_call(
        body, name="adamw", grid=(R // PACK_ROWS,), in_specs=[spec] * 5, out_specs=[spec] * 4,
        out_shape=[SDS((R, LANES), F32)] * 4,
        compiler_params=pltpu.CompilerParams(dimension_semantics=("parallel",)),
    )(pa, pb, w, m, v)


def gather_weights(wl):
    full = dict(zip(REG, gather_reg([wl[n].astype(BF16) for n in REG], [WSPEC[n][2] for n in REG])))
    sharded = [n for n in MISC if WSPEC[n][2] is not None]
    pieces = [wl[n].astype(BF16) if WSPEC[n][3] else lax.bitcast_convert_type(wl[n], BF16) for n in sharded]
    got = plane_allgather(_pack(pieces, BF16, 16))
    shapes = [v.shape for v in pieces]
    per_chip = [_unpack(got[k], shapes) for k in range(N_CHIPS)]
    for idx, n in enumerate(sharded):
        parts = [per_chip[k][idx] for k in range(N_CHIPS)]
        if not WSPEC[n][3]:
            parts = [lax.bitcast_convert_type(v, F32) for v in parts]
        full[n] = jnp.concatenate(parts, axis=WSPEC[n][2])
    for n in MISC:
        if WSPEC[n][2] is None:
            full[n] = wl[n]
    return full


def kernel(x, p, positions, ffn1_pre_g, ffn1_w_in, ffn1_w_down, ffn1_post_g, mix_pre_g, mix_post_g, ffn2_pre_g, ffn2_w_in, ffn2_w_down, ffn2_post_g, ple_pre_g, ple_w_gate, ple_w_proj, ple_post_g, hyb_w_in, gm_ln_g, gm_ln_b, gm_w_s, gm_b_s, ssd_conv_w, ssd_conv_b, ssd_dt_bias, ssd_a_log, ssd_d, ssd_norm_g, hyb_w_out, mla_w_in, mla_q_norm_g, mla_kv_norm_g, mla_w_uq, mla_w_ukv, mla_w_out, loss_target, m_ffn1_pre_g, m_ffn1_w_in, m_ffn1_w_down, m_ffn1_post_g, m_mix_pre_g, m_mix_post_g, m_ffn2_pre_g, m_ffn2_w_in, m_ffn2_w_down, m_ffn2_post_g, m_ple_pre_g, m_ple_w_gate, m_ple_w_proj, m_ple_post_g, m_hyb_w_in, m_gm_ln_g, m_gm_ln_b, m_gm_w_s, m_gm_b_s, m_ssd_conv_w, m_ssd_conv_b, m_ssd_dt_bias, m_ssd_a_log, m_ssd_d, m_ssd_norm_g, m_hyb_w_out, m_mla_w_in, m_mla_q_norm_g, m_mla_kv_norm_g, m_mla_w_uq, m_mla_w_ukv, m_mla_w_out, v_ffn1_pre_g, v_ffn1_w_in, v_ffn1_w_down, v_ffn1_post_g, v_mix_pre_g, v_mix_post_g, v_ffn2_pre_g, v_ffn2_w_in, v_ffn2_w_down, v_ffn2_post_g, v_ple_pre_g, v_ple_w_gate, v_ple_w_proj, v_ple_post_g, v_hyb_w_in, v_gm_ln_g, v_gm_ln_b, v_gm_w_s, v_gm_b_s, v_ssd_conv_w, v_ssd_conv_b, v_ssd_dt_bias, v_ssd_a_log, v_ssd_d, v_ssd_norm_g, v_hyb_w_out, v_mla_w_in, v_mla_q_norm_g, v_mla_kv_norm_g, v_mla_w_uq, v_mla_w_ukv, v_mla_w_out):
    args = locals()
    wl = {n: args[n] for n in WNAMES}
    ml = {n: args["m_" + n] for n in WNAMES}
    vl = {n: args["v_" + n] for n in WNAMES}

    full = gather_weights(wl)
    lw = [layer_weights(full, i) for i in range(DEPTH)]
    rope = rope_tables(positions[0])
    T = x.shape[1]
    sq, dx, grads = local_step(x[0], p.reshape(DEPTH, T, p.shape[-1]), rope, loss_target[0], lw)
    loss = lax.psum(0.5 * jnp.sum(sq) / D_MODEL, ("x", "y", "c"))

    fg = full_grads(grads)
    res = {}
    both = exchange_reg([fg[n] for n in REG], [WSPEC[n][2] for n in REG])
    for t, n in enumerate(REG):
        res[n] = adamw_reg(both[t], both[len(REG) + t], wl[n], ml[n], vl[n], name=f"adamw_{n}")
    dest = [_pack([_chip_slice(fg[n], WSPEC[n][2], k) for n in MISC], BF16, PACK_ROWS) for k in range(N_CHIPS)]
    mine = plane_sum(plane_alltoall(jnp.stack(dest)))
    other = sibling_swap(mine)
    slabs = adamw(mine, other, *[_pack([d[n] for n in MISC], F32, PACK_ROWS) for d in (wl, ml, vl)])
    shapes = [wl[n].shape for n in MISC]
    unpacked = [_unpack(s, shapes) for s in slabs]
    for idx, n in enumerate(MISC):
        res[n] = [u[idx] for u in unpacked]
    return (loss, dx[None], *[res[n][k] for k in range(4) for n in WNAMES])
```

```python
import functools
import math

import jax
import jax.numpy as jnp
import numpy as np
from jax import lax
from jax.experimental import pallas as pl
from jax.experimental.pallas import tpu as pltpu

F32 = jnp.float32
BF16 = jnp.bfloat16
SDS = jax.ShapeDtypeStruct
MESH = pl.DeviceIdType.MESH
HIGHEST = lax.Precision.HIGHEST

D_MODEL = 1024
DEPTH = 4
D_FF = 2816
NORM_EPS = 1e-6
LN_EPS = 1e-5
GM_HEADS = 8
CHUNK = 128
SSD_HEADS = 16
SSD_HEAD_DIM = 64
SSD_INNER = 1024
SSD_STATE = 128
SSD_CONV = 4
SSD_CONV_CH = 1536
HYB_MAIN = 4608
HYB_IN = 4624
HYB_PAD = 5120
MLA_HEADS = 16
MLA_NOPE = 128
MLA_ROPE = 64
MLA_QK = 192
MLA_QPAD = 256
MLA_Q_LORA = 256
MLA_KV_LORA = 128
ROPE_BASE = 10000.0
ADAM_LR = 0.001
ADAM_B1 = 0.9
ADAM_B2 = 0.999
ADAM_EPS = 1e-08
ADAM_WD = 0.01
ADAM_STEP = 10

N_CHIPS = 4
LANES = 128
VMEM_LIMIT = 56 * 1024 * 1024
PACK_ROWS = 2048

WEIGHTS = [
    ("ffn1_pre_g", (4, 1024), None, False),
    ("ffn1_w_in", (4, 1024, 5632), 2, True),
    ("ffn1_w_down", (4, 2816, 1024), 1, True),
    ("ffn1_post_g", (4, 1024), None, False),
    ("mix_pre_g", (4, 1024), None, False),
    ("mix_post_g", (4, 1024), None, False),
    ("ffn2_pre_g", (4, 1024), None, False),
    ("ffn2_w_in", (4, 1024, 5632), 2, True),
    ("ffn2_w_down", (4, 2816, 1024), 1, True),
    ("ffn2_post_g", (4, 1024), None, False),
    ("ple_pre_g", (4, 1024), None, False),
    ("ple_w_gate", (4, 1024, 1024), 1, True),
    ("ple_w_proj", (4, 256, 1024), 2, True),
    ("ple_post_g", (4, 1024), None, False),
    ("hyb_w_in", (2, 1024, 4624), 2, True),
    ("gm_ln_g", (2, 1024), None, False),
    ("gm_ln_b", (2, 1024), None, False),
    ("gm_w_s", (2, 8, 128, 128), None, False),
    ("gm_b_s", (2, 8, 128), None, False),
    ("ssd_conv_w", (2, 4, 1536), 2, False),
    ("ssd_conv_b", (2, 1536), None, False),
    ("ssd_dt_bias", (2, 16), None, False),
    ("ssd_a_log", (2, 16), None, False),
    ("ssd_d", (2, 16), None, False),
    ("ssd_norm_g", (2, 1024), None, False),
    ("hyb_w_out", (2, 2048, 1024), 1, True),
    ("mla_w_in", (2, 1024, 448), 1, True),
    ("mla_q_norm_g", (2, 256), 1, False),
    ("mla_kv_norm_g", (2, 128), None, False),
    ("mla_w_uq", (2, 256, 3072), 2, True),
    ("mla_w_ukv", (2, 128, 4096), 2, True),
    ("mla_w_out", (2, 2048, 1024), 1, True),
]
WNAMES = [w[0] for w in WEIGHTS]
WSPEC = {w[0]: w for w in WEIGHTS}
REG = ["ffn1_w_in", "ffn1_w_down", "ffn2_w_in", "ffn2_w_down", "ple_w_gate", "ple_w_proj", "hyb_w_out", "mla_w_out"]
MISC = [n for n in WNAMES if n not in REG]


def _pick(dim, target):
    if dim <= target:
        return dim
    t = (target // LANES) * LANES
    while t >= LANES:
        if dim % t == 0:
            return t
        t -= LANES
    return dim


def mm(a, b, *, ta=False, tb=False, out_dtype=F32, name, tm=1024, tn=1024, tk=1024):
    a, la = a if isinstance(a, tuple) else (a, None)
    b, lb = b if isinstance(b, tuple) else (b, None)
    if ta:
        K, M = a.shape[-2:]
    else:
        M, K = a.shape[-2:]
    if tb:
        N, K2 = b.shape[-2:]
    else:
        K2, N = b.shape[-2:]
    assert K == K2, (a.shape, b.shape, ta, tb)
    bm, bn, bk = _pick(M, tm), _pick(N, tn), _pick(K, tk)
    nk = K // bk

    def spec(shape, idx, layer):
        if layer is None:
            return pl.BlockSpec(shape, idx)
        return pl.BlockSpec((None,) + shape, lambda i, j, k: (layer,) + idx(i, j, k))

    a_spec = spec((bk, bm), lambda i, j, k: (k, i), la) if ta else spec((bm, bk), lambda i, j, k: (i, k), la)
    b_spec = spec((bn, bk), lambda i, j, k: (j, k), lb) if tb else spec((bk, bn), lambda i, j, k: (k, j), lb)
    dn = (((0 if ta else 1,), (1 if tb else 0,)), ((), ()))

    def body(a_ref, b_ref, o_ref, acc_ref):
        k = pl.program_id(2)

        @pl.when(k == 0)
        def _():
            acc_ref[...] = jnp.zeros_like(acc_ref)

        acc_ref[...] += lax.dot_general(a_ref[...].astype(BF16), b_ref[...].astype(BF16), dn,
                                        preferred_element_type=F32)

        @pl.when(k == nk - 1)
        def _():
            o_ref[...] = acc_ref[...].astype(o_ref.dtype)

    return pl.pallas_call(
        body, name=name, grid=(M // bm, N // bn, nk),
        in_specs=[a_spec, b_spec], out_specs=pl.BlockSpec((bm, bn), lambda i, j, k: (i, j)),
        out_shape=SDS((M, N), out_dtype), scratch_shapes=[pltpu.VMEM((bm, bn), F32)],
        compiler_params=pltpu.CompilerParams(dimension_semantics=("parallel", "parallel", "arbitrary"),
                                             vmem_limit_bytes=VMEM_LIMIT),
    )(a, b)


def row_call(fn, xs, ps, outs, accs=(), *, tb, name, reverse=False):
    xs = [x if isinstance(x, tuple) else (x, x.shape[1], 0) for x in xs]
    T = xs[0][0].shape[0]
    tb = min(tb, T)
    n = T // tb
    assert n * tb == T
    nx, npar, no, na = len(xs), len(ps), len(outs), len(accs)

    def ridx(i):
        return n - 1 - i if reverse else i

    in_specs = [pl.BlockSpec((tb, w), functools.partial(lambda i, cb: (ridx(i), cb), cb=cb)) for (_, w, cb) in xs]
    in_specs += [pl.BlockSpec(p.shape, functools.partial(lambda i, nd: (0,) * nd, nd=p.ndim)) for p in ps]
    out_specs = [pl.BlockSpec((tb, c), lambda i: (ridx(i), 0)) for (c, _) in outs]
    out_specs += [pl.BlockSpec(s, functools.partial(lambda i, nd: (0,) * nd, nd=len(s))) for s in accs]
    out_shape = [SDS((T, c), dt) for (c, dt) in outs] + [SDS(s, F32) for s in accs]

    def body(*refs):
        xr, pr = refs[:nx], refs[nx:nx + npar]
        orf, ar = refs[nx + npar:nx + npar + no], refs[nx + npar + no:]
        res = fn(*[r[...] for r in xr], *[r[...] for r in pr])
        for r, v in zip(orf, res[:no]):
            r[...] = v.astype(r.dtype)
        if na:
            @pl.when(pl.program_id(0) == 0)
            def _():
                for r in ar:
                    r[...] = jnp.zeros_like(r)

            for r, v in zip(ar, res[no:]):
                r[...] += v.astype(F32)

    res = pl.pallas_call(
        body, name=name, grid=(n,), in_specs=in_specs, out_specs=out_specs, out_shape=out_shape,
        compiler_params=pltpu.CompilerParams(dimension_semantics=("arbitrary",), vmem_limit_bytes=VMEM_LIMIT),
    )(*[x[0] for x in xs], *ps)
    return res


def _f32(*a):
    return [v.astype(F32) for v in a]


def t_rms(x, g):
    return x * lax.rsqrt(jnp.mean(x * x, axis=-1, keepdims=True) + NORM_EPS) * g


def t_swiglu(gu):
    return jax.nn.silu(gu[:, :D_FF]) * gu[:, D_FF:]


def t_ple(gl, pp, g):
    return t_rms(jax.nn.sigmoid(gl) * pp, g)


def _iota(shape, d):
    return lax.broadcasted_iota(jnp.int32, shape, d)


def _bdot(a, b, dn=(((1,), (0,)), ((), ()))):
    return lax.dot_general(a.astype(BF16), b.astype(BF16), dn, preferred_element_type=F32)


def _hdot(a, b):
    return jnp.dot(a, b, precision=HIGHEST, preferred_element_type=F32)


NT = (((1,), (1,)), ((), ()))
TN = (((0,), (0,)), ((), ()))


def t_gmlp(uv, ln_g, ln_b, w_s, b_st):
    tb = uv.shape[0]
    guv = jax.nn.gelu(uv)
    u, v = guv[:, :1024], guv[:, 1024:]
    tri = _iota((CHUNK, CHUNK), 1) <= _iota((CHUNK, CHUNK), 0)
    rows = []
    for c in range(tb // CHUNK):
        vc = v[c * CHUNK:(c + 1) * CHUNK]
        heads = []
        for h in range(GM_HEADS):
            sl = slice(h * 128, (h + 1) * 128)
            vh = vc[:, sl]
            xc = vh - jnp.mean(vh, axis=-1, keepdims=True)
            var = jnp.mean(xc * xc, axis=-1, keepdims=True)
            y = xc * lax.rsqrt(var + LN_EPS) * ln_g[:, sl] + ln_b[:, sl]
            wm = jnp.where(tri, w_s[sl, :], 0.0)
            heads.append(_bdot(wm, y) + b_st[:, h:h + 1])
        rows.append(jnp.concatenate(heads, axis=1))
    mixed = rows[0] if len(rows) == 1 else jnp.concatenate(rows, axis=0)
    return u * mixed


def t_ssd(pre, dtr, z, st, dt_bias, a_log, d_exp, norm_g):
    L = CHUNK
    xbc = jax.nn.silu(pre)
    xs, bm, cm = xbc[:, :1024], xbc[:, 1024:1280], xbc[:, 1280:1536]
    valid = _iota((1, LANES), 1) < SSD_HEADS
    dt16 = jnp.where(valid, jax.nn.softplus(dtr + dt_bias), 0.0)
    a16 = jnp.where(valid, -jnp.exp(a_log), 0.0)
    da16 = dt16 * a16
    tri = _iota((L, L), 1) <= _iota((L, L), 0)
    acs16 = _hdot(tri.astype(F32), da16)
    hh, cc = _iota((LANES, 1024), 0), _iota((LANES, 1024), 1)
    expand = ((cc >= hh * SSD_HEAD_DIM) & (cc < (hh + 1) * SSD_HEAD_DIM)).astype(F32)
    acs = _hdot(acs16, expand)
    dte = _hdot(dt16, expand)
    alast = jnp.sum(jnp.where(_iota((L, 1024), 0) == L - 1, acs, 0.0), axis=0, keepdims=True)
    xd = xs * dte
    groups = [slice(0, 512), slice(512, 1024)]
    bg = [bm[:, :128], bm[:, 128:]]
    cg = [cm[:, :128], cm[:, 128:]]
    yoff = jnp.concatenate([_bdot(cg[g], st[:, groups[g]]) for g in range(2)], axis=1) * jnp.exp(acs)
    xdw = xd * jnp.exp(alast - acs)
    s_t = jnp.concatenate([_bdot(bg[g], xdw[:, groups[g]], TN) for g in range(2)], axis=1)
    st_new = st * jnp.exp(alast) + s_t
    cb = [_bdot(cg[g], bg[g], NT) for g in range(2)]
    acs16_t = acs16.T
    lo = _iota((1, LANES), 1) < SSD_HEAD_DIM
    slabs = []
    for j in range(SSD_HEADS // 2):
        g = j // 4
        xslab = xd[:, j * 128:(j + 1) * 128]
        acc = None
        for half in range(2):
            h = 2 * j + half
            seg = acs16[:, h:h + 1] - acs16_t[h:h + 1, :]
            mmat = cb[g] * jnp.exp(jnp.where(tri, seg, -1e30))
            xm = jnp.where(lo if half == 0 else jnp.logical_not(lo), xslab, 0.0)
            term = _bdot(mmat, xm)
            acc = term if acc is None else acc + term
        slabs.append(acc)
    y = jnp.concatenate(slabs, axis=1) + yoff + d_exp * xs
    yg = y * jax.nn.silu(z)
    outs = []
    for g in range(2):
        t = yg[:, groups[g]]
        outs.append(t * lax.rsqrt(jnp.mean(t * t, axis=-1, keepdims=True) + NORM_EPS) * norm_g[:, groups[g]])
    return jnp.concatenate(outs, axis=1), st_new


def t_kprep(c_all, cs, sn, qg, kvg):
    cqn = t_rms(c_all[:, :256], qg)
    ckvn = t_rms(c_all[:, 256:384], kvg)
    kr = c_all[:, 384:512] * cs + c_all[:, 512:640] * sn
    return cqn, ckvn, kr


def t_qrope(qb, c256, s256):
    scale = MLA_QK ** -0.5
    half = MLA_HEADS * MLA_QPAD
    outs = []
    for h in range(MLA_HEADS):
        a = qb[:, h * MLA_QPAD:(h + 1) * MLA_QPAD]
        b = qb[:, half + h * MLA_QPAD:half + (h + 1) * MLA_QPAD]
        outs.append((a * c256 + b * s256) * scale)
    return jnp.concatenate(outs, axis=1)


def rms_fwd(h, g, *, name, tb=512):
    def fn(h, g):
        return (t_rms(h.astype(F32), g),)
    return row_call(fn, [h], [g], [(h.shape[1], BF16)], tb=tb, name=name)[0]


def rms_bwd(h, dhn, dres, g, *, name, tb=256):
    def fn(h, dhn, dres, g):
        h, dhn, dres = _f32(h, dhn, dres)
        _, vjp = jax.vjp(t_rms, h, g)
        dh, dg = vjp(dhn)
        return dres + dh, dg
    return row_call(fn, [h, dhn, dres], [g], [(h.shape[1], F32)], [g.shape], tb=tb, name=name)


def post_fwd(h, f, g, scale, *, name, tb=512):
    def fn(h, f, g):
        return (h + scale * t_rms(f.astype(F32), g),)
    return row_call(fn, [h, f], [g], [(h.shape[1], F32)], tb=tb, name=name)[0]


def post_bwd(f, dout, g, scale, *, name, tb=256):
    def fn(f, dout, g):
        f, dout = _f32(f, dout)
        _, vjp = jax.vjp(lambda f, g: scale * t_rms(f, g), f, g)
        return vjp(dout)
    return row_call(fn, [f, dout], [g], [(f.shape[1], BF16)], [g.shape], tb=tb, name=name)


def swiglu_fwd(gu, *, name, tb=256):
    def fn(gu):
        return (t_swiglu(gu.astype(F32)),)
    return row_call(fn, [gu], [], [(D_FF, BF16)], tb=tb, name=name)[0]


def swiglu_bwd(gu, da, *, name, tb=256):
    def fn(gu, da):
        gu, da = _f32(gu, da)
        _, vjp = jax.vjp(t_swiglu, gu)
        return vjp(da)
    return row_call(fn, [gu, da], [], [(2 * D_FF, BF16)], tb=tb, name=name)[0]


def ple_fwd(h, gl, pp, g, *, name, tb=512):
    def fn(h, gl, pp, g):
        return (h + t_ple(gl, pp, g),)
    return row_call(fn, [h, gl, pp], [g], [(D_MODEL, F32)], tb=tb, name=name)[0]


def ple_bwd(gl, pp, dout, g, *, name, tb=256):
    def fn(gl, pp, dout, g):
        _, vjp = jax.vjp(t_ple, gl, pp, g)
        return vjp(dout)
    return row_call(fn, [gl, pp, dout], [g], [(D_MODEL, BF16), (D_MODEL, BF16)], [g.shape], tb=tb, name=name)


def gmlp_fwd(proj, ln_g, ln_b, w_s, b_st, *, name, tb=256):
    def fn(uv, ln_g, ln_b, w_s, b_st):
        return (t_gmlp(uv, ln_g, ln_b, w_s, b_st),)
    return row_call(fn, [(proj, 2048, 0)], [ln_g, ln_b, w_s, b_st], [(1024, BF16)], tb=tb, name=name)[0]


def gmlp_bwd(proj, dya, ln_g, ln_b, w_s, b_st, *, name, tb=128):
    def fn(uv, dya, ln_g, ln_b, w_s, b_st):
        _, vjp = jax.vjp(t_gmlp, uv, ln_g, ln_b, w_s, b_st)
        return vjp(dya.astype(F32))
    return row_call(fn, [(proj, 2048, 0), (dya, 1024, 0)], [ln_g, ln_b, w_s, b_st], [(2048, BF16)],
                    [ln_g.shape, ln_b.shape, w_s.shape, b_st.shape], tb=tb, name=name)


def kprep_fwd(c_all, cs, sn, qg, kvg, *, name, tb=512):
    return row_call(t_kprep, [c_all, cs, sn], [qg, kvg], [(256, BF16), (128, BF16), (128, BF16)], tb=tb, name=name)


def kprep_bwd(c_all, cs, sn, dcqn, dckvn, dkr, qg, kvg, *, name, tb=256):
    def fn(c_all, cs, sn, dcqn, dckvn, dkr, qg, kvg):
        dcqn, dckvn, dkr = _f32(dcqn, dckvn, dkr)
        _, vjp = jax.vjp(lambda c, qg, kvg: t_kprep(c, cs, sn, qg, kvg), c_all, qg, kvg)
        return vjp((dcqn, dckvn, dkr))
    return row_call(fn, [c_all, cs, sn, dcqn, dckvn, dkr], [qg, kvg], [(640, BF16)], [qg.shape, kvg.shape],
                    tb=tb, name=name)


def qrope_fwd(qb, c256, s256, *, name, tb=256):
    def fn(qb, c256, s256):
        return (t_qrope(qb, c256, s256),)
    return row_call(fn, [qb, c256, s256], [], [(MLA_HEADS * MLA_QPAD, BF16)], tb=tb, name=name)[0]


def qrope_bwd(dq, c256, s256, *, name, tb=256):
    def fn(dq, c256, s256):
        scale = MLA_QK ** -0.5
        a, b = [], []
        for h in range(MLA_HEADS):
            d = dq[:, h * MLA_QPAD:(h + 1) * MLA_QPAD] * scale
            a.append(d * c256)
            b.append(d * s256)
        return (jnp.concatenate(a + b, axis=1),)
    return row_call(fn, [dq, c256, s256], [], [(2 * MLA_HEADS * MLA_QPAD, BF16)], tb=tb, name=name)[0]


STAT_SPLIT = 64


def stats_fwd(do, o, lse, *, name, tb=512):
    def fn(do, o, lse):
        do, o = _f32(do, o)
        low = _iota((1, 128), 1) < STAT_SPLIT
        outs = []
        for h in range(MLA_HEADS):
            sl = slice(h * 128, (h + 1) * 128)
            dl = jnp.sum(do[:, sl] * o[:, sl], axis=-1, keepdims=True)
            outs.append(jnp.where(low, lse[:, sl], dl))
        return (jnp.concatenate(outs, axis=1),)
    return row_call(fn, [do, o, lse], [], [(2048, F32)], tb=tb, name=name)[0]


def headsum(dkr_h, *, name, tb=512):
    def fn(d):
        acc = d[:, :128]
        for h in range(1, MLA_HEADS):
            acc = acc + d[:, h * 128:(h + 1) * 128]
        return (acc,)
    return row_call(fn, [dkr_h], [], [(128, F32)], tb=tb, name=name)[0]


def loss_fwd(y, t, *, name, tb=512):
    def fn(y, t):
        e = y - t
        return e * (1.0 / D_MODEL), jnp.sum(e * e, axis=0, keepdims=True)
    return row_call(fn, [y, t], [], [(D_MODEL, F32)], [(1, D_MODEL)], tb=tb, name=name)


def conv_fwd(proj, w, b, *, name, tb=256):
    T = proj.shape[0]
    n = T // tb
    hb = tb // CHUNK
    C = SSD_CONV_CH

    def body(cur, prev, w_ref, b_ref, o_ref, scr):
        i = pl.program_id(0)
        scr[pl.ds(0, CHUNK), :] = jnp.where(i > 0, prev[...], 0.0)
        scr[pl.ds(CHUNK, tb), :] = cur[...]
        y = b_ref[...] + w_ref[3:4, :] * cur[...]
        for k in range(SSD_CONV - 1):
            y = y + w_ref[k:k + 1, :] * scr[pl.ds(CHUNK - (SSD_CONV - 1) + k, tb), :]
        o_ref[...] = y

    return pl.pallas_call(
        body, name=name, grid=(n,),
        in_specs=[pl.BlockSpec((tb, C), lambda i: (i, 2)),
                  pl.BlockSpec((CHUNK, C), lambda i: (jnp.maximum(i * hb - 1, 0), 2)),
                  pl.BlockSpec((SSD_CONV, C), lambda i: (0, 0)), pl.BlockSpec((1, C), lambda i: (0, 0))],
        out_specs=pl.BlockSpec((tb, C), lambda i: (i, 0)), out_shape=SDS((T, C), F32),
        scratch_shapes=[pltpu.VMEM((CHUNK + tb, C), F32)],
        compiler_params=pltpu.CompilerParams(dimension_semantics=("arbitrary",), vmem_limit_bytes=VMEM_LIMIT),
    )(proj, proj, w, b)


def conv_bwd(dpre, proj, w, *, name, tb=256):
    T = proj.shape[0]
    n = T // tb
    hb = tb // CHUNK
    nh = T // CHUNK
    C = SSD_CONV_CH

    def body(dcur, dnext, xcur, xprev, w_ref, dx_ref, dw_ref, db_ref, dscr, xscr):
        i = pl.program_id(0)

        @pl.when(i == 0)
        def _():
            dw_ref[...] = jnp.zeros_like(dw_ref)
            db_ref[...] = jnp.zeros_like(db_ref)

        d = dcur[...]
        dscr[pl.ds(0, tb), :] = d
        dscr[pl.ds(tb, CHUNK), :] = jnp.where(i < n - 1, dnext[...], 0.0)
        xscr[pl.ds(0, CHUNK), :] = jnp.where(i > 0, xprev[...], 0.0)
        xscr[pl.ds(CHUNK, tb), :] = xcur[...]
        dx = w_ref[3:4, :] * d
        for k in range(SSD_CONV - 1):
            dx = dx + w_ref[k:k + 1, :] * dscr[pl.ds(SSD_CONV - 1 - k, tb), :]
        dx_ref[...] = dx.astype(dx_ref.dtype)
        for k in range(SSD_CONV):
            xk = xscr[pl.ds(CHUNK - (SSD_CONV - 1) + k, tb), :]
            dw_ref[k:k + 1, :] += jnp.sum(d * xk, axis=0, keepdims=True)
        db_ref[...] += jnp.sum(d, axis=0, keepdims=True)

    return pl.pallas_call(
        body, name=name, grid=(n,),
        in_specs=[pl.BlockSpec((tb, C), lambda i: (i, 0)),
                  pl.BlockSpec((CHUNK, C), lambda i: (jnp.minimum((i + 1) * hb, nh - 1), 0)),
                  pl.BlockSpec((tb, C), lambda i: (i, 2)),
                  pl.BlockSpec((CHUNK, C), lambda i: (jnp.maximum(i * hb - 1, 0), 2)),
                  pl.BlockSpec((SSD_CONV, C), lambda i: (0, 0))],
        out_specs=[pl.BlockSpec((tb, C), lambda i: (i, 0)), pl.BlockSpec((SSD_CONV, C), lambda i: (0, 0)),
                   pl.BlockSpec((1, C), lambda i: (0, 0))],
        out_shape=[SDS((T, C), BF16), SDS((SSD_CONV, C), F32), SDS((1, C), F32)],
        scratch_shapes=[pltpu.VMEM((tb + CHUNK, C), F32), pltpu.VMEM((CHUNK + tb, C), F32)],
        compiler_params=pltpu.CompilerParams(dimension_semantics=("arbitrary",), vmem_limit_bytes=VMEM_LIMIT),
    )(dpre, dpre, proj, proj, w)


def _ssd_specs(nc, rev):
    def r(c):
        return nc - 1 - c if rev else c
    pre = pl.BlockSpec((CHUNK, SSD_CONV_CH), lambda c: (r(c), 0))
    dtr = pl.BlockSpec((CHUNK, LANES), lambda c: (r(c), HYB_MAIN // LANES))
    z = pl.BlockSpec((CHUNK, 1024), lambda c: (r(c), 2))
    row = pl.BlockSpec((CHUNK, 1024), lambda c: (r(c), 0))
    return pre, dtr, z, row


def _pspec(shape):
    return pl.BlockSpec(shape, lambda c: (0,) * len(shape))


def ssd_fwd(pre, proj, dt_bias, a_log, d_exp, norm_g, *, name):
    T = pre.shape[0]
    nc = T // CHUNK
    s_pre, s_dt, s_z, s_row = _ssd_specs(nc, False)

    def body(pre_r, dt_r, z_r, b_r, a_r, d_r, g_r, y_r, sv_r, st):
        @pl.when(pl.program_id(0) == 0)
        def _():
            st[...] = jnp.zeros_like(st)

        s0 = st[...]
        sv_r[...] = s0
        y, s1 = t_ssd(pre_r[...], dt_r[...], z_r[...], s0, b_r[...], a_r[...], d_r[...], g_r[...])
        y_r[...] = y.astype(y_r.dtype)
        st[...] = s1

    return pl.pallas_call(
        body, name=name, grid=(nc,),
        in_specs=[s_pre, s_dt, s_z, _pspec((1, LANES)), _pspec((1, LANES)), _pspec((1, 1024)), _pspec((1, 1024))],
        out_specs=[s_row, s_row], out_shape=[SDS((T, 1024), BF16), SDS((T, 1024), F32)],
        scratch_shapes=[pltpu.VMEM((SSD_STATE, 1024), F32)],
        compiler_params=pltpu.CompilerParams(dimension_semantics=("arbitrary",), vmem_limit_bytes=VMEM_LIMIT),
    )(pre, proj, proj, dt_bias, a_log, d_exp, norm_g)


def ssd_bwd(pre, proj, states, dyab, dt_bias, a_log, d_exp, norm_g, *, name):
    T = pre.shape[0]
    nc = T // CHUNK
    s_pre, s_dt, s_z, s_row = _ssd_specs(nc, True)
    s_dtout = pl.BlockSpec((CHUNK, LANES), lambda c: (nc - 1 - c, 0))
    s_dy = pl.BlockSpec((CHUNK, 1024), lambda c: (nc - 1 - c, 1))

    def body(pre_r, dt_r, z_r, sv_r, dy_r, b_r, a_r, d_r, g_r, dpre_r, ddt_r, dz_r, db_r, da_r, dd_r, dg_r, dst):
        @pl.when(pl.program_id(0) == 0)
        def _():
            dst[...] = jnp.zeros_like(dst)
            for r in (db_r, da_r, dd_r, dg_r):
                r[...] = jnp.zeros_like(r)

        _, vjp = jax.vjp(t_ssd, pre_r[...], dt_r[...], z_r[...], sv_r[...], b_r[...], a_r[...], d_r[...], g_r[...])
        dpre, ddt, dz, ds0, db, da, dd, dg = vjp((dy_r[...].astype(F32), dst[...]))
        dpre_r[...] = dpre
        ddt_r[...] = ddt.astype(ddt_r.dtype)
        dz_r[...] = dz.astype(dz_r.dtype)
        dst[...] = ds0
        db_r[...] += db
        da_r[...] += da
        dd_r[...] += dd
        dg_r[...] += dg

    return pl.pallas_call(
        body, name=name, grid=(nc,),
        in_specs=[s_pre, s_dt, s_z, s_row, s_dy, _pspec((1, LANES)), _pspec((1, LANES)), _pspec((1, 1024)),
                  _pspec((1, 1024))],
        out_specs=[s_pre, s_dtout, s_row, _pspec((1, LANES)), _pspec((1, LANES)), _pspec((1, 1024)), _pspec((1, 1024))],
        out_shape=[SDS((T, SSD_CONV_CH), F32), SDS((T, LANES), BF16), SDS((T, 1024), BF16),
                   SDS((1, LANES), F32), SDS((1, LANES), F32), SDS((1, 1024), F32), SDS((1, 1024), F32)],
        scratch_shapes=[pltpu.VMEM((SSD_STATE, 1024), F32)],
        compiler_params=pltpu.CompilerParams(dimension_semantics=("arbitrary",), vmem_limit_bytes=VMEM_LIMIT),
    )(pre, proj, proj, states, dyab, dt_bias, a_log, d_exp, norm_g)


def _attn_tile(T):
    return min(512, T // 2)


def _causal(tq):
    return _iota((tq, tq), 1) <= _iota((tq, tq), 0)


def attn_fwd(q, kv, kr, *, name):
    T = q.shape[0]
    tq = _attn_tile(T)
    nq = T // tq

    def body(q_ref, kn_ref, v_ref, kr_ref, o_ref, lse_ref):
        qi = pl.program_id(1)
        qv = q_ref[...]

        def blk(ki, masked, carry):
            m, l, acc = carry
            off = pl.multiple_of(ki * tq, tq)
            k = jnp.concatenate([kn_ref[pl.ds(off, tq), :], kr_ref[pl.ds(off, tq), :]], axis=1)
            s = lax.dot_general(qv, k, NT, preferred_element_type=F32)
            if masked:
                s = jnp.where(_causal(tq), s, -1e30)
            m_new = jnp.maximum(m, jnp.max(s, axis=-1, keepdims=True))
            p = jnp.exp(s - m_new)
            alpha = jnp.exp(m - m_new)
            l = alpha * l + jnp.sum(p, axis=-1, keepdims=True)
            acc = alpha * acc + jnp.dot(p.astype(BF16), v_ref[pl.ds(off, tq), :], preferred_element_type=F32)
            return m_new, l, acc

        init = (jnp.full((tq, 1), -1e30, F32), jnp.zeros((tq, 1), F32), jnp.zeros((tq, 128), F32))
        carry = lax.fori_loop(0, qi, lambda ki, c: blk(ki, False, c), init)
        m, l, acc = blk(qi, True, carry)
        o_ref[...] = (acc / l).astype(o_ref.dtype)
        lse_ref[...] = jnp.broadcast_to(m + jnp.log(l), (tq, 128))

    return pl.pallas_call(
        body, name=name, grid=(MLA_HEADS, nq),
        in_specs=[pl.BlockSpec((tq, MLA_QPAD), lambda h, i: (i, h)),
                  pl.BlockSpec((T, 128), lambda h, i: (0, h)),
                  pl.BlockSpec((T, 128), lambda h, i: (0, MLA_HEADS + h)),
                  pl.BlockSpec((T, 128), lambda h, i: (0, 0))],
        out_specs=[pl.BlockSpec((tq, 128), lambda h, i: (i, h)), pl.BlockSpec((tq, 128), lambda h, i: (i, h))],
        out_shape=[SDS((T, 2048), BF16), SDS((T, 2048), F32)],
        compiler_params=pltpu.CompilerParams(dimension_semantics=("parallel", "arbitrary"),
                                             vmem_limit_bytes=VMEM_LIMIT),
    )(q, kv, kv, kr)


def attn_bwd(q, kv, kr, do, stats, *, name):
    T = q.shape[0]
    tq = _attn_tile(T)
    nq = T // tq

    def body(q_ref, do_ref, st_ref, kn_ref, v_ref, kr_ref, dq_ref, dkn_ref, dv_ref, dkr_ref):
        ki = pl.program_id(1)

        @pl.when(ki == 0)
        def _():
            dq_ref[...] = jnp.zeros_like(dq_ref)

        k = jnp.concatenate([kn_ref[...], kr_ref[...]], axis=1)
        v = v_ref[...]

        def blk(qi, masked, carry):
            dk, dv = carry
            rows = pl.ds(pl.multiple_of(qi * tq, tq), tq)
            qv, dov = q_ref[rows, :], do_ref[rows, :]
            lse, dl = st_ref[rows, 0:1], st_ref[rows, STAT_SPLIT:STAT_SPLIT + 1]
            s = lax.dot_general(qv, k, NT, preferred_element_type=F32)
            if masked:
                s = jnp.where(_causal(tq), s, -1e30)
            p = jnp.exp(s - lse)
            dv = dv + lax.dot_general(p.astype(BF16), dov, TN, preferred_element_type=F32)
            dp = lax.dot_general(dov, v, NT, preferred_element_type=F32)
            ds = (p * (dp - dl)).astype(BF16)
            dk = dk + lax.dot_general(ds, qv, TN, preferred_element_type=F32)
            dq_ref[rows, :] += jnp.dot(ds, k, preferred_element_type=F32)
            return dk, dv

        carry = blk(ki, True, (jnp.zeros((tq, MLA_QPAD), F32), jnp.zeros((tq, 128), F32)))
        dk, dv = lax.fori_loop(ki + 1, nq, lambda qi, c: blk(qi, False, c), carry)
        dkn_ref[...] = dk[:, :128].astype(dkn_ref.dtype)
        dkr_ref[...] = dk[:, 128:]
        dv_ref[...] = dv.astype(dv_ref.dtype)

    return pl.pallas_call(
        body, name=name, grid=(MLA_HEADS, nq),
        in_specs=[pl.BlockSpec((T, MLA_QPAD), lambda h, i: (0, h)),
                  pl.BlockSpec((T, 128), lambda h, i: (0, h)),
                  pl.BlockSpec((T, 128), lambda h, i: (0, h)),
                  pl.BlockSpec((tq, 128), lambda h, i: (i, h)),
                  pl.BlockSpec((tq, 128), lambda h, i: (i, MLA_HEADS + h)),
                  pl.BlockSpec((tq, 128), lambda h, i: (i, 0))],
        out_specs=[pl.BlockSpec((T, MLA_QPAD), lambda h, i: (0, h)),
                   pl.BlockSpec((tq, 128), lambda h, i: (i, h)), pl.BlockSpec((tq, 128), lambda h, i: (i, h)),
                   pl.BlockSpec((tq, 128), lambda h, i: (i, h))],
        out_shape=[SDS((T, MLA_HEADS * MLA_QPAD), F32), SDS((T, 2048), BF16), SDS((T, 2048), BF16),
                   SDS((T, 2048), F32)],
        compiler_params=pltpu.CompilerParams(dimension_semantics=("parallel", "arbitrary"),
                                             vmem_limit_bytes=VMEM_LIMIT),
    )(q, do, stats, kv, kv, kr)


def _ffn_fwd(h, pre_g, w_in, w_down, post_g, tag):
    hn = rms_fwd(h, pre_g, name=f"{tag}_pre")
    gu = mm(hn, w_in, out_dtype=BF16, name=f"{tag}_in", tn=1408)
    a = swiglu_fwd(gu, name=f"{tag}_act")
    f = mm(a, w_down, name=f"{tag}_down", tk=1408)
    h2 = post_fwd(h, f, post_g, 0.5, name=f"{tag}_post")
    return h2, (h, hn, gu, a, f)


def _ffn_bwd(dh2, saved, pre_g, w_in, w_down, post_g, tag):
    h, hn, gu, a, f = saved
    df, dpost = post_bwd(f, dh2, post_g, 0.5, name=f"{tag}_post_b")
    da = mm(df, w_down, tb=True, name=f"{tag}_down_bx", tn=1408)
    dw_down = mm(a, df, ta=True, out_dtype=BF16, name=f"{tag}_down_bw", tm=1408)
    dgu = swiglu_bwd(gu, da, name=f"{tag}_act_b")
    dhn = mm(dgu, w_in, tb=True, name=f"{tag}_in_bx", tk=1408)
    dw_in = mm(hn, dgu, ta=True, out_dtype=BF16, name=f"{tag}_in_bw", tn=1408)
    dh, dpre = rms_bwd(h, dhn, dh2, pre_g, name=f"{tag}_pre_b")
    return dh, dpre, dw_in, dw_down, dpost


def _hyb_fwd(hn, w, tag):
    proj = mm(hn, w["hyb_in"], name=f"{tag}_in")
    ya = gmlp_fwd(proj, w["ln_g"], w["ln_b"], w["w_s"], w["b_st"], name=f"{tag}_gmlp")
    pre = conv_fwd(proj, w["conv_w"], w["conv_b"], name=f"{tag}_conv")
    yb, states = ssd_fwd(pre, proj, w["dt_bias"], w["a_log"], w["d_exp"], w["norm_g"], name=f"{tag}_ssd")
    yab = jnp.concatenate([ya, yb], axis=1)
    mixed = mm(yab, w["hyb_out"], name=f"{tag}_out")
    return mixed, (proj, pre, states, yab)


def _hyb_bwd(dmixed, hn, saved, w, tag):
    proj, pre, states, yab = saved
    dyab = mm(dmixed, w["hyb_out"], tb=True, name=f"{tag}_out_bx")
    dw_out = mm(yab, dmixed, ta=True, out_dtype=BF16, name=f"{tag}_out_bw")
    duv, dln_g, dln_b, dw_s, db_st = gmlp_bwd(proj, dyab, w["ln_g"], w["ln_b"], w["w_s"], w["b_st"],
                                              name=f"{tag}_gmlp_b")
    dpre, ddt, dz, ddt_bias, da_log, dd_exp, dnorm_g = ssd_bwd(
        pre, proj, states, dyab, w["dt_bias"], w["a_log"], w["d_exp"], w["norm_g"], name=f"{tag}_ssd_b")
    dxbc, dconv_w, dconv_b = conv_bwd(dpre, proj, w["conv_w"], name=f"{tag}_conv_b")
    pad = jnp.zeros((duv.shape[0], HYB_PAD - HYB_MAIN - LANES), BF16)
    dproj = jnp.concatenate([duv, dz, dxbc, ddt, pad], axis=1)
    dhn = mm(dproj, w["hyb_in"], tb=True, name=f"{tag}_in_bx")
    dw_in = mm(hn, dproj, ta=True, name=f"{tag}_in_bw")
    g = dict(hyb_in=dw_in, hyb_out=dw_out, ln_g=dln_g, ln_b=dln_b, w_s=dw_s, b_st=db_st, conv_w=dconv_w,
             conv_b=dconv_b, dt_bias=ddt_bias, a_log=da_log, d_exp=dd_exp, norm_g=dnorm_g)
    return dhn, g


def _mla_fwd(hn, w, rope, tag):
    cs, sn, c256, s256 = rope
    c_all = mm(hn, w["mla_in"], name=f"{tag}_in")
    cqn, ckvn, kr = kprep_fwd(c_all, cs, sn, w["q_g"], w["kv_g"], name=f"{tag}_kprep")
    qb = mm(cqn, w["uq"], name=f"{tag}_uq")
    q = qrope_fwd(qb, c256, s256, name=f"{tag}_qrope")
    kv = mm(ckvn, w["ukv"], out_dtype=BF16, name=f"{tag}_ukv")
    o, lse = attn_fwd(q, kv, kr, name=f"{tag}_attn")
    mixed = mm(o, w["mla_out"], name=f"{tag}_out")
    return mixed, (c_all, cqn, ckvn, kr, q, kv, o, lse)


def _mla_bwd(dmixed, hn, saved, w, rope, tag):
    cs, sn, c256, s256 = rope
    c_all, cqn, ckvn, kr, q, kv, o, lse = saved
    do = mm(dmixed, w["mla_out"], tb=True, out_dtype=BF16, name=f"{tag}_out_bx")
    dw_out = mm(o, dmixed, ta=True, out_dtype=BF16, name=f"{tag}_out_bw")
    stats = stats_fwd(do, o, lse, name=f"{tag}_stats")
    dq, dkn, dv, dkr_h = attn_bwd(q, kv, kr, do, stats, name=f"{tag}_attn_b")
    dkr = headsum(dkr_h, name=f"{tag}_dkr")
    dkv = jnp.concatenate([dkn, dv], axis=1)
    dckvn = mm(dkv, w["ukv"], tb=True, name=f"{tag}_ukv_bx")
    dw_ukv = mm(ckvn, dkv, ta=True, name=f"{tag}_ukv_bw")
    dqb = qrope_bwd(dq, c256, s256, name=f"{tag}_qrope_b")
    dcqn = mm(dqb, w["uq"], tb=True, name=f"{tag}_uq_bx")
    dw_uq = mm(cqn, dqb, ta=True, name=f"{tag}_uq_bw")
    dc_all, dq_g, dkv_g = kprep_bwd(c_all, cs, sn, dcqn, dckvn, dkr, w["q_g"], w["kv_g"], name=f"{tag}_kprep_b")
    dhn = mm(dc_all, w["mla_in"], tb=True, name=f"{tag}_in_bx")
    dw_in = mm(hn, dc_all, ta=True, name=f"{tag}_in_bw")
    g = dict(mla_in=dw_in, mla_out=dw_out, uq=dw_uq, ukv=dw_ukv, q_g=dq_g, kv_g=dkv_g)
    return dhn, g


def local_step(x, p, rope, target, lw):
    h = x
    saved = []
    for i in range(DEPTH):
        w = lw[i]
        t = f"l{i}"
        h, s1 = _ffn_fwd(h, w["ffn1_pre_g"], w["ffn1_w_in"], w["ffn1_w_down"], w["ffn1_post_g"], f"{t}_f1")
        h1 = h
        hn = rms_fwd(h1, w["mix_pre_g"], name=f"{t}_mixpre")
        if i % 2 == 0:
            mixed, sm = _hyb_fwd(hn, w, f"{t}_hyb")
        else:
            mixed, sm = _mla_fwd(hn, w, rope, f"{t}_mla")
        h = post_fwd(h1, mixed, w["mix_post_g"], 1.0, name=f"{t}_mixpost")
        h, s2 = _ffn_fwd(h, w["ffn2_pre_g"], w["ffn2_w_in"], w["ffn2_w_down"], w["ffn2_post_g"], f"{t}_f2")
        h3 = h
        hn3 = rms_fwd(h3, w["ple_pre_g"], name=f"{t}_plepre")
        gl = mm(hn3, w["ple_w_gate"], name=f"{t}_plegate")
        pp = mm((p, i), w["ple_w_proj"], name=f"{t}_pleproj")
        h = ple_fwd(h3, gl, pp, w["ple_post_g"], name=f"{t}_plepost")
        saved.append((s1, h1, hn, sm, mixed, s2, h3, hn3, gl, pp))

    dh, sq = loss_fwd(h, target, name="loss")
    grads = [None] * DEPTH
    for i in reversed(range(DEPTH)):
        w = lw[i]
        t = f"l{i}"
        s1, h1, hn, sm, mixed, s2, h3, hn3, gl, pp = saved[i]
        g = {}
        dgl, dpp, g["ple_post_g"] = ple_bwd(gl, pp, dh, w["ple_post_g"], name=f"{t}_plepost_b")
        dhn3 = mm(dgl, w["ple_w_gate"], tb=True, name=f"{t}_plegate_bx")
        g["ple_w_gate"] = mm(hn3, dgl, ta=True, out_dtype=BF16, name=f"{t}_plegate_bw")
        g["ple_w_proj"] = mm((p, i), dpp, ta=True, out_dtype=BF16, name=f"{t}_pleproj_bw")
        dh, g["ple_pre_g"] = rms_bwd(h3, dhn3, dh, w["ple_pre_g"], name=f"{t}_plepre_b")
        dh, g["ffn2_pre_g"], g["ffn2_w_in"], g["ffn2_w_down"], g["ffn2_post_g"] = _ffn_bwd(
            dh, s2, w["ffn2_pre_g"], w["ffn2_w_in"], w["ffn2_w_down"], w["ffn2_post_g"], f"{t}_f2")
        dmixed, g["mix_post_g"] = post_bwd(mixed, dh, w["mix_post_g"], 1.0, name=f"{t}_mixpost_b")
        if i % 2 == 0:
            dhn, gm = _hyb_bwd(dmixed, hn, sm, w, f"{t}_hyb")
        else:
            dhn, gm = _mla_bwd(dmixed, hn, sm, w, rope, f"{t}_mla")
        g.update(gm)
        dh, g["mix_pre_g"] = rms_bwd(h1, dhn, dh, w["mix_pre_g"], name=f"{t}_mixpre_b")
        dh, g["ffn1_pre_g"], g["ffn1_w_in"], g["ffn1_w_down"], g["ffn1_post_g"] = _ffn_bwd(
            dh, s1, w["ffn1_pre_g"], w["ffn1_w_in"], w["ffn1_w_down"], w["ffn1_post_g"], f"{t}_f1")
        grads[i] = g
    return sq, dh, grads


def _zeros_like_cols(a, n):
    return jnp.zeros(a.shape[:-1] + (n,), a.dtype)


def layer_weights(full, i):
    j = i // 2
    row = lambda v: v.reshape(1, -1)
    w = {k: row(full[k][i]) for k in ("ffn1_pre_g", "ffn1_post_g", "mix_pre_g", "mix_post_g", "ffn2_pre_g",
                                      "ffn2_post_g", "ple_pre_g", "ple_post_g")}
    for k in ("ffn1_w_in", "ffn1_w_down", "ffn2_w_in", "ffn2_w_down", "ple_w_gate", "ple_w_proj"):
        w[k] = (full[k], i)
    if i % 2 == 0:
        hw = full["hyb_w_in"][j]
        w["hyb_in"] = jnp.concatenate([hw, _zeros_like_cols(hw, HYB_PAD - HYB_IN)], axis=1)
        w["hyb_out"] = (full["hyb_w_out"], j)
        w["ln_g"], w["ln_b"] = row(full["gm_ln_g"][j]), row(full["gm_ln_b"][j])
        w["w_s"] = full["gm_w_s"][j].reshape(GM_HEADS * CHUNK, CHUNK)
        w["b_st"] = jnp.pad(full["gm_b_s"][j].T, ((0, 0), (0, LANES - GM_HEADS)))
        w["conv_w"], w["conv_b"] = full["ssd_conv_w"][j], row(full["ssd_conv_b"][j])
        pad16 = lambda v: jnp.pad(v.reshape(1, -1), ((0, 0), (0, LANES - SSD_HEADS)))
        w["dt_bias"], w["a_log"] = pad16(full["ssd_dt_bias"][j]), pad16(full["ssd_a_log"][j])
        w["d_exp"] = row(jnp.repeat(full["ssd_d"][j], SSD_HEAD_DIM))
        w["norm_g"] = row(full["ssd_norm_g"][j])
    else:
        wi = full["mla_w_in"][j]
        z64 = _zeros_like_cols(wi, 64)
        w["mla_in"] = jnp.concatenate([wi[:, :384], wi[:, 384:448], z64, -wi[:, 416:448], wi[:, 384:416], z64], axis=1)
        uq = full["mla_w_uq"][j].reshape(MLA_Q_LORA, MLA_HEADS, MLA_QK)
        zq = jnp.zeros((MLA_Q_LORA, MLA_HEADS, 64), uq.dtype)
        pad_part = jnp.concatenate([uq, zq], axis=2)
        swp_part = jnp.concatenate([jnp.zeros_like(uq[:, :, :128]), -uq[:, :, 160:192], uq[:, :, 128:160], zq], axis=2)
        w["uq"] = jnp.concatenate([pad_part.reshape(MLA_Q_LORA, -1), swp_part.reshape(MLA_Q_LORA, -1)], axis=1)
        ukv = full["mla_w_ukv"][j].reshape(MLA_KV_LORA, MLA_HEADS, 256)
        w["ukv"] = jnp.concatenate([ukv[:, :, :128].reshape(MLA_KV_LORA, -1), ukv[:, :, 128:].reshape(MLA_KV_LORA, -1)],
                                   axis=1)
        w["mla_out"] = (full["mla_w_out"], j)
        w["q_g"], w["kv_g"] = row(full["mla_q_norm_g"][j]), row(full["mla_kv_norm_g"][j])
    return w


def full_grads(grads):
    out = {}
    stack = lambda k, idx: jnp.stack([grads[i][k] for i in idx])
    every, even, odd = range(DEPTH), range(0, DEPTH, 2), range(1, DEPTH, 2)
    for k in ("ffn1_pre_g", "ffn1_post_g", "mix_pre_g", "mix_post_g", "ffn2_pre_g", "ffn2_post_g", "ple_pre_g",
              "ple_post_g"):
        out[k] = stack(k, every).reshape(DEPTH, D_MODEL)
    for k in ("ffn1_w_in", "ffn1_w_down", "ffn2_w_in", "ffn2_w_down", "ple_w_gate", "ple_w_proj"):
        out[k] = stack(k, every)
    out["hyb_w_in"] = stack("hyb_in", even)[:, :, :HYB_IN]
    out["hyb_w_out"] = stack("hyb_out", even)
    out["gm_ln_g"] = stack("ln_g", even).reshape(2, 1024)
    out["gm_ln_b"] = stack("ln_b", even).reshape(2, 1024)
    out["gm_w_s"] = stack("w_s", even).reshape(2, GM_HEADS, CHUNK, CHUNK)
    out["gm_b_s"] = jnp.swapaxes(stack("b_st", even)[:, :, :GM_HEADS], 1, 2)
    out["ssd_conv_w"] = stack("conv_w", even)
    out["ssd_conv_b"] = stack("conv_b", even).reshape(2, SSD_CONV_CH)
    out["ssd_dt_bias"] = stack("dt_bias", even)[:, 0, :SSD_HEADS]
    out["ssd_a_log"] = stack("a_log", even)[:, 0, :SSD_HEADS]
    out["ssd_d"] = stack("d_exp", even).reshape(2, SSD_HEADS, SSD_HEAD_DIM).sum(axis=-1)
    out["ssd_norm_g"] = stack("norm_g", even).reshape(2, 1024)
    dwi = stack("mla_in", odd)
    out["mla_w_in"] = jnp.concatenate([dwi[:, :, :384], dwi[:, :, 384:416] + dwi[:, :, 544:576],
                                       dwi[:, :, 416:448] - dwi[:, :, 512:544]], axis=2)
    duq = stack("uq", odd)
    half = MLA_HEADS * MLA_QPAD
    dp = duq[:, :, :half].reshape(2, MLA_Q_LORA, MLA_HEADS, MLA_QPAD)
    ds = duq[:, :, half:].reshape(2, MLA_Q_LORA, MLA_HEADS, MLA_QPAD)
    out["mla_w_uq"] = jnp.concatenate([dp[..., :128], dp[..., 128:160] + ds[..., 160:192],
                                       dp[..., 160:192] - ds[..., 128:160]], axis=-1).reshape(2, MLA_Q_LORA, -1)
    dukv = stack("ukv", odd)
    dk = dukv[:, :, :2048].reshape(2, MLA_KV_LORA, MLA_HEADS, 128)
    dv = dukv[:, :, 2048:].reshape(2, MLA_KV_LORA, MLA_HEADS, 128)
    out["mla_w_ukv"] = jnp.concatenate([dk, dv], axis=-1).reshape(2, MLA_KV_LORA, -1)
    out["mla_w_out"] = stack("mla_out", odd)
    out["mla_q_norm_g"] = stack("q_g", odd).reshape(2, MLA_Q_LORA)
    out["mla_kv_norm_g"] = stack("kv_g", odd).reshape(2, MLA_KV_LORA)
    return out


def rope_tables(positions):
    T = positions.shape[0]
    inv = 1.0 / (ROPE_BASE ** (jnp.arange(0, MLA_ROPE, 2, dtype=F32) / MLA_ROPE))
    ang = positions.astype(F32)[:, None] * inv
    cos, sin = jnp.cos(ang), jnp.sin(ang)
    z64 = jnp.zeros((T, 64), F32)
    cs = jnp.concatenate([cos, cos, z64], axis=1)
    sn = jnp.concatenate([sin, sin, z64], axis=1)
    c256 = jnp.concatenate([jnp.ones((T, 128), F32), cs], axis=1)
    s256 = jnp.concatenate([jnp.zeros((T, 128), F32), sn], axis=1)
    return cs, sn, c256, s256


def _rows(n):
    return -(-n // LANES)


def _pack(pieces, dtype, row_multiple):
    flat = []
    total = 0
    for a in pieces:
        v = a.reshape(-1).astype(dtype)
        padn = _rows(v.shape[0]) * LANES - v.shape[0]
        if padn:
            v = jnp.concatenate([v, jnp.zeros((padn,), dtype)])
        flat.append(v)
        total += v.shape[0] // LANES
    tail = -total % row_multiple
    if tail:
        flat.append(jnp.zeros((tail * LANES,), dtype))
    return jnp.concatenate(flat).reshape(-1, LANES)


def _unpack(slab, shapes):
    out = []
    r = 0
    for s in shapes:
        n = int(np.prod(s))
        nr = _rows(n)
        out.append(slab[r:r + nr].reshape(-1)[:n].reshape(s))
        r += nr
    return out


def _shard_shape(shape, ax):
    if ax is None:
        return tuple(shape)
    s = list(shape)
    s[ax] //= N_CHIPS
    return tuple(s)


def _chip_slice(a, ax, k):
    if ax is None:
        return a
    n = a.shape[ax] // N_CHIPS
    return lax.slice_in_dim(a, k * n, (k + 1) * n, axis=ax)


def _plane_peers():
    x, y, c = lax.axis_index("x"), lax.axis_index("y"), lax.axis_index("c")
    return (x, y, c), [(1 - x, y, c), (x, 1 - y, c), (1 - x, 1 - y, c)]


ANY = pl.BlockSpec(memory_space=pl.ANY)


def plane_allgather(slab):
    R = slab.shape[0]

    def body(src, out, send_sems, recv_sems, local_sem):
        (x, y, c), peers = _plane_peers()
        me = 2 * x + y
        local = pltpu.make_async_copy(src, out.at[me], local_sem)
        local.start()
        copies = []
        for j, peer in enumerate(peers):
            cp = pltpu.make_async_remote_copy(src_ref=src, dst_ref=out.at[me], send_sem=send_sems.at[j],
                                              recv_sem=recv_sems.at[j], device_id=peer, device_id_type=MESH)
            cp.start()
            copies.append(cp)
        for cp in copies:
            cp.wait()
        local.wait()

    return pl.pallas_call(
        body, name="plane_allgather", out_shape=SDS((N_CHIPS, R, LANES), slab.dtype),
        in_specs=[ANY], out_specs=ANY,
        scratch_shapes=[pltpu.SemaphoreType.DMA((3,)), pltpu.SemaphoreType.DMA((3,)), pltpu.SemaphoreType.DMA],
    )(slab)


def plane_alltoall(buf):
    R = buf.shape[1]

    def body(src, out, send_sems, recv_sems, local_sem):
        (x, y, c), peers = _plane_peers()
        me = 2 * x + y
        local = pltpu.make_async_copy(src.at[me], out.at[me], local_sem)
        local.start()
        copies = []
        for j, peer in enumerate(peers):
            cp = pltpu.make_async_remote_copy(src_ref=src.at[2 * peer[0] + peer[1]], dst_ref=out.at[me],
                                              send_sem=send_sems.at[j], recv_sem=recv_sems.at[j], device_id=peer,
                                              device_id_type=MESH)
            cp.start()
            copies.append(cp)
        for cp in copies:
            cp.wait()
        local.wait()

    return pl.pallas_call(
        body, name="plane_alltoall", out_shape=SDS((N_CHIPS, R, LANES), buf.dtype),
        in_specs=[ANY], out_specs=ANY,
        scratch_shapes=[pltpu.SemaphoreType.DMA((3,)), pltpu.SemaphoreType.DMA((3,)), pltpu.SemaphoreType.DMA],
    )(buf)


def sibling_swap(buf):
    def body(src, out, send_sem, recv_sem):
        x, y, c = lax.axis_index("x"), lax.axis_index("y"), lax.axis_index("c")
        cp = pltpu.make_async_remote_copy(src_ref=src, dst_ref=out, send_sem=send_sem, recv_sem=recv_sem,
                                          device_id=(x, y, 1 - c), device_id_type=MESH)
        cp.start()
        cp.wait()

    return pl.pallas_call(
        body, name="sibling_swap", out_shape=SDS(buf.shape, buf.dtype), in_specs=[ANY], out_specs=ANY,
        scratch_shapes=[pltpu.SemaphoreType.DMA, pltpu.SemaphoreType.DMA],
    )(buf)


def _chip_block(ref, ax, k, n):
    start = pl.multiple_of(k * n, n)
    return ref.at[:, pl.ds(start, n), :] if ax == 1 else ref.at[:, :, pl.ds(start, n)]


def gather_reg(shards, axes):
    n = len(shards)
    fulls = []
    for s, ax in zip(shards, axes):
        shape = list(s.shape)
        shape[ax] *= N_CHIPS
        fulls.append(SDS(tuple(shape), s.dtype))

    def body(*refs):
        srcs, outs = refs[:n], refs[n:2 * n]
        send_sems, recv_sems, local_sems = refs[2 * n:]
        (x, y, c), peers = _plane_peers()
        me = 2 * x + y
        copies = []
        for t in range(n):
            dst = _chip_block(outs[t], axes[t], me, srcs[t].shape[axes[t]])
            local = pltpu.make_async_copy(srcs[t], dst, local_sems.at[t])
            local.start()
            copies.append(local)
            for j, peer in enumerate(peers):
                cp = pltpu.make_async_remote_copy(src_ref=srcs[t], dst_ref=dst, send_sem=send_sems.at[3 * t + j],
                                                  recv_sem=recv_sems.at[3 * t + j], device_id=peer, device_id_type=MESH)
                cp.start()
                copies.append(cp)
        for cp in copies:
            cp.wait()

    return pl.pallas_call(
        body, name="gather_reg", out_shape=fulls, in_specs=[ANY] * n, out_specs=[ANY] * n,
        scratch_shapes=[pltpu.SemaphoreType.DMA((3 * n,)), pltpu.SemaphoreType.DMA((3 * n,)),
                        pltpu.SemaphoreType.DMA((n,))],
    )(*shards)


def exchange_reg(grads, axes):
    n = len(grads)
    outs = []
    for g, ax in zip(grads, axes):
        shape = list(g.shape)
        shape[ax] //= N_CHIPS
        outs.append(SDS((N_CHIPS,) + tuple(shape), g.dtype))

    def body(*refs):
        srcs, mine, theirs = refs[:n], refs[n:2 * n], refs[2 * n:3 * n]
        send_sems, recv_sems, local_sems, fsend_sems, frecv_sems = refs[3 * n:]
        (x, y, c), peers = _plane_peers()
        sibling = (x, y, 1 - c)
        me = 2 * x + y
        pidx = [2 * px + py for (px, py, _) in peers]
        sends, locals_ = [], []
        for t in range(n):
            ns = mine[t].shape[axes[t] + 1]
            local = pltpu.make_async_copy(_chip_block(srcs[t], axes[t], me, ns), mine[t].at[me], local_sems.at[t])
            local.start()
            locals_.append(local)
            for j, peer in enumerate(peers):
                cp = pltpu.make_async_remote_copy(src_ref=_chip_block(srcs[t], axes[t], pidx[j], ns),
                                                  dst_ref=mine[t].at[me], send_sem=send_sems.at[3 * t + j],
                                                  recv_sem=recv_sems.at[3 * t + j], device_id=peer, device_id_type=MESH)
                cp.start()
                sends.append(cp)
        forwards = []
        for t in range(n):
            locals_[t].wait()
            blocks = [me] + pidx
            for q, blk in enumerate(blocks):
                if q > 0:
                    sends[3 * t + q - 1].wait_recv()
                fw = pltpu.make_async_remote_copy(src_ref=mine[t].at[blk], dst_ref=theirs[t].at[blk],
                                                  send_sem=fsend_sems.at[4 * t + q], recv_sem=frecv_sems.at[4 * t + q],
                                                  device_id=sibling, device_id_type=MESH)
                fw.start()
                forwards.append(fw)
        for cp in sends:
            cp.wait_send()
        for fw in forwards:
            fw.wait()

    return pl.pallas_call(
        body, name="exchange_reg", out_shape=outs + outs, in_specs=[ANY] * n, out_specs=[ANY] * (2 * n),
        scratch_shapes=[pltpu.SemaphoreType.DMA((3 * n,)), pltpu.SemaphoreType.DMA((3 * n,)),
                        pltpu.SemaphoreType.DMA((n,)), pltpu.SemaphoreType.DMA((4 * n,)),
                        pltpu.SemaphoreType.DMA((4 * n,))],
    )(*grads)


def _adam_update(g, w, m, v):
    mn = ADAM_B1 * m + (1.0 - ADAM_B1) * g
    vn = ADAM_B2 * v + (1.0 - ADAM_B2) * jnp.square(g)
    m_hat = mn / (1.0 - ADAM_B1 ** ADAM_STEP)
    v_hat = vn / (1.0 - ADAM_B2 ** ADAM_STEP)
    return -ADAM_LR * (m_hat / (jnp.sqrt(v_hat) + ADAM_EPS) + ADAM_WD * w), mn, vn


def adamw_reg(mine, theirs, w, m, v, *, name):
    L, rs, cs = w.shape
    tr = rs
    for cand in range(256, 15, -16):
        if rs % cand == 0:
            tr = cand
            break

    def body(a_r, b_r, w_r, m_r, v_r, g_o, d_o, m_o, v_o):
        pa = a_r[0].astype(F32)
        pb = b_r[0].astype(F32)
        for k in range(1, N_CHIPS):
            pa = pa + a_r[k].astype(F32)
            pb = pb + b_r[k].astype(F32)
        g = pa + pb
        d, mn, vn = _adam_update(g, w_r[...], m_r[...], v_r[...])
        g_o[...] = g
        d_o[...] = d
        m_o[...] = mn
        v_o[...] = vn

    s4 = pl.BlockSpec((N_CHIPS, None, tr, cs), lambda l, i: (0, l, i, 0))
    s1 = pl.BlockSpec((None, tr, cs), lambda l, i: (l, i, 0))
    return pl.pallas_call(
        body, name=name, grid=(L, rs // tr), in_specs=[s4, s4, s1, s1, s1], out_specs=[s1] * 4,
        out_shape=[SDS((L, rs, cs), F32)] * 4,
        compiler_params=pltpu.CompilerParams(dimension_semantics=("parallel", "parallel"),
                                             vmem_limit_bytes=VMEM_LIMIT),
    )(mine, theirs, w, m, v)


def plane_sum(r4):
    R = r4.shape[1]

    def body(r_ref, o_ref):
        acc = r_ref[0].astype(F32)
        for k in range(1, N_CHIPS):
            acc = acc + r_ref[k].astype(F32)
        o_ref[...] = acc

    return pl.pallas_call(
        body, name="plane_sum", grid=(R // PACK_ROWS,),
        in_specs=[pl.BlockSpec((N_CHIPS, PACK_ROWS, LANES), lambda i: (0, i, 0))],
        out_specs=pl.BlockSpec((PACK_ROWS, LANES), lambda i: (i, 0)), out_shape=SDS((R, LANES), F32),
        compiler_params=pltpu.CompilerParams(dimension_semantics=("parallel",)),
    )(r4)


def adamw(pa, pb, w, m, v):
    R = w.shape[0]

    def body(pa_r, pb_r, w_r, m_r, v_r, g_o, d_o, m_o, v_o):
        g = pa_r[...] + pb_r[...]
        d, mn, vn = _adam_update(g, w_r[...], m_r[...], v_r[...])
        g_o[...] = g
        d_o[...] = d
        m_o[...] = mn
        v_o[...] = vn

    spec = pl.BlockSpec((PACK_ROWS, LANES), lambda i: (i, 0))
    return pl.pallas_call(
        body, name="adamw", grid=(R // PACK_ROWS,), in_specs=[spec] * 5, out_specs=[spec] * 4,
        out_shape=[SDS((R, LANES), F32)] * 4,
        compiler_params=pltpu.CompilerParams(dimension_semantics=("parallel",)),
    )(pa, pb, w, m, v)


def gather_weights(wl):
    full = dict(zip(REG, gather_reg([wl[n].astype(BF16) for n in REG], [WSPEC[n][2] for n in REG])))
    sharded = [n for n in MISC if WSPEC[n][2] is not None]
    pieces = [wl[n].astype(BF16) if WSPEC[n][3] else lax.bitcast_convert_type(wl[n], BF16) for n in sharded]
    got = plane_allgather(_pack(pieces, BF16, 16))
    shapes = [v.shape for v in pieces]
    per_chip = [_unpack(got[k], shapes) for k in range(N_CHIPS)]
    for idx, n in enumerate(sharded):
        parts = [per_chip[k][idx] for k in range(N_CHIPS)]
        if not WSPEC[n][3]:
            parts = [lax.bitcast_convert_type(v, F32) for v in parts]
        full[n] = jnp.concatenate(parts, axis=WSPEC[n][2])
    for n in MISC:
        if WSPEC[n][2] is None:
            full[n] = wl[n]
    return full


def kernel(x, p, positions, ffn1_pre_g, ffn1_w_in, ffn1_w_down, ffn1_post_g, mix_pre_g, mix_post_g, ffn2_pre_g, ffn2_w_in, ffn2_w_down, ffn2_post_g, ple_pre_g, ple_w_gate, ple_w_proj, ple_post_g, hyb_w_in, gm_ln_g, gm_ln_b, gm_w_s, gm_b_s, ssd_conv_w, ssd_conv_b, ssd_dt_bias, ssd_a_log, ssd_d, ssd_norm_g, hyb_w_out, mla_w_in, mla_q_norm_g, mla_kv_norm_g, mla_w_uq, mla_w_ukv, mla_w_out, loss_target, m_ffn1_pre_g, m_ffn1_w_in, m_ffn1_w_down, m_ffn1_post_g, m_mix_pre_g, m_mix_post_g, m_ffn2_pre_g, m_ffn2_w_in, m_ffn2_w_down, m_ffn2_post_g, m_ple_pre_g, m_ple_w_gate, m_ple_w_proj, m_ple_post_g, m_hyb_w_in, m_gm_ln_g, m_gm_ln_b, m_gm_w_s, m_gm_b_s, m_ssd_conv_w, m_ssd_conv_b, m_ssd_dt_bias, m_ssd_a_log, m_ssd_d, m_ssd_norm_g, m_hyb_w_out, m_mla_w_in, m_mla_q_norm_g, m_mla_kv_norm_g, m_mla_w_uq, m_mla_w_ukv, m_mla_w_out, v_ffn1_pre_g, v_ffn1_w_in, v_ffn1_w_down, v_ffn1_post_g, v_mix_pre_g, v_mix_post_g, v_ffn2_pre_g, v_ffn2_w_in, v_ffn2_w_down, v_ffn2_post_g, v_ple_pre_g, v_ple_w_gate, v_ple_w_proj, v_ple_post_g, v_hyb_w_in, v_gm_ln_g, v_gm_ln_b, v_gm_w_s, v_gm_b_s, v_ssd_conv_w, v_ssd_conv_b, v_ssd_dt_bias, v_ssd_a_log, v_ssd_d, v_ssd_norm_g, v_hyb_w_out, v_mla_w_in, v_mla_q_norm_g, v_mla_kv_norm_g, v_mla_w_uq, v_mla_w_ukv, v_mla_w_out):
    args = locals()
    wl = {n: args[n] for n in WNAMES}
    ml = {n: args["m_" + n] for n in WNAMES}
    vl = {n: args["v_" + n] for n in WNAMES}

    full = gather_weights(wl)
    lw = [layer_weights(full, i) for i in range(DEPTH)]
    rope = rope_tables(positions[0])
    T = x.shape[1]
    sq, dx, grads = local_step(x[0], p.reshape(DEPTH, T, p.shape[-1]), rope, loss_target[0], lw)
    loss = lax.psum(0.5 * jnp.sum(sq) / D_MODEL, ("x", "y", "c"))

    fg = full_grads(grads)
    res = {}
    both = exchange_reg([fg[n] for n in REG], [WSPEC[n][2] for n in REG])
    for t, n in enumerate(REG):
        res[n] = adamw_reg(both[t], both[len(REG) + t], wl[n], ml[n], vl[n], name=f"adamw_{n}")
    dest = [_pack([_chip_slice(fg[n], WSPEC[n][2], k) for n in MISC], BF16, PACK_ROWS) for k in range(N_CHIPS)]
    mine = plane_sum(plane_alltoall(jnp.stack(dest)))
    other = sibling_swap(mine)
    slabs = adamw(mine, other, *[_pack([d[n] for n in MISC], F32, PACK_ROWS) for d in (wl, ml, vl)])
    shapes = [wl[n].shape for n in MISC]
    unpacked = [_unpack(s, shapes) for s in slabs]
    for idx, n in enumerate(MISC):
        res[n] = [u[idx] for u in unpacked]
    return (loss, dx[None], *[res[n][k] for k in range(4) for n in WNAMES])
```

```python
import functools
import math

import jax
import jax.numpy as jnp
import numpy as np
from jax import lax
from jax.experimental import pallas as pl
from jax.experimental.pallas import tpu as pltpu

F32 = jnp.float32
BF16 = jnp.bfloat16
SDS = jax.ShapeDtypeStruct
MESH = pl.DeviceIdType.MESH
HIGHEST = lax.Precision.HIGHEST

D_MODEL = 1024
DEPTH = 4
D_FF = 2816
NORM_EPS = 1e-6
LN_EPS = 1e-5
GM_HEADS = 8
CHUNK = 128
SSD_HEADS = 16
SSD_HEAD_DIM = 64
SSD_INNER = 1024
SSD_STATE = 128
SSD_CONV = 4
SSD_CONV_CH = 1536
HYB_MAIN = 4608
HYB_IN = 4624
HYB_PAD = 5120
MLA_HEADS = 16
MLA_NOPE = 128
MLA_ROPE = 64
MLA_QK = 192
MLA_QPAD = 256
MLA_Q_LORA = 256
MLA_KV_LORA = 128
ROPE_BASE = 10000.0
ADAM_LR = 0.001
ADAM_B1 = 0.9
ADAM_B2 = 0.999
ADAM_EPS = 1e-08
ADAM_WD = 0.01
ADAM_STEP = 10

N_CHIPS = 4
LANES = 128
VMEM_LIMIT = 56 * 1024 * 1024
PACK_ROWS = 2048

WEIGHTS = [
    ("ffn1_pre_g", (4, 1024), None, False),
    ("ffn1_w_in", (4, 1024, 5632), 2, True),
    ("ffn1_w_down", (4, 2816, 1024), 1, True),
    ("ffn1_post_g", (4, 1024), None, False),
    ("mix_pre_g", (4, 1024), None, False),
    ("mix_post_g", (4, 1024), None, False),
    ("ffn2_pre_g", (4, 1024), None, False),
    ("ffn2_w_in", (4, 1024, 5632), 2, True),
    ("ffn2_w_down", (4, 2816, 1024), 1, True),
    ("ffn2_post_g", (4, 1024), None, False),
    ("ple_pre_g", (4, 1024), None, False),
    ("ple_w_gate", (4, 1024, 1024), 1, True),
    ("ple_w_proj", (4, 256, 1024), 2, True),
    ("ple_post_g", (4, 1024), None, False),
    ("hyb_w_in", (2, 1024, 4624), 2, True),
    ("gm_ln_g", (2, 1024), None, False),
    ("gm_ln_b", (2, 1024), None, False),
    ("gm_w_s", (2, 8, 128, 128), None, False),
    ("gm_b_s", (2, 8, 128), None, False),
    ("ssd_conv_w", (2, 4, 1536), 2, False),
    ("ssd_conv_b", (2, 1536), None, False),
    ("ssd_dt_bias", (2, 16), None, False),
    ("ssd_a_log", (2, 16), None, False),
    ("ssd_d", (2, 16), None, False),
    ("ssd_norm_g", (2, 1024), None, False),
    ("hyb_w_out", (2, 2048, 1024), 1, True),
    ("mla_w_in", (2, 1024, 448), 1, True),
    ("mla_q_norm_g", (2, 256), 1, False),
    ("mla_kv_norm_g", (2, 128), None, False),
    ("mla_w_uq", (2, 256, 3072), 2, True),
    ("mla_w_ukv", (2, 128, 4096), 2, True),
    ("mla_w_out", (2, 2048, 1024), 1, True),
]
WNAMES = [w[0] for w in WEIGHTS]
WSPEC = {w[0]: w for w in WEIGHTS}
REG = ["ffn1_w_in", "ffn1_w_down", "ffn2_w_in", "ffn2_w_down", "ple_w_gate", "ple_w_proj", "hyb_w_out", "mla_w_out"]
MISC = [n for n in WNAMES if n not in REG]


def _pick(dim, target):
    if dim <= target:
        return dim
    t = (target // LANES) * LANES
    while t >= LANES:
        if dim % t == 0:
            return t
        t -= LANES
    return dim


def mm(a, b, *, ta=False, tb=False, out_dtype=F32, name, tm=1024, tn=1024, tk=1024):
    a, la = a if isinstance(a, tuple) else (a, None)
    b, lb = b if isinstance(b, tuple) else (b, None)
    if ta:
        K, M = a.shape[-2:]
    else:
        M, K = a.shape[-2:]
    if tb:
        N, K2 = b.shape[-2:]
    else:
        K2, N = b.shape[-2:]
    assert K == K2, (a.shape, b.shape, ta, tb)
    bm, bn, bk = _pick(M, tm), _pick(N, tn), _pick(K, tk)
    nk = K // bk

    def spec(shape, idx, layer):
        if layer is None:
            return pl.BlockSpec(shape, idx)
        return pl.BlockSpec((None,) + shape, lambda i, j, k: (layer,) + idx(i, j, k))

    a_spec = spec((bk, bm), lambda i, j, k: (k, i), la) if ta else spec((bm, bk), lambda i, j, k: (i, k), la)
    b_spec = spec((bn, bk), lambda i, j, k: (j, k), lb) if tb else spec((bk, bn), lambda i, j, k: (k, j), lb)
    dn = (((0 if ta else 1,), (1 if tb else 0,)), ((), ()))

    def body(a_ref, b_ref, o_ref, acc_ref):
        k = pl.program_id(2)

        @pl.when(k == 0)
        def _():
            acc_ref[...] = jnp.zeros_like(acc_ref)

        acc_ref[...] += lax.dot_general(a_ref[...].astype(BF16), b_ref[...].astype(BF16), dn,
                                        preferred_element_type=F32)

        @pl.when(k == nk - 1)
        def _():
            o_ref[...] = acc_ref[...].astype(o_ref.dtype)

    return pl.pallas_call(
        body, name=name, grid=(M // bm, N // bn, nk),
        in_specs=[a_spec, b_spec], out_specs=pl.BlockSpec((bm, bn), lambda i, j, k: (i, j)),
        out_shape=SDS((M, N), out_dtype), scratch_shapes=[pltpu.VMEM((bm, bn), F32)],
        compiler_params=pltpu.CompilerParams(dimension_semantics=("parallel", "parallel", "arbitrary"),
                                             vmem_limit_bytes=VMEM_LIMIT),
    )(a, b)


def row_call(fn, xs, ps, outs, accs=(), *, tb, name, reverse=False):
    xs = [x if isinstance(x, tuple) else (x, x.shape[1], 0) for x in xs]
    T = xs[0][0].shape[0]
    tb = min(tb, T)
    n = T // tb
    assert n * tb == T
    nx, npar, no, na = len(xs), len(ps), len(outs), len(accs)

    def ridx(i):
        return n - 1 - i if reverse else i

    in_specs = [pl.BlockSpec((tb, w), functools.partial(lambda i, cb: (ridx(i), cb), cb=cb)) for (_, w, cb) in xs]
    in_specs += [pl.BlockSpec(p.shape, functools.partial(lambda i, nd: (0,) * nd, nd=p.ndim)) for p in ps]
    out_specs = [pl.BlockSpec((tb, c), lambda i: (ridx(i), 0)) for (c, _) in outs]
    out_specs += [pl.BlockSpec(s, functools.partial(lambda i, nd: (0,) * nd, nd=len(s))) for s in accs]
    out_shape = [SDS((T, c), dt) for (c, dt) in outs] + [SDS(s, F32) for s in accs]

    def body(*refs):
        xr, pr = refs[:nx], refs[nx:nx + npar]
        orf, ar = refs[nx + npar:nx + npar + no], refs[nx + npar + no:]
        res = fn(*[r[...] for r in xr], *[r[...] for r in pr])
        for r, v in zip(orf, res[:no]):
            r[...] = v.astype(r.dtype)
        if na:
            @pl.when(pl.program_id(0) == 0)
            def _():
                for r in ar:
                    r[...] = jnp.zeros_like(r)

            for r, v in zip(ar, res[no:]):
                r[...] += v.astype(F32)

    res = pl.pallas_call(
        body, name=name, grid=(n,), in_specs=in_specs, out_specs=out_specs, out_shape=out_shape,
        compiler_params=pltpu.CompilerParams(dimension_semantics=("arbitrary",), vmem_limit_bytes=VMEM_LIMIT),
    )(*[x[0] for x in xs], *ps)
    return res


def _f32(*a):
    return [v.astype(F32) for v in a]


def t_rms(x, g):
    return x * lax.rsqrt(jnp.mean(x * x, axis=-1, keepdims=True) + NORM_EPS) * g


def t_swiglu(gu):
    return jax.nn.silu(gu[:, :D_FF]) * gu[:, D_FF:]


def t_ple(gl, pp, g):
    return t_rms(jax.nn.sigmoid(gl) * pp, g)


def _iota(shape, d):
    return lax.broadcasted_iota(jnp.int32, shape, d)


def _bdot(a, b, dn=(((1,), (0,)), ((), ()))):
    return lax.dot_general(a.astype(BF16), b.astype(BF16), dn, preferred_element_type=F32)


def _hdot(a, b):
    return jnp.dot(a, b, precision=HIGHEST, preferred_element_type=F32)


NT = (((1,), (1,)), ((), ()))
TN = (((0,), (0,)), ((), ()))


def t_gmlp(uv, ln_g, ln_b, w_s, b_st):
    tb = uv.shape[0]
    guv = jax.nn.gelu(uv)
    u, v = guv[:, :1024], guv[:, 1024:]
    tri = _iota((CHUNK, CHUNK), 1) <= _iota((CHUNK, CHUNK), 0)
    rows = []
    for c in range(tb // CHUNK):
        vc = v[c * CHUNK:(c + 1) * CHUNK]
        heads = []
        for h in range(GM_HEADS):
            sl = slice(h * 128, (h + 1) * 128)
            vh = vc[:, sl]
            xc = vh - jnp.mean(vh, axis=-1, keepdims=True)
            var = jnp.mean(xc * xc, axis=-1, keepdims=True)
            y = xc * lax.rsqrt(var + LN_EPS) * ln_g[:, sl] + ln_b[:, sl]
            wm = jnp.where(tri, w_s[sl, :], 0.0)
            heads.append(_bdot(wm, y) + b_st[:, h:h + 1])
        rows.append(jnp.concatenate(heads, axis=1))
    mixed = rows[0] if len(rows) == 1 else jnp.concatenate(rows, axis=0)
    return u * mixed


def t_ssd(pre, dtr, z, st, dt_bias, a_log, d_exp, norm_g):
    L = CHUNK
    xbc = jax.nn.silu(pre)
    xs, bm, cm = xbc[:, :1024], xbc[:, 1024:1280], xbc[:, 1280:1536]
    valid = _iota((1, LANES), 1) < SSD_HEADS
    dt16 = jnp.where(valid, jax.nn.softplus(dtr + dt_bias), 0.0)
    a16 = jnp.where(valid, -jnp.exp(a_log), 0.0)
    da16 = dt16 * a16
    tri = _iota((L, L), 1) <= _iota((L, L), 0)
    acs16 = _hdot(tri.astype(F32), da16)
    hh, cc = _iota((LANES, 1024), 0), _iota((LANES, 1024), 1)
    expand = ((cc >= hh * SSD_HEAD_DIM) & (cc < (hh + 1) * SSD_HEAD_DIM)).astype(F32)
    acs = _hdot(acs16, expand)
    dte = _hdot(dt16, expand)
    alast = jnp.sum(jnp.where(_iota((L, 1024), 0) == L - 1, acs, 0.0), axis=0, keepdims=True)
    xd = xs * dte
    groups = [slice(0, 512), slice(512, 1024)]
    bg = [bm[:, :128], bm[:, 128:]]
    cg = [cm[:, :128], cm[:, 128:]]
    yoff = jnp.concatenate([_bdot(cg[g], st[:, groups[g]]) for g in range(2)], axis=1) * jnp.exp(acs)
    xdw = xd * jnp.exp(alast - acs)
    s_t = jnp.concatenate([_bdot(bg[g], xdw[:, groups[g]], TN) for g in range(2)], axis=1)
    st_new = st * jnp.exp(alast) + s_t
    cb = [_bdot(cg[g], bg[g], NT) for g in range(2)]
    acs16_t = acs16.T
    lo = _iota((1, LANES), 1) < SSD_HEAD_DIM
    slabs = []
    for j in range(SSD_HEADS // 2):
        g = j // 4
        xslab = xd[:, j * 128:(j + 1) * 128]
        acc = None
        for half in range(2):
            h = 2 * j + half
            seg = acs16[:, h:h + 1] - acs16_t[h:h + 1, :]
            mmat = cb[g] * jnp.exp(jnp.where(tri, seg, -1e30))
            xm = jnp.where(lo if half == 0 else jnp.logical_not(lo), xslab, 0.0)
            term = _bdot(mmat, xm)
            acc = term if acc is None else acc + term
        slabs.append(acc)
    y = jnp.concatenate(slabs, axis=1) + yoff + d_exp * xs
    yg = y * jax.nn.silu(z)
    outs = []
    for g in range(2):
        t = yg[:, groups[g]]
        outs.append(t * lax.rsqrt(jnp.mean(t * t, axis=-1, keepdims=True) + NORM_EPS) * norm_g[:, groups[g]])
    return jnp.concatenate(outs, axis=1), st_new


def t_kprep(c_all, cs, sn, qg, kvg):
    cqn = t_rms(c_all[:, :256], qg)
    ckvn = t_rms(c_all[:, 256:384], kvg)
    kr = c_all[:, 384:512] * cs + c_all[:, 512:640] * sn
    return cqn, ckvn, kr


def t_qrope(qb, c256, s256):
    scale = MLA_QK ** -0.5
    half = MLA_HEADS * MLA_QPAD
    outs = []
    for h in range(MLA_HEADS):
        a = qb[:, h * MLA_QPAD:(h + 1) * MLA_QPAD]
        b = qb[:, half + h * MLA_QPAD:half + (h + 1) * MLA_QPAD]
        outs.append((a * c256 + b * s256) * scale)
    return jnp.concatenate(outs, axis=1)


def rms_fwd(h, g, *, name, tb=512):
    def fn(h, g):
        return (t_rms(h.astype(F32), g),)
    return row_call(fn, [h], [g], [(h.shape[1], BF16)], tb=tb, name=name)[0]


def rms_bwd(h, dhn, dres, g, *, name, tb=256):
    def fn(h, dhn, dres, g):
        h, dhn, dres = _f32(h, dhn, dres)
        _, vjp = jax.vjp(t_rms, h, g)
        dh, dg = vjp(dhn)
        return dres + dh, dg
    return row_call(fn, [h, dhn, dres], [g], [(h.shape[1], F32)], [g.shape], tb=tb, name=name)


def post_fwd(h, f, g, scale, *, name, tb=512):
    def fn(h, f, g):
        return (h + scale * t_rms(f.astype(F32), g),)
    return row_call(fn, [h, f], [g], [(h.shape[1], F32)], tb=tb, name=name)[0]


def post_bwd(f, dout, g, scale, *, name, tb=256):
    def fn(f, dout, g):
        f, dout = _f32(f, dout)
        _, vjp = jax.vjp(lambda f, g: scale * t_rms(f, g), f, g)
        return vjp(dout)
    return row_call(fn, [f, dout], [g], [(f.shape[1], BF16)], [g.shape], tb=tb, name=name)


def swiglu_fwd(gu, *, name, tb=256):
    def fn(gu):
        return (t_swiglu(gu.astype(F32)),)
    return row_call(fn, [gu], [], [(D_FF, BF16)], tb=tb, name=name)[0]


def swiglu_bwd(gu, da, *, name, tb=256):
    def fn(gu, da):
        gu, da = _f32(gu, da)
        _, vjp = jax.vjp(t_swiglu, gu)
        return vjp(da)
    return row_call(fn, [gu, da], [], [(2 * D_FF, BF16)], tb=tb, name=name)[0]


def ple_fwd(h, gl, pp, g, *, name, tb=512):
    def fn(h, gl, pp, g):
        return (h + t_ple(gl, pp, g),)
    return row_call(fn, [h, gl, pp], [g], [(D_MODEL, F32)], tb=tb, name=name)[0]


def ple_bwd(gl, pp, dout, g, *, name, tb=256):
    def fn(gl, pp, dout, g):
        _, vjp = jax.vjp(t_ple, gl, pp, g)
        return vjp(dout)
    return row_call(fn, [gl, pp, dout], [g], [(D_MODEL, BF16), (D_MODEL, BF16)], [g.shape], tb=tb, name=name)


def gmlp_fwd(proj, ln_g, ln_b, w_s, b_st, *, name, tb=256):
    def fn(uv, ln_g, ln_b, w_s, b_st):
        return (t_gmlp(uv, ln_g, ln_b, w_s, b_st),)
    return row_call(fn, [(proj, 2048, 0)], [ln_g, ln_b, w_s, b_st], [(1024, BF16)], tb=tb, name=name)[0]


def gmlp_bwd(proj, dya, ln_g, ln_b, w_s, b_st, *, name, tb=128):
    def fn(uv, dya, ln_g, ln_b, w_s, b_st):
        _, vjp = jax.vjp(t_gmlp, uv, ln_g, ln_b, w_s, b_st)
        return vjp(dya.astype(F32))
    return row_call(fn, [(proj, 2048, 0), (dya, 1024, 0)], [ln_g, ln_b, w_s, b_st], [(2048, BF16)],
                    [ln_g.shape, ln_b.shape, w_s.shape, b_st.shape], tb=tb, name=name)


def kprep_fwd(c_all, cs, sn, qg, kvg, *, name, tb=512):
    return row_call(t_kprep, [c_all, cs, sn], [qg, kvg], [(256, BF16), (128, BF16), (128, BF16)], tb=tb, name=name)


def kprep_bwd(c_all, cs, sn, dcqn, dckvn, dkr, qg, kvg, *, name, tb=256):
    def fn(c_all, cs, sn, dcqn, dckvn, dkr, qg, kvg):
        dcqn, dckvn, dkr = _f32(dcqn, dckvn, dkr)
        _, vjp = jax.vjp(lambda c, qg, kvg: t_kprep(c, cs, sn, qg, kvg), c_all, qg, kvg)
        return vjp((dcqn, dckvn, dkr))
    return row_call(fn, [c_all, cs, sn, dcqn, dckvn, dkr], [qg, kvg], [(640, BF16)], [qg.shape, kvg.shape],
                    tb=tb, name=name)


def qrope_fwd(qb, c256, s256, *, name, tb=256):
    def fn(qb, c256, s256):
        return (t_qrope(qb, c256, s256),)
    return row_call(fn, [qb, c256, s256], [], [(MLA_HEADS * MLA_QPAD, BF16)], tb=tb, name=name)[0]


def qrope_bwd(dq, c256, s256, *, name, tb=256):
    def fn(dq, c256, s256):
        scale = MLA_QK ** -0.5
        a, b = [], []
        for h in range(MLA_HEADS):
            d = dq[:, h * MLA_QPAD:(h + 1) * MLA_QPAD] * scale
            a.append(d * c256)
            b.append(d * s256)
        return (jnp.concatenate(a + b, axis=1),)
    return row_call(fn, [dq, c256, s256], [], [(2 * MLA_HEADS * MLA_QPAD, BF16)], tb=tb, name=name)[0]


STAT_SPLIT = 64
ATTN_UNROLL = 2


def stats_fwd(do, o, lse, *, name, tb=512):
    def fn(do, o, lse):
        do, o = _f32(do, o)
        low = _iota((1, 128), 1) < STAT_SPLIT
        outs = []
        for h in range(MLA_HEADS):
            sl = slice(h * 128, (h + 1) * 128)
            dl = jnp.sum(do[:, sl] * o[:, sl], axis=-1, keepdims=True)
            outs.append(jnp.where(low, lse[:, sl], dl))
        return (jnp.concatenate(outs, axis=1),)
    return row_call(fn, [do, o, lse], [], [(2048, F32)], tb=tb, name=name)[0]


def headsum(dkr_h, *, name, tb=512):
    def fn(d):
        acc = d[:, :128]
        for h in range(1, MLA_HEADS):
            acc = acc + d[:, h * 128:(h + 1) * 128]
        return (acc,)
    return row_call(fn, [dkr_h], [], [(128, F32)], tb=tb, name=name)[0]


def loss_fwd(y, t, *, name, tb=512):
    def fn(y, t):
        e = y - t
        return e * (1.0 / D_MODEL), jnp.sum(e * e, axis=0, keepdims=True)
    return row_call(fn, [y, t], [], [(D_MODEL, F32)], [(1, D_MODEL)], tb=tb, name=name)


def conv_fwd(proj, w, b, *, name, tb=256):
    T = proj.shape[0]
    n = T // tb
    hb = tb // CHUNK
    C = SSD_CONV_CH

    def body(cur, prev, w_ref, b_ref, o_ref, scr):
        i = pl.program_id(0)
        scr[pl.ds(0, CHUNK), :] = jnp.where(i > 0, prev[...], 0.0)
        scr[pl.ds(CHUNK, tb), :] = cur[...]
        y = b_ref[...] + w_ref[3:4, :] * cur[...]
        for k in range(SSD_CONV - 1):
            y = y + w_ref[k:k + 1, :] * scr[pl.ds(CHUNK - (SSD_CONV - 1) + k, tb), :]
        o_ref[...] = y

    return pl.pallas_call(
        body, name=name, grid=(n,),
        in_specs=[pl.BlockSpec((tb, C), lambda i: (i, 2)),
                  pl.BlockSpec((CHUNK, C), lambda i: (jnp.maximum(i * hb - 1, 0), 2)),
                  pl.BlockSpec((SSD_CONV, C), lambda i: (0, 0)), pl.BlockSpec((1, C), lambda i: (0, 0))],
        out_specs=pl.BlockSpec((tb, C), lambda i: (i, 0)), out_shape=SDS((T, C), F32),
        scratch_shapes=[pltpu.VMEM((CHUNK + tb, C), F32)],
        compiler_params=pltpu.CompilerParams(dimension_semantics=("arbitrary",), vmem_limit_bytes=VMEM_LIMIT),
    )(proj, proj, w, b)


def conv_bwd(dpre, proj, w, *, name, tb=256):
    T = proj.shape[0]
    n = T // tb
    hb = tb // CHUNK
    nh = T // CHUNK
    C = SSD_CONV_CH

    def body(dcur, dnext, xcur, xprev, w_ref, dx_ref, dw_ref, db_ref, dscr, xscr):
        i = pl.program_id(0)

        @pl.when(i == 0)
        def _():
            dw_ref[...] = jnp.zeros_like(dw_ref)
            db_ref[...] = jnp.zeros_like(db_ref)

        d = dcur[...]
        dscr[pl.ds(0, tb), :] = d
        dscr[pl.ds(tb, CHUNK), :] = jnp.where(i < n - 1, dnext[...], 0.0)
        xscr[pl.ds(0, CHUNK), :] = jnp.where(i > 0, xprev[...], 0.0)
        xscr[pl.ds(CHUNK, tb), :] = xcur[...]
        dx = w_ref[3:4, :] * d
        for k in range(SSD_CONV - 1):
            dx = dx + w_ref[k:k + 1, :] * dscr[pl.ds(SSD_CONV - 1 - k, tb), :]
        dx_ref[...] = dx.astype(dx_ref.dtype)
        for k in range(SSD_CONV):
            xk = xscr[pl.ds(CHUNK - (SSD_CONV - 1) + k, tb), :]
            dw_ref[k:k + 1, :] += jnp.sum(d * xk, axis=0, keepdims=True)
        db_ref[...] += jnp.sum(d, axis=0, keepdims=True)

    return pl.pallas_call(
        body, name=name, grid=(n,),
        in_specs=[pl.BlockSpec((tb, C), lambda i: (i, 0)),
                  pl.BlockSpec((CHUNK, C), lambda i: (jnp.minimum((i + 1) * hb, nh - 1), 0)),
                  pl.BlockSpec((tb, C), lambda i: (i, 2)),
                  pl.BlockSpec((CHUNK, C), lambda i: (jnp.maximum(i * hb - 1, 0), 2)),
                  pl.BlockSpec((SSD_CONV, C), lambda i: (0, 0))],
        out_specs=[pl.BlockSpec((tb, C), lambda i: (i, 0)), pl.BlockSpec((SSD_CONV, C), lambda i: (0, 0)),
                   pl.BlockSpec((1, C), lambda i: (0, 0))],
        out_shape=[SDS((T, C), BF16), SDS((SSD_CONV, C), F32), SDS((1, C), F32)],
        scratch_shapes=[pltpu.VMEM((tb + CHUNK, C), F32), pltpu.VMEM((CHUNK + tb, C), F32)],
        compiler_params=pltpu.CompilerParams(dimension_semantics=("arbitrary",), vmem_limit_bytes=VMEM_LIMIT),
    )(dpre, dpre, proj, proj, w)


def _ssd_specs(nc, rev):
    def r(c):
        return nc - 1 - c if rev else c
    pre = pl.BlockSpec((CHUNK, SSD_CONV_CH), lambda c: (r(c), 0))
    dtr = pl.BlockSpec((CHUNK, LANES), lambda c: (r(c), HYB_MAIN // LANES))
    z = pl.BlockSpec((CHUNK, 1024), lambda c: (r(c), 2))
    row = pl.BlockSpec((CHUNK, 1024), lambda c: (r(c), 0))
    return pre, dtr, z, row


def _pspec(shape):
    return pl.BlockSpec(shape, lambda c: (0,) * len(shape))


def ssd_fwd(pre, proj, dt_bias, a_log, d_exp, norm_g, *, name):
    T = pre.shape[0]
    nc = T // CHUNK
    s_pre, s_dt, s_z, s_row = _ssd_specs(nc, False)

    def body(pre_r, dt_r, z_r, b_r, a_r, d_r, g_r, y_r, sv_r, st):
        @pl.when(pl.program_id(0) == 0)
        def _():
            st[...] = jnp.zeros_like(st)

        s0 = st[...]
        sv_r[...] = s0
        y, s1 = t_ssd(pre_r[...], dt_r[...], z_r[...], s0, b_r[...], a_r[...], d_r[...], g_r[...])
        y_r[...] = y.astype(y_r.dtype)
        st[...] = s1

    return pl.pallas_call(
        body, name=name, grid=(nc,),
        in_specs=[s_pre, s_dt, s_z, _pspec((1, LANES)), _pspec((1, LANES)), _pspec((1, 1024)), _pspec((1, 1024))],
        out_specs=[s_row, s_row], out_shape=[SDS((T, 1024), BF16), SDS((T, 1024), F32)],
        scratch_shapes=[pltpu.VMEM((SSD_STATE, 1024), F32)],
        compiler_params=pltpu.CompilerParams(dimension_semantics=("arbitrary",), vmem_limit_bytes=VMEM_LIMIT),
    )(pre, proj, proj, dt_bias, a_log, d_exp, norm_g)


def ssd_bwd(pre, proj, states, dyab, dt_bias, a_log, d_exp, norm_g, *, name):
    T = pre.shape[0]
    nc = T // CHUNK
    s_pre, s_dt, s_z, s_row = _ssd_specs(nc, True)
    s_dtout = pl.BlockSpec((CHUNK, LANES), lambda c: (nc - 1 - c, 0))
    s_dy = pl.BlockSpec((CHUNK, 1024), lambda c: (nc - 1 - c, 1))

    def body(pre_r, dt_r, z_r, sv_r, dy_r, b_r, a_r, d_r, g_r, dpre_r, ddt_r, dz_r, db_r, da_r, dd_r, dg_r, dst):
        @pl.when(pl.program_id(0) == 0)
        def _():
            dst[...] = jnp.zeros_like(dst)
            for r in (db_r, da_r, dd_r, dg_r):
                r[...] = jnp.zeros_like(r)

        _, vjp = jax.vjp(t_ssd, pre_r[...], dt_r[...], z_r[...], sv_r[...], b_r[...], a_r[...], d_r[...], g_r[...])
        dpre, ddt, dz, ds0, db, da, dd, dg = vjp((dy_r[...].astype(F32), dst[...]))
        dpre_r[...] = dpre
        ddt_r[...] = ddt.astype(ddt_r.dtype)
        dz_r[...] = dz.astype(dz_r.dtype)
        dst[...] = ds0
        db_r[...] += db
        da_r[...] += da
        dd_r[...] += dd
        dg_r[...] += dg

    return pl.pallas_call(
        body, name=name, grid=(nc,),
        in_specs=[s_pre, s_dt, s_z, s_row, s_dy, _pspec((1, LANES)), _pspec((1, LANES)), _pspec((1, 1024)),
                  _pspec((1, 1024))],
        out_specs=[s_pre, s_dtout, s_row, _pspec((1, LANES)), _pspec((1, LANES)), _pspec((1, 1024)), _pspec((1, 1024))],
        out_shape=[SDS((T, SSD_CONV_CH), F32), SDS((T, LANES), BF16), SDS((T, 1024), BF16),
                   SDS((1, LANES), F32), SDS((1, LANES), F32), SDS((1, 1024), F32), SDS((1, 1024), F32)],
        scratch_shapes=[pltpu.VMEM((SSD_STATE, 1024), F32)],
        compiler_params=pltpu.CompilerParams(dimension_semantics=("arbitrary",), vmem_limit_bytes=VMEM_LIMIT),
    )(pre, proj, proj, states, dyab, dt_bias, a_log, d_exp, norm_g)


def _attn_tile(T, target=512):
    return min(target, T // 2)


def _causal(tq):
    return _iota((tq, tq), 1) <= _iota((tq, tq), 0)


def _job_parts(job):
    if job is None:
        return 0, 0, [], [], []
    ni, no = len(job["inputs"]), len(job["out_shape"])
    return ni, no, list(job["inputs"]), list(job["out_shape"]), list(job["scratch"])


def _job_phase(job, which, refs, when):
    if job is not None and job["phases"][which] is not None:
        pl.when(when)(functools.partial(job["phases"][which], *refs))


def attn_fwd(q, kv, kr, *, name, job=None):
    T = q.shape[0]
    tq = _attn_tile(T)
    nq = T // tq
    ng = MLA_HEADS // 2
    ni, no, jins, jouts, jscratch = _job_parts(job)

    def body(*refs):
        q_ref, kn_ref, v_ref, kr_ref = refs[:4]
        o_ref, lse_ref = refs[4 + ni:6 + ni]
        jrefs = (refs[4:4 + ni], refs[6 + ni:6 + ni + no], refs[6 + ni + no:])
        g = pl.program_id(0)
        qi = pl.program_id(1)
        _job_phase(job, 0, jrefs, (g == 0) & (qi == 0))
        qv = [q_ref[:, e * MLA_QPAD:(e + 1) * MLA_QPAD] for e in range(2)]

        def blk(ki, masked, carry):
            rows = pl.ds(pl.multiple_of(ki * tq, tq), tq)
            kr = kr_ref[rows, :]
            out = []
            for e in range(2):
                m, l, acc = carry[e]
                cols = slice(e * 128, (e + 1) * 128)
                k = jnp.concatenate([kn_ref[rows, cols], kr], axis=1)
                s = lax.dot_general(qv[e], k, NT, preferred_element_type=F32)
                if masked:
                    s = jnp.where(_causal(tq), s, -1e30)
                m_new = jnp.maximum(m, jnp.max(s, axis=-1, keepdims=True))
                p = jnp.exp(s - m_new)
                alpha = jnp.exp(m - m_new)
                l = alpha * l + jnp.sum(p, axis=-1, keepdims=True)
                acc = alpha * acc + jnp.dot(p.astype(BF16), v_ref[rows, cols], preferred_element_type=F32)
                out.append((m_new, l, acc))
            return tuple(out)

        one = (jnp.full((tq, 1), -1e30, F32), jnp.zeros((tq, 1), F32), jnp.zeros((tq, 128), F32))
        carry = lax.fori_loop(0, qi, lambda ki, c: blk(ki, False, c), (one, one))
        carry = blk(qi, True, carry)
        for e in range(2):
            m, l, acc = carry[e]
            cols = slice(e * 128, (e + 1) * 128)
            o_ref[:, cols] = (acc / l).astype(o_ref.dtype)
            lse_ref[:, cols] = jnp.broadcast_to(m + jnp.log(l), (tq, 128))
        _job_phase(job, 2, jrefs, (g == ng - 1) & (qi == nq - 1))

    res = pl.pallas_call(
        body, name=name, grid=(ng, nq),
        in_specs=[pl.BlockSpec((tq, 2 * MLA_QPAD), lambda g, i: (i, g)),
                  pl.BlockSpec((T, 256), lambda g, i: (0, g)),
                  pl.BlockSpec((T, 256), lambda g, i: (0, ng + g)),
                  pl.BlockSpec((T, 128), lambda g, i: (0, 0))] + [ANY] * ni,
        out_specs=[pl.BlockSpec((tq, 256), lambda g, i: (i, g)), pl.BlockSpec((tq, 256), lambda g, i: (i, g))]
        + [ANY] * no,
        out_shape=[SDS((T, 2048), BF16), SDS((T, 2048), F32)] + jouts, scratch_shapes=jscratch,
        compiler_params=pltpu.CompilerParams(dimension_semantics=("arbitrary", "arbitrary"),
                                             vmem_limit_bytes=VMEM_LIMIT),
    )(q, kv, kv, kr, *jins)
    return res[0], res[1], list(res[2:])


def attn_bwd(q, kv, kr, do, stats, *, name, job=None):
    T = q.shape[0]
    tq = _attn_tile(T)
    nq = T // tq
    ni, no, jins, jouts, jscratch = _job_parts(job)

    def body(*refs):
        q_ref, do_ref, st_ref, kn_ref, v_ref, kr_ref = refs[:6]
        dq_ref, dkn_ref, dv_ref, dkr_ref = refs[6 + ni:10 + ni]
        jrefs = (refs[6:6 + ni], refs[10 + ni:10 + ni + no], refs[10 + ni + no:])
        h = pl.program_id(0)
        ki = pl.program_id(1)
        _job_phase(job, 0, jrefs, (h == 0) & (ki == 0))
        _job_phase(job, 1, jrefs, (h == MLA_HEADS // 2) & (ki == 0))

        @pl.when(ki == 0)
        def _():
            dq_ref[...] = jnp.zeros_like(dq_ref)

        k = jnp.concatenate([kn_ref[...], kr_ref[...]], axis=1)
        v = v_ref[...]

        def blk(qi, masked, carry):
            dk, dv = carry
            rows = pl.ds(pl.multiple_of(qi * tq, tq), tq)
            qv, dov = q_ref[rows, :], do_ref[rows, :]
            lse, dl = st_ref[rows, 0:1], st_ref[rows, STAT_SPLIT:STAT_SPLIT + 1]
            s = lax.dot_general(qv, k, NT, preferred_element_type=F32)
            if masked:
                s = jnp.where(_causal(tq), s, -1e30)
            p = jnp.exp(s - lse)
            dv = dv + lax.dot_general(p.astype(BF16), dov, TN, preferred_element_type=F32)
            dp = lax.dot_general(dov, v, NT, preferred_element_type=F32)
            ds = (p * (dp - dl)).astype(BF16)
            dk = dk + lax.dot_general(ds, qv, TN, preferred_element_type=F32)
            dq_ref[rows, :] += jnp.dot(ds, k, preferred_element_type=F32)
            return dk, dv

        carry = blk(ki, True, (jnp.zeros((tq, MLA_QPAD), F32), jnp.zeros((tq, 128), F32)))
        rest = nq - 1 - ki

        def group(j, c):
            for u in range(ATTN_UNROLL):
                c = blk(ki + 1 + ATTN_UNROLL * j + u, False, c)
            return c

        carry = lax.fori_loop(0, rest // ATTN_UNROLL, group, carry)
        for u in range(ATTN_UNROLL - 1):
            carry = lax.cond(rest % ATTN_UNROLL > u, functools.partial(lambda c, u: blk(nq - 1 - u, False, c), u=u),
                             lambda c: c, carry)
        dk, dv = carry
        dkn_ref[...] = dk[:, :128].astype(dkn_ref.dtype)
        dkr_ref[...] = dk[:, 128:]
        dv_ref[...] = dv.astype(dv_ref.dtype)
        _job_phase(job, 2, jrefs, (h == MLA_HEADS - 1) & (ki == nq - 1))

    res = pl.pallas_call(
        body, name=name, grid=(MLA_HEADS, nq),
        in_specs=[pl.BlockSpec((T, MLA_QPAD), lambda h, i: (0, h)),
                  pl.BlockSpec((T, 128), lambda h, i: (0, h)),
                  pl.BlockSpec((T, 128), lambda h, i: (0, h)),
                  pl.BlockSpec((tq, 128), lambda h, i: (i, h)),
                  pl.BlockSpec((tq, 128), lambda h, i: (i, MLA_HEADS + h)),
                  pl.BlockSpec((tq, 128), lambda h, i: (i, 0))] + [ANY] * ni,
        out_specs=[pl.BlockSpec((T, MLA_QPAD), lambda h, i: (0, h)),
                   pl.BlockSpec((tq, 128), lambda h, i: (i, h)), pl.BlockSpec((tq, 128), lambda h, i: (i, h)),
                   pl.BlockSpec((tq, 128), lambda h, i: (i, h))] + [ANY] * no,
        out_shape=[SDS((T, MLA_HEADS * MLA_QPAD), F32), SDS((T, 2048), BF16), SDS((T, 2048), BF16),
                   SDS((T, 2048), F32)] + jouts, scratch_shapes=jscratch,
        compiler_params=pltpu.CompilerParams(dimension_semantics=("arbitrary", "arbitrary"),
                                             vmem_limit_bytes=VMEM_LIMIT),
    )(q, do, stats, kv, kv, kr, *jins)
    return res[0], res[1], res[2], res[3], list(res[4:])


def _ffn_fwd(h, pre_g, w_in, w_down, post_g, tag):
    hn = rms_fwd(h, pre_g, name=f"{tag}_pre")
    gu = mm(hn, w_in, out_dtype=BF16, name=f"{tag}_in", tn=1408)
    a = swiglu_fwd(gu, name=f"{tag}_act")
    f = mm(a, w_down, name=f"{tag}_down", tk=1408)
    h2 = post_fwd(h, f, post_g, 0.5, name=f"{tag}_post")
    return h2, (h, hn, gu, a, f)


def _ffn_bwd(dh2, saved, pre_g, w_in, w_down, post_g, tag):
    h, hn, gu, a, f = saved
    df, dpost = post_bwd(f, dh2, post_g, 0.5, name=f"{tag}_post_b")
    da = mm(df, w_down, tb=True, name=f"{tag}_down_bx", tn=1408)
    dw_down = mm(a, df, ta=True, out_dtype=BF16, name=f"{tag}_down_bw", tm=1408)
    dgu = swiglu_bwd(gu, da, name=f"{tag}_act_b")
    dhn = mm(dgu, w_in, tb=True, name=f"{tag}_in_bx", tk=1408)
    dw_in = mm(hn, dgu, ta=True, out_dtype=BF16, name=f"{tag}_in_bw", tn=1408)
    dh, dpre = rms_bwd(h, dhn, dh2, pre_g, name=f"{tag}_pre_b")
    return dh, dpre, dw_in, dw_down, dpost


def _hyb_fwd(hn, w, tag):
    proj = mm(hn, w["hyb_in"], name=f"{tag}_in")
    ya = gmlp_fwd(proj, w["ln_g"], w["ln_b"], w["w_s"], w["b_st"], name=f"{tag}_gmlp")
    pre = conv_fwd(proj, w["conv_w"], w["conv_b"], name=f"{tag}_conv")
    yb, states = ssd_fwd(pre, proj, w["dt_bias"], w["a_log"], w["d_exp"], w["norm_g"], name=f"{tag}_ssd")
    yab = jnp.concatenate([ya, yb], axis=1)
    mixed = mm(yab, w["hyb_out"], name=f"{tag}_out")
    return mixed, (proj, pre, states, yab)


def _hyb_bwd(dmixed, hn, saved, w, tag):
    proj, pre, states, yab = saved
    dyab = mm(dmixed, w["hyb_out"], tb=True, name=f"{tag}_out_bx")
    dw_out = mm(yab, dmixed, ta=True, out_dtype=BF16, name=f"{tag}_out_bw")
    duv, dln_g, dln_b, dw_s, db_st = gmlp_bwd(proj, dyab, w["ln_g"], w["ln_b"], w["w_s"], w["b_st"],
                                              name=f"{tag}_gmlp_b")
    dpre, ddt, dz, ddt_bias, da_log, dd_exp, dnorm_g = ssd_bwd(
        pre, proj, states, dyab, w["dt_bias"], w["a_log"], w["d_exp"], w["norm_g"], name=f"{tag}_ssd_b")
    dxbc, dconv_w, dconv_b = conv_bwd(dpre, proj, w["conv_w"], name=f"{tag}_conv_b")
    pad = jnp.zeros((duv.shape[0], HYB_PAD - HYB_MAIN - LANES), BF16)
    dproj = jnp.concatenate([duv, dz, dxbc, ddt, pad], axis=1)
    dhn = mm(dproj, w["hyb_in"], tb=True, name=f"{tag}_in_bx")
    dw_in = mm(hn, dproj, ta=True, name=f"{tag}_in_bw")
    g = dict(hyb_in=dw_in, hyb_out=dw_out, ln_g=dln_g, ln_b=dln_b, w_s=dw_s, b_st=db_st, conv_w=dconv_w,
             conv_b=dconv_b, dt_bias=ddt_bias, a_log=da_log, d_exp=dd_exp, norm_g=dnorm_g)
    return dhn, g


def _mla_fwd(hn, w, rope, tag, job=None):
    cs, sn, c256, s256 = rope
    c_all = mm(hn, w["mla_in"], name=f"{tag}_in")
    cqn, ckvn, kr = kprep_fwd(c_all, cs, sn, w["q_g"], w["kv_g"], name=f"{tag}_kprep")
    qb = mm(cqn, w["uq"], name=f"{tag}_uq")
    q = qrope_fwd(qb, c256, s256, name=f"{tag}_qrope")
    kv = mm(ckvn, w["ukv"], out_dtype=BF16, name=f"{tag}_ukv")
    o, lse, jouts = attn_fwd(q, kv, kr, name=f"{tag}_attn", job=job)
    mixed = mm(o, w["mla_out"], name=f"{tag}_out")
    return mixed, (c_all, cqn, ckvn, kr, q, kv, o, lse), jouts


def _mla_bwd(dmixed, hn, saved, w, rope, tag, job=None):
    cs, sn, c256, s256 = rope
    c_all, cqn, ckvn, kr, q, kv, o, lse = saved
    do = mm(dmixed, w["mla_out"], tb=True, out_dtype=BF16, name=f"{tag}_out_bx")
    dw_out = mm(o, dmixed, ta=True, out_dtype=BF16, name=f"{tag}_out_bw")
    stats = stats_fwd(do, o, lse, name=f"{tag}_stats")
    dq, dkn, dv, dkr_h, jouts = attn_bwd(q, kv, kr, do, stats, name=f"{tag}_attn_b", job=job)
    dkr = headsum(dkr_h, name=f"{tag}_dkr")
    dkv = jnp.concatenate([dkn, dv], axis=1)
    dckvn = mm(dkv, w["ukv"], tb=True, name=f"{tag}_ukv_bx")
    dw_ukv = mm(ckvn, dkv, ta=True, name=f"{tag}_ukv_bw")
    dqb = qrope_bwd(dq, c256, s256, name=f"{tag}_qrope_b")
    dcqn = mm(dqb, w["uq"], tb=True, name=f"{tag}_uq_bx")
    dw_uq = mm(cqn, dqb, ta=True, name=f"{tag}_uq_bw")
    dc_all, dq_g, dkv_g = kprep_bwd(c_all, cs, sn, dcqn, dckvn, dkr, w["q_g"], w["kv_g"], name=f"{tag}_kprep_b")
    dhn = mm(dc_all, w["mla_in"], tb=True, name=f"{tag}_in_bx")
    dw_in = mm(hn, dc_all, ta=True, name=f"{tag}_in_bw")
    g = dict(mla_in=dw_in, mla_out=dw_out, uq=dw_uq, ukv=dw_ukv, q_g=dq_g, kv_g=dkv_g)
    return dhn, g, jouts


class NoJobs:
    def fwd_job(self, i):
        return None

    def fwd_done(self, i, outs):
        pass

    def bwd_job(self, i, grads):
        return None

    def bwd_done(self, i, outs):
        pass


def local_step(x, p, rope, target, weights_of, jobs=NoJobs()):
    h = x
    saved = []
    lw = []
    for i in range(DEPTH):
        w = weights_of(i)
        lw.append(w)
        t = f"l{i}"
        h, s1 = _ffn_fwd(h, w["ffn1_pre_g"], w["ffn1_w_in"], w["ffn1_w_down"], w["ffn1_post_g"], f"{t}_f1")
        h1 = h
        hn = rms_fwd(h1, w["mix_pre_g"], name=f"{t}_mixpre")
        if i % 2 == 0:
            mixed, sm = _hyb_fwd(hn, w, f"{t}_hyb")
        else:
            job = jobs.fwd_job(i)
            mixed, sm, jouts = _mla_fwd(hn, w, rope, f"{t}_mla", job)
            if job is not None:
                jobs.fwd_done(i, jouts)
        h = post_fwd(h1, mixed, w["mix_post_g"], 1.0, name=f"{t}_mixpost")
        h, s2 = _ffn_fwd(h, w["ffn2_pre_g"], w["ffn2_w_in"], w["ffn2_w_down"], w["ffn2_post_g"], f"{t}_f2")
        h3 = h
        hn3 = rms_fwd(h3, w["ple_pre_g"], name=f"{t}_plepre")
        gl = mm(hn3, w["ple_w_gate"], name=f"{t}_plegate")
        pp = mm((p, i), w["ple_w_proj"], name=f"{t}_pleproj")
        h = ple_fwd(h3, gl, pp, w["ple_post_g"], name=f"{t}_plepost")
        saved.append((s1, h1, hn, sm, mixed, s2, h3, hn3, gl, pp))

    dh, sq = loss_fwd(h, target, name="loss")
    grads = [None] * DEPTH
    for i in reversed(range(DEPTH)):
        w = lw[i]
        t = f"l{i}"
        s1, h1, hn, sm, mixed, s2, h3, hn3, gl, pp = saved[i]
        g = {}
        dgl, dpp, g["ple_post_g"] = ple_bwd(gl, pp, dh, w["ple_post_g"], name=f"{t}_plepost_b")
        dhn3 = mm(dgl, w["ple_w_gate"], tb=True, name=f"{t}_plegate_bx")
        g["ple_w_gate"] = mm(hn3, dgl, ta=True, out_dtype=BF16, name=f"{t}_plegate_bw")
        g["ple_w_proj"] = mm((p, i), dpp, ta=True, out_dtype=BF16, name=f"{t}_pleproj_bw")
        dh, g["ple_pre_g"] = rms_bwd(h3, dhn3, dh, w["ple_pre_g"], name=f"{t}_plepre_b")
        dh, g["ffn2_pre_g"], g["ffn2_w_in"], g["ffn2_w_down"], g["ffn2_post_g"] = _ffn_bwd(
            dh, s2, w["ffn2_pre_g"], w["ffn2_w_in"], w["ffn2_w_down"], w["ffn2_post_g"], f"{t}_f2")
        dmixed, g["mix_post_g"] = post_bwd(mixed, dh, w["mix_post_g"], 1.0, name=f"{t}_mixpost_b")
        if i % 2 == 0:
            dhn, gm = _hyb_bwd(dmixed, hn, sm, w, f"{t}_hyb")
        else:
            job = jobs.bwd_job(i, grads)
            dhn, gm, jouts = _mla_bwd(dmixed, hn, sm, w, rope, f"{t}_mla", job)
            if job is not None:
                jobs.bwd_done(i, jouts)
        g.update(gm)
        dh, g["mix_pre_g"] = rms_bwd(h1, dhn, dh, w["mix_pre_g"], name=f"{t}_mixpre_b")
        dh, g["ffn1_pre_g"], g["ffn1_w_in"], g["ffn1_w_down"], g["ffn1_post_g"] = _ffn_bwd(
            dh, s1, w["ffn1_pre_g"], w["ffn1_w_in"], w["ffn1_w_down"], w["ffn1_post_g"], f"{t}_f1")
        grads[i] = g
    return sq, dh, grads


def _zeros_like_cols(a, n):
    return jnp.zeros(a.shape[:-1] + (n,), a.dtype)


def layer_weights(full, i):
    j = i // 2
    row = lambda v: v.reshape(1, -1)
    w = {k: row(full[k][i]) for k in ("ffn1_pre_g", "ffn1_post_g", "mix_pre_g", "mix_post_g", "ffn2_pre_g",
                                      "ffn2_post_g", "ple_pre_g", "ple_post_g")}
    for k in ("ffn1_w_in", "ffn1_w_down", "ffn2_w_in", "ffn2_w_down", "ple_w_gate", "ple_w_proj"):
        w[k] = full[k][i]
    if i % 2 == 0:
        hw = full["hyb_w_in"][j]
        w["hyb_in"] = jnp.concatenate([hw, _zeros_like_cols(hw, HYB_PAD - HYB_IN)], axis=1)
        w["hyb_out"] = full["hyb_w_out"][j]
        w["ln_g"], w["ln_b"] = row(full["gm_ln_g"][j]), row(full["gm_ln_b"][j])
        w["w_s"] = full["gm_w_s"][j].reshape(GM_HEADS * CHUNK, CHUNK)
        w["b_st"] = jnp.pad(full["gm_b_s"][j].T, ((0, 0), (0, LANES - GM_HEADS)))
        w["conv_w"], w["conv_b"] = full["ssd_conv_w"][j], row(full["ssd_conv_b"][j])
        pad16 = lambda v: jnp.pad(v.reshape(1, -1), ((0, 0), (0, LANES - SSD_HEADS)))
        w["dt_bias"], w["a_log"] = pad16(full["ssd_dt_bias"][j]), pad16(full["ssd_a_log"][j])
        w["d_exp"] = row(jnp.repeat(full["ssd_d"][j], SSD_HEAD_DIM))
        w["norm_g"] = row(full["ssd_norm_g"][j])
    else:
        wi = full["mla_w_in"][j]
        z64 = _zeros_like_cols(wi, 64)
        w["mla_in"] = jnp.concatenate([wi[:, :384], wi[:, 384:448], z64, -wi[:, 416:448], wi[:, 384:416], z64], axis=1)
        uq = full["mla_w_uq"][j].reshape(MLA_Q_LORA, MLA_HEADS, MLA_QK)
        zq = jnp.zeros((MLA_Q_LORA, MLA_HEADS, 64), uq.dtype)
        pad_part = jnp.concatenate([uq, zq], axis=2)
        swp_part = jnp.concatenate([jnp.zeros_like(uq[:, :, :128]), -uq[:, :, 160:192], uq[:, :, 128:160], zq], axis=2)
        w["uq"] = jnp.concatenate([pad_part.reshape(MLA_Q_LORA, -1), swp_part.reshape(MLA_Q_LORA, -1)], axis=1)
        ukv = full["mla_w_ukv"][j].reshape(MLA_KV_LORA, MLA_HEADS, 256)
        w["ukv"] = jnp.concatenate([ukv[:, :, :128].reshape(MLA_KV_LORA, -1), ukv[:, :, 128:].reshape(MLA_KV_LORA, -1)],
                                   axis=1)
        w["mla_out"] = full["mla_w_out"][j]
        w["q_g"], w["kv_g"] = row(full["mla_q_norm_g"][j]), row(full["mla_kv_norm_g"][j])
    return w


def full_grads(grads):
    out = {}
    stack = lambda k, idx: jnp.stack([grads[i][k] for i in idx])
    every, even, odd = range(DEPTH), range(0, DEPTH, 2), range(1, DEPTH, 2)
    for k in ("ffn1_pre_g", "ffn1_post_g", "mix_pre_g", "mix_post_g", "ffn2_pre_g", "ffn2_post_g", "ple_pre_g",
              "ple_post_g"):
        out[k] = stack(k, every).reshape(DEPTH, D_MODEL)
    for k in ("ffn1_w_in", "ffn1_w_down", "ffn2_w_in", "ffn2_w_down", "ple_w_gate", "ple_w_proj"):
        out[k] = stack(k, every)
    out["hyb_w_in"] = stack("hyb_in", even)[:, :, :HYB_IN]
    out["hyb_w_out"] = stack("hyb_out", even)
    out["gm_ln_g"] = stack("ln_g", even).reshape(2, 1024)
    out["gm_ln_b"] = stack("ln_b", even).reshape(2, 1024)
    out["gm_w_s"] = stack("w_s", even).reshape(2, GM_HEADS, CHUNK, CHUNK)
    out["gm_b_s"] = jnp.swapaxes(stack("b_st", even)[:, :, :GM_HEADS], 1, 2)
    out["ssd_conv_w"] = stack("conv_w", even)
    out["ssd_conv_b"] = stack("conv_b", even).reshape(2, SSD_CONV_CH)
    out["ssd_dt_bias"] = stack("dt_bias", even)[:, 0, :SSD_HEADS]
    out["ssd_a_log"] = stack("a_log", even)[:, 0, :SSD_HEADS]
    out["ssd_d"] = stack("d_exp", even).reshape(2, SSD_HEADS, SSD_HEAD_DIM).sum(axis=-1)
    out["ssd_norm_g"] = stack("norm_g", even).reshape(2, 1024)
    dwi = stack("mla_in", odd)
    out["mla_w_in"] = jnp.concatenate([dwi[:, :, :384], dwi[:, :, 384:416] + dwi[:, :, 544:576],
                                       dwi[:, :, 416:448] - dwi[:, :, 512:544]], axis=2)
    duq = stack("uq", odd)
    half = MLA_HEADS * MLA_QPAD
    dp = duq[:, :, :half].reshape(2, MLA_Q_LORA, MLA_HEADS, MLA_QPAD)
    ds = duq[:, :, half:].reshape(2, MLA_Q_LORA, MLA_HEADS, MLA_QPAD)
    out["mla_w_uq"] = jnp.concatenate([dp[..., :128], dp[..., 128:160] + ds[..., 160:192],
                                       dp[..., 160:192] - ds[..., 128:160]], axis=-1).reshape(2, MLA_Q_LORA, -1)
    dukv = stack("ukv", odd)
    dk = dukv[:, :, :2048].reshape(2, MLA_KV_LORA, MLA_HEADS, 128)
    dv = dukv[:, :, 2048:].reshape(2, MLA_KV_LORA, MLA_HEADS, 128)
    out["mla_w_ukv"] = jnp.concatenate([dk, dv], axis=-1).reshape(2, MLA_KV_LORA, -1)
    out["mla_w_out"] = stack("mla_out", odd)
    out["mla_q_norm_g"] = stack("q_g", odd).reshape(2, MLA_Q_LORA)
    out["mla_kv_norm_g"] = stack("kv_g", odd).reshape(2, MLA_KV_LORA)
    return out


def rope_tables(positions):
    T = positions.shape[0]
    inv = 1.0 / (ROPE_BASE ** (jnp.arange(0, MLA_ROPE, 2, dtype=F32) / MLA_ROPE))
    ang = positions.astype(F32)[:, None] * inv
    cos, sin = jnp.cos(ang), jnp.sin(ang)
    z64 = jnp.zeros((T, 64), F32)
    cs = jnp.concatenate([cos, cos, z64], axis=1)
    sn = jnp.concatenate([sin, sin, z64], axis=1)
    c256 = jnp.concatenate([jnp.ones((T, 128), F32), cs], axis=1)
    s256 = jnp.concatenate([jnp.zeros((T, 128), F32), sn], axis=1)
    return cs, sn, c256, s256


def _rows(n):
    return -(-n // LANES)


def _pack(pieces, dtype, row_multiple):
    flat = []
    total = 0
    for a in pieces:
        v = a.reshape(-1).astype(dtype)
        padn = _rows(v.shape[0]) * LANES - v.shape[0]
        if padn:
            v = jnp.concatenate([v, jnp.zeros((padn,), dtype)])
        flat.append(v)
        total += v.shape[0] // LANES
    tail = -total % row_multiple
    if tail:
        flat.append(jnp.zeros((tail * LANES,), dtype))
    return jnp.concatenate(flat).reshape(-1, LANES)


def _unpack(slab, shapes):
    out = []
    r = 0
    for s in shapes:
        n = int(np.prod(s))
        nr = _rows(n)
        out.append(slab[r:r + nr].reshape(-1)[:n].reshape(s))
        r += nr
    return out


def _shard_shape(shape, ax):
    if ax is None:
        return tuple(shape)
    s = list(shape)
    s[ax] //= N_CHIPS
    return tuple(s)


def _chip_slice(a, ax, k):
    if ax is None:
        return a
    n = a.shape[ax] // N_CHIPS
    return lax.slice_in_dim(a, k * n, (k + 1) * n, axis=ax)


def _plane_peers():
    x, y, c = lax.axis_index("x"), lax.axis_index("y"), lax.axis_index("c")
    return (x, y, c), [(1 - x, y, c), (x, 1 - y, c), (1 - x, 1 - y, c)]


ANY = pl.BlockSpec(memory_space=pl.ANY)


def plane_allgather(slab):
    R = slab.shape[0]

    def body(src, out, send_sems, recv_sems, local_sem):
        (x, y, c), peers = _plane_peers()
        me = 2 * x + y
        local = pltpu.make_async_copy(src, out.at[me], local_sem)
        local.start()
        copies = []
        for j, peer in enumerate(peers):
            cp = pltpu.make_async_remote_copy(src_ref=src, dst_ref=out.at[me], send_sem=send_sems.at[j],
                                              recv_sem=recv_sems.at[j], device_id=peer, device_id_type=MESH)
            cp.start()
            copies.append(cp)
        for cp in copies:
            cp.wait()
        local.wait()

    return pl.pallas_call(
        body, name="plane_allgather", out_shape=SDS((N_CHIPS, R, LANES), slab.dtype),
        in_specs=[ANY], out_specs=ANY,
        scratch_shapes=[pltpu.SemaphoreType.DMA((3,)), pltpu.SemaphoreType.DMA((3,)), pltpu.SemaphoreType.DMA],
    )(slab)


def plane_alltoall(buf):
    R = buf.shape[1]

    def body(src, out, send_sems, recv_sems, local_sem):
        (x, y, c), peers = _plane_peers()
        me = 2 * x + y
        local = pltpu.make_async_copy(src.at[me], out.at[me], local_sem)
        local.start()
        copies = []
        for j, peer in enumerate(peers):
            cp = pltpu.make_async_remote_copy(src_ref=src.at[2 * peer[0] + peer[1]], dst_ref=out.at[me],
                                              send_sem=send_sems.at[j], recv_sem=recv_sems.at[j], device_id=peer,
                                              device_id_type=MESH)
            cp.start()
            copies.append(cp)
        for cp in copies:
            cp.wait()
        local.wait()

    return pl.pallas_call(
        body, name="plane_alltoall", out_shape=SDS((N_CHIPS, R, LANES), buf.dtype),
        in_specs=[ANY], out_specs=ANY,
        scratch_shapes=[pltpu.SemaphoreType.DMA((3,)), pltpu.SemaphoreType.DMA((3,)), pltpu.SemaphoreType.DMA],
    )(buf)


def sibling_swap(buf):
    def body(src, out, send_sem, recv_sem):
        x, y, c = lax.axis_index("x"), lax.axis_index("y"), lax.axis_index("c")
        cp = pltpu.make_async_remote_copy(src_ref=src, dst_ref=out, send_sem=send_sem, recv_sem=recv_sem,
                                          device_id=(x, y, 1 - c), device_id_type=MESH)
        cp.start()
        cp.wait()

    return pl.pallas_call(
        body, name="sibling_swap", out_shape=SDS(buf.shape, buf.dtype), in_specs=[ANY], out_specs=ANY,
        scratch_shapes=[pltpu.SemaphoreType.DMA, pltpu.SemaphoreType.DMA],
    )(buf)


def _chip_block(ref, ax, k, n):
    start = pl.multiple_of(k * n, n)
    return ref.at[:, pl.ds(start, n), :] if ax == 1 else ref.at[:, :, pl.ds(start, n)]


def gather_job(shards, axes):
    n = len(shards)
    fulls = []
    for s, ax in zip(shards, axes):
        shape = list(s.shape)
        shape[ax] *= N_CHIPS
        fulls.append(SDS(tuple(shape), s.dtype))

    def copies(srcs, outs, sems):
        send_sems, recv_sems, local_sems = sems
        (x, y, c), peers = _plane_peers()
        me = 2 * x + y
        cps = []
        for t in range(n):
            dst = _chip_block(outs[t], axes[t], me, srcs[t].shape[axes[t]])
            cps.append(pltpu.make_async_copy(srcs[t], dst, local_sems.at[t]))
            for j, peer in enumerate(peers):
                cps.append(pltpu.make_async_remote_copy(src_ref=srcs[t], dst_ref=dst, send_sem=send_sems.at[3 * t + j],
                                                        recv_sem=recv_sems.at[3 * t + j], device_id=peer,
                                                        device_id_type=MESH))
        return cps

    def start(srcs, outs, sems):
        for cp in copies(srcs, outs, sems):
            cp.start()

    def finish(srcs, outs, sems):
        for cp in copies(srcs, outs, sems):
            cp.wait()

    scratch = [pltpu.SemaphoreType.DMA((3 * n,)), pltpu.SemaphoreType.DMA((3 * n,)), pltpu.SemaphoreType.DMA((n,))]
    return dict(inputs=list(shards), out_shape=fulls, scratch=scratch, phases=(start, None, finish))


def exchange_job(grads, axes):
    n = len(grads)
    outs = []
    for g, ax in zip(grads, axes):
        shape = list(g.shape)
        shape[ax] //= N_CHIPS
        outs.append(SDS((N_CHIPS,) + tuple(shape), g.dtype))

    def copies(srcs, res, sems):
        mine, theirs = res[:n], res[n:]
        send_sems, recv_sems, local_sems, fsend_sems, frecv_sems = sems
        (x, y, c), peers = _plane_peers()
        sibling = (x, y, 1 - c)
        me = 2 * x + y
        blocks = [me] + [2 * px + py for (px, py, _) in peers]
        locals_, sends, forwards = [], [], []
        for t in range(n):
            ns = mine[t].shape[axes[t] + 1]
            locals_.append(pltpu.make_async_copy(_chip_block(srcs[t], axes[t], me, ns), mine[t].at[me],
                                                 local_sems.at[t]))
            for j, peer in enumerate(peers):
                sends.append(pltpu.make_async_remote_copy(
                    src_ref=_chip_block(srcs[t], axes[t], blocks[j + 1], ns), dst_ref=mine[t].at[me],
                    send_sem=send_sems.at[3 * t + j], recv_sem=recv_sems.at[3 * t + j], device_id=peer,
                    device_id_type=MESH))
            for q, blk in enumerate(blocks):
                forwards.append(pltpu.make_async_remote_copy(
                    src_ref=mine[t].at[blk], dst_ref=theirs[t].at[blk], send_sem=fsend_sems.at[4 * t + q],
                    recv_sem=frecv_sems.at[4 * t + q], device_id=sibling, device_id_type=MESH))
        return locals_, sends, forwards

    def start(srcs, res, sems):
        locals_, sends, _ = copies(srcs, res, sems)
        for cp in locals_ + sends:
            cp.start()

    def middle(srcs, res, sems):
        locals_, sends, forwards = copies(srcs, res, sems)
        for t in range(n):
            locals_[t].wait()
            for q in range(N_CHIPS):
                if q > 0:
                    sends[3 * t + q - 1].wait_recv()
                forwards[4 * t + q].start()

    def finish(srcs, res, sems):
        _, sends, forwards = copies(srcs, res, sems)
        for cp in sends:
            cp.wait_send()
        for fw in forwards:
            fw.wait()

    scratch = [pltpu.SemaphoreType.DMA((3 * n,)), pltpu.SemaphoreType.DMA((3 * n,)), pltpu.SemaphoreType.DMA((n,)),
               pltpu.SemaphoreType.DMA((4 * n,)), pltpu.SemaphoreType.DMA((4 * n,))]
    return dict(inputs=list(grads), out_shape=outs + outs, scratch=scratch, phases=(start, middle, finish))


def run_job(job, ins, outs, sems, first=None, mid=None, last=None):
    for phase, when in zip(job["phases"], (first, mid, last)):
        if phase is None:
            continue
        if when is None:
            phase(ins, outs, sems)
        else:
            pl.when(when)(functools.partial(phase, ins, outs, sems))


def job_call(job, *, name):
    ni, no = len(job["inputs"]), len(job["out_shape"])

    def body(*refs):
        run_job(job, refs[:ni], refs[ni:ni + no], refs[ni + no:])

    return pl.pallas_call(body, name=name, out_shape=job["out_shape"], in_specs=[ANY] * ni, out_specs=[ANY] * no,
                          scratch_shapes=job["scratch"])(*job["inputs"])


def _adam_update(g, w, m, v):
    mn = ADAM_B1 * m + (1.0 - ADAM_B1) * g
    vn = ADAM_B2 * v + (1.0 - ADAM_B2) * jnp.square(g)
    m_hat = mn / (1.0 - ADAM_B1 ** ADAM_STEP)
    v_hat = vn / (1.0 - ADAM_B2 ** ADAM_STEP)
    return -ADAM_LR * (m_hat / (jnp.sqrt(v_hat) + ADAM_EPS) + ADAM_WD * w), mn, vn


def adamw_reg(groups, w, m, v, *, name):
    L, rs, cs = w.shape
    half = L // 2
    tr = rs
    for cand in range(256, 15, -16):
        if rs % cand == 0:
            tr = cand
            break

    def plane_sums(a_r, b_r):
        pa = a_r[0].astype(F32)
        pb = b_r[0].astype(F32)
        for k in range(1, N_CHIPS):
            pa = pa + a_r[k].astype(F32)
            pb = pb + b_r[k].astype(F32)
        return pa + pb

    def body(a0_r, b0_r, a1_r, b1_r, w_r, m_r, v_r, g_o, d_o, m_o, v_o):
        g = jnp.where(pl.program_id(0) < half, plane_sums(a0_r, b0_r), plane_sums(a1_r, b1_r))
        d, mn, vn = _adam_update(g, w_r[...], m_r[...], v_r[...])
        g_o[...] = g
        d_o[...] = d
        m_o[...] = mn
        v_o[...] = vn

    first = pl.BlockSpec((N_CHIPS, None, tr, cs), lambda l, i: (0, jnp.minimum(l, half - 1), i, 0))
    second = pl.BlockSpec((N_CHIPS, None, tr, cs), lambda l, i: (0, jnp.maximum(l - half, 0), i, 0))
    s1 = pl.BlockSpec((None, tr, cs), lambda l, i: (l, i, 0))
    (a0, b0), (a1, b1) = groups
    return pl.pallas_call(
        body, name=name, grid=(L, rs // tr), in_specs=[first, first, second, second, s1, s1, s1], out_specs=[s1] * 4,
        out_shape=[SDS((L, rs, cs), F32)] * 4,
        compiler_params=pltpu.CompilerParams(dimension_semantics=("parallel", "parallel"),
                                             vmem_limit_bytes=VMEM_LIMIT),
    )(a0, b0, a1, b1, w, m, v)


def plane_sum(r4):
    R = r4.shape[1]

    def body(r_ref, o_ref):
        acc = r_ref[0].astype(F32)
        for k in range(1, N_CHIPS):
            acc = acc + r_ref[k].astype(F32)
        o_ref[...] = acc

    return pl.pallas_call(
        body, name="plane_sum", grid=(R // PACK_ROWS,),
        in_specs=[pl.BlockSpec((N_CHIPS, PACK_ROWS, LANES), lambda i: (0, i, 0))],
        out_specs=pl.BlockSpec((PACK_ROWS, LANES), lambda i: (i, 0)), out_shape=SDS((R, LANES), F32),
        compiler_params=pltpu.CompilerParams(dimension_semantics=("parallel",)),
    )(r4)


def adamw(pa, pb, w, m, v):
    R = w.shape[0]

    def body(pa_r, pb_r, w_r, m_r, v_r, g_o, d_o, m_o, v_o):
        g = pa_r[...] + pb_r[...]
        d, mn, vn = _adam_update(g, w_r[...], m_r[...], v_r[...])
        g_o[...] = g
        d_o[...] = d
        m_o[...] = mn
        v_o[...] = vn

    spec = pl.BlockSpec((PACK_ROWS, LANES), lambda i: (i, 0))
    return pl.pallas_call(
        body, name="adamw", grid=(R // PACK_ROWS,), in_specs=[spec] * 5, out_specs=[spec] * 4,
        out_shape=[SDS((R, LANES), F32)] * 4,
        compiler_params=pltpu.CompilerParams(dimension_semantics=("parallel",)),
    )(pa, pb, w, m, v)


REG_AXES = [WSPEC[n][2] for n in REG]


def reg_shards(wl, group):
    out = []
    for n in REG:
        half = WSPEC[n][1][0] // 2
        out.append(wl[n][group * half:(group + 1) * half].astype(BF16))
    return out


def reg_grads(grads, group):
    out = []
    for n in REG:
        if WSPEC[n][1][0] == DEPTH:
            layers, key = (2 * group, 2 * group + 1), n
        elif n == "hyb_w_out":
            layers, key = (2 * group,), "hyb_out"
        else:
            layers, key = (2 * group + 1,), "mla_out"
        out.append(jnp.stack([grads[l][key] for l in layers]))
    return out


def gather_misc(wl):
    full = {}
    sharded = [n for n in MISC if WSPEC[n][2] is not None]
    pieces = [wl[n].astype(BF16) if WSPEC[n][3] else lax.bitcast_convert_type(wl[n], BF16) for n in sharded]
    got = plane_allgather(_pack(pieces, BF16, 16))
    shapes = [v.shape for v in pieces]
    per_chip = [_unpack(got[k], shapes) for k in range(N_CHIPS)]
    for idx, n in enumerate(sharded):
        parts = [per_chip[k][idx] for k in range(N_CHIPS)]
        if not WSPEC[n][3]:
            parts = [lax.bitcast_convert_type(v, F32) for v in parts]
        full[n] = jnp.concatenate(parts, axis=WSPEC[n][2])
    for n in MISC:
        if WSPEC[n][2] is None:
            full[n] = wl[n]
    return full


def kernel(x, p, positions, ffn1_pre_g, ffn1_w_in, ffn1_w_down, ffn1_post_g, mix_pre_g, mix_post_g, ffn2_pre_g, ffn2_w_in, ffn2_w_down, ffn2_post_g, ple_pre_g, ple_w_gate, ple_w_proj, ple_post_g, hyb_w_in, gm_ln_g, gm_ln_b, gm_w_s, gm_b_s, ssd_conv_w, ssd_conv_b, ssd_dt_bias, ssd_a_log, ssd_d, ssd_norm_g, hyb_w_out, mla_w_in, mla_q_norm_g, mla_kv_norm_g, mla_w_uq, mla_w_ukv, mla_w_out, loss_target, m_ffn1_pre_g, m_ffn1_w_in, m_ffn1_w_down, m_ffn1_post_g, m_mix_pre_g, m_mix_post_g, m_ffn2_pre_g, m_ffn2_w_in, m_ffn2_w_down, m_ffn2_post_g, m_ple_pre_g, m_ple_w_gate, m_ple_w_proj, m_ple_post_g, m_hyb_w_in, m_gm_ln_g, m_gm_ln_b, m_gm_w_s, m_gm_b_s, m_ssd_conv_w, m_ssd_conv_b, m_ssd_dt_bias, m_ssd_a_log, m_ssd_d, m_ssd_norm_g, m_hyb_w_out, m_mla_w_in, m_mla_q_norm_g, m_mla_kv_norm_g, m_mla_w_uq, m_mla_w_ukv, m_mla_w_out, v_ffn1_pre_g, v_ffn1_w_in, v_ffn1_w_down, v_ffn1_post_g, v_mix_pre_g, v_mix_post_g, v_ffn2_pre_g, v_ffn2_w_in, v_ffn2_w_down, v_ffn2_post_g, v_ple_pre_g, v_ple_w_gate, v_ple_w_proj, v_ple_post_g, v_hyb_w_in, v_gm_ln_g, v_gm_ln_b, v_gm_w_s, v_gm_b_s, v_ssd_conv_w, v_ssd_conv_b, v_ssd_dt_bias, v_ssd_a_log, v_ssd_d, v_ssd_norm_g, v_hyb_w_out, v_mla_w_in, v_mla_q_norm_g, v_mla_kv_norm_g, v_mla_w_uq, v_mla_w_ukv, v_mla_w_out):
    args = locals()
    wl = {n: args[n] for n in WNAMES}
    ml = {n: args["m_" + n] for n in WNAMES}
    vl = {n: args["v_" + n] for n in WNAMES}

    full = gather_misc(wl)
    nreg = len(REG)
    halves = [WSPEC[n][1][0] // 2 for n in REG]
    first = job_call(gather_job(reg_shards(wl, 0), REG_AXES), name="gather_first")
    for t, n in enumerate(REG):
        full[n] = [(first[t], l) for l in range(halves[t])] + [None] * halves[t]
    exchanged = [None, None]

    class Jobs(NoJobs):
        def fwd_job(self, i):
            return gather_job(reg_shards(wl, 1), REG_AXES) if i == 1 else None

        def fwd_done(self, i, outs):
            for t, n in enumerate(REG):
                for l in range(halves[t]):
                    full[n][halves[t] + l] = (outs[t], l)

        def bwd_job(self, i, grads):
            return exchange_job(reg_grads(grads, 1), REG_AXES) if i == 1 else None

        def bwd_done(self, i, outs):
            exchanged[1] = outs

    rope = rope_tables(positions[0])
    T = x.shape[1]
    sq, dx, grads = local_step(x[0], p.reshape(DEPTH, T, p.shape[-1]), rope, loss_target[0],
                               lambda i: layer_weights(full, i), Jobs())
    loss = lax.psum(0.5 * jnp.sum(sq) / D_MODEL, ("x", "y", "c"))

    res = {}
    exchanged[0] = job_call(exchange_job(reg_grads(grads, 0), REG_AXES), name="exchange_first")
    for t, n in enumerate(REG):
        res[n] = adamw_reg([(e[t], e[nreg + t]) for e in exchanged], wl[n], ml[n], vl[n], name=f"adamw_{n}")
    fg = full_grads(grads)
    dest = [_pack([_chip_slice(fg[n], WSPEC[n][2], k) for n in MISC], BF16, PACK_ROWS) for k in range(N_CHIPS)]
    mine = plane_sum(plane_alltoall(jnp.stack(dest)))
    other = sibling_swap(mine)
    slabs = adamw(mine, other, *[_pack([d[n] for n in MISC], F32, PACK_ROWS) for d in (wl, ml, vl)])
    shapes = [wl[n].shape for n in MISC]
    unpacked = [_unpack(s, shapes) for s in slabs]
    for idx, n in enumerate(MISC):
        res[n] = [u[idx] for u in unpacked]
    return (loss, dx[None], *[res[n][k] for k in range(4) for n in WNAMES])
```

```python
import functools
import math

import jax
import jax.numpy as jnp
import numpy as np
from jax import lax
from jax.experimental import pallas as pl
from jax.experimental.pallas import tpu as pltpu

F32 = jnp.float32
BF16 = jnp.bfloat16
SDS = jax.ShapeDtypeStruct
MESH = pl.DeviceIdType.MESH
HIGHEST = lax.Precision.HIGHEST

D_MODEL = 1024
DEPTH = 4
D_FF = 2816
NORM_EPS = 1e-6
LN_EPS = 1e-5
GM_HEADS = 8
CHUNK = 128
SSD_HEADS = 16
SSD_HEAD_DIM = 64
SSD_INNER = 1024
SSD_STATE = 128
SSD_CONV = 4
SSD_CONV_CH = 1536
HYB_MAIN = 4608
HYB_IN = 4624
HYB_PAD = 5120
MLA_HEADS = 16
MLA_NOPE = 128
MLA_ROPE = 64
MLA_QK = 192
MLA_QPAD = 256
MLA_Q_LORA = 256
MLA_KV_LORA = 128
ROPE_BASE = 10000.0
ADAM_LR = 0.001
ADAM_B1 = 0.9
ADAM_B2 = 0.999
ADAM_EPS = 1e-08
ADAM_WD = 0.01
ADAM_STEP = 10

N_CHIPS = 4
LANES = 128
VMEM_LIMIT = 56 * 1024 * 1024
PACK_ROWS = 2048

WEIGHTS = [
    ("ffn1_pre_g", (4, 1024), None, False),
    ("ffn1_w_in", (4, 1024, 5632), 2, True),
    ("ffn1_w_down", (4, 2816, 1024), 1, True),
    ("ffn1_post_g", (4, 1024), None, False),
    ("mix_pre_g", (4, 1024), None, False),
    ("mix_post_g", (4, 1024), None, False),
    ("ffn2_pre_g", (4, 1024), None, False),
    ("ffn2_w_in", (4, 1024, 5632), 2, True),
    ("ffn2_w_down", (4, 2816, 1024), 1, True),
    ("ffn2_post_g", (4, 1024), None, False),
    ("ple_pre_g", (4, 1024), None, False),
    ("ple_w_gate", (4, 1024, 1024), 1, True),
    ("ple_w_proj", (4, 256, 1024), 2, True),
    ("ple_post_g", (4, 1024), None, False),
    ("hyb_w_in", (2, 1024, 4624), 2, True),
    ("gm_ln_g", (2, 1024), None, False),
    ("gm_ln_b", (2, 1024), None, False),
    ("gm_w_s", (2, 8, 128, 128), None, False),
    ("gm_b_s", (2, 8, 128), None, False),
    ("ssd_conv_w", (2, 4, 1536), 2, False),
    ("ssd_conv_b", (2, 1536), None, False),
    ("ssd_dt_bias", (2, 16), None, False),
    ("ssd_a_log", (2, 16), None, False),
    ("ssd_d", (2, 16), None, False),
    ("ssd_norm_g", (2, 1024), None, False),
    ("hyb_w_out", (2, 2048, 1024), 1, True),
    ("mla_w_in", (2, 1024, 448), 1, True),
    ("mla_q_norm_g", (2, 256), 1, False),
    ("mla_kv_norm_g", (2, 128), None, False),
    ("mla_w_uq", (2, 256, 3072), 2, True),
    ("mla_w_ukv", (2, 128, 4096), 2, True),
    ("mla_w_out", (2, 2048, 1024), 1, True),
]
WNAMES = [w[0] for w in WEIGHTS]
WSPEC = {w[0]: w for w in WEIGHTS}
REG = ["ffn1_w_in", "ffn1_w_down", "ffn2_w_in", "ffn2_w_down", "ple_w_gate", "ple_w_proj", "hyb_w_out", "mla_w_out"]
WIN = "hyb_w_in"
WIN_SHARD = 4624 // 4
WIN_STRIDE = (WIN_SHARD // 128) * 128
WIN_WIDTH = -(-(WIN_SHARD + 3 * (WIN_SHARD - WIN_STRIDE)) // 128) * 128
MOVED = REG + [WIN]
MISC = [n for n in WNAMES if n not in MOVED]


def _pick(dim, target):
    if dim <= target:
        return dim
    t = (target // LANES) * LANES
    while t >= LANES:
        if dim % t == 0:
            return t
        t -= LANES
    return dim


def mm(a, b, *, ta=False, tb=False, out_dtype=F32, name, tm=1024, tn=1024, tk=1024):
    a, la = a if isinstance(a, tuple) else (a, None)
    b, lb = b if isinstance(b, tuple) else (b, None)
    if ta:
        K, M = a.shape[-2:]
    else:
        M, K = a.shape[-2:]
    if tb:
        N, K2 = b.shape[-2:]
    else:
        K2, N = b.shape[-2:]
    assert K == K2, (a.shape, b.shape, ta, tb)
    bm, bn, bk = _pick(M, tm), _pick(N, tn), _pick(K, tk)
    nk = K // bk

    def spec(shape, idx, layer):
        if layer is None:
            return pl.BlockSpec(shape, idx)
        return pl.BlockSpec((None,) + shape, lambda i, j, k: (layer,) + idx(i, j, k))

    a_spec = spec((bk, bm), lambda i, j, k: (k, i), la) if ta else spec((bm, bk), lambda i, j, k: (i, k), la)
    b_spec = spec((bn, bk), lambda i, j, k: (j, k), lb) if tb else spec((bk, bn), lambda i, j, k: (k, j), lb)
    dn = (((0 if ta else 1,), (1 if tb else 0,)), ((), ()))

    def body(a_ref, b_ref, o_ref, acc_ref):
        k = pl.program_id(2)

        @pl.when(k == 0)
        def _():
            acc_ref[...] = jnp.zeros_like(acc_ref)

        acc_ref[...] += lax.dot_general(a_ref[...].astype(BF16), b_ref[...].astype(BF16), dn,
                                        preferred_element_type=F32)

        @pl.when(k == nk - 1)
        def _():
            o_ref[...] = acc_ref[...].astype(o_ref.dtype)

    return pl.pallas_call(
        body, name=name, grid=(M // bm, N // bn, nk),
        in_specs=[a_spec, b_spec], out_specs=pl.BlockSpec((bm, bn), lambda i, j, k: (i, j)),
        out_shape=SDS((M, N), out_dtype), scratch_shapes=[pltpu.VMEM((bm, bn), F32)],
        compiler_params=pltpu.CompilerParams(dimension_semantics=("parallel", "parallel", "arbitrary"),
                                             vmem_limit_bytes=VMEM_LIMIT),
    )(a, b)


def row_call(fn, xs, ps, outs, accs=(), *, tb, name, reverse=False):
    xs = [x if isinstance(x, tuple) else (x, x.shape[1], 0) for x in xs]
    T = xs[0][0].shape[0]
    tb = min(tb, T)
    n = T // tb
    assert n * tb == T
    nx, npar, no, na = len(xs), len(ps), len(outs), len(accs)

    def ridx(i):
        return n - 1 - i if reverse else i

    in_specs = [pl.BlockSpec((tb, w), functools.partial(lambda i, cb: (ridx(i), cb), cb=cb)) for (_, w, cb) in xs]
    in_specs += [pl.BlockSpec(p.shape, functools.partial(lambda i, nd: (0,) * nd, nd=p.ndim)) for p in ps]
    out_specs = [pl.BlockSpec((tb, c), lambda i: (ridx(i), 0)) for (c, _) in outs]
    out_specs += [pl.BlockSpec(s, functools.partial(lambda i, nd: (0,) * nd, nd=len(s))) for s in accs]
    out_shape = [SDS((T, c), dt) for (c, dt) in outs] + [SDS(s, F32) for s in accs]

    def body(*refs):
        xr, pr = refs[:nx], refs[nx:nx + npar]
        orf, ar = refs[nx + npar:nx + npar + no], refs[nx + npar + no:]
        res = fn(*[r[...] for r in xr], *[r[...] for r in pr])
        for r, v in zip(orf, res[:no]):
            r[...] = v.astype(r.dtype)
        if na:
            @pl.when(pl.program_id(0) == 0)
            def _():
                for r in ar:
                    r[...] = jnp.zeros_like(r)

            for r, v in zip(ar, res[no:]):
                r[...] += v.astype(F32)

    res = pl.pallas_call(
        body, name=name, grid=(n,), in_specs=in_specs, out_specs=out_specs, out_shape=out_shape,
        compiler_params=pltpu.CompilerParams(dimension_semantics=("arbitrary",), vmem_limit_bytes=VMEM_LIMIT),
    )(*[x[0] for x in xs], *ps)
    return res


def _f32(*a):
    return [v.astype(F32) for v in a]


def t_rms(x, g):
    return x * lax.rsqrt(jnp.mean(x * x, axis=-1, keepdims=True) + NORM_EPS) * g


def t_swiglu(gu):
    return jax.nn.silu(gu[:, :D_FF]) * gu[:, D_FF:]


def t_ple(gl, pp, g):
    return t_rms(jax.nn.sigmoid(gl) * pp, g)


def _iota(shape, d):
    return lax.broadcasted_iota(jnp.int32, shape, d)


def _bdot(a, b, dn=(((1,), (0,)), ((), ()))):
    return lax.dot_general(a.astype(BF16), b.astype(BF16), dn, preferred_element_type=F32)


def _hdot(a, b):
    return jnp.dot(a, b, precision=HIGHEST, preferred_element_type=F32)


NT = (((1,), (1,)), ((), ()))
TN = (((0,), (0,)), ((), ()))


def t_gmlp(uv, ln_g, ln_b, w_s, b_st):
    tb = uv.shape[0]
    guv = jax.nn.gelu(uv)
    u, v = guv[:, :1024], guv[:, 1024:]
    tri = _iota((CHUNK, CHUNK), 1) <= _iota((CHUNK, CHUNK), 0)
    rows = []
    for c in range(tb // CHUNK):
        vc = v[c * CHUNK:(c + 1) * CHUNK]
        heads = []
        for h in range(GM_HEADS):
            sl = slice(h * 128, (h + 1) * 128)
            vh = vc[:, sl]
            xc = vh - jnp.mean(vh, axis=-1, keepdims=True)
            var = jnp.mean(xc * xc, axis=-1, keepdims=True)
            y = xc * lax.rsqrt(var + LN_EPS) * ln_g[:, sl] + ln_b[:, sl]
            wm = jnp.where(tri, w_s[sl, :], 0.0)
            heads.append(_bdot(wm, y) + b_st[:, h:h + 1])
        rows.append(jnp.concatenate(heads, axis=1))
    mixed = rows[0] if len(rows) == 1 else jnp.concatenate(rows, axis=0)
    return u * mixed


def t_ssd(pre, dtr, z, st, dt_bias, a_log, d_exp, norm_g):
    L = CHUNK
    xbc = jax.nn.silu(pre)
    xs, bm, cm = xbc[:, :1024], xbc[:, 1024:1280], xbc[:, 1280:1536]
    valid = _iota((1, LANES), 1) < SSD_HEADS
    dt16 = jnp.where(valid, jax.nn.softplus(dtr + dt_bias), 0.0)
    a16 = jnp.where(valid, -jnp.exp(a_log), 0.0)
    da16 = dt16 * a16
    tri = _iota((L, L), 1) <= _iota((L, L), 0)
    acs16 = _hdot(tri.astype(F32), da16)
    hh, cc = _iota((LANES, 1024), 0), _iota((LANES, 1024), 1)
    expand = ((cc >= hh * SSD_HEAD_DIM) & (cc < (hh + 1) * SSD_HEAD_DIM)).astype(F32)
    acs = _hdot(acs16, expand)
    dte = _hdot(dt16, expand)
    alast = jnp.sum(jnp.where(_iota((L, 1024), 0) == L - 1, acs, 0.0), axis=0, keepdims=True)
    xd = xs * dte
    groups = [slice(0, 512), slice(512, 1024)]
    bg = [bm[:, :128], bm[:, 128:]]
    cg = [cm[:, :128], cm[:, 128:]]
    yoff = jnp.concatenate([_bdot(cg[g], st[:, groups[g]]) for g in range(2)], axis=1) * jnp.exp(acs)
    xdw = xd * jnp.exp(alast - acs)
    s_t = jnp.concatenate([_bdot(bg[g], xdw[:, groups[g]], TN) for g in range(2)], axis=1)
    st_new = st * jnp.exp(alast) + s_t
    cb = [_bdot(cg[g], bg[g], NT) for g in range(2)]
    acs16_t = acs16.T
    lo = _iota((1, LANES), 1) < SSD_HEAD_DIM
    slabs = []
    for j in range(SSD_HEADS // 2):
        g = j // 4
        xslab = xd[:, j * 128:(j + 1) * 128]
        acc = None
        for half in range(2):
            h = 2 * j + half
            seg = acs16[:, h:h + 1] - acs16_t[h:h + 1, :]
            mmat = cb[g] * jnp.exp(jnp.where(tri, seg, -1e30))
            xm = jnp.where(lo if half == 0 else jnp.logical_not(lo), xslab, 0.0)
            term = _bdot(mmat, xm)
            acc = term if acc is None else acc + term
        slabs.append(acc)
    y = jnp.concatenate(slabs, axis=1) + yoff + d_exp * xs
    yg = y * jax.nn.silu(z)
    outs = []
    for g in range(2):
        t = yg[:, groups[g]]
        outs.append(t * lax.rsqrt(jnp.mean(t * t, axis=-1, keepdims=True) + NORM_EPS) * norm_g[:, groups[g]])
    return jnp.concatenate(outs, axis=1), st_new


def t_kprep(c_all, cs, sn, qg, kvg):
    cqn = t_rms(c_all[:, :256], qg)
    ckvn = t_rms(c_all[:, 256:384], kvg)
    kr = c_all[:, 384:512] * cs + c_all[:, 512:640] * sn
    return cqn, ckvn, kr


def t_qrope(qb, c256, s256):
    scale = MLA_QK ** -0.5
    half = MLA_HEADS * MLA_QPAD
    outs = []
    for h in range(MLA_HEADS):
        a = qb[:, h * MLA_QPAD:(h + 1) * MLA_QPAD]
        b = qb[:, half + h * MLA_QPAD:half + (h + 1) * MLA_QPAD]
        outs.append((a * c256 + b * s256) * scale)
    return jnp.concatenate(outs, axis=1)


def rms_fwd(h, g, *, name, tb=512):
    def fn(h, g):
        return (t_rms(h.astype(F32), g),)
    return row_call(fn, [h], [g], [(h.shape[1], BF16)], tb=tb, name=name)[0]


def rms_bwd(h, dhn, dres, g, *, name, tb=256):
    def fn(h, dhn, dres, g):
        h, dhn, dres = _f32(h, dhn, dres)
        _, vjp = jax.vjp(t_rms, h, g)
        dh, dg = vjp(dhn)
        return dres + dh, dg
    return row_call(fn, [h, dhn, dres], [g], [(h.shape[1], F32)], [g.shape], tb=tb, name=name)


def post_fwd(h, f, g, scale, *, name, tb=512):
    def fn(h, f, g):
        return (h + scale * t_rms(f.astype(F32), g),)
    return row_call(fn, [h, f], [g], [(h.shape[1], F32)], tb=tb, name=name)[0]


def post_bwd(f, dout, g, scale, *, name, tb=256):
    def fn(f, dout, g):
        f, dout = _f32(f, dout)
        _, vjp = jax.vjp(lambda f, g: scale * t_rms(f, g), f, g)
        return vjp(dout)
    return row_call(fn, [f, dout], [g], [(f.shape[1], BF16)], [g.shape], tb=tb, name=name)


def swiglu_fwd(gu, *, name, tb=256):
    def fn(gu):
        return (t_swiglu(gu.astype(F32)),)
    return row_call(fn, [gu], [], [(D_FF, BF16)], tb=tb, name=name)[0]


def swiglu_bwd(gu, da, *, name, tb=256):
    def fn(gu, da):
        gu, da = _f32(gu, da)
        _, vjp = jax.vjp(t_swiglu, gu)
        return vjp(da)
    return row_call(fn, [gu, da], [], [(2 * D_FF, BF16)], tb=tb, name=name)[0]


def ple_fwd(h, gl, pp, g, *, name, tb=512):
    def fn(h, gl, pp, g):
        return (h + t_ple(gl, pp, g),)
    return row_call(fn, [h, gl, pp], [g], [(D_MODEL, F32)], tb=tb, name=name)[0]


def ple_bwd(gl, pp, dout, g, *, name, tb=256):
    def fn(gl, pp, dout, g):
        _, vjp = jax.vjp(t_ple, gl, pp, g)
        return vjp(dout)
    return row_call(fn, [gl, pp, dout], [g], [(D_MODEL, BF16), (D_MODEL, BF16)], [g.shape], tb=tb, name=name)


def gmlp_fwd(proj, ln_g, ln_b, w_s, b_st, *, name, tb=256):
    def fn(uv, ln_g, ln_b, w_s, b_st):
        return (t_gmlp(uv, ln_g, ln_b, w_s, b_st),)
    return row_call(fn, [(proj, 2048, 0)], [ln_g, ln_b, w_s, b_st], [(1024, BF16)], tb=tb, name=name)[0]


def gmlp_bwd(proj, dya, ln_g, ln_b, w_s, b_st, *, name, tb=128):
    def fn(uv, dya, ln_g, ln_b, w_s, b_st):
        _, vjp = jax.vjp(t_gmlp, uv, ln_g, ln_b, w_s, b_st)
        return vjp(dya.astype(F32))
    return row_call(fn, [(proj, 2048, 0), (dya, 1024, 0)], [ln_g, ln_b, w_s, b_st], [(2048, BF16)],
                    [ln_g.shape, ln_b.shape, w_s.shape, b_st.shape], tb=tb, name=name)


def kprep_fwd(c_all, cs, sn, qg, kvg, *, name, tb=512):
    return row_call(t_kprep, [c_all, cs, sn], [qg, kvg], [(256, BF16), (128, BF16), (128, BF16)], tb=tb, name=name)


def kprep_bwd(c_all, cs, sn, dcqn, dckvn, dkr, qg, kvg, *, name, tb=256):
    def fn(c_all, cs, sn, dcqn, dckvn, dkr, qg, kvg):
        dcqn, dckvn, dkr = _f32(dcqn, dckvn, dkr)
        _, vjp = jax.vjp(lambda c, qg, kvg: t_kprep(c, cs, sn, qg, kvg), c_all, qg, kvg)
        return vjp((dcqn, dckvn, dkr))
    return row_call(fn, [c_all, cs, sn, dcqn, dckvn, dkr], [qg, kvg], [(640, BF16)], [qg.shape, kvg.shape],
                    tb=tb, name=name)


def qrope_fwd(qb, c256, s256, *, name, tb=256):
    def fn(qb, c256, s256):
        return (t_qrope(qb, c256, s256),)
    return row_call(fn, [qb, c256, s256], [], [(MLA_HEADS * MLA_QPAD, BF16)], tb=tb, name=name)[0]


def qrope_bwd(dq, c256, s256, *, name, tb=256):
    def fn(dq, c256, s256):
        scale = MLA_QK ** -0.5
        a, b = [], []
        for h in range(MLA_HEADS):
            d = dq[:, h * MLA_QPAD:(h + 1) * MLA_QPAD] * scale
            a.append(d * c256)
            b.append(d * s256)
        return (jnp.concatenate(a + b, axis=1),)
    return row_call(fn, [dq, c256, s256], [], [(2 * MLA_HEADS * MLA_QPAD, BF16)], tb=tb, name=name)[0]


STAT_SPLIT = 64
ATTN_UNROLL = 2


def stats_fwd(do, o, lse, *, name, tb=512):
    def fn(do, o, lse):
        do, o = _f32(do, o)
        low = _iota((1, 128), 1) < STAT_SPLIT
        outs = []
        for h in range(MLA_HEADS):
            sl = slice(h * 128, (h + 1) * 128)
            dl = jnp.sum(do[:, sl] * o[:, sl], axis=-1, keepdims=True)
            outs.append(jnp.where(low, lse[:, sl], dl))
        return (jnp.concatenate(outs, axis=1),)
    return row_call(fn, [do, o, lse], [], [(2048, F32)], tb=tb, name=name)[0]


def headsum(dkr_h, *, name, tb=512):
    def fn(d):
        acc = d[:, :128]
        for h in range(1, MLA_HEADS):
            acc = acc + d[:, h * 128:(h + 1) * 128]
        return (acc,)
    return row_call(fn, [dkr_h], [], [(128, F32)], tb=tb, name=name)[0]


def loss_fwd(y, t, *, name, tb=512):
    def fn(y, t):
        e = y - t
        return e * (1.0 / D_MODEL), jnp.sum(e * e, axis=0, keepdims=True)
    return row_call(fn, [y, t], [], [(D_MODEL, F32)], [(1, D_MODEL)], tb=tb, name=name)


def conv_fwd(proj, w, b, *, name, tb=256):
    T = proj.shape[0]
    n = T // tb
    hb = tb // CHUNK
    C = SSD_CONV_CH

    def body(cur, prev, w_ref, b_ref, o_ref, scr):
        i = pl.program_id(0)
        scr[pl.ds(0, CHUNK), :] = jnp.where(i > 0, prev[...], 0.0)
        scr[pl.ds(CHUNK, tb), :] = cur[...]
        y = b_ref[...] + w_ref[3:4, :] * cur[...]
        for k in range(SSD_CONV - 1):
            y = y + w_ref[k:k + 1, :] * scr[pl.ds(CHUNK - (SSD_CONV - 1) + k, tb), :]
        o_ref[...] = y

    return pl.pallas_call(
        body, name=name, grid=(n,),
        in_specs=[pl.BlockSpec((tb, C), lambda i: (i, 2)),
                  pl.BlockSpec((CHUNK, C), lambda i: (jnp.maximum(i * hb - 1, 0), 2)),
                  pl.BlockSpec((SSD_CONV, C), lambda i: (0, 0)), pl.BlockSpec((1, C), lambda i: (0, 0))],
        out_specs=pl.BlockSpec((tb, C), lambda i: (i, 0)), out_shape=SDS((T, C), F32),
        scratch_shapes=[pltpu.VMEM((CHUNK + tb, C), F32)],
        compiler_params=pltpu.CompilerParams(dimension_semantics=("arbitrary",), vmem_limit_bytes=VMEM_LIMIT),
    )(proj, proj, w, b)


def conv_bwd(dpre, proj, w, *, name, tb=256):
    T = proj.shape[0]
    n = T // tb
    hb = tb // CHUNK
    nh = T // CHUNK
    C = SSD_CONV_CH

    def body(dcur, dnext, xcur, xprev, w_ref, dx_ref, dw_ref, db_ref, dscr, xscr):
        i = pl.program_id(0)

        @pl.when(i == 0)
        def _():
            dw_ref[...] = jnp.zeros_like(dw_ref)
            db_ref[...] = jnp.zeros_like(db_ref)

        d = dcur[...]
        dscr[pl.ds(0, tb), :] = d
        dscr[pl.ds(tb, CHUNK), :] = jnp.where(i < n - 1, dnext[...], 0.0)
        xscr[pl.ds(0, CHUNK), :] = jnp.where(i > 0, xprev[...], 0.0)
        xscr[pl.ds(CHUNK, tb), :] = xcur[...]
        dx = w_ref[3:4, :] * d
        for k in range(SSD_CONV - 1):
            dx = dx + w_ref[k:k + 1, :] * dscr[pl.ds(SSD_CONV - 1 - k, tb), :]
        dx_ref[...] = dx.astype(dx_ref.dtype)
        for k in range(SSD_CONV):
            xk = xscr[pl.ds(CHUNK - (SSD_CONV - 1) + k, tb), :]
            dw_ref[k:k + 1, :] += jnp.sum(d * xk, axis=0, keepdims=True)
        db_ref[...] += jnp.sum(d, axis=0, keepdims=True)

    return pl.pallas_call(
        body, name=name, grid=(n,),
        in_specs=[pl.BlockSpec((tb, C), lambda i: (i, 0)),
                  pl.BlockSpec((CHUNK, C), lambda i: (jnp.minimum((i + 1) * hb, nh - 1), 0)),
                  pl.BlockSpec((tb, C), lambda i: (i, 2)),
                  pl.BlockSpec((CHUNK, C), lambda i: (jnp.maximum(i * hb - 1, 0), 2)),
                  pl.BlockSpec((SSD_CONV, C), lambda i: (0, 0))],
        out_specs=[pl.BlockSpec((tb, C), lambda i: (i, 0)), pl.BlockSpec((SSD_CONV, C), lambda i: (0, 0)),
                   pl.BlockSpec((1, C), lambda i: (0, 0))],
        out_shape=[SDS((T, C), BF16), SDS((SSD_CONV, C), F32), SDS((1, C), F32)],
        scratch_shapes=[pltpu.VMEM((tb + CHUNK, C), F32), pltpu.VMEM((CHUNK + tb, C), F32)],
        compiler_params=pltpu.CompilerParams(dimension_semantics=("arbitrary",), vmem_limit_bytes=VMEM_LIMIT),
    )(dpre, dpre, proj, proj, w)


def _ssd_specs(nc, rev):
    def r(c):
        return nc - 1 - c if rev else c
    pre = pl.BlockSpec((CHUNK, SSD_CONV_CH), lambda c: (r(c), 0))
    dtr = pl.BlockSpec((CHUNK, LANES), lambda c: (r(c), HYB_MAIN // LANES))
    z = pl.BlockSpec((CHUNK, 1024), lambda c: (r(c), 2))
    row = pl.BlockSpec((CHUNK, 1024), lambda c: (r(c), 0))
    return pre, dtr, z, row


def _pspec(shape):
    return pl.BlockSpec(shape, lambda c: (0,) * len(shape))


def ssd_fwd(pre, proj, dt_bias, a_log, d_exp, norm_g, *, name):
    T = pre.shape[0]
    nc = T // CHUNK
    s_pre, s_dt, s_z, s_row = _ssd_specs(nc, False)

    def body(pre_r, dt_r, z_r, b_r, a_r, d_r, g_r, y_r, sv_r, st):
        @pl.when(pl.program_id(0) == 0)
        def _():
            st[...] = jnp.zeros_like(st)

        s0 = st[...]
        sv_r[...] = s0
        y, s1 = t_ssd(pre_r[...], dt_r[...], z_r[...], s0, b_r[...], a_r[...], d_r[...], g_r[...])
        y_r[...] = y.astype(y_r.dtype)
        st[...] = s1

    return pl.pallas_call(
        body, name=name, grid=(nc,),
        in_specs=[s_pre, s_dt, s_z, _pspec((1, LANES)), _pspec((1, LANES)), _pspec((1, 1024)), _pspec((1, 1024))],
        out_specs=[s_row, s_row], out_shape=[SDS((T, 1024), BF16), SDS((T, 1024), F32)],
        scratch_shapes=[pltpu.VMEM((SSD_STATE, 1024), F32)],
        compiler_params=pltpu.CompilerParams(dimension_semantics=("arbitrary",), vmem_limit_bytes=VMEM_LIMIT),
    )(pre, proj, proj, dt_bias, a_log, d_exp, norm_g)


def ssd_bwd(pre, proj, states, dyab, dt_bias, a_log, d_exp, norm_g, *, name):
    T = pre.shape[0]
    nc = T // CHUNK
    s_pre, s_dt, s_z, s_row = _ssd_specs(nc, True)
    s_dtout = pl.BlockSpec((CHUNK, LANES), lambda c: (nc - 1 - c, 0))
    s_dy = pl.BlockSpec((CHUNK, 1024), lambda c: (nc - 1 - c, 1))

    def body(pre_r, dt_r, z_r, sv_r, dy_r, b_r, a_r, d_r, g_r, dpre_r, ddt_r, dz_r, db_r, da_r, dd_r, dg_r, dst):
        @pl.when(pl.program_id(0) == 0)
        def _():
            dst[...] = jnp.zeros_like(dst)
            for r in (db_r, da_r, dd_r, dg_r):
                r[...] = jnp.zeros_like(r)

        _, vjp = jax.vjp(t_ssd, pre_r[...], dt_r[...], z_r[...], sv_r[...], b_r[...], a_r[...], d_r[...], g_r[...])
        dpre, ddt, dz, ds0, db, da, dd, dg = vjp((dy_r[...].astype(F32), dst[...]))
        dpre_r[...] = dpre
        ddt_r[...] = ddt.astype(ddt_r.dtype)
        dz_r[...] = dz.astype(dz_r.dtype)
        dst[...] = ds0
        db_r[...] += db
        da_r[...] += da
        dd_r[...] += dd
        dg_r[...] += dg

    return pl.pallas_call(
        body, name=name, grid=(nc,),
        in_specs=[s_pre, s_dt, s_z, s_row, s_dy, _pspec((1, LANES)), _pspec((1, LANES)), _pspec((1, 1024)),
                  _pspec((1, 1024))],
        out_specs=[s_pre, s_dtout, s_row, _pspec((1, LANES)), _pspec((1, LANES)), _pspec((1, 1024)), _pspec((1, 1024))],
        out_shape=[SDS((T, SSD_CONV_CH), F32), SDS((T, LANES), BF16), SDS((T, 1024), BF16),
                   SDS((1, LANES), F32), SDS((1, LANES), F32), SDS((1, 1024), F32), SDS((1, 1024), F32)],
        scratch_shapes=[pltpu.VMEM((SSD_STATE, 1024), F32)],
        compiler_params=pltpu.CompilerParams(dimension_semantics=("arbitrary",), vmem_limit_bytes=VMEM_LIMIT),
    )(pre, proj, proj, states, dyab, dt_bias, a_log, d_exp, norm_g)


def _attn_tile(T, target=512):
    return min(target, T // 2)


def _causal(tq):
    return _iota((tq, tq), 1) <= _iota((tq, tq), 0)


def _job_parts(job):
    if job is None:
        return 0, 0, [], [], []
    ni, no = len(job["inputs"]), len(job["out_shape"])
    return ni, no, list(job["inputs"]), list(job["out_shape"]), list(job["scratch"])


def _job_phase(job, which, refs, when):
    if job is not None and job["phases"][which] is not None:
        pl.when(when)(functools.partial(job["phases"][which], *refs))


def attn_fwd(q, kv, kr, *, name, job=None):
    T = q.shape[0]
    tq = _attn_tile(T)
    nq = T // tq
    ng = MLA_HEADS // 2
    ni, no, jins, jouts, jscratch = _job_parts(job)

    def body(*refs):
        q_ref, kn_ref, v_ref, kr_ref = refs[:4]
        o_ref, lse_ref = refs[4 + ni:6 + ni]
        jrefs = (refs[4:4 + ni], refs[6 + ni:6 + ni + no], refs[6 + ni + no:])
        g = pl.program_id(0)
        qi = pl.program_id(1)
        _job_phase(job, 0, jrefs, (g == 0) & (qi == 0))
        qv = [q_ref[:, e * MLA_QPAD:(e + 1) * MLA_QPAD] for e in range(2)]

        def blk(ki, masked, carry):
            rows = pl.ds(pl.multiple_of(ki * tq, tq), tq)
            kr = kr_ref[rows, :]
            out = []
            for e in range(2):
                m, l, acc = carry[e]
                cols = slice(e * 128, (e + 1) * 128)
                k = jnp.concatenate([kn_ref[rows, cols], kr], axis=1)
                s = lax.dot_general(qv[e], k, NT, preferred_element_type=F32)
                if masked:
                    s = jnp.where(_causal(tq), s, -1e30)
                m_new = jnp.maximum(m, jnp.max(s, axis=-1, keepdims=True))
                p = jnp.exp(s - m_new)
                alpha = jnp.exp(m - m_new)
                l = alpha * l + jnp.sum(p, axis=-1, keepdims=True)
                acc = alpha * acc + jnp.dot(p.astype(BF16), v_ref[rows, cols], preferred_element_type=F32)
                out.append((m_new, l, acc))
            return tuple(out)

        one = (jnp.full((tq, 1), -1e30, F32), jnp.zeros((tq, 1), F32), jnp.zeros((tq, 128), F32))
        carry = lax.fori_loop(0, qi, lambda ki, c: blk(ki, False, c), (one, one))
        carry = blk(qi, True, carry)
        for e in range(2):
            m, l, acc = carry[e]
            cols = slice(e * 128, (e + 1) * 128)
            o_ref[:, cols] = (acc / l).astype(o_ref.dtype)
            lse_ref[:, cols] = jnp.broadcast_to(m + jnp.log(l), (tq, 128))
        _job_phase(job, 2, jrefs, (g == ng - 1) & (qi == nq - 1))

    res = pl.pallas_call(
        body, name=name, grid=(ng, nq),
        in_specs=[pl.BlockSpec((tq, 2 * MLA_QPAD), lambda g, i: (i, g)),
                  pl.BlockSpec((T, 256), lambda g, i: (0, g)),
                  pl.BlockSpec((T, 256), lambda g, i: (0, ng + g)),
                  pl.BlockSpec((T, 128), lambda g, i: (0, 0))] + [ANY] * ni,
        out_specs=[pl.BlockSpec((tq, 256), lambda g, i: (i, g)), pl.BlockSpec((tq, 256), lambda g, i: (i, g))]
        + [ANY] * no,
        out_shape=[SDS((T, 2048), BF16), SDS((T, 2048), F32)] + jouts, scratch_shapes=jscratch,
        compiler_params=pltpu.CompilerParams(dimension_semantics=("arbitrary", "arbitrary"),
                                             vmem_limit_bytes=VMEM_LIMIT),
    )(q, kv, kv, kr, *jins)
    return res[0], res[1], list(res[2:])


def attn_bwd(q, kv, kr, do, stats, *, name, job=None):
    T = q.shape[0]
    tq = _attn_tile(T)
    nq = T // tq
    ni, no, jins, jouts, jscratch = _job_parts(job)

    def body(*refs):
        q_ref, do_ref, st_ref, kn_ref, v_ref, kr_ref = refs[:6]
        dq_ref, dkn_ref, dv_ref, dkr_ref = refs[6 + ni:10 + ni]
        jrefs = (refs[6:6 + ni], refs[10 + ni:10 + ni + no], refs[10 + ni + no:])
        h = pl.program_id(0)
        ki = pl.program_id(1)
        _job_phase(job, 0, jrefs, (h == 0) & (ki == 0))
        _job_phase(job, 1, jrefs, (h == MLA_HEADS // 2) & (ki == 0))

        @pl.when(ki == 0)
        def _():
            dq_ref[...] = jnp.zeros_like(dq_ref)

        k = jnp.concatenate([kn_ref[...], kr_ref[...]], axis=1)
        v = v_ref[...]

        def blk(qi, masked, carry):
            dk, dv = carry
            rows = pl.ds(pl.multiple_of(qi * tq, tq), tq)
            qv, dov = q_ref[rows, :], do_ref[rows, :]
            lse, dl = st_ref[rows, 0:1], st_ref[rows, STAT_SPLIT:STAT_SPLIT + 1]
            s = lax.dot_general(qv, k, NT, preferred_element_type=F32)
            if masked:
                s = jnp.where(_causal(tq), s, -1e30)
            p = jnp.exp(s - lse)
            dv = dv + lax.dot_general(p.astype(BF16), dov, TN, preferred_element_type=F32)
            dp = lax.dot_general(dov, v, NT, preferred_element_type=F32)
            ds = (p * (dp - dl)).astype(BF16)
            dk = dk + lax.dot_general(ds, qv, TN, preferred_element_type=F32)
            dq_ref[rows, :] += jnp.dot(ds, k, preferred_element_type=F32)
            return dk, dv

        carry = blk(ki, True, (jnp.zeros((tq, MLA_QPAD), F32), jnp.zeros((tq, 128), F32)))
        rest = nq - 1 - ki

        def group(j, c):
            for u in range(ATTN_UNROLL):
                c = blk(ki + 1 + ATTN_UNROLL * j + u, False, c)
            return c

        carry = lax.fori_loop(0, rest // ATTN_UNROLL, group, carry)
        for u in range(ATTN_UNROLL - 1):
            carry = lax.cond(rest % ATTN_UNROLL > u, functools.partial(lambda c, u: blk(nq - 1 - u, False, c), u=u),
                             lambda c: c, carry)
        dk, dv = carry
        dkn_ref[...] = dk[:, :128].astype(dkn_ref.dtype)
        dkr_ref[...] = dk[:, 128:]
        dv_ref[...] = dv.astype(dv_ref.dtype)
        _job_phase(job, 2, jrefs, (h == MLA_HEADS - 1) & (ki == nq - 1))

    res = pl.pallas_call(
        body, name=name, grid=(MLA_HEADS, nq),
        in_specs=[pl.BlockSpec((T, MLA_QPAD), lambda h, i: (0, h)),
                  pl.BlockSpec((T, 128), lambda h, i: (0, h)),
                  pl.BlockSpec((T, 128), lambda h, i: (0, h)),
                  pl.BlockSpec((tq, 128), lambda h, i: (i, h)),
                  pl.BlockSpec((tq, 128), lambda h, i: (i, MLA_HEADS + h)),
                  pl.BlockSpec((tq, 128), lambda h, i: (i, 0))] + [ANY] * ni,
        out_specs=[pl.BlockSpec((T, MLA_QPAD), lambda h, i: (0, h)),
                   pl.BlockSpec((tq, 128), lambda h, i: (i, h)), pl.BlockSpec((tq, 128), lambda h, i: (i, h)),
                   pl.BlockSpec((tq, 128), lambda h, i: (i, h))] + [ANY] * no,
        out_shape=[SDS((T, MLA_HEADS * MLA_QPAD), F32), SDS((T, 2048), BF16), SDS((T, 2048), BF16),
                   SDS((T, 2048), F32)] + jouts, scratch_shapes=jscratch,
        compiler_params=pltpu.CompilerParams(dimension_semantics=("arbitrary", "arbitrary"),
                                             vmem_limit_bytes=VMEM_LIMIT),
    )(q, do, stats, kv, kv, kr, *jins)
    return res[0], res[1], res[2], res[3], list(res[4:])


def _ffn_fwd(h, pre_g, w_in, w_down, post_g, tag):
    hn = rms_fwd(h, pre_g, name=f"{tag}_pre")
    gu = mm(hn, w_in, out_dtype=BF16, name=f"{tag}_in", tn=1408)
    a = swiglu_fwd(gu, name=f"{tag}_act")
    f = mm(a, w_down, name=f"{tag}_down", tk=1408)
    h2 = post_fwd(h, f, post_g, 0.5, name=f"{tag}_post")
    return h2, (h, hn, gu, a, f)


def _ffn_bwd(dh2, saved, pre_g, w_in, w_down, post_g, tag):
    h, hn, gu, a, f = saved
    df, dpost = post_bwd(f, dh2, post_g, 0.5, name=f"{tag}_post_b")
    da = mm(df, w_down, tb=True, name=f"{tag}_down_bx", tn=1408)
    dw_down = mm(a, df, ta=True, out_dtype=BF16, name=f"{tag}_down_bw", tm=1408)
    dgu = swiglu_bwd(gu, da, name=f"{tag}_act_b")
    dhn = mm(dgu, w_in, tb=True, name=f"{tag}_in_bx", tk=1408)
    dw_in = mm(hn, dgu, ta=True, out_dtype=BF16, name=f"{tag}_in_bw", tn=1408)
    dh, dpre = rms_bwd(h, dhn, dh2, pre_g, name=f"{tag}_pre_b")
    return dh, dpre, dw_in, dw_down, dpost


def _hyb_fwd(hn, w, tag):
    proj = mm(hn, w["hyb_in"], name=f"{tag}_in")
    ya = gmlp_fwd(proj, w["ln_g"], w["ln_b"], w["w_s"], w["b_st"], name=f"{tag}_gmlp")
    pre = conv_fwd(proj, w["conv_w"], w["conv_b"], name=f"{tag}_conv")
    yb, states = ssd_fwd(pre, proj, w["dt_bias"], w["a_log"], w["d_exp"], w["norm_g"], name=f"{tag}_ssd")
    yab = jnp.concatenate([ya, yb], axis=1)
    mixed = mm(yab, w["hyb_out"], name=f"{tag}_out")
    return mixed, (proj, pre, states, yab)


def _hyb_bwd(dmixed, hn, saved, w, tag):
    proj, pre, states, yab = saved
    dyab = mm(dmixed, w["hyb_out"], tb=True, name=f"{tag}_out_bx")
    dw_out = mm(yab, dmixed, ta=True, out_dtype=BF16, name=f"{tag}_out_bw")
    duv, dln_g, dln_b, dw_s, db_st = gmlp_bwd(proj, dyab, w["ln_g"], w["ln_b"], w["w_s"], w["b_st"],
                                              name=f"{tag}_gmlp_b")
    dpre, ddt, dz, ddt_bias, da_log, dd_exp, dnorm_g = ssd_bwd(
        pre, proj, states, dyab, w["dt_bias"], w["a_log"], w["d_exp"], w["norm_g"], name=f"{tag}_ssd_b")
    dxbc, dconv_w, dconv_b = conv_bwd(dpre, proj, w["conv_w"], name=f"{tag}_conv_b")
    pad = jnp.zeros((duv.shape[0], HYB_PAD - HYB_MAIN - LANES), BF16)
    dproj = jnp.concatenate([duv, dz, dxbc, ddt, pad], axis=1)
    dhn = mm(dproj, w["hyb_in"], tb=True, name=f"{tag}_in_bx")
    dw_in = mm(hn, dproj, ta=True, out_dtype=BF16, name=f"{tag}_in_bw")
    g = dict(hyb_in=dw_in, hyb_out=dw_out, ln_g=dln_g, ln_b=dln_b, w_s=dw_s, b_st=db_st, conv_w=dconv_w,
             conv_b=dconv_b, dt_bias=ddt_bias, a_log=da_log, d_exp=dd_exp, norm_g=dnorm_g)
    return dhn, g


def _mla_fwd(hn, w, rope, tag, job=None):
    cs, sn, c256, s256 = rope
    c_all = mm(hn, w["mla_in"], name=f"{tag}_in")
    cqn, ckvn, kr = kprep_fwd(c_all, cs, sn, w["q_g"], w["kv_g"], name=f"{tag}_kprep")
    qb = mm(cqn, w["uq"], name=f"{tag}_uq")
    q = qrope_fwd(qb, c256, s256, name=f"{tag}_qrope")
    kv = mm(ckvn, w["ukv"], out_dtype=BF16, name=f"{tag}_ukv")
    o, lse, jouts = attn_fwd(q, kv, kr, name=f"{tag}_attn", job=job)
    mixed = mm(o, w["mla_out"], name=f"{tag}_out")
    return mixed, (c_all, cqn, ckvn, kr, q, kv, o, lse), jouts


def _mla_bwd(dmixed, hn, saved, w, rope, tag, job=None):
    cs, sn, c256, s256 = rope
    c_all, cqn, ckvn, kr, q, kv, o, lse = saved
    do = mm(dmixed, w["mla_out"], tb=True, out_dtype=BF16, name=f"{tag}_out_bx")
    dw_out = mm(o, dmixed, ta=True, out_dtype=BF16, name=f"{tag}_out_bw")
    stats = stats_fwd(do, o, lse, name=f"{tag}_stats")
    dq, dkn, dv, dkr_h, jouts = attn_bwd(q, kv, kr, do, stats, name=f"{tag}_attn_b", job=job)
    dkr = headsum(dkr_h, name=f"{tag}_dkr")
    dkv = jnp.concatenate([dkn, dv], axis=1)
    dckvn = mm(dkv, w["ukv"], tb=True, name=f"{tag}_ukv_bx")
    dw_ukv = mm(ckvn, dkv, ta=True, name=f"{tag}_ukv_bw")
    dqb = qrope_bwd(dq, c256, s256, name=f"{tag}_qrope_b")
    dcqn = mm(dqb, w["uq"], tb=True, name=f"{tag}_uq_bx")
    dw_uq = mm(cqn, dqb, ta=True, name=f"{tag}_uq_bw")
    dc_all, dq_g, dkv_g = kprep_bwd(c_all, cs, sn, dcqn, dckvn, dkr, w["q_g"], w["kv_g"], name=f"{tag}_kprep_b")
    dhn = mm(dc_all, w["mla_in"], tb=True, name=f"{tag}_in_bx")
    dw_in = mm(hn, dc_all, ta=True, name=f"{tag}_in_bw")
    g = dict(mla_in=dw_in, mla_out=dw_out, uq=dw_uq, ukv=dw_ukv, q_g=dq_g, kv_g=dkv_g)
    return dhn, g, jouts


class NoJobs:
    def fwd_job(self, i):
        return None

    def fwd_done(self, i, outs):
        pass

    def bwd_job(self, i, grads):
        return None

    def bwd_done(self, i, outs):
        pass


def local_step(x, p, rope, target, weights_of, jobs=NoJobs()):
    h = x
    saved = []
    lw = []
    for i in range(DEPTH):
        w = weights_of(i)
        lw.append(w)
        t = f"l{i}"
        h, s1 = _ffn_fwd(h, w["ffn1_pre_g"], w["ffn1_w_in"], w["ffn1_w_down"], w["ffn1_post_g"], f"{t}_f1")
        h1 = h
        hn = rms_fwd(h1, w["mix_pre_g"], name=f"{t}_mixpre")
        if i % 2 == 0:
            mixed, sm = _hyb_fwd(hn, w, f"{t}_hyb")
        else:
            job = jobs.fwd_job(i)
            mixed, sm, jouts = _mla_fwd(hn, w, rope, f"{t}_mla", job)
            if job is not None:
                jobs.fwd_done(i, jouts)
        h = post_fwd(h1, mixed, w["mix_post_g"], 1.0, name=f"{t}_mixpost")
        h, s2 = _ffn_fwd(h, w["ffn2_pre_g"], w["ffn2_w_in"], w["ffn2_w_down"], w["ffn2_post_g"], f"{t}_f2")
        h3 = h
        hn3 = rms_fwd(h3, w["ple_pre_g"], name=f"{t}_plepre")
        gl = mm(hn3, w["ple_w_gate"], name=f"{t}_plegate")
        pp = mm((p, i), w["ple_w_proj"], name=f"{t}_pleproj")
        h = ple_fwd(h3, gl, pp, w["ple_post_g"], name=f"{t}_plepost")
        saved.append((s1, h1, hn, sm, mixed, s2, h3, hn3, gl, pp))

    dh, sq = loss_fwd(h, target, name="loss")
    grads = [None] * DEPTH
    for i in reversed(range(DEPTH)):
        w = lw[i]
        t = f"l{i}"
        s1, h1, hn, sm, mixed, s2, h3, hn3, gl, pp = saved[i]
        g = {}
        dgl, dpp, g["ple_post_g"] = ple_bwd(gl, pp, dh, w["ple_post_g"], name=f"{t}_plepost_b")
        dhn3 = mm(dgl, w["ple_w_gate"], tb=True, name=f"{t}_plegate_bx")
        g["ple_w_gate"] = mm(hn3, dgl, ta=True, out_dtype=BF16, name=f"{t}_plegate_bw")
        g["ple_w_proj"] = mm((p, i), dpp, ta=True, out_dtype=BF16, name=f"{t}_pleproj_bw")
        dh, g["ple_pre_g"] = rms_bwd(h3, dhn3, dh, w["ple_pre_g"], name=f"{t}_plepre_b")
        dh, g["ffn2_pre_g"], g["ffn2_w_in"], g["ffn2_w_down"], g["ffn2_post_g"] = _ffn_bwd(
            dh, s2, w["ffn2_pre_g"], w["ffn2_w_in"], w["ffn2_w_down"], w["ffn2_post_g"], f"{t}_f2")
        dmixed, g["mix_post_g"] = post_bwd(mixed, dh, w["mix_post_g"], 1.0, name=f"{t}_mixpost_b")
        if i % 2 == 0:
            dhn, gm = _hyb_bwd(dmixed, hn, sm, w, f"{t}_hyb")
        else:
            job = jobs.bwd_job(i, grads)
            dhn, gm, jouts = _mla_bwd(dmixed, hn, sm, w, rope, f"{t}_mla", job)
            if job is not None:
                jobs.bwd_done(i, jouts)
        g.update(gm)
        dh, g["mix_pre_g"] = rms_bwd(h1, dhn, dh, w["mix_pre_g"], name=f"{t}_mixpre_b")
        dh, g["ffn1_pre_g"], g["ffn1_w_in"], g["ffn1_w_down"], g["ffn1_post_g"] = _ffn_bwd(
            dh, s1, w["ffn1_pre_g"], w["ffn1_w_in"], w["ffn1_w_down"], w["ffn1_post_g"], f"{t}_f1")
        grads[i] = g
    return sq, dh, grads


def _zeros_like_cols(a, n):
    return jnp.zeros(a.shape[:-1] + (n,), a.dtype)


def layer_weights(full, i):
    j = i // 2
    row = lambda v: v.reshape(1, -1)
    w = {k: row(full[k][i]) for k in ("ffn1_pre_g", "ffn1_post_g", "mix_pre_g", "mix_post_g", "ffn2_pre_g",
                                      "ffn2_post_g", "ple_pre_g", "ple_post_g")}
    for k in ("ffn1_w_in", "ffn1_w_down", "ffn2_w_in", "ffn2_w_down", "ple_w_gate", "ple_w_proj"):
        w[k] = full[k][i]
    if i % 2 == 0:
        hw = full["hyb_w_in"][j]
        w["hyb_in"] = jnp.concatenate([hw, _zeros_like_cols(hw, HYB_PAD - HYB_IN)], axis=1)
        w["hyb_out"] = full["hyb_w_out"][j]
        w["ln_g"], w["ln_b"] = row(full["gm_ln_g"][j]), row(full["gm_ln_b"][j])
        w["w_s"] = full["gm_w_s"][j].reshape(GM_HEADS * CHUNK, CHUNK)
        w["b_st"] = jnp.pad(full["gm_b_s"][j].T, ((0, 0), (0, LANES - GM_HEADS)))
        w["conv_w"], w["conv_b"] = full["ssd_conv_w"][j], row(full["ssd_conv_b"][j])
        pad16 = lambda v: jnp.pad(v.reshape(1, -1), ((0, 0), (0, LANES - SSD_HEADS)))
        w["dt_bias"], w["a_log"] = pad16(full["ssd_dt_bias"][j]), pad16(full["ssd_a_log"][j])
        w["d_exp"] = row(jnp.repeat(full["ssd_d"][j], SSD_HEAD_DIM))
        w["norm_g"] = row(full["ssd_norm_g"][j])
    else:
        wi = full["mla_w_in"][j]
        z64 = _zeros_like_cols(wi, 64)
        w["mla_in"] = jnp.concatenate([wi[:, :384], wi[:, 384:448], z64, -wi[:, 416:448], wi[:, 384:416], z64], axis=1)
        uq = full["mla_w_uq"][j].reshape(MLA_Q_LORA, MLA_HEADS, MLA_QK)
        zq = jnp.zeros((MLA_Q_LORA, MLA_HEADS, 64), uq.dtype)
        pad_part = jnp.concatenate([uq, zq], axis=2)
        swp_part = jnp.concatenate([jnp.zeros_like(uq[:, :, :128]), -uq[:, :, 160:192], uq[:, :, 128:160], zq], axis=2)
        w["uq"] = jnp.concatenate([pad_part.reshape(MLA_Q_LORA, -1), swp_part.reshape(MLA_Q_LORA, -1)], axis=1)
        ukv = full["mla_w_ukv"][j].reshape(MLA_KV_LORA, MLA_HEADS, 256)
        w["ukv"] = jnp.concatenate([ukv[:, :, :128].reshape(MLA_KV_LORA, -1), ukv[:, :, 128:].reshape(MLA_KV_LORA, -1)],
                                   axis=1)
        w["mla_out"] = full["mla_w_out"][j]
        w["q_g"], w["kv_g"] = row(full["mla_q_norm_g"][j]), row(full["mla_kv_norm_g"][j])
    return w


def full_grads(grads):
    out = {}
    stack = lambda k, idx: jnp.stack([grads[i][k] for i in idx])
    every, even, odd = range(DEPTH), range(0, DEPTH, 2), range(1, DEPTH, 2)
    for k in ("ffn1_pre_g", "ffn1_post_g", "mix_pre_g", "mix_post_g", "ffn2_pre_g", "ffn2_post_g", "ple_pre_g",
              "ple_post_g"):
        out[k] = stack(k, every).reshape(DEPTH, D_MODEL)
    for k in ("ffn1_w_in", "ffn1_w_down", "ffn2_w_in", "ffn2_w_down", "ple_w_gate", "ple_w_proj"):
        out[k] = stack(k, every)
    out["hyb_w_in"] = stack("hyb_in", even)[:, :, :HYB_IN]
    out["hyb_w_out"] = stack("hyb_out", even)
    out["gm_ln_g"] = stack("ln_g", even).reshape(2, 1024)
    out["gm_ln_b"] = stack("ln_b", even).reshape(2, 1024)
    out["gm_w_s"] = stack("w_s", even).reshape(2, GM_HEADS, CHUNK, CHUNK)
    out["gm_b_s"] = jnp.swapaxes(stack("b_st", even)[:, :, :GM_HEADS], 1, 2)
    out["ssd_conv_w"] = stack("conv_w", even)
    out["ssd_conv_b"] = stack("conv_b", even).reshape(2, SSD_CONV_CH)
    out["ssd_dt_bias"] = stack("dt_bias", even)[:, 0, :SSD_HEADS]
    out["ssd_a_log"] = stack("a_log", even)[:, 0, :SSD_HEADS]
    out["ssd_d"] = stack("d_exp", even).reshape(2, SSD_HEADS, SSD_HEAD_DIM).sum(axis=-1)
    out["ssd_norm_g"] = stack("norm_g", even).reshape(2, 1024)
    dwi = stack("mla_in", odd)
    out["mla_w_in"] = jnp.concatenate([dwi[:, :, :384], dwi[:, :, 384:416] + dwi[:, :, 544:576],
                                       dwi[:, :, 416:448] - dwi[:, :, 512:544]], axis=2)
    duq = stack("uq", odd)
    half = MLA_HEADS * MLA_QPAD
    dp = duq[:, :, :half].reshape(2, MLA_Q_LORA, MLA_HEADS, MLA_QPAD)
    ds = duq[:, :, half:].reshape(2, MLA_Q_LORA, MLA_HEADS, MLA_QPAD)
    out["mla_w_uq"] = jnp.concatenate([dp[..., :128], dp[..., 128:160] + ds[..., 160:192],
                                       dp[..., 160:192] - ds[..., 128:160]], axis=-1).reshape(2, MLA_Q_LORA, -1)
    dukv = stack("ukv", odd)
    dk = dukv[:, :, :2048].reshape(2, MLA_KV_LORA, MLA_HEADS, 128)
    dv = dukv[:, :, 2048:].reshape(2, MLA_KV_LORA, MLA_HEADS, 128)
    out["mla_w_ukv"] = jnp.concatenate([dk, dv], axis=-1).reshape(2, MLA_KV_LORA, -1)
    out["mla_w_out"] = stack("mla_out", odd)
    out["mla_q_norm_g"] = stack("q_g", odd).reshape(2, MLA_Q_LORA)
    out["mla_kv_norm_g"] = stack("kv_g", odd).reshape(2, MLA_KV_LORA)
    return out


def rope_tables(positions):
    T = positions.shape[0]
    inv = 1.0 / (ROPE_BASE ** (jnp.arange(0, MLA_ROPE, 2, dtype=F32) / MLA_ROPE))
    ang = positions.astype(F32)[:, None] * inv
    cos, sin = jnp.cos(ang), jnp.sin(ang)
    z64 = jnp.zeros((T, 64), F32)
    cs = jnp.concatenate([cos, cos, z64], axis=1)
    sn = jnp.concatenate([sin, sin, z64], axis=1)
    c256 = jnp.concatenate([jnp.ones((T, 128), F32), cs], axis=1)
    s256 = jnp.concatenate([jnp.zeros((T, 128), F32), sn], axis=1)
    return cs, sn, c256, s256


def _rows(n):
    return -(-n // LANES)


def _pack(pieces, dtype, row_multiple):
    flat = []
    total = 0
    for a in pieces:
        v = a.reshape(-1).astype(dtype)
        padn = _rows(v.shape[0]) * LANES - v.shape[0]
        if padn:
            v = jnp.concatenate([v, jnp.zeros((padn,), dtype)])
        flat.append(v)
        total += v.shape[0] // LANES
    tail = -total % row_multiple
    if tail:
        flat.append(jnp.zeros((tail * LANES,), dtype))
    return jnp.concatenate(flat).reshape(-1, LANES)


def _unpack(slab, shapes):
    out = []
    r = 0
    for s in shapes:
        n = int(np.prod(s))
        nr = _rows(n)
        out.append(slab[r:r + nr].reshape(-1)[:n].reshape(s))
        r += nr
    return out


def _shard_shape(shape, ax):
    if ax is None:
        return tuple(shape)
    s = list(shape)
    s[ax] //= N_CHIPS
    return tuple(s)


def _chip_slice(a, ax, k):
    if ax is None:
        return a
    n = a.shape[ax] // N_CHIPS
    return lax.slice_in_dim(a, k * n, (k + 1) * n, axis=ax)


def _plane_peers():
    x, y, c = lax.axis_index("x"), lax.axis_index("y"), lax.axis_index("c")
    return (x, y, c), [(1 - x, y, c), (x, 1 - y, c), (1 - x, 1 - y, c)]


ANY = pl.BlockSpec(memory_space=pl.ANY)


def plane_allgather(slab):
    R = slab.shape[0]

    def body(src, out, send_sems, recv_sems, local_sem):
        (x, y, c), peers = _plane_peers()
        me = 2 * x + y
        local = pltpu.make_async_copy(src, out.at[me], local_sem)
        local.start()
        copies = []
        for j, peer in enumerate(peers):
            cp = pltpu.make_async_remote_copy(src_ref=src, dst_ref=out.at[me], send_sem=send_sems.at[j],
                                              recv_sem=recv_sems.at[j], device_id=peer, device_id_type=MESH)
            cp.start()
            copies.append(cp)
        for cp in copies:
            cp.wait()
        local.wait()

    return pl.pallas_call(
        body, name="plane_allgather", out_shape=SDS((N_CHIPS, R, LANES), slab.dtype),
        in_specs=[ANY], out_specs=ANY,
        scratch_shapes=[pltpu.SemaphoreType.DMA((3,)), pltpu.SemaphoreType.DMA((3,)), pltpu.SemaphoreType.DMA],
    )(slab)


def plane_alltoall(buf):
    R = buf.shape[1]

    def body(src, out, send_sems, recv_sems, local_sem):
        (x, y, c), peers = _plane_peers()
        me = 2 * x + y
        local = pltpu.make_async_copy(src.at[me], out.at[me], local_sem)
        local.start()
        copies = []
        for j, peer in enumerate(peers):
            cp = pltpu.make_async_remote_copy(src_ref=src.at[2 * peer[0] + peer[1]], dst_ref=out.at[me],
                                              send_sem=send_sems.at[j], recv_sem=recv_sems.at[j], device_id=peer,
                                              device_id_type=MESH)
            cp.start()
            copies.append(cp)
        for cp in copies:
            cp.wait()
        local.wait()

    return pl.pallas_call(
        body, name="plane_alltoall", out_shape=SDS((N_CHIPS, R, LANES), buf.dtype),
        in_specs=[ANY], out_specs=ANY,
        scratch_shapes=[pltpu.SemaphoreType.DMA((3,)), pltpu.SemaphoreType.DMA((3,)), pltpu.SemaphoreType.DMA],
    )(buf)


def sibling_swap(buf):
    def body(src, out, send_sem, recv_sem):
        x, y, c = lax.axis_index("x"), lax.axis_index("y"), lax.axis_index("c")
        cp = pltpu.make_async_remote_copy(src_ref=src, dst_ref=out, send_sem=send_sem, recv_sem=recv_sem,
                                          device_id=(x, y, 1 - c), device_id_type=MESH)
        cp.start()
        cp.wait()

    return pl.pallas_call(
        body, name="sibling_swap", out_shape=SDS(buf.shape, buf.dtype), in_specs=[ANY], out_specs=ANY,
        scratch_shapes=[pltpu.SemaphoreType.DMA, pltpu.SemaphoreType.DMA],
    )(buf)


def _chip_block(ref, ax, k, n, stride=None):
    stride = n if stride is None else stride
    start = pl.multiple_of(k * stride, math.gcd(n, stride))
    return ref.at[:, pl.ds(start, n), :] if ax == 1 else ref.at[:, :, pl.ds(start, n)]


def gather_job(shards, axes):
    n = len(shards)
    fulls = []
    for s, ax in zip(shards, axes):
        if ax == "stack":
            fulls.append(SDS((N_CHIPS,) + tuple(s.shape), s.dtype))
            continue
        shape = list(s.shape)
        shape[ax] *= N_CHIPS
        fulls.append(SDS(tuple(shape), s.dtype))

    def copies(srcs, outs, sems):
        send_sems, recv_sems, local_sems = sems
        (x, y, c), peers = _plane_peers()
        me = 2 * x + y
        cps = []
        for t in range(n):
            if axes[t] == "stack":
                dst = outs[t].at[me]
            else:
                dst = _chip_block(outs[t], axes[t], me, srcs[t].shape[axes[t]])
            cps.append(pltpu.make_async_copy(srcs[t], dst, local_sems.at[t]))
            for j, peer in enumerate(peers):
                cps.append(pltpu.make_async_remote_copy(src_ref=srcs[t], dst_ref=dst, send_sem=send_sems.at[3 * t + j],
                                                        recv_sem=recv_sems.at[3 * t + j], device_id=peer,
                                                        device_id_type=MESH))
        return cps

    def start(srcs, outs, sems):
        for cp in copies(srcs, outs, sems):
            cp.start()

    def finish(srcs, outs, sems):
        for cp in copies(srcs, outs, sems):
            cp.wait()

    scratch = [pltpu.SemaphoreType.DMA((3 * n,)), pltpu.SemaphoreType.DMA((3 * n,)), pltpu.SemaphoreType.DMA((n,))]
    return dict(inputs=list(shards), out_shape=fulls, scratch=scratch, phases=(start, None, finish))


def exchange_job(grads, axes):
    n = len(grads)
    outs = []
    spans = []
    for g, ax in zip(grads, axes):
        ax, width, stride = ax if isinstance(ax, tuple) else (ax, g.shape[ax] // N_CHIPS, None)
        spans.append((ax, width, stride))
        shape = list(g.shape)
        shape[ax] = width
        outs.append(SDS((N_CHIPS,) + tuple(shape), g.dtype))

    def copies(srcs, res, sems):
        mine, theirs = res[:n], res[n:]
        send_sems, recv_sems, local_sems, fsend_sems, frecv_sems = sems
        (x, y, c), peers = _plane_peers()
        sibling = (x, y, 1 - c)
        me = 2 * x + y
        blocks = [me] + [2 * px + py for (px, py, _) in peers]
        locals_, sends, forwards = [], [], []
        for t in range(n):
            ax, width, stride = spans[t]
            locals_.append(pltpu.make_async_copy(_chip_block(srcs[t], ax, me, width, stride), mine[t].at[me],
                                                 local_sems.at[t]))
            for j, peer in enumerate(peers):
                sends.append(pltpu.make_async_remote_copy(
                    src_ref=_chip_block(srcs[t], ax, blocks[j + 1], width, stride), dst_ref=mine[t].at[me],
                    send_sem=send_sems.at[3 * t + j], recv_sem=recv_sems.at[3 * t + j], device_id=peer,
                    device_id_type=MESH))
            for q, blk in enumerate(blocks):
                forwards.append(pltpu.make_async_remote_copy(
                    src_ref=mine[t].at[blk], dst_ref=theirs[t].at[blk], send_sem=fsend_sems.at[4 * t + q],
                    recv_sem=frecv_sems.at[4 * t + q], device_id=sibling, device_id_type=MESH))
        return locals_, sends, forwards

    def start(srcs, res, sems):
        locals_, sends, _ = copies(srcs, res, sems)
        for cp in locals_ + sends:
            cp.start()

    def middle(srcs, res, sems):
        locals_, sends, forwards = copies(srcs, res, sems)
        for t in range(n):
            locals_[t].wait()
            for q in range(N_CHIPS):
                if q > 0:
                    sends[3 * t + q - 1].wait_recv()
                forwards[4 * t + q].start()

    def finish(srcs, res, sems):
        _, sends, forwards = copies(srcs, res, sems)
        for cp in sends:
            cp.wait_send()
        for fw in forwards:
            fw.wait()

    scratch = [pltpu.SemaphoreType.DMA((3 * n,)), pltpu.SemaphoreType.DMA((3 * n,)), pltpu.SemaphoreType.DMA((n,)),
               pltpu.SemaphoreType.DMA((4 * n,)), pltpu.SemaphoreType.DMA((4 * n,))]
    return dict(inputs=list(grads), out_shape=outs + outs, scratch=scratch, phases=(start, middle, finish))


def run_job(job, ins, outs, sems, first=None, mid=None, last=None):
    for phase, when in zip(job["phases"], (first, mid, last)):
        if phase is None:
            continue
        if when is None:
            phase(ins, outs, sems)
        else:
            pl.when(when)(functools.partial(phase, ins, outs, sems))


def job_call(job, *, name):
    ni, no = len(job["inputs"]), len(job["out_shape"])

    def body(*refs):
        run_job(job, refs[:ni], refs[ni:ni + no], refs[ni + no:])

    return pl.pallas_call(body, name=name, out_shape=job["out_shape"], in_specs=[ANY] * ni, out_specs=[ANY] * no,
                          scratch_shapes=job["scratch"])(*job["inputs"])


def _adam_update(g, w, m, v):
    mn = ADAM_B1 * m + (1.0 - ADAM_B1) * g
    vn = ADAM_B2 * v + (1.0 - ADAM_B2) * jnp.square(g)
    m_hat = mn / (1.0 - ADAM_B1 ** ADAM_STEP)
    v_hat = vn / (1.0 - ADAM_B2 ** ADAM_STEP)
    return -ADAM_LR * (m_hat / (jnp.sqrt(v_hat) + ADAM_EPS) + ADAM_WD * w), mn, vn


def _row_tile(rs):
    for cand in range(256, 15, -16):
        if rs % cand == 0:
            return cand
    return rs


def _group_sum(half, a0_r, b0_r, a1_r, b1_r):
    def plane_sums(a_r, b_r):
        pa = a_r[0].astype(F32)
        pb = b_r[0].astype(F32)
        for k in range(1, N_CHIPS):
            pa = pa + a_r[k].astype(F32)
            pb = pb + b_r[k].astype(F32)
        return pa + pb
    return jnp.where(pl.program_id(0) < half, plane_sums(a0_r, b0_r), plane_sums(a1_r, b1_r))


def _group_specs(half, tr, cs):
    first = pl.BlockSpec((N_CHIPS, None, tr, cs), lambda l, i: (0, jnp.minimum(l, half - 1), i, 0))
    second = pl.BlockSpec((N_CHIPS, None, tr, cs), lambda l, i: (0, jnp.maximum(l - half, 0), i, 0))
    return [first, first, second, second]


def adamw_reg(groups, w, m, v, *, name):
    L, rs, cs = w.shape
    tr = _row_tile(rs)

    def body(a0_r, b0_r, a1_r, b1_r, w_r, m_r, v_r, g_o, d_o, m_o, v_o):
        g = _group_sum(L // 2, a0_r, b0_r, a1_r, b1_r)
        d, mn, vn = _adam_update(g, w_r[...], m_r[...], v_r[...])
        g_o[...] = g
        d_o[...] = d
        m_o[...] = mn
        v_o[...] = vn

    s1 = pl.BlockSpec((None, tr, cs), lambda l, i: (l, i, 0))
    (a0, b0), (a1, b1) = groups
    return pl.pallas_call(
        body, name=name, grid=(L, rs // tr), in_specs=_group_specs(L // 2, tr, cs) + [s1, s1, s1], out_specs=[s1] * 4,
        out_shape=[SDS((L, rs, cs), F32)] * 4,
        compiler_params=pltpu.CompilerParams(dimension_semantics=("parallel", "parallel"),
                                             vmem_limit_bytes=VMEM_LIMIT),
    )(a0, b0, a1, b1, w, m, v)


def groups_sum(groups, *, name):
    (a0, b0), (a1, b1) = groups
    _, half, rs, cs = a0.shape
    tr = _row_tile(rs)

    def body(a0_r, b0_r, a1_r, b1_r, g_o):
        g_o[...] = _group_sum(half, a0_r, b0_r, a1_r, b1_r)

    return pl.pallas_call(
        body, name=name, grid=(2 * half, rs // tr), in_specs=_group_specs(half, tr, cs),
        out_specs=pl.BlockSpec((None, tr, cs), lambda l, i: (l, i, 0)), out_shape=SDS((2 * half, rs, cs), F32),
        compiler_params=pltpu.CompilerParams(dimension_semantics=("parallel", "parallel"),
                                             vmem_limit_bytes=VMEM_LIMIT),
    )(a0, b0, a1, b1)


def adamw_plain(g, w, m, v, *, name):
    L, rs, cs = w.shape
    tr = _row_tile(rs)

    def body(g_r, w_r, m_r, v_r, d_o, m_o, v_o):
        d, mn, vn = _adam_update(g_r[...], w_r[...], m_r[...], v_r[...])
        d_o[...] = d
        m_o[...] = mn
        v_o[...] = vn

    s1 = pl.BlockSpec((None, tr, cs), lambda l, i: (l, i, 0))
    return pl.pallas_call(
        body, name=name, grid=(L, rs // tr), in_specs=[s1] * 4, out_specs=[s1] * 3,
        out_shape=[SDS((L, rs, cs), F32)] * 3,
        compiler_params=pltpu.CompilerParams(dimension_semantics=("parallel", "parallel"),
                                             vmem_limit_bytes=VMEM_LIMIT),
    )(g, w, m, v)


def plane_sum(r4):
    R = r4.shape[1]

    def body(r_ref, o_ref):
        acc = r_ref[0].astype(F32)
        for k in range(1, N_CHIPS):
            acc = acc + r_ref[k].astype(F32)
        o_ref[...] = acc

    return pl.pallas_call(
        body, name="plane_sum", grid=(R // PACK_ROWS,),
        in_specs=[pl.BlockSpec((N_CHIPS, PACK_ROWS, LANES), lambda i: (0, i, 0))],
        out_specs=pl.BlockSpec((PACK_ROWS, LANES), lambda i: (i, 0)), out_shape=SDS((R, LANES), F32),
        compiler_params=pltpu.CompilerParams(dimension_semantics=("parallel",)),
    )(r4)


def adamw(pa, pb, w, m, v):
    R = w.shape[0]

    def body(pa_r, pb_r, w_r, m_r, v_r, g_o, d_o, m_o, v_o):
        g = pa_r[...] + pb_r[...]
        d, mn, vn = _adam_update(g, w_r[...], m_r[...], v_r[...])
        g_o[...] = g
        d_o[...] = d
        m_o[...] = mn
        v_o[...] = vn

    spec = pl.BlockSpec((PACK_ROWS, LANES), lambda i: (i, 0))
    return pl.pallas_call(
        body, name="adamw", grid=(R // PACK_ROWS,), in_specs=[spec] * 5, out_specs=[spec] * 4,
        out_shape=[SDS((R, LANES), F32)] * 4,
        compiler_params=pltpu.CompilerParams(dimension_semantics=("parallel",)),
    )(pa, pb, w, m, v)


GATHER_AXES = [WSPEC[n][2] for n in REG] + ["stack"]
EXCHANGE_AXES = [WSPEC[n][2] for n in REG] + [(2, WIN_WIDTH, WIN_STRIDE)]


def moved_shards(wl, group):
    out = []
    for n in MOVED:
        half = WSPEC[n][1][0] // 2
        out.append(wl[n][group * half:(group + 1) * half].astype(BF16))
    return out


def moved_grads(grads, group):
    out = []
    for n in MOVED:
        if WSPEC[n][1][0] == DEPTH:
            layers, key = (2 * group, 2 * group + 1), n
        elif n == "mla_w_out":
            layers, key = (2 * group + 1,), "mla_out"
        else:
            layers, key = (2 * group,), {"hyb_w_out": "hyb_out", WIN: "hyb_in"}[n]
        out.append(jnp.stack([grads[l][key] for l in layers]))
    return out


def gather_misc(wl):
    full = {}
    sharded = [n for n in MISC if WSPEC[n][2] is not None]
    pieces = [wl[n].astype(BF16) if WSPEC[n][3] else lax.bitcast_convert_type(wl[n], BF16) for n in sharded]
    got = plane_allgather(_pack(pieces, BF16, 16))
    shapes = [v.shape for v in pieces]
    per_chip = [_unpack(got[k], shapes) for k in range(N_CHIPS)]
    for idx, n in enumerate(sharded):
        parts = [per_chip[k][idx] for k in range(N_CHIPS)]
        if not WSPEC[n][3]:
            parts = [lax.bitcast_convert_type(v, F32) for v in parts]
        full[n] = jnp.concatenate(parts, axis=WSPEC[n][2])
    for n in MISC:
        if WSPEC[n][2] is None:
            full[n] = wl[n]
    return full


def kernel(x, p, positions, ffn1_pre_g, ffn1_w_in, ffn1_w_down, ffn1_post_g, mix_pre_g, mix_post_g, ffn2_pre_g, ffn2_w_in, ffn2_w_down, ffn2_post_g, ple_pre_g, ple_w_gate, ple_w_proj, ple_post_g, hyb_w_in, gm_ln_g, gm_ln_b, gm_w_s, gm_b_s, ssd_conv_w, ssd_conv_b, ssd_dt_bias, ssd_a_log, ssd_d, ssd_norm_g, hyb_w_out, mla_w_in, mla_q_norm_g, mla_kv_norm_g, mla_w_uq, mla_w_ukv, mla_w_out, loss_target, m_ffn1_pre_g, m_ffn1_w_in, m_ffn1_w_down, m_ffn1_post_g, m_mix_pre_g, m_mix_post_g, m_ffn2_pre_g, m_ffn2_w_in, m_ffn2_w_down, m_ffn2_post_g, m_ple_pre_g, m_ple_w_gate, m_ple_w_proj, m_ple_post_g, m_hyb_w_in, m_gm_ln_g, m_gm_ln_b, m_gm_w_s, m_gm_b_s, m_ssd_conv_w, m_ssd_conv_b, m_ssd_dt_bias, m_ssd_a_log, m_ssd_d, m_ssd_norm_g, m_hyb_w_out, m_mla_w_in, m_mla_q_norm_g, m_mla_kv_norm_g, m_mla_w_uq, m_mla_w_ukv, m_mla_w_out, v_ffn1_pre_g, v_ffn1_w_in, v_ffn1_w_down, v_ffn1_post_g, v_mix_pre_g, v_mix_post_g, v_ffn2_pre_g, v_ffn2_w_in, v_ffn2_w_down, v_ffn2_post_g, v_ple_pre_g, v_ple_w_gate, v_ple_w_proj, v_ple_post_g, v_hyb_w_in, v_gm_ln_g, v_gm_ln_b, v_gm_w_s, v_gm_b_s, v_ssd_conv_w, v_ssd_conv_b, v_ssd_dt_bias, v_ssd_a_log, v_ssd_d, v_ssd_norm_g, v_hyb_w_out, v_mla_w_in, v_mla_q_norm_g, v_mla_kv_norm_g, v_mla_w_uq, v_mla_w_ukv, v_mla_w_out):
    args = locals()
    wl = {n: args[n] for n in WNAMES}
    ml = {n: args["m_" + n] for n in WNAMES}
    vl = {n: args["v_" + n] for n in WNAMES}

    full = gather_misc(wl)
    nmoved = len(MOVED)
    halves = [WSPEC[n][1][0] // 2 for n in MOVED]
    for n, half in zip(MOVED, halves):
        full[n] = [None] * (2 * half)
    exchanged = [None, None]

    def gathered(group, outs):
        for t, n in enumerate(MOVED):
            for l in range(halves[t]):
                if n == WIN:
                    full[n][group * halves[t] + l] = jnp.concatenate([outs[t][k, l] for k in range(N_CHIPS)], axis=-1)
                else:
                    full[n][group * halves[t] + l] = (outs[t], l)

    gathered(0, job_call(gather_job(moved_shards(wl, 0), GATHER_AXES), name="gather_first"))

    class Jobs(NoJobs):
        def fwd_job(self, i):
            return gather_job(moved_shards(wl, 1), GATHER_AXES) if i == 1 else None

        def fwd_done(self, i, outs):
            gathered(1, outs)

        def bwd_job(self, i, grads):
            return exchange_job(moved_grads(grads, 1), EXCHANGE_AXES) if i == 1 else None

        def bwd_done(self, i, outs):
            exchanged[1] = outs

    rope = rope_tables(positions[0])
    T = x.shape[1]
    sq, dx, grads = local_step(x[0], p.reshape(DEPTH, T, p.shape[-1]), rope, loss_target[0],
                               lambda i: layer_weights(full, i), Jobs())
    loss = lax.psum(0.5 * jnp.sum(sq) / D_MODEL, ("x", "y", "c"))

    res = {}
    exchanged[0] = job_call(exchange_job(moved_grads(grads, 0), EXCHANGE_AXES), name="exchange_first")
    for t, n in enumerate(MOVED):
        groups = [(e[t], e[nmoved + t]) for e in exchanged]
        if n == WIN:
            window = groups_sum(groups, name=f"sum_{n}")
            chip = 2 * lax.axis_index("x") + lax.axis_index("y")
            g = lax.dynamic_slice_in_dim(window, chip * (WIN_SHARD - WIN_STRIDE), WIN_SHARD, axis=2)
            res[n] = [g] + list(adamw_plain(g, wl[n], ml[n], vl[n], name=f"adamw_{n}"))
        else:
            res[n] = adamw_reg(groups, wl[n], ml[n], vl[n], name=f"adamw_{n}")
    fg = full_grads(grads)
    dest = [_pack([_chip_slice(fg[n], WSPEC[n][2], k) for n in MISC], BF16, PACK_ROWS) for k in range(N_CHIPS)]
    mine = plane_sum(plane_alltoall(jnp.stack(dest)))
    other = sibling_swap(mine)
    slabs = adamw(mine, other, *[_pack([d[n] for n in MISC], F32, PACK_ROWS) for d in (wl, ml, vl)])
    shapes = [wl[n].shape for n in MISC]
    unpacked = [_unpack(s, shapes) for s in slabs]
    for idx, n in enumerate(MISC):
        res[n] = [u[idx] for u in unpacked]
    return (loss, dx[None], *[res[n][k] for k in range(4) for n in WNAMES])
```

```python
import functools
import math

import jax
import jax.numpy as jnp
import numpy as np
from jax import lax
from jax.experimental import pallas as pl
from jax.experimental.pallas import tpu as pltpu

F32 = jnp.float32
BF16 = jnp.bfloat16
SDS = jax.ShapeDtypeStruct
MESH = pl.DeviceIdType.MESH
HIGHEST = lax.Precision.HIGHEST

D_MODEL = 1024
DEPTH = 4
D_FF = 2816
NORM_EPS = 1e-6
LN_EPS = 1e-5
GM_HEADS = 8
CHUNK = 128
SSD_HEADS = 16
SSD_HEAD_DIM = 64
SSD_INNER = 1024
SSD_STATE = 128
SSD_CONV = 4
SSD_CONV_CH = 1536
HYB_MAIN = 4608
HYB_IN = 4624
HYB_PAD = 5120
MLA_HEADS = 16
MLA_NOPE = 128
MLA_ROPE = 64
MLA_QK = 192
MLA_QPAD = 256
MLA_Q_LORA = 256
MLA_KV_LORA = 128
ROPE_BASE = 10000.0
ADAM_LR = 0.001
ADAM_B1 = 0.9
ADAM_B2 = 0.999
ADAM_EPS = 1e-08
ADAM_WD = 0.01
ADAM_STEP = 10

N_CHIPS = 4
LANES = 128
VMEM_LIMIT = 56 * 1024 * 1024
PACK_ROWS = 2048

WEIGHTS = [
    ("ffn1_pre_g", (4, 1024), None, False),
    ("ffn1_w_in", (4, 1024, 5632), 2, True),
    ("ffn1_w_down", (4, 2816, 1024), 1, True),
    ("ffn1_post_g", (4, 1024), None, False),
    ("mix_pre_g", (4, 1024), None, False),
    ("mix_post_g", (4, 1024), None, False),
    ("ffn2_pre_g", (4, 1024), None, False),
    ("ffn2_w_in", (4, 1024, 5632), 2, True),
    ("ffn2_w_down", (4, 2816, 1024), 1, True),
    ("ffn2_post_g", (4, 1024), None, False),
    ("ple_pre_g", (4, 1024), None, False),
    ("ple_w_gate", (4, 1024, 1024), 1, True),
    ("ple_w_proj", (4, 256, 1024), 2, True),
    ("ple_post_g", (4, 1024), None, False),
    ("hyb_w_in", (2, 1024, 4624), 2, True),
    ("gm_ln_g", (2, 1024), None, False),
    ("gm_ln_b", (2, 1024), None, False),
    ("gm_w_s", (2, 8, 128, 128), None, False),
    ("gm_b_s", (2, 8, 128), None, False),
    ("ssd_conv_w", (2, 4, 1536), 2, False),
    ("ssd_conv_b", (2, 1536), None, False),
    ("ssd_dt_bias", (2, 16), None, False),
    ("ssd_a_log", (2, 16), None, False),
    ("ssd_d", (2, 16), None, False),
    ("ssd_norm_g", (2, 1024), None, False),
    ("hyb_w_out", (2, 2048, 1024), 1, True),
    ("mla_w_in", (2, 1024, 448), 1, True),
    ("mla_q_norm_g", (2, 256), 1, False),
    ("mla_kv_norm_g", (2, 128), None, False),
    ("mla_w_uq", (2, 256, 3072), 2, True),
    ("mla_w_ukv", (2, 128, 4096), 2, True),
    ("mla_w_out", (2, 2048, 1024), 1, True),
]
WNAMES = [w[0] for w in WEIGHTS]
WSPEC = {w[0]: w for w in WEIGHTS}
REG = ["ffn1_w_in", "ffn1_w_down", "ffn2_w_in", "ffn2_w_down", "ple_w_gate", "ple_w_proj", "hyb_w_out", "mla_w_out"]
WIN = "hyb_w_in"
WIN_SHARD = 4624 // 4
WIN_STRIDE = (WIN_SHARD // 128) * 128
WIN_WIDTH = -(-(WIN_SHARD + 3 * (WIN_SHARD - WIN_STRIDE)) // 128) * 128
MOVED = REG + [WIN]
MISC = [n for n in WNAMES if n not in MOVED]


def _pick(dim, target):
    if dim <= target:
        return dim
    t = (target // LANES) * LANES
    while t >= LANES:
        if dim % t == 0:
            return t
        t -= LANES
    return dim


def mm(a, b, *, ta=False, tb=False, out_dtype=F32, name, tm=1024, tn=1024, tk=1024):
    a, la = a if isinstance(a, tuple) else (a, None)
    b, lb = b if isinstance(b, tuple) else (b, None)

    def dims(x, mode):
        r, c = x.shape[-2:]
        return (r, c * x.shape[0]) if mode == "planes" else (r, c)

    K, M = dims(a, la) if ta else dims(a, la)[::-1]
    N, K2 = dims(b, lb) if tb else dims(b, lb)[::-1]
    assert K == K2, (a.shape, b.shape, ta, tb)
    assert not (la == "planes" and ta) and not (lb == "planes" and tb)
    bm, bn = _pick(M, tm), _pick(b.shape[-1] if lb == "planes" else N, tn)
    bk = _pick(a.shape[-1] if la == "planes" else K, tk)
    nk = K // bk

    def spec(shape, idx, layer, x):
        if layer is None:
            return pl.BlockSpec(shape, idx)
        if layer == "planes":
            per = x.shape[-1] // shape[1]
            return pl.BlockSpec((None,) + shape, lambda i, j, k: (idx(i, j, k)[1] // per, idx(i, j, k)[0],
                                                                  idx(i, j, k)[1] % per))
        return pl.BlockSpec((None,) + shape, lambda i, j, k: (layer,) + idx(i, j, k))

    a_spec = spec((bk, bm), lambda i, j, k: (k, i), la, a) if ta else spec((bm, bk), lambda i, j, k: (i, k), la, a)
    b_spec = spec((bn, bk), lambda i, j, k: (j, k), lb, b) if tb else spec((bk, bn), lambda i, j, k: (k, j), lb, b)
    dn = (((0 if ta else 1,), (1 if tb else 0,)), ((), ()))

    def body(a_ref, b_ref, o_ref, acc_ref):
        k = pl.program_id(2)

        @pl.when(k == 0)
        def _():
            acc_ref[...] = jnp.zeros_like(acc_ref)

        acc_ref[...] += lax.dot_general(a_ref[...].astype(BF16), b_ref[...].astype(BF16), dn,
                                        preferred_element_type=F32)

        @pl.when(k == nk - 1)
        def _():
            o_ref[...] = acc_ref[...].astype(o_ref.dtype)

    return pl.pallas_call(
        body, name=name, grid=(M // bm, N // bn, nk),
        in_specs=[a_spec, b_spec], out_specs=pl.BlockSpec((bm, bn), lambda i, j, k: (i, j)),
        out_shape=SDS((M, N), out_dtype), scratch_shapes=[pltpu.VMEM((bm, bn), F32)],
        compiler_params=pltpu.CompilerParams(dimension_semantics=("parallel", "parallel", "arbitrary"),
                                             vmem_limit_bytes=VMEM_LIMIT),
    )(a, b)


def row_call(fn, xs, ps, outs, accs=(), *, tb, name, reverse=False):
    xs = [x if isinstance(x, tuple) else (x, x.shape[1], 0) for x in xs]
    T = xs[0][0].shape[0]
    tb = min(tb, T)
    n = T // tb
    assert n * tb == T
    nx, npar, no, na = len(xs), len(ps), len(outs), len(accs)

    def ridx(i):
        return n - 1 - i if reverse else i

    in_specs = [pl.BlockSpec((tb, w), functools.partial(lambda i, cb: (ridx(i), cb), cb=cb)) for (_, w, cb) in xs]
    in_specs += [pl.BlockSpec(p.shape, functools.partial(lambda i, nd: (0,) * nd, nd=p.ndim)) for p in ps]
    out_specs = [pl.BlockSpec((tb, c), lambda i: (ridx(i), 0)) for (c, _) in outs]
    out_specs += [pl.BlockSpec(s, functools.partial(lambda i, nd: (0,) * nd, nd=len(s))) for s in accs]
    out_shape = [SDS((T, c), dt) for (c, dt) in outs] + [SDS(s, F32) for s in accs]

    def body(*refs):
        xr, pr = refs[:nx], refs[nx:nx + npar]
        orf, ar = refs[nx + npar:nx + npar + no], refs[nx + npar + no:]
        res = fn(*[r[...] for r in xr], *[r[...] for r in pr])
        for r, v in zip(orf, res[:no]):
            r[...] = v.astype(r.dtype)
        if na:
            @pl.when(pl.program_id(0) == 0)
            def _():
                for r in ar:
                    r[...] = jnp.zeros_like(r)

            for r, v in zip(ar, res[no:]):
                r[...] += v.astype(F32)

    res = pl.pallas_call(
        body, name=name, grid=(n,), in_specs=in_specs, out_specs=out_specs, out_shape=out_shape,
        compiler_params=pltpu.CompilerParams(dimension_semantics=("arbitrary",), vmem_limit_bytes=VMEM_LIMIT),
    )(*[x[0] for x in xs], *ps)
    return res


def _f32(*a):
    return [v.astype(F32) for v in a]


def t_rms(x, g):
    return x * lax.rsqrt(jnp.mean(x * x, axis=-1, keepdims=True) + NORM_EPS) * g


def t_swiglu(gate, up):
    return jax.nn.silu(gate) * up


def t_ple(gl, pp, g):
    return t_rms(jax.nn.sigmoid(gl) * pp, g)


def _iota(shape, d):
    return lax.broadcasted_iota(jnp.int32, shape, d)


def _bdot(a, b, dn=(((1,), (0,)), ((), ()))):
    return lax.dot_general(a.astype(BF16), b.astype(BF16), dn, preferred_element_type=F32)


def _hdot(a, b):
    return jnp.dot(a, b, precision=HIGHEST, preferred_element_type=F32)


NT = (((1,), (1,)), ((), ()))
TN = (((0,), (0,)), ((), ()))


def t_gmlp(uv, ln_g, ln_b, w_s, b_st):
    tb = uv.shape[0]
    guv = jax.nn.gelu(uv)
    u, v = guv[:, :1024], guv[:, 1024:]
    tri = _iota((CHUNK, CHUNK), 1) <= _iota((CHUNK, CHUNK), 0)
    rows = []
    for c in range(tb // CHUNK):
        vc = v[c * CHUNK:(c + 1) * CHUNK]
        heads = []
        for h in range(GM_HEADS):
            sl = slice(h * 128, (h + 1) * 128)
            vh = vc[:, sl]
            xc = vh - jnp.mean(vh, axis=-1, keepdims=True)
            var = jnp.mean(xc * xc, axis=-1, keepdims=True)
            y = xc * lax.rsqrt(var + LN_EPS) * ln_g[:, sl] + ln_b[:, sl]
            wm = jnp.where(tri, w_s[sl, :], 0.0)
            heads.append(_bdot(wm, y) + b_st[:, h:h + 1])
        rows.append(jnp.concatenate(heads, axis=1))
    mixed = rows[0] if len(rows) == 1 else jnp.concatenate(rows, axis=0)
    return u * mixed


def t_ssd(pre, dtr, z, st, dt_bias, a_log, d_exp, norm_g):
    L = CHUNK
    xbc = jax.nn.silu(pre)
    xs, bm, cm = xbc[:, :1024], xbc[:, 1024:1280], xbc[:, 1280:1536]
    valid = _iota((1, LANES), 1) < SSD_HEADS
    dt16 = jnp.where(valid, jax.nn.softplus(dtr + dt_bias), 0.0)
    a16 = jnp.where(valid, -jnp.exp(a_log), 0.0)
    da16 = dt16 * a16
    tri = _iota((L, L), 1) <= _iota((L, L), 0)
    acs16 = _hdot(tri.astype(F32), da16)
    hh, cc = _iota((LANES, 1024), 0), _iota((LANES, 1024), 1)
    expand = ((cc >= hh * SSD_HEAD_DIM) & (cc < (hh + 1) * SSD_HEAD_DIM)).astype(F32)
    acs = _hdot(acs16, expand)
    dte = _hdot(dt16, expand)
    alast = jnp.sum(jnp.where(_iota((L, 1024), 0) == L - 1, acs, 0.0), axis=0, keepdims=True)
    xd = xs * dte
    groups = [slice(0, 512), slice(512, 1024)]
    bg = [bm[:, :128], bm[:, 128:]]
    cg = [cm[:, :128], cm[:, 128:]]
    yoff = jnp.concatenate([_bdot(cg[g], st[:, groups[g]]) for g in range(2)], axis=1) * jnp.exp(acs)
    xdw = xd * jnp.exp(alast - acs)
    s_t = jnp.concatenate([_bdot(bg[g], xdw[:, groups[g]], TN) for g in range(2)], axis=1)
    st_new = st * jnp.exp(alast) + s_t
    cb = [_bdot(cg[g], bg[g], NT) for g in range(2)]
    acs16_t = acs16.T
    lo = _iota((1, LANES), 1) < SSD_HEAD_DIM
    slabs = []
    for j in range(SSD_HEADS // 2):
        g = j // 4
        xslab = xd[:, j * 128:(j + 1) * 128]
        acc = None
        for half in range(2):
            h = 2 * j + half
            seg = acs16[:, h:h + 1] - acs16_t[h:h + 1, :]
            mmat = cb[g] * jnp.exp(jnp.where(tri, seg, -1e30))
            xm = jnp.where(lo if half == 0 else jnp.logical_not(lo), xslab, 0.0)
            term = _bdot(mmat, xm)
            acc = term if acc is None else acc + term
        slabs.append(acc)
    y = jnp.concatenate(slabs, axis=1) + yoff + d_exp * xs
    yg = y * jax.nn.silu(z)
    outs = []
    for g in range(2):
        t = yg[:, groups[g]]
        outs.append(t * lax.rsqrt(jnp.mean(t * t, axis=-1, keepdims=True) + NORM_EPS) * norm_g[:, groups[g]])
    return jnp.concatenate(outs, axis=1), st_new


def t_kprep(c_all, cs, sn, qg, kvg):
    cqn = t_rms(c_all[:, :256], qg)
    ckvn = t_rms(c_all[:, 256:384], kvg)
    kr = c_all[:, 384:512] * cs + c_all[:, 512:640] * sn
    return cqn, ckvn, kr


def t_qrope(qb, c256, s256):
    scale = MLA_QK ** -0.5
    half = MLA_HEADS * MLA_QPAD
    outs = []
    for h in range(MLA_HEADS):
        a = qb[:, h * MLA_QPAD:(h + 1) * MLA_QPAD]
        b = qb[:, half + h * MLA_QPAD:half + (h + 1) * MLA_QPAD]
        outs.append((a * c256 + b * s256) * scale)
    return jnp.concatenate(outs, axis=1)


def rms_fwd(h, g, *, name, tb=512):
    def fn(h, g):
        return (t_rms(h.astype(F32), g),)
    return row_call(fn, [h], [g], [(h.shape[1], BF16)], tb=tb, name=name)[0]


def rms_bwd(h, dhn, dres, g, *, name, tb=256):
    def fn(h, dhn, dres, g):
        h, dhn, dres = _f32(h, dhn, dres)
        _, vjp = jax.vjp(t_rms, h, g)
        dh, dg = vjp(dhn)
        return dres + dh, dg
    return row_call(fn, [h, dhn, dres], [g], [(h.shape[1], F32)], [g.shape], tb=tb, name=name)


def post_fwd(h, f, g, scale, *, name, tb=512):
    def fn(h, f, g):
        return (h + scale * t_rms(f.astype(F32), g),)
    return row_call(fn, [h, f], [g], [(h.shape[1], F32)], tb=tb, name=name)[0]


def post_bwd(f, dout, g, scale, *, name, tb=256):
    def fn(f, dout, g):
        f, dout = _f32(f, dout)
        _, vjp = jax.vjp(lambda f, g: scale * t_rms(f, g), f, g)
        return vjp(dout)
    return row_call(fn, [f, dout], [g], [(f.shape[1], BF16)], [g.shape], tb=tb, name=name)


FFN_TILE = D_FF // 2


def _stacked(w):
    return w if isinstance(w, tuple) else (w[None], 0)


def ffn_in_act(hn, w_in, *, name, tm=512):
    w, layer = _stacked(w_in)
    T, K = hn.shape
    bm, bn = _pick(T, tm), FFN_TILE
    nj = D_FF // bn

    def body(x_ref, wg_ref, wu_ref, gu_ref, act_ref):
        x = x_ref[...].astype(BF16)
        g = jnp.dot(x, wg_ref[...].astype(BF16), preferred_element_type=F32)
        u = jnp.dot(x, wu_ref[...].astype(BF16), preferred_element_type=F32)
        gu_ref[0] = g.astype(gu_ref.dtype)
        gu_ref[1] = u.astype(gu_ref.dtype)
        act_ref[...] = t_swiglu(g, u).astype(act_ref.dtype)

    return pl.pallas_call(
        body, name=name, grid=(T // bm, nj),
        in_specs=[pl.BlockSpec((bm, K), lambda i, j: (i, 0)),
                  pl.BlockSpec((None, K, bn), lambda i, j: (layer, 0, j)),
                  pl.BlockSpec((None, K, bn), lambda i, j: (layer, 0, nj + j))],
        out_specs=[pl.BlockSpec((2, bm, bn), lambda i, j: (0, i, j)), pl.BlockSpec((bm, bn), lambda i, j: (i, j))],
        out_shape=[SDS((2, T, D_FF), BF16), SDS((T, D_FF), BF16)],
        compiler_params=pltpu.CompilerParams(dimension_semantics=("parallel", "parallel"),
                                             vmem_limit_bytes=VMEM_LIMIT),
    )(hn, w, w)


def ffn_down_bx_act(df, w_down, gu, *, name, tm=512):
    w, layer = _stacked(w_down)
    T, K = df.shape
    bm, bn = _pick(T, tm), FFN_TILE

    def body(df_ref, wd_ref, gu_ref, dgu_ref):
        da = lax.dot_general(df_ref[...].astype(BF16), wd_ref[...].astype(BF16), NT, preferred_element_type=F32)
        _, vjp = jax.vjp(t_swiglu, gu_ref[0].astype(F32), gu_ref[1].astype(F32))
        dg, du = vjp(da)
        dgu_ref[0] = dg.astype(dgu_ref.dtype)
        dgu_ref[1] = du.astype(dgu_ref.dtype)

    return pl.pallas_call(
        body, name=name, grid=(T // bm, D_FF // bn),
        in_specs=[pl.BlockSpec((bm, K), lambda i, j: (i, 0)),
                  pl.BlockSpec((None, bn, K), lambda i, j: (layer, j, 0)),
                  pl.BlockSpec((2, bm, bn), lambda i, j: (0, i, j))],
        out_specs=pl.BlockSpec((2, bm, bn), lambda i, j: (0, i, j)), out_shape=SDS((2, T, D_FF), BF16),
        compiler_params=pltpu.CompilerParams(dimension_semantics=("parallel", "parallel"),
                                             vmem_limit_bytes=VMEM_LIMIT),
    )(df, w, gu)


def ple_fwd(h, gl, pp, g, *, name, tb=512):
    def fn(h, gl, pp, g):
        return (h + t_ple(gl, pp, g),)
    return row_call(fn, [h, gl, pp], [g], [(D_MODEL, F32)], tb=tb, name=name)[0]


def ple_bwd(gl, pp, dout, g, *, name, tb=256):
    def fn(gl, pp, dout, g):
        _, vjp = jax.vjp(t_ple, gl, pp, g)
        return vjp(dout)
    return row_call(fn, [gl, pp, dout], [g], [(D_MODEL, BF16), (D_MODEL, BF16)], [g.shape], tb=tb, name=name)


def gmlp_fwd(proj, ln_g, ln_b, w_s, b_st, *, name, tb=256):
    def fn(uv, ln_g, ln_b, w_s, b_st):
        return (t_gmlp(uv, ln_g, ln_b, w_s, b_st),)
    return row_call(fn, [(proj, 2048, 0)], [ln_g, ln_b, w_s, b_st], [(1024, BF16)], tb=tb, name=name)[0]


def gmlp_bwd(proj, dya, ln_g, ln_b, w_s, b_st, *, name, tb=128):
    def fn(uv, dya, ln_g, ln_b, w_s, b_st):
        _, vjp = jax.vjp(t_gmlp, uv, ln_g, ln_b, w_s, b_st)
        return vjp(dya.astype(F32))
    return row_call(fn, [(proj, 2048, 0), (dya, 1024, 0)], [ln_g, ln_b, w_s, b_st], [(2048, BF16)],
                    [ln_g.shape, ln_b.shape, w_s.shape, b_st.shape], tb=tb, name=name)


def kprep_fwd(c_all, cs, sn, qg, kvg, *, name, tb=512):
    return row_call(t_kprep, [c_all, cs, sn], [qg, kvg], [(256, BF16), (128, BF16), (128, BF16)], tb=tb, name=name)


def kprep_bwd(c_all, cs, sn, dcqn, dckvn, dkr, qg, kvg, *, name, tb=256):
    def fn(c_all, cs, sn, dcqn, dckvn, dkr, qg, kvg):
        dcqn, dckvn, dkr = _f32(dcqn, dckvn, dkr)
        _, vjp = jax.vjp(lambda c, qg, kvg: t_kprep(c, cs, sn, qg, kvg), c_all, qg, kvg)
        return vjp((dcqn, dckvn, dkr))
    return row_call(fn, [c_all, cs, sn, dcqn, dckvn, dkr], [qg, kvg], [(640, BF16)], [qg.shape, kvg.shape],
                    tb=tb, name=name)


def qrope_fwd(qb, c256, s256, *, name, tb=256):
    def fn(qb, c256, s256):
        return (t_qrope(qb, c256, s256),)
    return row_call(fn, [qb, c256, s256], [], [(MLA_HEADS * MLA_QPAD, BF16)], tb=tb, name=name)[0]


def qrope_bwd(dq, c256, s256, *, name, tb=256):
    def fn(dq, c256, s256):
        scale = MLA_QK ** -0.5
        a, b = [], []
        for h in range(MLA_HEADS):
            d = dq[:, h * MLA_QPAD:(h + 1) * MLA_QPAD] * scale
            a.append(d * c256)
            b.append(d * s256)
        return (jnp.concatenate(a + b, axis=1),)
    return row_call(fn, [dq, c256, s256], [], [(2 * MLA_HEADS * MLA_QPAD, BF16)], tb=tb, name=name)[0]


STAT_SPLIT = 64
ATTN_UNROLL = 2
ATTN_HEADS_PER_STEP = 2


def stats_fwd(do, o, lse, *, name, tb=512):
    def fn(do, o, lse):
        do, o = _f32(do, o)
        low = _iota((1, 128), 1) < STAT_SPLIT
        outs = []
        for h in range(MLA_HEADS):
            sl = slice(h * 128, (h + 1) * 128)
            dl = jnp.sum(do[:, sl] * o[:, sl], axis=-1, keepdims=True)
            outs.append(jnp.where(low, lse[:, sl], dl))
        return (jnp.concatenate(outs, axis=1),)
    return row_call(fn, [do, o, lse], [], [(2048, F32)], tb=tb, name=name)[0]


def headsum(dkr_h, *, name, tb=512):
    def fn(d):
        acc = d[:, :128]
        for h in range(1, MLA_HEADS):
            acc = acc + d[:, h * 128:(h + 1) * 128]
        return (acc,)
    return row_call(fn, [dkr_h], [], [(128, F32)], tb=tb, name=name)[0]


def loss_fwd(y, t, *, name, tb=512):
    def fn(y, t):
        e = y - t
        return e * (1.0 / D_MODEL), jnp.sum(e * e, axis=0, keepdims=True)
    return row_call(fn, [y, t], [], [(D_MODEL, F32)], [(1, D_MODEL)], tb=tb, name=name)


def conv_fwd(proj, w, b, *, name, tb=256):
    T = proj.shape[0]
    n = T // tb
    hb = tb // CHUNK
    C = SSD_CONV_CH

    def body(cur, prev, w_ref, b_ref, o_ref, scr):
        i = pl.program_id(0)
        scr[pl.ds(0, CHUNK), :] = jnp.where(i > 0, prev[...], 0.0)
        scr[pl.ds(CHUNK, tb), :] = cur[...]
        y = b_ref[...] + w_ref[3:4, :] * cur[...]
        for k in range(SSD_CONV - 1):
            y = y + w_ref[k:k + 1, :] * scr[pl.ds(CHUNK - (SSD_CONV - 1) + k, tb), :]
        o_ref[...] = y

    return pl.pallas_call(
        body, name=name, grid=(n,),
        in_specs=[pl.BlockSpec((tb, C), lambda i: (i, 2)),
                  pl.BlockSpec((CHUNK, C), lambda i: (jnp.maximum(i * hb - 1, 0), 2)),
                  pl.BlockSpec((SSD_CONV, C), lambda i: (0, 0)), pl.BlockSpec((1, C), lambda i: (0, 0))],
        out_specs=pl.BlockSpec((tb, C), lambda i: (i, 0)), out_shape=SDS((T, C), F32),
        scratch_shapes=[pltpu.VMEM((CHUNK + tb, C), F32)],
        compiler_params=pltpu.CompilerParams(dimension_semantics=("arbitrary",), vmem_limit_bytes=VMEM_LIMIT),
    )(proj, proj, w, b)


def conv_bwd(dpre, proj, w, *, name, tb=256):
    T = proj.shape[0]
    n = T // tb
    hb = tb // CHUNK
    nh = T // CHUNK
    C = SSD_CONV_CH

    def body(dcur, dnext, xcur, xprev, w_ref, dx_ref, dw_ref, db_ref, dscr, xscr):
        i = pl.program_id(0)

        @pl.when(i == 0)
        def _():
            dw_ref[...] = jnp.zeros_like(dw_ref)
            db_ref[...] = jnp.zeros_like(db_ref)

        d = dcur[...]
        dscr[pl.ds(0, tb), :] = d
        dscr[pl.ds(tb, CHUNK), :] = jnp.where(i < n - 1, dnext[...], 0.0)
        xscr[pl.ds(0, CHUNK), :] = jnp.where(i > 0, xprev[...], 0.0)
        xscr[pl.ds(CHUNK, tb), :] = xcur[...]
        dx = w_ref[3:4, :] * d
        for k in range(SSD_CONV - 1):
            dx = dx + w_ref[k:k + 1, :] * dscr[pl.ds(SSD_CONV - 1 - k, tb), :]
        dx_ref[...] = dx.astype(dx_ref.dtype)
        for k in range(SSD_CONV):
            xk = xscr[pl.ds(CHUNK - (SSD_CONV - 1) + k, tb), :]
            dw_ref[k:k + 1, :] += jnp.sum(d * xk, axis=0, keepdims=True)
        db_ref[...] += jnp.sum(d, axis=0, keepdims=True)

    return pl.pallas_call(
        body, name=name, grid=(n,),
        in_specs=[pl.BlockSpec((tb, C), lambda i: (i, 0)),
                  pl.BlockSpec((CHUNK, C), lambda i: (jnp.minimum((i + 1) * hb, nh - 1), 0)),
                  pl.BlockSpec((tb, C), lambda i: (i, 2)),
                  pl.BlockSpec((CHUNK, C), lambda i: (jnp.maximum(i * hb - 1, 0), 2)),
                  pl.BlockSpec((SSD_CONV, C), lambda i: (0, 0))],
        out_specs=[pl.BlockSpec((tb, C), lambda i: (i, 0)), pl.BlockSpec((SSD_CONV, C), lambda i: (0, 0)),
                   pl.BlockSpec((1, C), lambda i: (0, 0))],
        out_shape=[SDS((T, C), BF16), SDS((SSD_CONV, C), F32), SDS((1, C), F32)],
        scratch_shapes=[pltpu.VMEM((tb + CHUNK, C), F32), pltpu.VMEM((CHUNK + tb, C), F32)],
        compiler_params=pltpu.CompilerParams(dimension_semantics=("arbitrary",), vmem_limit_bytes=VMEM_LIMIT),
    )(dpre, dpre, proj, proj, w)


def _ssd_specs(nc, rev):
    def r(c):
        return nc - 1 - c if rev else c
    pre = pl.BlockSpec((CHUNK, SSD_CONV_CH), lambda c: (r(c), 0))
    dtr = pl.BlockSpec((CHUNK, LANES), lambda c: (r(c), HYB_MAIN // LANES))
    z = pl.BlockSpec((CHUNK, 1024), lambda c: (r(c), 2))
    row = pl.BlockSpec((CHUNK, 1024), lambda c: (r(c), 0))
    return pre, dtr, z, row


def _pspec(shape):
    return pl.BlockSpec(shape, lambda c: (0,) * len(shape))


def ssd_fwd(pre, proj, dt_bias, a_log, d_exp, norm_g, *, name):
    T = pre.shape[0]
    nc = T // CHUNK
    s_pre, s_dt, s_z, s_row = _ssd_specs(nc, False)

    def body(pre_r, dt_r, z_r, b_r, a_r, d_r, g_r, y_r, sv_r, st):
        @pl.when(pl.program_id(0) == 0)
        def _():
            st[...] = jnp.zeros_like(st)

        s0 = st[...]
        sv_r[...] = s0
        y, s1 = t_ssd(pre_r[...], dt_r[...], z_r[...], s0, b_r[...], a_r[...], d_r[...], g_r[...])
        y_r[...] = y.astype(y_r.dtype)
        st[...] = s1

    return pl.pallas_call(
        body, name=name, grid=(nc,),
        in_specs=[s_pre, s_dt, s_z, _pspec((1, LANES)), _pspec((1, LANES)), _pspec((1, 1024)), _pspec((1, 1024))],
        out_specs=[s_row, s_row], out_shape=[SDS((T, 1024), BF16), SDS((T, 1024), F32)],
        scratch_shapes=[pltpu.VMEM((SSD_STATE, 1024), F32)],
        compiler_params=pltpu.CompilerParams(dimension_semantics=("arbitrary",), vmem_limit_bytes=VMEM_LIMIT),
    )(pre, proj, proj, dt_bias, a_log, d_exp, norm_g)


def ssd_bwd(pre, proj, states, dyab, dt_bias, a_log, d_exp, norm_g, *, name):
    T = pre.shape[0]
    nc = T // CHUNK
    s_pre, s_dt, s_z, s_row = _ssd_specs(nc, True)
    s_dtout = pl.BlockSpec((CHUNK, LANES), lambda c: (nc - 1 - c, 0))
    s_dy = pl.BlockSpec((CHUNK, 1024), lambda c: (nc - 1 - c, 1))

    def body(pre_r, dt_r, z_r, sv_r, dy_r, b_r, a_r, d_r, g_r, dpre_r, ddt_r, dz_r, db_r, da_r, dd_r, dg_r, dst):
        @pl.when(pl.program_id(0) == 0)
        def _():
            dst[...] = jnp.zeros_like(dst)
            for r in (db_r, da_r, dd_r, dg_r):
                r[...] = jnp.zeros_like(r)

        _, vjp = jax.vjp(t_ssd, pre_r[...], dt_r[...], z_r[...], sv_r[...], b_r[...], a_r[...], d_r[...], g_r[...])
        dpre, ddt, dz, ds0, db, da, dd, dg = vjp((dy_r[...].astype(F32), dst[...]))
        dpre_r[...] = dpre
        ddt_r[...] = ddt.astype(ddt_r.dtype)
        dz_r[...] = dz.astype(dz_r.dtype)
        dst[...] = ds0
        db_r[...] += db
        da_r[...] += da
        dd_r[...] += dd
        dg_r[...] += dg

    return pl.pallas_call(
        body, name=name, grid=(nc,),
        in_specs=[s_pre, s_dt, s_z, s_row, s_dy, _pspec((1, LANES)), _pspec((1, LANES)), _pspec((1, 1024)),
                  _pspec((1, 1024))],
        out_specs=[s_pre, s_dtout, s_row, _pspec((1, LANES)), _pspec((1, LANES)), _pspec((1, 1024)), _pspec((1, 1024))],
        out_shape=[SDS((T, SSD_CONV_CH), F32), SDS((T, LANES), BF16), SDS((T, 1024), BF16),
                   SDS((1, LANES), F32), SDS((1, LANES), F32), SDS((1, 1024), F32), SDS((1, 1024), F32)],
        scratch_shapes=[pltpu.VMEM((SSD_STATE, 1024), F32)],
        compiler_params=pltpu.CompilerParams(dimension_semantics=("arbitrary",), vmem_limit_bytes=VMEM_LIMIT),
    )(pre, proj, proj, states, dyab, dt_bias, a_log, d_exp, norm_g)


def _attn_tile(T, target=512):
    return min(target, T // 2)


def _causal(tq):
    return _iota((tq, tq), 1) <= _iota((tq, tq), 0)


def _job_parts(job):
    if job is None:
        return 0, 0, [], [], []
    ni, no = len(job["inputs"]), len(job["out_shape"])
    return ni, no, list(job["inputs"]), list(job["out_shape"]), list(job["scratch"])


def _job_phase(job, which, refs, when):
    if job is not None and job["phases"][which] is not None:
        pl.when(when)(functools.partial(job["phases"][which], *refs))


def attn_fwd(q, kv, kr, *, name, job=None):
    T = q.shape[0]
    tq = _attn_tile(T)
    nq = T // tq
    hp = ATTN_HEADS_PER_STEP
    ng = MLA_HEADS // hp
    ni, no, jins, jouts, jscratch = _job_parts(job)

    def body(*refs):
        q_ref, kn_ref, v_ref, kr_ref = refs[:4]
        o_ref, lse_ref = refs[4 + ni:6 + ni]
        jrefs = (refs[4:4 + ni], refs[6 + ni:6 + ni + no], refs[6 + ni + no:])
        g = pl.program_id(0)
        qi = pl.program_id(1)
        _job_phase(job, 0, jrefs, (g == 0) & (qi == 0))
        qv = [q_ref[:, e * MLA_QPAD:(e + 1) * MLA_QPAD] for e in range(hp)]

        def blk(ki, masked, carry):
            rows = pl.ds(pl.multiple_of(ki * tq, tq), tq)
            kr = kr_ref[rows, :]
            out = []
            for e in range(hp):
                m, l, acc = carry[e]
                cols = slice(e * 128, (e + 1) * 128)
                k = jnp.concatenate([kn_ref[rows, cols], kr], axis=1)
                s = lax.dot_general(qv[e], k, NT, preferred_element_type=F32)
                if masked:
                    s = jnp.where(_causal(tq), s, -1e30)
                m_new = jnp.maximum(m, jnp.max(s, axis=-1, keepdims=True))
                p = jnp.exp(s - m_new)
                alpha = jnp.exp(m - m_new)
                l = alpha * l + jnp.sum(p, axis=-1, keepdims=True)
                acc = alpha * acc + jnp.dot(p.astype(BF16), v_ref[rows, cols], preferred_element_type=F32)
                out.append((m_new, l, acc))
            return tuple(out)

        one = (jnp.full((tq, 1), -1e30, F32), jnp.zeros((tq, 1), F32), jnp.zeros((tq, 128), F32))
        carry = lax.fori_loop(0, qi, lambda ki, c: blk(ki, False, c), (one,) * hp)
        carry = blk(qi, True, carry)
        for e in range(hp):
            m, l, acc = carry[e]
            cols = slice(e * 128, (e + 1) * 128)
            o_ref[:, cols] = (acc / l).astype(o_ref.dtype)
            lse_ref[:, cols] = jnp.broadcast_to(m + jnp.log(l), (tq, 128))
        _job_phase(job, 2, jrefs, (g == ng - 1) & (qi == nq - 1))

    res = pl.pallas_call(
        body, name=name, grid=(ng, nq),
        in_specs=[pl.BlockSpec((tq, hp * MLA_QPAD), lambda g, i: (i, g)),
                  pl.BlockSpec((T, hp * 128), lambda g, i: (0, g)),
                  pl.BlockSpec((T, hp * 128), lambda g, i: (0, ng + g)),
                  pl.BlockSpec((T, 128), lambda g, i: (0, 0))] + [ANY] * ni,
        out_specs=[pl.BlockSpec((tq, hp * 128), lambda g, i: (i, g)), pl.BlockSpec((tq, hp * 128), lambda g, i: (i, g))]
        + [ANY] * no,
        out_shape=[SDS((T, 2048), BF16), SDS((T, 2048), F32)] + jouts, scratch_shapes=jscratch,
        compiler_params=pltpu.CompilerParams(dimension_semantics=("arbitrary", "arbitrary"),
                                             vmem_limit_bytes=VMEM_LIMIT),
    )(q, kv, kv, kr, *jins)
    return res[0], res[1], list(res[2:])


def attn_bwd(q, kv, kr, do, stats, *, name, job=None):
    T = q.shape[0]
    tq = _attn_tile(T)
    nq = T // tq
    ni, no, jins, jouts, jscratch = _job_parts(job)

    def body(*refs):
        q_ref, do_ref, st_ref, kn_ref, v_ref, kr_ref = refs[:6]
        dq_ref, dkn_ref, dv_ref, dkr_ref = refs[6 + ni:10 + ni]
        jrefs = (refs[6:6 + ni], refs[10 + ni:10 + ni + no], refs[10 + ni + no:])
        h = pl.program_id(0)
        ki = pl.program_id(1)
        _job_phase(job, 0, jrefs, (h == 0) & (ki == 0))
        _job_phase(job, 1, jrefs, (h == MLA_HEADS // 2) & (ki == 0))

        @pl.when(ki == 0)
        def _():
            dq_ref[...] = jnp.zeros_like(dq_ref)

        k = jnp.concatenate([kn_ref[...], kr_ref[...]], axis=1)
        v = v_ref[...]

        def blk(qi, masked, carry):
            dk, dv = carry
            rows = pl.ds(pl.multiple_of(qi * tq, tq), tq)
            qv, dov = q_ref[rows, :], do_ref[rows, :]
            lse, dl = st_ref[rows, 0:1], st_ref[rows, STAT_SPLIT:STAT_SPLIT + 1]
            s = lax.dot_general(qv, k, NT, preferred_element_type=F32)
            if masked:
                s = jnp.where(_causal(tq), s, -1e30)
            p = jnp.exp(s - lse)
            dv = dv + lax.dot_general(p.astype(BF16), dov, TN, preferred_element_type=F32)
            dp = lax.dot_general(dov, v, NT, preferred_element_type=F32)
            ds = (p * (dp - dl)).astype(BF16)
            dk = dk + lax.dot_general(ds, qv, TN, preferred_element_type=F32)
            dq_ref[rows, :] += jnp.dot(ds, k, preferred_element_type=F32)
            return dk, dv

        carry = blk(ki, True, (jnp.zeros((tq, MLA_QPAD), F32), jnp.zeros((tq, 128), F32)))
        rest = nq - 1 - ki

        def group(j, c):
            for u in range(ATTN_UNROLL):
                c = blk(ki + 1 + ATTN_UNROLL * j + u, False, c)
            return c

        carry = lax.fori_loop(0, rest // ATTN_UNROLL, group, carry)
        for u in range(ATTN_UNROLL - 1):
            carry = lax.cond(rest % ATTN_UNROLL > u, functools.partial(lambda c, u: blk(nq - 1 - u, False, c), u=u),
                             lambda c: c, carry)
        dk, dv = carry
        dkn_ref[...] = dk[:, :128].astype(dkn_ref.dtype)
        dkr_ref[...] = dk[:, 128:]
        dv_ref[...] = dv.astype(dv_ref.dtype)
        _job_phase(job, 2, jrefs, (h == MLA_HEADS - 1) & (ki == nq - 1))

    res = pl.pallas_call(
        body, name=name, grid=(MLA_HEADS, nq),
        in_specs=[pl.BlockSpec((T, MLA_QPAD), lambda h, i: (0, h)),
                  pl.BlockSpec((T, 128), lambda h, i: (0, h)),
                  pl.BlockSpec((T, 128), lambda h, i: (0, h)),
                  pl.BlockSpec((tq, 128), lambda h, i: (i, h)),
                  pl.BlockSpec((tq, 128), lambda h, i: (i, MLA_HEADS + h)),
                  pl.BlockSpec((tq, 128), lambda h, i: (i, 0))] + [ANY] * ni,
        out_specs=[pl.BlockSpec((T, MLA_QPAD), lambda h, i: (0, h)),
                   pl.BlockSpec((tq, 128), lambda h, i: (i, h)), pl.BlockSpec((tq, 128), lambda h, i: (i, h)),
                   pl.BlockSpec((tq, 128), lambda h, i: (i, h))] + [ANY] * no,
        out_shape=[SDS((T, MLA_HEADS * MLA_QPAD), F32), SDS((T, 2048), BF16), SDS((T, 2048), BF16),
                   SDS((T, 2048), F32)] + jouts, scratch_shapes=jscratch,
        compiler_params=pltpu.CompilerParams(dimension_semantics=("arbitrary", "arbitrary"),
                                             vmem_limit_bytes=VMEM_LIMIT),
    )(q, do, stats, kv, kv, kr, *jins)
    return res[0], res[1], res[2], res[3], list(res[4:])


def _ffn_fwd(h, pre_g, w_in, w_down, post_g, tag):
    hn = rms_fwd(h, pre_g, name=f"{tag}_pre")
    gu, a = ffn_in_act(hn, w_in, name=f"{tag}_in")
    f = mm(a, w_down, name=f"{tag}_down", tk=1408)
    h2 = post_fwd(h, f, post_g, 0.5, name=f"{tag}_post")
    return h2, (h, hn, gu, a, f)


def _ffn_bwd(dh2, saved, pre_g, w_in, w_down, post_g, tag):
    h, hn, gu, a, f = saved
    df, dpost = post_bwd(f, dh2, post_g, 0.5, name=f"{tag}_post_b")
    dgu = ffn_down_bx_act(df, w_down, gu, name=f"{tag}_down_bx")
    dw_down = mm(a, df, ta=True, out_dtype=BF16, name=f"{tag}_down_bw", tm=1408)
    dhn = mm((dgu, "planes"), w_in, tb=True, name=f"{tag}_in_bx", tk=1408)
    dw_in = mm(hn, (dgu, "planes"), ta=True, out_dtype=BF16, name=f"{tag}_in_bw", tn=1408)
    dh, dpre = rms_bwd(h, dhn, dh2, pre_g, name=f"{tag}_pre_b")
    return dh, dpre, dw_in, dw_down, dpost


def _hyb_fwd(hn, w, tag):
    proj = mm(hn, w["hyb_in"], name=f"{tag}_in")
    ya = gmlp_fwd(proj, w["ln_g"], w["ln_b"], w["w_s"], w["b_st"], name=f"{tag}_gmlp")
    pre = conv_fwd(proj, w["conv_w"], w["conv_b"], name=f"{tag}_conv")
    yb, states = ssd_fwd(pre, proj, w["dt_bias"], w["a_log"], w["d_exp"], w["norm_g"], name=f"{tag}_ssd")
    yab = jnp.concatenate([ya, yb], axis=1)
    mixed = mm(yab, w["hyb_out"], name=f"{tag}_out")
    return mixed, (proj, pre, states, yab)


def _hyb_bwd(dmixed, hn, saved, w, tag):
    proj, pre, states, yab = saved
    dyab = mm(dmixed, w["hyb_out"], tb=True, name=f"{tag}_out_bx")
    dw_out = mm(yab, dmixed, ta=True, out_dtype=BF16, name=f"{tag}_out_bw")
    duv, dln_g, dln_b, dw_s, db_st = gmlp_bwd(proj, dyab, w["ln_g"], w["ln_b"], w["w_s"], w["b_st"],
                                              name=f"{tag}_gmlp_b")
    dpre, ddt, dz, ddt_bias, da_log, dd_exp, dnorm_g = ssd_bwd(
        pre, proj, states, dyab, w["dt_bias"], w["a_log"], w["d_exp"], w["norm_g"], name=f"{tag}_ssd_b")
    dxbc, dconv_w, dconv_b = conv_bwd(dpre, proj, w["conv_w"], name=f"{tag}_conv_b")
    pad = jnp.zeros((duv.shape[0], HYB_PAD - HYB_MAIN - LANES), BF16)
    dproj = jnp.concatenate([duv, dz, dxbc, ddt, pad], axis=1)
    dhn = mm(dproj, w["hyb_in"], tb=True, name=f"{tag}_in_bx")
    dw_in = mm(hn, dproj, ta=True, out_dtype=BF16, name=f"{tag}_in_bw")
    g = dict(hyb_in=dw_in, hyb_out=dw_out, ln_g=dln_g, ln_b=dln_b, w_s=dw_s, b_st=db_st, conv_w=dconv_w,
             conv_b=dconv_b, dt_bias=ddt_bias, a_log=da_log, d_exp=dd_exp, norm_g=dnorm_g)
    return dhn, g


def _mla_fwd(hn, w, rope, tag, job=None):
    cs, sn, c256, s256 = rope
    c_all = mm(hn, w["mla_in"], name=f"{tag}_in")
    cqn, ckvn, kr = kprep_fwd(c_all, cs, sn, w["q_g"], w["kv_g"], name=f"{tag}_kprep")
    qb = mm(cqn, w["uq"], name=f"{tag}_uq")
    q = qrope_fwd(qb, c256, s256, name=f"{tag}_qrope")
    kv = mm(ckvn, w["ukv"], out_dtype=BF16, name=f"{tag}_ukv")
    o, lse, jouts = attn_fwd(q, kv, kr, name=f"{tag}_attn", job=job)
    mixed = mm(o, w["mla_out"], name=f"{tag}_out")
    return mixed, (c_all, cqn, ckvn, kr, q, kv, o, lse), jouts


def _mla_bwd(dmixed, hn, saved, w, rope, tag, job=None):
    cs, sn, c256, s256 = rope
    c_all, cqn, ckvn, kr, q, kv, o, lse = saved
    do = mm(dmixed, w["mla_out"], tb=True, out_dtype=BF16, name=f"{tag}_out_bx")
    dw_out = mm(o, dmixed, ta=True, out_dtype=BF16, name=f"{tag}_out_bw")
    stats = stats_fwd(do, o, lse, name=f"{tag}_stats")
    dq, dkn, dv, dkr_h, jouts = attn_bwd(q, kv, kr, do, stats, name=f"{tag}_attn_b", job=job)
    dkr = headsum(dkr_h, name=f"{tag}_dkr")
    dkv = jnp.concatenate([dkn, dv], axis=1)
    dckvn = mm(dkv, w["ukv"], tb=True, name=f"{tag}_ukv_bx")
    dw_ukv = mm(ckvn, dkv, ta=True, name=f"{tag}_ukv_bw")
    dqb = qrope_bwd(dq, c256, s256, name=f"{tag}_qrope_b")
    dcqn = mm(dqb, w["uq"], tb=True, name=f"{tag}_uq_bx")
    dw_uq = mm(cqn, dqb, ta=True, name=f"{tag}_uq_bw")
    dc_all, dq_g, dkv_g = kprep_bwd(c_all, cs, sn, dcqn, dckvn, dkr, w["q_g"], w["kv_g"], name=f"{tag}_kprep_b")
    dhn = mm(dc_all, w["mla_in"], tb=True, name=f"{tag}_in_bx")
    dw_in = mm(hn, dc_all, ta=True, name=f"{tag}_in_bw")
    g = dict(mla_in=dw_in, mla_out=dw_out, uq=dw_uq, ukv=dw_ukv, q_g=dq_g, kv_g=dkv_g)
    return dhn, g, jouts


class NoJobs:
    def fwd_job(self, i):
        return None

    def fwd_done(self, i, outs):
        pass

    def bwd_job(self, i, grads):
        return None

    def bwd_done(self, i, outs):
        pass


def local_step(x, p, rope, target, weights_of, jobs=NoJobs()):
    h = x
    saved = []
    lw = []
    for i in range(DEPTH):
        w = weights_of(i)
        lw.append(w)
        t = f"l{i}"
        h, s1 = _ffn_fwd(h, w["ffn1_pre_g"], w["ffn1_w_in"], w["ffn1_w_down"], w["ffn1_post_g"], f"{t}_f1")
        h1 = h
        hn = rms_fwd(h1, w["mix_pre_g"], name=f"{t}_mixpre")
        if i % 2 == 0:
            mixed, sm = _hyb_fwd(hn, w, f"{t}_hyb")
        else:
            job = jobs.fwd_job(i)
            mixed, sm, jouts = _mla_fwd(hn, w, rope, f"{t}_mla", job)
            if job is not None:
                jobs.fwd_done(i, jouts)
        h = post_fwd(h1, mixed, w["mix_post_g"], 1.0, name=f"{t}_mixpost")
        h, s2 = _ffn_fwd(h, w["ffn2_pre_g"], w["ffn2_w_in"], w["ffn2_w_down"], w["ffn2_post_g"], f"{t}_f2")
        h3 = h
        hn3 = rms_fwd(h3, w["ple_pre_g"], name=f"{t}_plepre")
        gl = mm(hn3, w["ple_w_gate"], name=f"{t}_plegate")
        pp = mm((p, i), w["ple_w_proj"], name=f"{t}_pleproj")
        h = ple_fwd(h3, gl, pp, w["ple_post_g"], name=f"{t}_plepost")
        saved.append((s1, h1, hn, sm, mixed, s2, h3, hn3, gl, pp))

    dh, sq = loss_fwd(h, target, name="loss")
    grads = [None] * DEPTH
    for i in reversed(range(DEPTH)):
        w = lw[i]
        t = f"l{i}"
        s1, h1, hn, sm, mixed, s2, h3, hn3, gl, pp = saved[i]
        g = {}
        dgl, dpp, g["ple_post_g"] = ple_bwd(gl, pp, dh, w["ple_post_g"], name=f"{t}_plepost_b")
        dhn3 = mm(dgl, w["ple_w_gate"], tb=True, name=f"{t}_plegate_bx")
        g["ple_w_gate"] = mm(hn3, dgl, ta=True, out_dtype=BF16, name=f"{t}_plegate_bw")
        g["ple_w_proj"] = mm((p, i), dpp, ta=True, out_dtype=BF16, name=f"{t}_pleproj_bw")
        dh, g["ple_pre_g"] = rms_bwd(h3, dhn3, dh, w["ple_pre_g"], name=f"{t}_plepre_b")
        dh, g["ffn2_pre_g"], g["ffn2_w_in"], g["ffn2_w_down"], g["ffn2_post_g"] = _ffn_bwd(
            dh, s2, w["ffn2_pre_g"], w["ffn2_w_in"], w["ffn2_w_down"], w["ffn2_post_g"], f"{t}_f2")
        dmixed, g["mix_post_g"] = post_bwd(mixed, dh, w["mix_post_g"], 1.0, name=f"{t}_mixpost_b")
        if i % 2 == 0:
            dhn, gm = _hyb_bwd(dmixed, hn, sm, w, f"{t}_hyb")
        else:
            job = jobs.bwd_job(i, grads)
            dhn, gm, jouts = _mla_bwd(dmixed, hn, sm, w, rope, f"{t}_mla", job)
            if job is not None:
                jobs.bwd_done(i, jouts)
        g.update(gm)
        dh, g["mix_pre_g"] = rms_bwd(h1, dhn, dh, w["mix_pre_g"], name=f"{t}_mixpre_b")
        dh, g["ffn1_pre_g"], g["ffn1_w_in"], g["ffn1_w_down"], g["ffn1_post_g"] = _ffn_bwd(
            dh, s1, w["ffn1_pre_g"], w["ffn1_w_in"], w["ffn1_w_down"], w["ffn1_post_g"], f"{t}_f1")
        grads[i] = g
    return sq, dh, grads


def _zeros_like_cols(a, n):
    return jnp.zeros(a.shape[:-1] + (n,), a.dtype)


def layer_weights(full, i):
    j = i // 2
    row = lambda v: v.reshape(1, -1)
    w = {k: row(full[k][i]) for k in ("ffn1_pre_g", "ffn1_post_g", "mix_pre_g", "mix_post_g", "ffn2_pre_g",
                                      "ffn2_post_g", "ple_pre_g", "ple_post_g")}
    for k in ("ffn1_w_in", "ffn1_w_down", "ffn2_w_in", "ffn2_w_down", "ple_w_gate", "ple_w_proj"):
        w[k] = full[k][i]
    if i % 2 == 0:
        hw = full["hyb_w_in"][j]
        w["hyb_in"] = jnp.concatenate([hw, _zeros_like_cols(hw, HYB_PAD - HYB_IN)], axis=1)
        w["hyb_out"] = full["hyb_w_out"][j]
        w["ln_g"], w["ln_b"] = row(full["gm_ln_g"][j]), row(full["gm_ln_b"][j])
        w["w_s"] = full["gm_w_s"][j].reshape(GM_HEADS * CHUNK, CHUNK)
        w["b_st"] = jnp.pad(full["gm_b_s"][j].T, ((0, 0), (0, LANES - GM_HEADS)))
        w["conv_w"], w["conv_b"] = full["ssd_conv_w"][j], row(full["ssd_conv_b"][j])
        pad16 = lambda v: jnp.pad(v.reshape(1, -1), ((0, 0), (0, LANES - SSD_HEADS)))
        w["dt_bias"], w["a_log"] = pad16(full["ssd_dt_bias"][j]), pad16(full["ssd_a_log"][j])
        w["d_exp"] = row(jnp.repeat(full["ssd_d"][j], SSD_HEAD_DIM))
        w["norm_g"] = row(full["ssd_norm_g"][j])
    else:
        wi = full["mla_w_in"][j]
        z64 = _zeros_like_cols(wi, 64)
        w["mla_in"] = jnp.concatenate([wi[:, :384], wi[:, 384:448], z64, -wi[:, 416:448], wi[:, 384:416], z64], axis=1)
        uq = full["mla_w_uq"][j].reshape(MLA_Q_LORA, MLA_HEADS, MLA_QK)
        zq = jnp.zeros((MLA_Q_LORA, MLA_HEADS, 64), uq.dtype)
        pad_part = jnp.concatenate([uq, zq], axis=2)
        swp_part = jnp.concatenate([jnp.zeros_like(uq[:, :, :128]), -uq[:, :, 160:192], uq[:, :, 128:160], zq], axis=2)
        w["uq"] = jnp.concatenate([pad_part.reshape(MLA_Q_LORA, -1), swp_part.reshape(MLA_Q_LORA, -1)], axis=1)
        ukv = full["mla_w_ukv"][j].reshape(MLA_KV_LORA, MLA_HEADS, 256)
        w["ukv"] = jnp.concatenate([ukv[:, :, :128].reshape(MLA_KV_LORA, -1), ukv[:, :, 128:].reshape(MLA_KV_LORA, -1)],
                                   axis=1)
        w["mla_out"] = full["mla_w_out"][j]
        w["q_g"], w["kv_g"] = row(full["mla_q_norm_g"][j]), row(full["mla_kv_norm_g"][j])
    return w


def full_grads(grads):
    out = {}
    stack = lambda k, idx: jnp.stack([grads[i][k] for i in idx])
    every, even, odd = range(DEPTH), range(0, DEPTH, 2), range(1, DEPTH, 2)
    for k in ("ffn1_pre_g", "ffn1_post_g", "mix_pre_g", "mix_post_g", "ffn2_pre_g", "ffn2_post_g", "ple_pre_g",
              "ple_post_g"):
        out[k] = stack(k, every).reshape(DEPTH, D_MODEL)
    for k in ("ffn1_w_in", "ffn1_w_down", "ffn2_w_in", "ffn2_w_down", "ple_w_gate", "ple_w_proj"):
        out[k] = stack(k, every)
    out["hyb_w_in"] = stack("hyb_in", even)[:, :, :HYB_IN]
    out["hyb_w_out"] = stack("hyb_out", even)
    out["gm_ln_g"] = stack("ln_g", even).reshape(2, 1024)
    out["gm_ln_b"] = stack("ln_b", even).reshape(2, 1024)
    out["gm_w_s"] = stack("w_s", even).reshape(2, GM_HEADS, CHUNK, CHUNK)
    out["gm_b_s"] = jnp.swapaxes(stack("b_st", even)[:, :, :GM_HEADS], 1, 2)
    out["ssd_conv_w"] = stack("conv_w", even)
    out["ssd_conv_b"] = stack("conv_b", even).reshape(2, SSD_CONV_CH)
    out["ssd_dt_bias"] = stack("dt_bias", even)[:, 0, :SSD_HEADS]
    out["ssd_a_log"] = stack("a_log", even)[:, 0, :SSD_HEADS]
    out["ssd_d"] = stack("d_exp", even).reshape(2, SSD_HEADS, SSD_HEAD_DIM).sum(axis=-1)
    out["ssd_norm_g"] = stack("norm_g", even).reshape(2, 1024)
    dwi = stack("mla_in", odd)
    out["mla_w_in"] = jnp.concatenate([dwi[:, :, :384], dwi[:, :, 384:416] + dwi[:, :, 544:576],
                                       dwi[:, :, 416:448] - dwi[:, :, 512:544]], axis=2)
    duq = stack("uq", odd)
    half = MLA_HEADS * MLA_QPAD
    dp = duq[:, :, :half].reshape(2, MLA_Q_LORA, MLA_HEADS, MLA_QPAD)
    ds = duq[:, :, half:].reshape(2, MLA_Q_LORA, MLA_HEADS, MLA_QPAD)
    out["mla_w_uq"] = jnp.concatenate([dp[..., :128], dp[..., 128:160] + ds[..., 160:192],
                                       dp[..., 160:192] - ds[..., 128:160]], axis=-1).reshape(2, MLA_Q_LORA, -1)
    dukv = stack("ukv", odd)
    dk = dukv[:, :, :2048].reshape(2, MLA_KV_LORA, MLA_HEADS, 128)
    dv = dukv[:, :, 2048:].reshape(2, MLA_KV_LORA, MLA_HEADS, 128)
    out["mla_w_ukv"] = jnp.concatenate([dk, dv], axis=-1).reshape(2, MLA_KV_LORA, -1)
    out["mla_w_out"] = stack("mla_out", odd)
    out["mla_q_norm_g"] = stack("q_g", odd).reshape(2, MLA_Q_LORA)
    out["mla_kv_norm_g"] = stack("kv_g", odd).reshape(2, MLA_KV_LORA)
    return out


def rope_tables(positions):
    T = positions.shape[0]
    inv = 1.0 / (ROPE_BASE ** (jnp.arange(0, MLA_ROPE, 2, dtype=F32) / MLA_ROPE))
    ang = positions.astype(F32)[:, None] * inv
    cos, sin = jnp.cos(ang), jnp.sin(ang)
    z64 = jnp.zeros((T, 64), F32)
    cs = jnp.concatenate([cos, cos, z64], axis=1)
    sn = jnp.concatenate([sin, sin, z64], axis=1)
    c256 = jnp.concatenate([jnp.ones((T, 128), F32), cs], axis=1)
    s256 = jnp.concatenate([jnp.zeros((T, 128), F32), sn], axis=1)
    return cs, sn, c256, s256


def _rows(n):
    return -(-n // LANES)


def _pack(pieces, dtype, row_multiple):
    flat = []
    total = 0
    for a in pieces:
        v = a.reshape(-1).astype(dtype)
        padn = _rows(v.shape[0]) * LANES - v.shape[0]
        if padn:
            v = jnp.concatenate([v, jnp.zeros((padn,), dtype)])
        flat.append(v)
        total += v.shape[0] // LANES
    tail = -total % row_multiple
    if tail:
        flat.append(jnp.zeros((tail * LANES,), dtype))
    return jnp.concatenate(flat).reshape(-1, LANES)


def _unpack(slab, shapes):
    out = []
    r = 0
    for s in shapes:
        n = int(np.prod(s))
        nr = _rows(n)
        out.append(slab[r:r + nr].reshape(-1)[:n].reshape(s))
        r += nr
    return out


def _shard_shape(shape, ax):
    if ax is None:
        return tuple(shape)
    s = list(shape)
    s[ax] //= N_CHIPS
    return tuple(s)


def _chip_slice(a, ax, k):
    if ax is None:
        return a
    n = a.shape[ax] // N_CHIPS
    return lax.slice_in_dim(a, k * n, (k + 1) * n, axis=ax)


def _plane_peers():
    x, y, c = lax.axis_index("x"), lax.axis_index("y"), lax.axis_index("c")
    return (x, y, c), [(1 - x, y, c), (x, 1 - y, c), (1 - x, 1 - y, c)]


ANY = pl.BlockSpec(memory_space=pl.ANY)


def plane_allgather(slab):
    R = slab.shape[0]

    def body(src, out, send_sems, recv_sems, local_sem):
        (x, y, c), peers = _plane_peers()
        me = 2 * x + y
        local = pltpu.make_async_copy(src, out.at[me], local_sem)
        local.start()
        copies = []
        for j, peer in enumerate(peers):
            cp = pltpu.make_async_remote_copy(src_ref=src, dst_ref=out.at[me], send_sem=send_sems.at[j],
                                              recv_sem=recv_sems.at[j], device_id=peer, device_id_type=MESH)
            cp.start()
            copies.append(cp)
        for cp in copies:
            cp.wait()
        local.wait()

    return pl.pallas_call(
        body, name="plane_allgather", out_shape=SDS((N_CHIPS, R, LANES), slab.dtype),
        in_specs=[ANY], out_specs=ANY,
        scratch_shapes=[pltpu.SemaphoreType.DMA((3,)), pltpu.SemaphoreType.DMA((3,)), pltpu.SemaphoreType.DMA],
    )(slab)


def plane_alltoall(buf):
    R = buf.shape[1]

    def body(src, out, send_sems, recv_sems, local_sem):
        (x, y, c), peers = _plane_peers()
        me = 2 * x + y
        local = pltpu.make_async_copy(src.at[me], out.at[me], local_sem)
        local.start()
        copies = []
        for j, peer in enumerate(peers):
            cp = pltpu.make_async_remote_copy(src_ref=src.at[2 * peer[0] + peer[1]], dst_ref=out.at[me],
                                              send_sem=send_sems.at[j], recv_sem=recv_sems.at[j], device_id=peer,
                                              device_id_type=MESH)
            cp.start()
            copies.append(cp)
        for cp in copies:
            cp.wait()
        local.wait()

    return pl.pallas_call(
        body, name="plane_alltoall", out_shape=SDS((N_CHIPS, R, LANES), buf.dtype),
        in_specs=[ANY], out_specs=ANY,
        scratch_shapes=[pltpu.SemaphoreType.DMA((3,)), pltpu.SemaphoreType.DMA((3,)), pltpu.SemaphoreType.DMA],
    )(buf)


def sibling_swap(buf):
    def body(src, out, send_sem, recv_sem):
        x, y, c = lax.axis_index("x"), lax.axis_index("y"), lax.axis_index("c")
        cp = pltpu.make_async_remote_copy(src_ref=src, dst_ref=out, send_sem=send_sem, recv_sem=recv_sem,
                                          device_id=(x, y, 1 - c), device_id_type=MESH)
        cp.start()
        cp.wait()

    return pl.pallas_call(
        body, name="sibling_swap", out_shape=SDS(buf.shape, buf.dtype), in_specs=[ANY], out_specs=ANY,
        scratch_shapes=[pltpu.SemaphoreType.DMA, pltpu.SemaphoreType.DMA],
    )(buf)


def _chip_block(ref, ax, k, n, stride=None):
    stride = n if stride is None else stride
    start = pl.multiple_of(k * stride, math.gcd(n, stride))
    return ref.at[:, pl.ds(start, n), :] if ax == 1 else ref.at[:, :, pl.ds(start, n)]


def gather_job(shards, axes):
    n = len(shards)
    fulls = []
    for s, ax in zip(shards, axes):
        if ax == "stack":
            fulls.append(SDS((N_CHIPS,) + tuple(s.shape), s.dtype))
            continue
        shape = list(s.shape)
        shape[ax] *= N_CHIPS
        fulls.append(SDS(tuple(shape), s.dtype))

    def copies(srcs, outs, sems):
        send_sems, recv_sems, local_sems = sems
        (x, y, c), peers = _plane_peers()
        me = 2 * x + y
        cps = []
        for t in range(n):
            if axes[t] == "stack":
                dst = outs[t].at[me]
            else:
                dst = _chip_block(outs[t], axes[t], me, srcs[t].shape[axes[t]])
            cps.append(pltpu.make_async_copy(srcs[t], dst, local_sems.at[t]))
            for j, peer in enumerate(peers):
                cps.append(pltpu.make_async_remote_copy(src_ref=srcs[t], dst_ref=dst, send_sem=send_sems.at[3 * t + j],
                                                        recv_sem=recv_sems.at[3 * t + j], device_id=peer,
                                                        device_id_type=MESH))
        return cps

    def start(srcs, outs, sems):
        for cp in copies(srcs, outs, sems):
            cp.start()

    def finish(srcs, outs, sems):
        for cp in copies(srcs, outs, sems):
            cp.wait()

    scratch = [pltpu.SemaphoreType.DMA((3 * n,)), pltpu.SemaphoreType.DMA((3 * n,)), pltpu.SemaphoreType.DMA((n,))]
    return dict(inputs=list(shards), out_shape=fulls, scratch=scratch, phases=(start, None, finish))


def split_gather_job(shards, axes):
    n = len(shards)
    plain = gather_job(shards, axes)

    def region(out, t, chip, shard_shape, half_of):
        ax = axes[t]
        ref = out.at[chip] if ax == "stack" else out
        starts, sizes = [0, 0, 0], list(shard_shape)
        if ax != "stack":
            starts[ax] = chip * shard_shape[ax]
        if half_of is not None:
            sa = 0 if shard_shape[0] % 2 == 0 else 1
            sizes[sa] = shard_shape[sa] // 2
            starts[sa] = starts[sa] + half_of * sizes[sa]
        starts = [s if isinstance(s, int) else pl.multiple_of(s, math.gcd(full, z))
                  for s, z, full in zip(starts, sizes, shard_shape)]
        return ref.at[tuple(pl.ds(s, z) for s, z in zip(starts, sizes))]

    def copies(srcs, outs, sems):
        send_sems, recv_sems, local_sems, fsend_sems, frecv_sems = sems
        (x, y, c), peers = _plane_peers()
        sibling = (x, y, 1 - c)
        me = 2 * x + y
        locals_, sends, forwards = [], [], []
        for t in range(n):
            shape = srcs[t].shape
            locals_.append(pltpu.make_async_copy(srcs[t], region(outs[t], t, me, shape, None), local_sems.at[t]))
            sa = 0 if shape[0] % 2 == 0 else 1
            hs = shape[sa] // 2
            mine = srcs[t].at[pl.ds(c * hs, hs)] if sa == 0 else srcs[t].at[:, pl.ds(c * hs, hs), :]
            for j, (px, py, _) in enumerate(peers):
                sends.append(pltpu.make_async_remote_copy(
                    src_ref=mine, dst_ref=region(outs[t], t, me, shape, c), send_sem=send_sems.at[3 * t + j],
                    recv_sem=recv_sems.at[3 * t + j], device_id=(px, py, c), device_id_type=MESH))
                landed = region(outs[t], t, 2 * px + py, shape, c)
                forwards.append(pltpu.make_async_remote_copy(
                    src_ref=landed, dst_ref=landed, send_sem=fsend_sems.at[3 * t + j],
                    recv_sem=frecv_sems.at[3 * t + j], device_id=sibling, device_id_type=MESH))
        return locals_, sends, forwards

    def start(srcs, outs, sems):
        locals_, sends, _ = copies(srcs, outs, sems)
        for cp in locals_ + sends:
            cp.start()

    def middle(srcs, outs, sems):
        _, sends, forwards = copies(srcs, outs, sems)
        for cp, fw in zip(sends, forwards):
            cp.wait_recv()
            fw.start()

    def finish(srcs, outs, sems):
        locals_, sends, forwards = copies(srcs, outs, sems)
        for cp in locals_:
            cp.wait()
        for cp in sends:
            cp.wait_send()
        for fw in forwards:
            fw.wait()

    scratch = [pltpu.SemaphoreType.DMA((3 * n,)) for _ in range(2)] + [pltpu.SemaphoreType.DMA((n,))] + \
              [pltpu.SemaphoreType.DMA((3 * n,)) for _ in range(2)]
    return dict(inputs=list(shards), out_shape=plain["out_shape"], scratch=scratch, phases=(start, middle, finish))


def exchange_job(grads, axes):
    n = len(grads)
    outs = []
    spans = []
    for g, ax in zip(grads, axes):
        ax, width, stride = ax if isinstance(ax, tuple) else (ax, g.shape[ax] // N_CHIPS, None)
        spans.append((ax, width, stride))
        shape = list(g.shape)
        shape[ax] = width
        outs.append(SDS((N_CHIPS,) + tuple(shape), g.dtype))

    def copies(srcs, res, sems):
        mine, theirs = res[:n], res[n:]
        send_sems, recv_sems, local_sems, fsend_sems, frecv_sems = sems
        (x, y, c), peers = _plane_peers()
        sibling = (x, y, 1 - c)
        me = 2 * x + y
        blocks = [me] + [2 * px + py for (px, py, _) in peers]
        locals_, sends, forwards = [], [], []
        for t in range(n):
            ax, width, stride = spans[t]
            locals_.append(pltpu.make_async_copy(_chip_block(srcs[t], ax, me, width, stride), mine[t].at[me],
                                                 local_sems.at[t]))
            for j, peer in enumerate(peers):
                sends.append(pltpu.make_async_remote_copy(
                    src_ref=_chip_block(srcs[t], ax, blocks[j + 1], width, stride), dst_ref=mine[t].at[me],
                    send_sem=send_sems.at[3 * t + j], recv_sem=recv_sems.at[3 * t + j], device_id=peer,
                    device_id_type=MESH))
            for q, blk in enumerate(blocks):
                forwards.append(pltpu.make_async_remote_copy(
                    src_ref=mine[t].at[blk], dst_ref=theirs[t].at[blk], send_sem=fsend_sems.at[4 * t + q],
                    recv_sem=frecv_sems.at[4 * t + q], device_id=sibling, device_id_type=MESH))
        return locals_, sends, forwards

    def start(srcs, res, sems):
        locals_, sends, _ = copies(srcs, res, sems)
        for cp in locals_ + sends:
            cp.start()

    def middle(srcs, res, sems):
        locals_, sends, forwards = copies(srcs, res, sems)
        for t in range(n):
            locals_[t].wait()
            for q in range(N_CHIPS):
                if q > 0:
                    sends[3 * t + q - 1].wait_recv()
                forwards[4 * t + q].start()

    def finish(srcs, res, sems):
        _, sends, forwards = copies(srcs, res, sems)
        for cp in sends:
            cp.wait_send()
        for fw in forwards:
            fw.wait()

    scratch = [pltpu.SemaphoreType.DMA((3 * n,)), pltpu.SemaphoreType.DMA((3 * n,)), pltpu.SemaphoreType.DMA((n,)),
               pltpu.SemaphoreType.DMA((4 * n,)), pltpu.SemaphoreType.DMA((4 * n,))]
    return dict(inputs=list(grads), out_shape=outs + outs, scratch=scratch, phases=(start, middle, finish))


def run_job(job, ins, outs, sems, first=None, mid=None, last=None):
    for phase, when in zip(job["phases"], (first, mid, last)):
        if phase is None:
            continue
        if when is None:
            phase(ins, outs, sems)
        else:
            pl.when(when)(functools.partial(phase, ins, outs, sems))


def job_call(job, *, name):
    ni, no = len(job["inputs"]), len(job["out_shape"])

    def body(*refs):
        run_job(job, refs[:ni], refs[ni:ni + no], refs[ni + no:])

    return pl.pallas_call(body, name=name, out_shape=job["out_shape"], in_specs=[ANY] * ni, out_specs=[ANY] * no,
                          scratch_shapes=job["scratch"])(*job["inputs"])


def _adam_update(g, w, m, v):
    mn = ADAM_B1 * m + (1.0 - ADAM_B1) * g
    vn = ADAM_B2 * v + (1.0 - ADAM_B2) * jnp.square(g)
    m_hat = mn / (1.0 - ADAM_B1 ** ADAM_STEP)
    v_hat = vn / (1.0 - ADAM_B2 ** ADAM_STEP)
    return -ADAM_LR * (m_hat / (jnp.sqrt(v_hat) + ADAM_EPS) + ADAM_WD * w), mn, vn


def _row_tile(rs):
    for cand in range(256, 15, -16):
        if rs % cand == 0:
            return cand
    return rs


def _group_sum(half, a0_r, b0_r, a1_r, b1_r):
    def plane_sums(a_r, b_r):
        pa = a_r[0].astype(F32)
        pb = b_r[0].astype(F32)
        for k in range(1, N_CHIPS):
            pa = pa + a_r[k].astype(F32)
            pb = pb + b_r[k].astype(F32)
        return pa + pb
    return jnp.where(pl.program_id(0) < half, plane_sums(a0_r, b0_r), plane_sums(a1_r, b1_r))


def _group_specs(half, tr, cs):
    first = pl.BlockSpec((N_CHIPS, None, tr, cs), lambda l, i: (0, jnp.minimum(l, half - 1), i, 0))
    second = pl.BlockSpec((N_CHIPS, None, tr, cs), lambda l, i: (0, jnp.maximum(l - half, 0), i, 0))
    return [first, first, second, second]


def adamw_reg(groups, w, m, v, *, name):
    L, rs, cs = w.shape
    tr = _row_tile(rs)

    def body(a0_r, b0_r, a1_r, b1_r, w_r, m_r, v_r, g_o, d_o, m_o, v_o):
        g = _group_sum(L // 2, a0_r, b0_r, a1_r, b1_r)
        d, mn, vn = _adam_update(g, w_r[...], m_r[...], v_r[...])
        g_o[...] = g
        d_o[...] = d
        m_o[...] = mn
        v_o[...] = vn

    s1 = pl.BlockSpec((None, tr, cs), lambda l, i: (l, i, 0))
    (a0, b0), (a1, b1) = groups
    return pl.pallas_call(
        body, name=name, grid=(L, rs // tr), in_specs=_group_specs(L // 2, tr, cs) + [s1, s1, s1], out_specs=[s1] * 4,
        out_shape=[SDS((L, rs, cs), F32)] * 4,
        compiler_params=pltpu.CompilerParams(dimension_semantics=("parallel", "parallel"),
                                             vmem_limit_bytes=VMEM_LIMIT),
    )(a0, b0, a1, b1, w, m, v)


def groups_sum(groups, *, name):
    (a0, b0), (a1, b1) = groups
    _, half, rs, cs = a0.shape
    tr = _row_tile(rs)

    def body(a0_r, b0_r, a1_r, b1_r, g_o):
        g_o[...] = _group_sum(half, a0_r, b0_r, a1_r, b1_r)

    return pl.pallas_call(
        body, name=name, grid=(2 * half, rs // tr), in_specs=_group_specs(half, tr, cs),
        out_specs=pl.BlockSpec((None, tr, cs), lambda l, i: (l, i, 0)), out_shape=SDS((2 * half, rs, cs), F32),
        compiler_params=pltpu.CompilerParams(dimension_semantics=("parallel", "parallel"),
                                             vmem_limit_bytes=VMEM_LIMIT),
    )(a0, b0, a1, b1)


def adamw_plain(g, w, m, v, *, name):
    L, rs, cs = w.shape
    tr = _row_tile(rs)

    def body(g_r, w_r, m_r, v_r, d_o, m_o, v_o):
        d, mn, vn = _adam_update(g_r[...], w_r[...], m_r[...], v_r[...])
        d_o[...] = d
        m_o[...] = mn
        v_o[...] = vn

    s1 = pl.BlockSpec((None, tr, cs), lambda l, i: (l, i, 0))
    return pl.pallas_call(
        body, name=name, grid=(L, rs // tr), in_specs=[s1] * 4, out_specs=[s1] * 3,
        out_shape=[SDS((L, rs, cs), F32)] * 3,
        compiler_params=pltpu.CompilerParams(dimension_semantics=("parallel", "parallel"),
                                             vmem_limit_bytes=VMEM_LIMIT),
    )(g, w, m, v)


def plane_sum(r4):
    R = r4.shape[1]

    def body(r_ref, o_ref):
        acc = r_ref[0].astype(F32)
        for k in range(1, N_CHIPS):
            acc = acc + r_ref[k].astype(F32)
        o_ref[...] = acc

    return pl.pallas_call(
        body, name="plane_sum", grid=(R // PACK_ROWS,),
        in_specs=[pl.BlockSpec((N_CHIPS, PACK_ROWS, LANES), lambda i: (0, i, 0))],
        out_specs=pl.BlockSpec((PACK_ROWS, LANES), lambda i: (i, 0)), out_shape=SDS((R, LANES), F32),
        compiler_params=pltpu.CompilerParams(dimension_semantics=("parallel",)),
    )(r4)


def adamw(pa, pb, w, m, v):
    R = w.shape[0]

    def body(pa_r, pb_r, w_r, m_r, v_r, g_o, d_o, m_o, v_o):
        g = pa_r[...] + pb_r[...]
        d, mn, vn = _adam_update(g, w_r[...], m_r[...], v_r[...])
        g_o[...] = g
        d_o[...] = d
        m_o[...] = mn
        v_o[...] = vn

    spec = pl.BlockSpec((PACK_ROWS, LANES), lambda i: (i, 0))
    return pl.pallas_call(
        body, name="adamw", grid=(R // PACK_ROWS,), in_specs=[spec] * 5, out_specs=[spec] * 4,
        out_shape=[SDS((R, LANES), F32)] * 4,
        compiler_params=pltpu.CompilerParams(dimension_semantics=("parallel",)),
    )(pa, pb, w, m, v)


GATHER_AXES = [WSPEC[n][2] for n in REG] + ["stack"]
EXCHANGE_AXES = [WSPEC[n][2] for n in REG] + [(2, WIN_WIDTH, WIN_STRIDE)]


def moved_shards(wl, group):
    out = []
    for n in MOVED:
        half = WSPEC[n][1][0] // 2
        out.append(wl[n][group * half:(group + 1) * half].astype(BF16))
    return out


def moved_grads(grads, group):
    out = []
    for n in MOVED:
        if WSPEC[n][1][0] == DEPTH:
            layers, key = (2 * group, 2 * group + 1), n
        elif n == "mla_w_out":
            layers, key = (2 * group + 1,), "mla_out"
        else:
            layers, key = (2 * group,), {"hyb_w_out": "hyb_out", WIN: "hyb_in"}[n]
        out.append(jnp.stack([grads[l][key] for l in layers]))
    return out


def gather_misc(wl):
    full = {}
    sharded = [n for n in MISC if WSPEC[n][2] is not None]
    pieces = [wl[n].astype(BF16) if WSPEC[n][3] else lax.bitcast_convert_type(wl[n], BF16) for n in sharded]
    got = plane_allgather(_pack(pieces, BF16, 16))
    shapes = [v.shape for v in pieces]
    per_chip = [_unpack(got[k], shapes) for k in range(N_CHIPS)]
    for idx, n in enumerate(sharded):
        parts = [per_chip[k][idx] for k in range(N_CHIPS)]
        if not WSPEC[n][3]:
            parts = [lax.bitcast_convert_type(v, F32) for v in parts]
        full[n] = jnp.concatenate(parts, axis=WSPEC[n][2])
    for n in MISC:
        if WSPEC[n][2] is None:
            full[n] = wl[n]
    return full


def kernel(x, p, positions, ffn1_pre_g, ffn1_w_in, ffn1_w_down, ffn1_post_g, mix_pre_g, mix_post_g, ffn2_pre_g, ffn2_w_in, ffn2_w_down, ffn2_post_g, ple_pre_g, ple_w_gate, ple_w_proj, ple_post_g, hyb_w_in, gm_ln_g, gm_ln_b, gm_w_s, gm_b_s, ssd_conv_w, ssd_conv_b, ssd_dt_bias, ssd_a_log, ssd_d, ssd_norm_g, hyb_w_out, mla_w_in, mla_q_norm_g, mla_kv_norm_g, mla_w_uq, mla_w_ukv, mla_w_out, loss_target, m_ffn1_pre_g, m_ffn1_w_in, m_ffn1_w_down, m_ffn1_post_g, m_mix_pre_g, m_mix_post_g, m_ffn2_pre_g, m_ffn2_w_in, m_ffn2_w_down, m_ffn2_post_g, m_ple_pre_g, m_ple_w_gate, m_ple_w_proj, m_ple_post_g, m_hyb_w_in, m_gm_ln_g, m_gm_ln_b, m_gm_w_s, m_gm_b_s, m_ssd_conv_w, m_ssd_conv_b, m_ssd_dt_bias, m_ssd_a_log, m_ssd_d, m_ssd_norm_g, m_hyb_w_out, m_mla_w_in, m_mla_q_norm_g, m_mla_kv_norm_g, m_mla_w_uq, m_mla_w_ukv, m_mla_w_out, v_ffn1_pre_g, v_ffn1_w_in, v_ffn1_w_down, v_ffn1_post_g, v_mix_pre_g, v_mix_post_g, v_ffn2_pre_g, v_ffn2_w_in, v_ffn2_w_down, v_ffn2_post_g, v_ple_pre_g, v_ple_w_gate, v_ple_w_proj, v_ple_post_g, v_hyb_w_in, v_gm_ln_g, v_gm_ln_b, v_gm_w_s, v_gm_b_s, v_ssd_conv_w, v_ssd_conv_b, v_ssd_dt_bias, v_ssd_a_log, v_ssd_d, v_ssd_norm_g, v_hyb_w_out, v_mla_w_in, v_mla_q_norm_g, v_mla_kv_norm_g, v_mla_w_uq, v_mla_w_ukv, v_mla_w_out):
    args = locals()
    wl = {n: args[n] for n in WNAMES}
    ml = {n: args["m_" + n] for n in WNAMES}
    vl = {n: args["v_" + n] for n in WNAMES}

    full = gather_misc(wl)
    nmoved = len(MOVED)
    halves = [WSPEC[n][1][0] // 2 for n in MOVED]
    for n, half in zip(MOVED, halves):
        full[n] = [None] * (2 * half)
    exchanged = [None, None]

    def gathered(group, outs):
        for t, n in enumerate(MOVED):
            for l in range(halves[t]):
                if n == WIN:
                    full[n][group * halves[t] + l] = jnp.concatenate([outs[t][k, l] for k in range(N_CHIPS)], axis=-1)
                else:
                    full[n][group * halves[t] + l] = (outs[t], l)

    gathered(0, job_call(split_gather_job(moved_shards(wl, 0), GATHER_AXES), name="gather_first"))

    class Jobs(NoJobs):
        def fwd_job(self, i):
            return gather_job(moved_shards(wl, 1), GATHER_AXES) if i == 1 else None

        def fwd_done(self, i, outs):
            gathered(1, outs)

        def bwd_job(self, i, grads):
            return exchange_job(moved_grads(grads, 1), EXCHANGE_AXES) if i == 1 else None

        def bwd_done(self, i, outs):
            exchanged[1] = outs

    rope = rope_tables(positions[0])
    T = x.shape[1]
    sq, dx, grads = local_step(x[0], p.reshape(DEPTH, T, p.shape[-1]), rope, loss_target[0],
                               lambda i: layer_weights(full, i), Jobs())
    loss = lax.psum(0.5 * jnp.sum(sq) / D_MODEL, ("x", "y", "c"))

    res = {}
    exchanged[0] = job_call(exchange_job(moved_grads(grads, 0), EXCHANGE_AXES), name="exchange_first")
    for t, n in enumerate(MOVED):
        groups = [(e[t], e[nmoved + t]) for e in exchanged]
        if n == WIN:
            window = groups_sum(groups, name=f"sum_{n}")
            chip = 2 * lax.axis_index("x") + lax.axis_index("y")
            g = lax.dynamic_slice_in_dim(window, chip * (WIN_SHARD - WIN_STRIDE), WIN_SHARD, axis=2)
            res[n] = [g] + list(adamw_plain(g, wl[n], ml[n], vl[n], name=f"adamw_{n}"))
        else:
            res[n] = adamw_reg(groups, wl[n], ml[n], vl[n], name=f"adamw_{n}")
    fg = full_grads(grads)
    dest = [_pack([_chip_slice(fg[n], WSPEC[n][2], k) for n in MISC], BF16, PACK_ROWS) for k in range(N_CHIPS)]
    mine = plane_sum(plane_alltoall(jnp.stack(dest)))
    other = sibling_swap(mine)
    slabs = adamw(mine, other, *[_pack([d[n] for n in MISC], F32, PACK_ROWS) for d in (wl, ml, vl)])
    shapes = [wl[n].shape for n in MISC]
    unpacked = [_unpack(s, shapes) for s in slabs]
    for idx, n in enumerate(MISC):
        res[n] = [u[idx] for u in unpacked]
    return (loss, dx[None], *[res[n][k] for k in range(4) for n in WNAMES])
```

```python
import functools
import math

import jax
import jax.numpy as jnp
import numpy as np
from jax import lax
from jax.experimental import pallas as pl
from jax.experimental.pallas import tpu as pltpu

F32 = jnp.float32
BF16 = jnp.bfloat16
SDS = jax.ShapeDtypeStruct
MESH = pl.DeviceIdType.MESH
HIGHEST = lax.Precision.HIGHEST

D_MODEL = 1024
DEPTH = 4
D_FF = 2816
NORM_EPS = 1e-6
LN_EPS = 1e-5
GM_HEADS = 8
CHUNK = 128
SSD_HEADS = 16
SSD_HEAD_DIM = 64
SSD_INNER = 1024
SSD_STATE = 128
SSD_CONV = 4
SSD_CONV_CH = 1536
HYB_MAIN = 4608
HYB_IN = 4624
HYB_PAD = 5120
MLA_HEADS = 16
MLA_NOPE = 128
MLA_ROPE = 64
MLA_QK = 192
MLA_QPAD = 256
MLA_Q_LORA = 256
MLA_KV_LORA = 128
ROPE_BASE = 10000.0
ADAM_LR = 0.001
ADAM_B1 = 0.9
ADAM_B2 = 0.999
ADAM_EPS = 1e-08
ADAM_WD = 0.01
ADAM_STEP = 10

N_CHIPS = 4
LANES = 128
VMEM_LIMIT = 56 * 1024 * 1024
PACK_ROWS = 2048

WEIGHTS = [
    ("ffn1_pre_g", (4, 1024), None, False),
    ("ffn1_w_in", (4, 1024, 5632), 2, True),
    ("ffn1_w_down", (4, 2816, 1024), 1, True),
    ("ffn1_post_g", (4, 1024), None, False),
    ("mix_pre_g", (4, 1024), None, False),
    ("mix_post_g", (4, 1024), None, False),
    ("ffn2_pre_g", (4, 1024), None, False),
    ("ffn2_w_in", (4, 1024, 5632), 2, True),
    ("ffn2_w_down", (4, 2816, 1024), 1, True),
    ("ffn2_post_g", (4, 1024), None, False),
    ("ple_pre_g", (4, 1024), None, False),
    ("ple_w_gate", (4, 1024, 1024), 1, True),
    ("ple_w_proj", (4, 256, 1024), 2, True),
    ("ple_post_g", (4, 1024), None, False),
    ("hyb_w_in", (2, 1024, 4624), 2, True),
    ("gm_ln_g", (2, 1024), None, False),
    ("gm_ln_b", (2, 1024), None, False),
    ("gm_w_s", (2, 8, 128, 128), None, False),
    ("gm_b_s", (2, 8, 128), None, False),
    ("ssd_conv_w", (2, 4, 1536), 2, False),
    ("ssd_conv_b", (2, 1536), None, False),
    ("ssd_dt_bias", (2, 16), None, False),
    ("ssd_a_log", (2, 16), None, False),
    ("ssd_d", (2, 16), None, False),
    ("ssd_norm_g", (2, 1024), None, False),
    ("hyb_w_out", (2, 2048, 1024), 1, True),
    ("mla_w_in", (2, 1024, 448), 1, True),
    ("mla_q_norm_g", (2, 256), 1, False),
    ("mla_kv_norm_g", (2, 128), None, False),
    ("mla_w_uq", (2, 256, 3072), 2, True),
    ("mla_w_ukv", (2, 128, 4096), 2, True),
    ("mla_w_out", (2, 2048, 1024), 1, True),
]
WNAMES = [w[0] for w in WEIGHTS]
WSPEC = {w[0]: w for w in WEIGHTS}
REG = ["ffn1_w_in", "ffn1_w_down", "ffn2_w_in", "ffn2_w_down", "ple_w_gate", "ple_w_proj", "hyb_w_out", "mla_w_out"]
WIN = "hyb_w_in"
WIN_SHARD = 4624 // 4
WIN_STRIDE = (WIN_SHARD // 128) * 128
WIN_WIDTH = -(-(WIN_SHARD + 3 * (WIN_SHARD - WIN_STRIDE)) // 128) * 128
MOVED = REG + [WIN]
MISC = [n for n in WNAMES if n not in MOVED]


def _pick(dim, target):
    if dim <= target:
        return dim
    t = (target // LANES) * LANES
    while t >= LANES:
        if dim % t == 0:
            return t
        t -= LANES
    return dim


def mm(a, b, *, ta=False, tb=False, out_dtype=F32, name, tm=1024, tn=1024, tk=1024):
    a, la = a if isinstance(a, tuple) else (a, None)
    b, lb = b if isinstance(b, tuple) else (b, None)

    def dims(x, mode):
        r, c = x.shape[-2:]
        return (r, c * x.shape[0]) if mode == "planes" else (r, c)

    K, M = dims(a, la) if ta else dims(a, la)[::-1]
    N, K2 = dims(b, lb) if tb else dims(b, lb)[::-1]
    assert K == K2, (a.shape, b.shape, ta, tb)
    assert not (la == "planes" and ta) and not (lb == "planes" and tb)
    bm, bn = _pick(M, tm), _pick(b.shape[-1] if lb == "planes" else N, tn)
    bk = _pick(a.shape[-1] if la == "planes" else K, tk)
    nk = K // bk

    def spec(shape, idx, layer, x):
        if layer is None:
            return pl.BlockSpec(shape, idx)
        if layer == "planes":
            per = x.shape[-1] // shape[1]
            return pl.BlockSpec((None,) + shape, lambda i, j, k: (idx(i, j, k)[1] // per, idx(i, j, k)[0],
                                                                  idx(i, j, k)[1] % per))
        return pl.BlockSpec((None,) + shape, lambda i, j, k: (layer,) + idx(i, j, k))

    a_spec = spec((bk, bm), lambda i, j, k: (k, i), la, a) if ta else spec((bm, bk), lambda i, j, k: (i, k), la, a)
    b_spec = spec((bn, bk), lambda i, j, k: (j, k), lb, b) if tb else spec((bk, bn), lambda i, j, k: (k, j), lb, b)
    dn = (((0 if ta else 1,), (1 if tb else 0,)), ((), ()))

    def body(a_ref, b_ref, o_ref, acc_ref):
        k = pl.program_id(2)

        @pl.when(k == 0)
        def _():
            acc_ref[...] = jnp.zeros_like(acc_ref)

        acc_ref[...] += lax.dot_general(a_ref[...].astype(BF16), b_ref[...].astype(BF16), dn,
                                        preferred_element_type=F32)

        @pl.when(k == nk - 1)
        def _():
            o_ref[...] = acc_ref[...].astype(o_ref.dtype)

    return pl.pallas_call(
        body, name=name, grid=(M // bm, N // bn, nk),
        in_specs=[a_spec, b_spec], out_specs=pl.BlockSpec((bm, bn), lambda i, j, k: (i, j)),
        out_shape=SDS((M, N), out_dtype), scratch_shapes=[pltpu.VMEM((bm, bn), F32)],
        compiler_params=pltpu.CompilerParams(dimension_semantics=("parallel", "parallel", "arbitrary"),
                                             vmem_limit_bytes=VMEM_LIMIT),
    )(a, b)


def row_call(fn, xs, ps, outs, accs=(), *, tb, name, reverse=False):
    xs = [x if isinstance(x, tuple) else (x, x.shape[1], 0) for x in xs]
    T = xs[0][0].shape[0]
    tb = min(tb, T)
    n = T // tb
    assert n * tb == T
    nx, npar, no, na = len(xs), len(ps), len(outs), len(accs)

    def ridx(i):
        return n - 1 - i if reverse else i

    in_specs = [pl.BlockSpec((tb, w), functools.partial(lambda i, cb: (ridx(i), cb), cb=cb)) for (_, w, cb) in xs]
    in_specs += [pl.BlockSpec(p.shape, functools.partial(lambda i, nd: (0,) * nd, nd=p.ndim)) for p in ps]
    out_specs = [pl.BlockSpec((tb, c), lambda i: (ridx(i), 0)) for (c, _) in outs]
    out_specs += [pl.BlockSpec(s, functools.partial(lambda i, nd: (0,) * nd, nd=len(s))) for s in accs]
    out_shape = [SDS((T, c), dt) for (c, dt) in outs] + [SDS(s, F32) for s in accs]

    def body(*refs):
        xr, pr = refs[:nx], refs[nx:nx + npar]
        orf, ar = refs[nx + npar:nx + npar + no], refs[nx + npar + no:]
        res = fn(*[r[...] for r in xr], *[r[...] for r in pr])
        for r, v in zip(orf, res[:no]):
            r[...] = v.astype(r.dtype)
        if na:
            @pl.when(pl.program_id(0) == 0)
            def _():
                for r in ar:
                    r[...] = jnp.zeros_like(r)

            for r, v in zip(ar, res[no:]):
                r[...] += v.astype(F32)

    res = pl.pallas_call(
        body, name=name, grid=(n,), in_specs=in_specs, out_specs=out_specs, out_shape=out_shape,
        compiler_params=pltpu.CompilerParams(dimension_semantics=("arbitrary",), vmem_limit_bytes=VMEM_LIMIT),
    )(*[x[0] for x in xs], *ps)
    return res


def _f32(*a):
    return [v.astype(F32) for v in a]


def t_rms(x, g):
    return x * lax.rsqrt(jnp.mean(x * x, axis=-1, keepdims=True) + NORM_EPS) * g


def t_swiglu(gate, up):
    return jax.nn.silu(gate) * up


def t_ple(gl, pp, g):
    return t_rms(jax.nn.sigmoid(gl) * pp, g)


def _iota(shape, d):
    return lax.broadcasted_iota(jnp.int32, shape, d)


def _bdot(a, b, dn=(((1,), (0,)), ((), ()))):
    return lax.dot_general(a.astype(BF16), b.astype(BF16), dn, preferred_element_type=F32)


def _hdot(a, b):
    return jnp.dot(a, b, precision=HIGHEST, preferred_element_type=F32)


NT = (((1,), (1,)), ((), ()))
TN = (((0,), (0,)), ((), ()))


def t_gmlp(uv, ln_g, ln_b, w_s, b_st):
    tb = uv.shape[0]
    guv = jax.nn.gelu(uv)
    u, v = guv[:, :1024], guv[:, 1024:]
    tri = _iota((CHUNK, CHUNK), 1) <= _iota((CHUNK, CHUNK), 0)
    rows = []
    for c in range(tb // CHUNK):
        vc = v[c * CHUNK:(c + 1) * CHUNK]
        heads = []
        for h in range(GM_HEADS):
            sl = slice(h * 128, (h + 1) * 128)
            vh = vc[:, sl]
            xc = vh - jnp.mean(vh, axis=-1, keepdims=True)
            var = jnp.mean(xc * xc, axis=-1, keepdims=True)
            y = xc * lax.rsqrt(var + LN_EPS) * ln_g[:, sl] + ln_b[:, sl]
            wm = jnp.where(tri, w_s[sl, :], 0.0)
            heads.append(_bdot(wm, y) + b_st[:, h:h + 1])
        rows.append(jnp.concatenate(heads, axis=1))
    mixed = rows[0] if len(rows) == 1 else jnp.concatenate(rows, axis=0)
    return u * mixed


def t_ssd(pre, dtr, z, st, dt_bias, a_log, d_exp, norm_g):
    L = CHUNK
    xbc = jax.nn.silu(pre)
    xs, bm, cm = xbc[:, :1024], xbc[:, 1024:1280], xbc[:, 1280:1536]
    valid = _iota((1, LANES), 1) < SSD_HEADS
    dt16 = jnp.where(valid, jax.nn.softplus(dtr + dt_bias), 0.0)
    a16 = jnp.where(valid, -jnp.exp(a_log), 0.0)
    da16 = dt16 * a16
    tri = _iota((L, L), 1) <= _iota((L, L), 0)
    acs16 = _hdot(tri.astype(F32), da16)
    hh, cc = _iota((LANES, 1024), 0), _iota((LANES, 1024), 1)
    expand = ((cc >= hh * SSD_HEAD_DIM) & (cc < (hh + 1) * SSD_HEAD_DIM)).astype(F32)
    acs = _hdot(acs16, expand)
    dte = _hdot(dt16, expand)
    alast = jnp.sum(jnp.where(_iota((L, 1024), 0) == L - 1, acs, 0.0), axis=0, keepdims=True)
    xd = xs * dte
    groups = [slice(0, 512), slice(512, 1024)]
    bg = [bm[:, :128], bm[:, 128:]]
    cg = [cm[:, :128], cm[:, 128:]]
    yoff = jnp.concatenate([_bdot(cg[g], st[:, groups[g]]) for g in range(2)], axis=1) * jnp.exp(acs)
    xdw = xd * jnp.exp(alast - acs)
    s_t = jnp.concatenate([_bdot(bg[g], xdw[:, groups[g]], TN) for g in range(2)], axis=1)
    st_new = st * jnp.exp(alast) + s_t
    cb = [_bdot(cg[g], bg[g], NT) for g in range(2)]
    acs16_t = acs16.T
    lo = _iota((1, LANES), 1) < SSD_HEAD_DIM
    slabs = []
    for j in range(SSD_HEADS // 2):
        g = j // 4
        xslab = xd[:, j * 128:(j + 1) * 128]
        acc = None
        for half in range(2):
            h = 2 * j + half
            seg = acs16[:, h:h + 1] - acs16_t[h:h + 1, :]
            mmat = cb[g] * jnp.exp(jnp.where(tri, seg, -1e30))
            xm = jnp.where(lo if half == 0 else jnp.logical_not(lo), xslab, 0.0)
            term = _bdot(mmat, xm)
            acc = term if acc is None else acc + term
        slabs.append(acc)
    y = jnp.concatenate(slabs, axis=1) + yoff + d_exp * xs
    yg = y * jax.nn.silu(z)
    outs = []
    for g in range(2):
        t = yg[:, groups[g]]
        outs.append(t * lax.rsqrt(jnp.mean(t * t, axis=-1, keepdims=True) + NORM_EPS) * norm_g[:, groups[g]])
    return jnp.concatenate(outs, axis=1), st_new


def t_kprep(c_all, cs, sn, qg, kvg):
    cqn = t_rms(c_all[:, :256], qg)
    ckvn = t_rms(c_all[:, 256:384], kvg)
    kr = c_all[:, 384:512] * cs + c_all[:, 512:640] * sn
    return cqn, ckvn, kr


def t_qrope(qb, c256, s256):
    scale = MLA_QK ** -0.5
    half = MLA_HEADS * MLA_QPAD
    outs = []
    for h in range(MLA_HEADS):
        a = qb[:, h * MLA_QPAD:(h + 1) * MLA_QPAD]
        b = qb[:, half + h * MLA_QPAD:half + (h + 1) * MLA_QPAD]
        outs.append((a * c256 + b * s256) * scale)
    return jnp.concatenate(outs, axis=1)


def rms_fwd(h, g, *, name, tb=512):
    def fn(h, g):
        return (t_rms(h.astype(F32), g),)
    return row_call(fn, [h], [g], [(h.shape[1], BF16)], tb=tb, name=name)[0]


def rms_bwd(h, dhn, dres, g, *, name, tb=256):
    def fn(h, dhn, dres, g):
        h, dhn, dres = _f32(h, dhn, dres)
        _, vjp = jax.vjp(t_rms, h, g)
        dh, dg = vjp(dhn)
        return dres + dh, dg
    return row_call(fn, [h, dhn, dres], [g], [(h.shape[1], F32)], [g.shape], tb=tb, name=name)


def post_fwd(h, f, g, scale, *, name, tb=512):
    def fn(h, f, g):
        return (h + scale * t_rms(f.astype(F32), g),)
    return row_call(fn, [h, f], [g], [(h.shape[1], F32)], tb=tb, name=name)[0]


def post_bwd(f, dout, g, scale, *, name, tb=256):
    def fn(f, dout, g):
        f, dout = _f32(f, dout)
        _, vjp = jax.vjp(lambda f, g: scale * t_rms(f, g), f, g)
        return vjp(dout)
    return row_call(fn, [f, dout], [g], [(f.shape[1], BF16)], [g.shape], tb=tb, name=name)


FFN_TILE = D_FF // 2


def _stacked(w):
    return w if isinstance(w, tuple) else (w[None], 0)


def ffn_in_act(hn, w_in, *, name, tm=512):
    w, layer = _stacked(w_in)
    T, K = hn.shape
    bm, bn = _pick(T, tm), FFN_TILE
    nj = D_FF // bn

    def body(x_ref, wg_ref, wu_ref, gu_ref, act_ref):
        x = x_ref[...].astype(BF16)
        g = jnp.dot(x, wg_ref[...].astype(BF16), preferred_element_type=F32)
        u = jnp.dot(x, wu_ref[...].astype(BF16), preferred_element_type=F32)
        gu_ref[0] = g.astype(gu_ref.dtype)
        gu_ref[1] = u.astype(gu_ref.dtype)
        act_ref[...] = t_swiglu(g, u).astype(act_ref.dtype)

    return pl.pallas_call(
        body, name=name, grid=(T // bm, nj),
        in_specs=[pl.BlockSpec((bm, K), lambda i, j: (i, 0)),
                  pl.BlockSpec((None, K, bn), lambda i, j: (layer, 0, j)),
                  pl.BlockSpec((None, K, bn), lambda i, j: (layer, 0, nj + j))],
        out_specs=[pl.BlockSpec((2, bm, bn), lambda i, j: (0, i, j)), pl.BlockSpec((bm, bn), lambda i, j: (i, j))],
        out_shape=[SDS((2, T, D_FF), BF16), SDS((T, D_FF), BF16)],
        compiler_params=pltpu.CompilerParams(dimension_semantics=("parallel", "parallel"),
                                             vmem_limit_bytes=VMEM_LIMIT),
    )(hn, w, w)


def ffn_down_bx_act(df, w_down, gu, *, name, tm=512):
    w, layer = _stacked(w_down)
    T, K = df.shape
    bm, bn = _pick(T, tm), FFN_TILE

    def body(df_ref, wd_ref, gu_ref, dgu_ref):
        da = lax.dot_general(df_ref[...].astype(BF16), wd_ref[...].astype(BF16), NT, preferred_element_type=F32)
        _, vjp = jax.vjp(t_swiglu, gu_ref[0].astype(F32), gu_ref[1].astype(F32))
        dg, du = vjp(da)
        dgu_ref[0] = dg.astype(dgu_ref.dtype)
        dgu_ref[1] = du.astype(dgu_ref.dtype)

    return pl.pallas_call(
        body, name=name, grid=(T // bm, D_FF // bn),
        in_specs=[pl.BlockSpec((bm, K), lambda i, j: (i, 0)),
                  pl.BlockSpec((None, bn, K), lambda i, j: (layer, j, 0)),
                  pl.BlockSpec((2, bm, bn), lambda i, j: (0, i, j))],
        out_specs=pl.BlockSpec((2, bm, bn), lambda i, j: (0, i, j)), out_shape=SDS((2, T, D_FF), BF16),
        compiler_params=pltpu.CompilerParams(dimension_semantics=("parallel", "parallel"),
                                             vmem_limit_bytes=VMEM_LIMIT),
    )(df, w, gu)


def ple_fwd(h, gl, pp, g, *, name, tb=512):
    def fn(h, gl, pp, g):
        return (h + t_ple(gl, pp, g),)
    return row_call(fn, [h, gl, pp], [g], [(D_MODEL, F32)], tb=tb, name=name)[0]


def ple_bwd(gl, pp, dout, g, *, name, tb=256):
    def fn(gl, pp, dout, g):
        _, vjp = jax.vjp(t_ple, gl, pp, g)
        return vjp(dout)
    return row_call(fn, [gl, pp, dout], [g], [(D_MODEL, BF16), (D_MODEL, BF16)], [g.shape], tb=tb, name=name)


def gmlp_fwd(proj, ln_g, ln_b, w_s, b_st, *, name, tb=256):
    def fn(uv, ln_g, ln_b, w_s, b_st):
        return (t_gmlp(uv, ln_g, ln_b, w_s, b_st),)
    return row_call(fn, [(proj, 2048, 0)], [ln_g, ln_b, w_s, b_st], [(1024, BF16)], tb=tb, name=name)[0]


def gmlp_bwd(proj, dya, ln_g, ln_b, w_s, b_st, *, name, tb=128):
    def fn(uv, dya, ln_g, ln_b, w_s, b_st):
        _, vjp = jax.vjp(t_gmlp, uv, ln_g, ln_b, w_s, b_st)
        return vjp(dya.astype(F32))
    return row_call(fn, [(proj, 2048, 0), (dya, 1024, 0)], [ln_g, ln_b, w_s, b_st], [(2048, BF16)],
                    [ln_g.shape, ln_b.shape, w_s.shape, b_st.shape], tb=tb, name=name)


def kprep_fwd(c_all, cs, sn, qg, kvg, *, name, tb=512):
    return row_call(t_kprep, [c_all, cs, sn], [qg, kvg], [(256, BF16), (128, BF16), (128, BF16)], tb=tb, name=name)


def kprep_bwd(c_all, cs, sn, dcqn, dckvn, dkr, qg, kvg, *, name, tb=256):
    def fn(c_all, cs, sn, dcqn, dckvn, dkr, qg, kvg):
        dcqn, dckvn, dkr = _f32(dcqn, dckvn, dkr)
        _, vjp = jax.vjp(lambda c, qg, kvg: t_kprep(c, cs, sn, qg, kvg), c_all, qg, kvg)
        return vjp((dcqn, dckvn, dkr))
    return row_call(fn, [c_all, cs, sn, dcqn, dckvn, dkr], [qg, kvg], [(640, BF16)], [qg.shape, kvg.shape],
                    tb=tb, name=name)


def qrope_fwd(qb, c256, s256, *, name, tb=256):
    def fn(qb, c256, s256):
        return (t_qrope(qb, c256, s256),)
    return row_call(fn, [qb, c256, s256], [], [(MLA_HEADS * MLA_QPAD, BF16)], tb=tb, name=name)[0]


def qrope_bwd(dq, c256, s256, *, name, tb=256):
    def fn(dq, c256, s256):
        scale = MLA_QK ** -0.5
        a, b = [], []
        for h in range(MLA_HEADS):
            d = dq[:, h * MLA_QPAD:(h + 1) * MLA_QPAD] * scale
            a.append(d * c256)
            b.append(d * s256)
        return (jnp.concatenate(a + b, axis=1),)
    return row_call(fn, [dq, c256, s256], [], [(2 * MLA_HEADS * MLA_QPAD, BF16)], tb=tb, name=name)[0]


STAT_SPLIT = 64
ATTN_UNROLL = 2
ATTN_HEADS_PER_STEP = 2


def stats_fwd(do, o, lse, *, name, tb=512):
    def fn(do, o, lse):
        do, o = _f32(do, o)
        low = _iota((1, 128), 1) < STAT_SPLIT
        outs = []
        for h in range(MLA_HEADS):
            sl = slice(h * 128, (h + 1) * 128)
            dl = jnp.sum(do[:, sl] * o[:, sl], axis=-1, keepdims=True)
            outs.append(jnp.where(low, lse[:, sl], dl))
        return (jnp.concatenate(outs, axis=1),)
    return row_call(fn, [do, o, lse], [], [(2048, F32)], tb=tb, name=name)[0]


def headsum(dkr_h, *, name, tb=512):
    def fn(d):
        acc = d[:, :128]
        for h in range(1, MLA_HEADS):
            acc = acc + d[:, h * 128:(h + 1) * 128]
        return (acc,)
    return row_call(fn, [dkr_h], [], [(128, F32)], tb=tb, name=name)[0]


def loss_fwd(y, t, *, name, tb=512):
    def fn(y, t):
        e = y - t
        return e * (1.0 / D_MODEL), jnp.sum(e * e, axis=0, keepdims=True)
    return row_call(fn, [y, t], [], [(D_MODEL, F32)], [(1, D_MODEL)], tb=tb, name=name)


def conv_fwd(proj, w, b, *, name, tb=256):
    T = proj.shape[0]
    n = T // tb
    hb = tb // CHUNK
    C = SSD_CONV_CH

    def body(cur, prev, w_ref, b_ref, o_ref, scr):
        i = pl.program_id(0)
        scr[pl.ds(0, CHUNK), :] = jnp.where(i > 0, prev[...], 0.0)
        scr[pl.ds(CHUNK, tb), :] = cur[...]
        y = b_ref[...] + w_ref[3:4, :] * cur[...]
        for k in range(SSD_CONV - 1):
            y = y + w_ref[k:k + 1, :] * scr[pl.ds(CHUNK - (SSD_CONV - 1) + k, tb), :]
        o_ref[...] = y

    return pl.pallas_call(
        body, name=name, grid=(n,),
        in_specs=[pl.BlockSpec((tb, C), lambda i: (i, 2)),
                  pl.BlockSpec((CHUNK, C), lambda i: (jnp.maximum(i * hb - 1, 0), 2)),
                  pl.BlockSpec((SSD_CONV, C), lambda i: (0, 0)), pl.BlockSpec((1, C), lambda i: (0, 0))],
        out_specs=pl.BlockSpec((tb, C), lambda i: (i, 0)), out_shape=SDS((T, C), F32),
        scratch_shapes=[pltpu.VMEM((CHUNK + tb, C), F32)],
        compiler_params=pltpu.CompilerParams(dimension_semantics=("arbitrary",), vmem_limit_bytes=VMEM_LIMIT),
    )(proj, proj, w, b)


def conv_bwd(dpre, proj, w, *, name, tb=256):
    T = proj.shape[0]
    n = T // tb
    hb = tb // CHUNK
    nh = T // CHUNK
    C = SSD_CONV_CH

    def body(dcur, dnext, xcur, xprev, w_ref, dx_ref, dw_ref, db_ref, dscr, xscr):
        i = pl.program_id(0)

        @pl.when(i == 0)
        def _():
            dw_ref[...] = jnp.zeros_like(dw_ref)
            db_ref[...] = jnp.zeros_like(db_ref)

        d = dcur[...]
        dscr[pl.ds(0, tb), :] = d
        dscr[pl.ds(tb, CHUNK), :] = jnp.where(i < n - 1, dnext[...], 0.0)
        xscr[pl.ds(0, CHUNK), :] = jnp.where(i > 0, xprev[...], 0.0)
        xscr[pl.ds(CHUNK, tb), :] = xcur[...]
        dx = w_ref[3:4, :] * d
        for k in range(SSD_CONV - 1):
            dx = dx + w_ref[k:k + 1, :] * dscr[pl.ds(SSD_CONV - 1 - k, tb), :]
        dx_ref[...] = dx.astype(dx_ref.dtype)
        for k in range(SSD_CONV):
            xk = xscr[pl.ds(CHUNK - (SSD_CONV - 1) + k, tb), :]
            dw_ref[k:k + 1, :] += jnp.sum(d * xk, axis=0, keepdims=True)
        db_ref[...] += jnp.sum(d, axis=0, keepdims=True)

    return pl.pallas_call(
        body, name=name, grid=(n,),
        in_specs=[pl.BlockSpec((tb, C), lambda i: (i, 0)),
                  pl.BlockSpec((CHUNK, C), lambda i: (jnp.minimum((i + 1) * hb, nh - 1), 0)),
                  pl.BlockSpec((tb, C), lambda i: (i, 2)),
                  pl.BlockSpec((CHUNK, C), lambda i: (jnp.maximum(i * hb - 1, 0), 2)),
                  pl.BlockSpec((SSD_CONV, C), lambda i: (0, 0))],
        out_specs=[pl.BlockSpec((tb, C), lambda i: (i, 0)), pl.BlockSpec((SSD_CONV, C), lambda i: (0, 0)),
                   pl.BlockSpec((1, C), lambda i: (0, 0))],
        out_shape=[SDS((T, C), BF16), SDS((SSD_CONV, C), F32), SDS((1, C), F32)],
        scratch_shapes=[pltpu.VMEM((tb + CHUNK, C), F32), pltpu.VMEM((CHUNK + tb, C), F32)],
        compiler_params=pltpu.CompilerParams(dimension_semantics=("arbitrary",), vmem_limit_bytes=VMEM_LIMIT),
    )(dpre, dpre, proj, proj, w)


def _ssd_specs(nc, rev):
    def r(c):
        return nc - 1 - c if rev else c
    pre = pl.BlockSpec((CHUNK, SSD_CONV_CH), lambda c: (r(c), 0))
    dtr = pl.BlockSpec((CHUNK, LANES), lambda c: (r(c), HYB_MAIN // LANES))
    z = pl.BlockSpec((CHUNK, 1024), lambda c: (r(c), 2))
    row = pl.BlockSpec((CHUNK, 1024), lambda c: (r(c), 0))
    return pre, dtr, z, row


def _pspec(shape):
    return pl.BlockSpec(shape, lambda c: (0,) * len(shape))


def ssd_fwd(pre, proj, dt_bias, a_log, d_exp, norm_g, *, name):
    T = pre.shape[0]
    nc = T // CHUNK
    s_pre, s_dt, s_z, s_row = _ssd_specs(nc, False)

    def body(pre_r, dt_r, z_r, b_r, a_r, d_r, g_r, y_r, sv_r, st):
        @pl.when(pl.program_id(0) == 0)
        def _():
            st[...] = jnp.zeros_like(st)

        s0 = st[...]
        sv_r[...] = s0
        y, s1 = t_ssd(pre_r[...], dt_r[...], z_r[...], s0, b_r[...], a_r[...], d_r[...], g_r[...])
        y_r[...] = y.astype(y_r.dtype)
        st[...] = s1

    return pl.pallas_call(
        body, name=name, grid=(nc,),
        in_specs=[s_pre, s_dt, s_z, _pspec((1, LANES)), _pspec((1, LANES)), _pspec((1, 1024)), _pspec((1, 1024))],
        out_specs=[s_row, s_row], out_shape=[SDS((T, 1024), BF16), SDS((T, 1024), F32)],
        scratch_shapes=[pltpu.VMEM((SSD_STATE, 1024), F32)],
        compiler_params=pltpu.CompilerParams(dimension_semantics=("arbitrary",), vmem_limit_bytes=VMEM_LIMIT),
    )(pre, proj, proj, dt_bias, a_log, d_exp, norm_g)


def ssd_bwd(pre, proj, states, dyab, dt_bias, a_log, d_exp, norm_g, *, name):
    T = pre.shape[0]
    nc = T // CHUNK
    s_pre, s_dt, s_z, s_row = _ssd_specs(nc, True)
    s_dtout = pl.BlockSpec((CHUNK, LANES), lambda c: (nc - 1 - c, 0))
    s_dy = pl.BlockSpec((CHUNK, 1024), lambda c: (nc - 1 - c, 1))

    def body(pre_r, dt_r, z_r, sv_r, dy_r, b_r, a_r, d_r, g_r, dpre_r, ddt_r, dz_r, db_r, da_r, dd_r, dg_r, dst):
        @pl.when(pl.program_id(0) == 0)
        def _():
            dst[...] = jnp.zeros_like(dst)
            for r in (db_r, da_r, dd_r, dg_r):
                r[...] = jnp.zeros_like(r)

        _, vjp = jax.vjp(t_ssd, pre_r[...], dt_r[...], z_r[...], sv_r[...], b_r[...], a_r[...], d_r[...], g_r[...])
        dpre, ddt, dz, ds0, db, da, dd, dg = vjp((dy_r[...].astype(F32), dst[...]))
        dpre_r[...] = dpre
        ddt_r[...] = ddt.astype(ddt_r.dtype)
        dz_r[...] = dz.astype(dz_r.dtype)
        dst[...] = ds0
        db_r[...] += db
        da_r[...] += da
        dd_r[...] += dd
        dg_r[...] += dg

    return pl.pallas_call(
        body, name=name, grid=(nc,),
        in_specs=[s_pre, s_dt, s_z, s_row, s_dy, _pspec((1, LANES)), _pspec((1, LANES)), _pspec((1, 1024)),
                  _pspec((1, 1024))],
        out_specs=[s_pre, s_dtout, s_row, _pspec((1, LANES)), _pspec((1, LANES)), _pspec((1, 1024)), _pspec((1, 1024))],
        out_shape=[SDS((T, SSD_CONV_CH), F32), SDS((T, LANES), BF16), SDS((T, 1024), BF16),
                   SDS((1, LANES), F32), SDS((1, LANES), F32), SDS((1, 1024), F32), SDS((1, 1024), F32)],
        scratch_shapes=[pltpu.VMEM((SSD_STATE, 1024), F32)],
        compiler_params=pltpu.CompilerParams(dimension_semantics=("arbitrary",), vmem_limit_bytes=VMEM_LIMIT),
    )(pre, proj, proj, states, dyab, dt_bias, a_log, d_exp, norm_g)


def _attn_tile(T, target=512):
    return min(target, T // 2)


def _causal(tq):
    return _iota((tq, tq), 1) <= _iota((tq, tq), 0)


def _job_parts(job):
    if job is None:
        return 0, 0, [], [], []
    ni, no = len(job["inputs"]), len(job["out_shape"])
    return ni, no, list(job["inputs"]), list(job["out_shape"]), list(job["scratch"])


def _job_phase(job, which, refs, when):
    if job is not None and job["phases"][which] is not None:
        pl.when(when)(functools.partial(job["phases"][which], *refs))


def attn_fwd(q, kv, kr, *, name, job=None):
    T = q.shape[0]
    tq = _attn_tile(T)
    nq = T // tq
    hp = ATTN_HEADS_PER_STEP
    ng = MLA_HEADS // hp
    ni, no, jins, jouts, jscratch = _job_parts(job)

    def body(*refs):
        q_ref, kn_ref, v_ref, kr_ref = refs[:4]
        o_ref, lse_ref = refs[4 + ni:6 + ni]
        jrefs = (refs[4:4 + ni], refs[6 + ni:6 + ni + no], refs[6 + ni + no:])
        g = pl.program_id(0)
        qi = pl.program_id(1)
        _job_phase(job, 0, jrefs, (g == 0) & (qi == 0))
        qv = [q_ref[:, e * MLA_QPAD:(e + 1) * MLA_QPAD] for e in range(hp)]

        def blk(ki, masked, carry):
            rows = pl.ds(pl.multiple_of(ki * tq, tq), tq)
            kr = kr_ref[rows, :]
            out = []
            for e in range(hp):
                m, l, acc = carry[e]
                cols = slice(e * 128, (e + 1) * 128)
                k = jnp.concatenate([kn_ref[rows, cols], kr], axis=1)
                s = lax.dot_general(qv[e], k, NT, preferred_element_type=F32)
                if masked:
                    s = jnp.where(_causal(tq), s, -1e30)
                m_new = jnp.maximum(m, jnp.max(s, axis=-1, keepdims=True))
                p = jnp.exp(s - m_new)
                alpha = jnp.exp(m - m_new)
                l = alpha * l + jnp.sum(p, axis=-1, keepdims=True)
                acc = alpha * acc + jnp.dot(p.astype(BF16), v_ref[rows, cols], preferred_element_type=F32)
                out.append((m_new, l, acc))
            return tuple(out)

        one = (jnp.full((tq, 1), -1e30, F32), jnp.zeros((tq, 1), F32), jnp.zeros((tq, 128), F32))
        carry = lax.fori_loop(0, qi, lambda ki, c: blk(ki, False, c), (one,) * hp)
        carry = blk(qi, True, carry)
        for e in range(hp):
            m, l, acc = carry[e]
            cols = slice(e * 128, (e + 1) * 128)
            o_ref[:, cols] = (acc / l).astype(o_ref.dtype)
            lse_ref[:, cols] = jnp.broadcast_to(m + jnp.log(l), (tq, 128))
        _job_phase(job, 2, jrefs, (g == ng - 1) & (qi == nq - 1))

    res = pl.pallas_call(
        body, name=name, grid=(ng, nq),
        in_specs=[pl.BlockSpec((tq, hp * MLA_QPAD), lambda g, i: (i, g)),
                  pl.BlockSpec((T, hp * 128), lambda g, i: (0, g)),
                  pl.BlockSpec((T, hp * 128), lambda g, i: (0, ng + g)),
                  pl.BlockSpec((T, 128), lambda g, i: (0, 0))] + [ANY] * ni,
        out_specs=[pl.BlockSpec((tq, hp * 128), lambda g, i: (i, g)), pl.BlockSpec((tq, hp * 128), lambda g, i: (i, g))]
        + [ANY] * no,
        out_shape=[SDS((T, 2048), BF16), SDS((T, 2048), F32)] + jouts, scratch_shapes=jscratch,
        compiler_params=pltpu.CompilerParams(dimension_semantics=("arbitrary", "arbitrary"),
                                             vmem_limit_bytes=VMEM_LIMIT),
    )(q, kv, kv, kr, *jins)
    return res[0], res[1], list(res[2:])


def attn_bwd(q, kv, kr, do, stats, *, name, job=None):
    T = q.shape[0]
    tq = _attn_tile(T)
    nq = T // tq
    ni, no, jins, jouts, jscratch = _job_parts(job)

    def body(*refs):
        q_ref, do_ref, st_ref, kn_ref, v_ref, kr_ref = refs[:6]
        dq_ref, dkn_ref, dv_ref, dkr_ref = refs[6 + ni:10 + ni]
        jrefs = (refs[6:6 + ni], refs[10 + ni:10 + ni + no], refs[10 + ni + no:])
        h = pl.program_id(0)
        ki = pl.program_id(1)
        _job_phase(job, 0, jrefs, (h == 0) & (ki == 0))
        _job_phase(job, 1, jrefs, (h == MLA_HEADS // 2) & (ki == 0))

        @pl.when(ki == 0)
        def _():
            dq_ref[...] = jnp.zeros_like(dq_ref)

        k = jnp.concatenate([kn_ref[...], kr_ref[...]], axis=1)
        v = v_ref[...]

        def blk(qi, masked, carry):
            dk, dv = carry
            rows = pl.ds(pl.multiple_of(qi * tq, tq), tq)
            qv, dov = q_ref[rows, :], do_ref[rows, :]
            lse, dl = st_ref[rows, 0:1], st_ref[rows, STAT_SPLIT:STAT_SPLIT + 1]
            s = lax.dot_general(qv, k, NT, preferred_element_type=F32)
            if masked:
                s = jnp.where(_causal(tq), s, -1e30)
            p = jnp.exp(s - lse)
            dv = dv + lax.dot_general(p.astype(BF16), dov, TN, preferred_element_type=F32)
            dp = lax.dot_general(dov, v, NT, preferred_element_type=F32)
            ds = (p * (dp - dl)).astype(BF16)
            dk = dk + lax.dot_general(ds, qv, TN, preferred_element_type=F32)
            dq_ref[rows, :] += jnp.dot(ds, k, preferred_element_type=F32)
            return dk, dv

        carry = blk(ki, True, (jnp.zeros((tq, MLA_QPAD), F32), jnp.zeros((tq, 128), F32)))
        rest = nq - 1 - ki

        def group(j, c):
            for u in range(ATTN_UNROLL):
                c = blk(ki + 1 + ATTN_UNROLL * j + u, False, c)
            return c

        carry = lax.fori_loop(0, rest // ATTN_UNROLL, group, carry)
        for u in range(ATTN_UNROLL - 1):
            carry = lax.cond(rest % ATTN_UNROLL > u, functools.partial(lambda c, u: blk(nq - 1 - u, False, c), u=u),
                             lambda c: c, carry)
        dk, dv = carry
        dkn_ref[...] = dk[:, :128].astype(dkn_ref.dtype)
        dkr_ref[...] = dk[:, 128:]
        dv_ref[...] = dv.astype(dv_ref.dtype)
        _job_phase(job, 2, jrefs, (h == MLA_HEADS - 1) & (ki == nq - 1))

    res = pl.pallas_call(
        body, name=name, grid=(MLA_HEADS, nq),
        in_specs=[pl.BlockSpec((T, MLA_QPAD), lambda h, i: (0, h)),
                  pl.BlockSpec((T, 128), lambda h, i: (0, h)),
                  pl.BlockSpec((T, 128), lambda h, i: (0, h)),
                  pl.BlockSpec((tq, 128), lambda h, i: (i, h)),
                  pl.BlockSpec((tq, 128), lambda h, i: (i, MLA_HEADS + h)),
                  pl.BlockSpec((tq, 128), lambda h, i: (i, 0))] + [ANY] * ni,
        out_specs=[pl.BlockSpec((T, MLA_QPAD), lambda h, i: (0, h)),
                   pl.BlockSpec((tq, 128), lambda h, i: (i, h)), pl.BlockSpec((tq, 128), lambda h, i: (i, h)),
                   pl.BlockSpec((tq, 128), lambda h, i: (i, h))] + [ANY] * no,
        out_shape=[SDS((T, MLA_HEADS * MLA_QPAD), F32), SDS((T, 2048), BF16), SDS((T, 2048), BF16),
                   SDS((T, 2048), F32)] + jouts, scratch_shapes=jscratch,
        compiler_params=pltpu.CompilerParams(dimension_semantics=("arbitrary", "arbitrary"),
                                             vmem_limit_bytes=VMEM_LIMIT),
    )(q, do, stats, kv, kv, kr, *jins)
    return res[0], res[1], res[2], res[3], list(res[4:])


def _ffn_fwd(h, pre_g, w_in, w_down, post_g, tag):
    hn = rms_fwd(h, pre_g, name=f"{tag}_pre")
    gu, a = ffn_in_act(hn, w_in, name=f"{tag}_in")
    f = mm(a, w_down, name=f"{tag}_down", tk=1408)
    h2 = post_fwd(h, f, post_g, 0.5, name=f"{tag}_post")
    return h2, (h, hn, gu, a, f)


def _ffn_bwd(dh2, saved, pre_g, w_in, w_down, post_g, tag):
    h, hn, gu, a, f = saved
    df, dpost = post_bwd(f, dh2, post_g, 0.5, name=f"{tag}_post_b")
    dgu = ffn_down_bx_act(df, w_down, gu, name=f"{tag}_down_bx")
    dw_down = mm(a, df, ta=True, out_dtype=BF16, name=f"{tag}_down_bw", tm=1408)
    dhn = mm((dgu, "planes"), w_in, tb=True, name=f"{tag}_in_bx", tk=1408)
    dw_in = mm(hn, (dgu, "planes"), ta=True, out_dtype=BF16, name=f"{tag}_in_bw", tn=1408)
    dh, dpre = rms_bwd(h, dhn, dh2, pre_g, name=f"{tag}_pre_b")
    return dh, dpre, dw_in, dw_down, dpost


def _hyb_fwd(hn, w, tag):
    proj = mm(hn, w["hyb_in"], name=f"{tag}_in")
    ya = gmlp_fwd(proj, w["ln_g"], w["ln_b"], w["w_s"], w["b_st"], name=f"{tag}_gmlp")
    pre = conv_fwd(proj, w["conv_w"], w["conv_b"], name=f"{tag}_conv")
    yb, states = ssd_fwd(pre, proj, w["dt_bias"], w["a_log"], w["d_exp"], w["norm_g"], name=f"{tag}_ssd")
    yab = jnp.concatenate([ya, yb], axis=1)
    mixed = mm(yab, w["hyb_out"], name=f"{tag}_out")
    return mixed, (proj, pre, states, yab)


def _hyb_bwd(dmixed, hn, saved, w, tag):
    proj, pre, states, yab = saved
    dyab = mm(dmixed, w["hyb_out"], tb=True, name=f"{tag}_out_bx")
    dw_out = mm(yab, dmixed, ta=True, out_dtype=BF16, name=f"{tag}_out_bw")
    duv, dln_g, dln_b, dw_s, db_st = gmlp_bwd(proj, dyab, w["ln_g"], w["ln_b"], w["w_s"], w["b_st"],
                                              name=f"{tag}_gmlp_b")
    dpre, ddt, dz, ddt_bias, da_log, dd_exp, dnorm_g = ssd_bwd(
        pre, proj, states, dyab, w["dt_bias"], w["a_log"], w["d_exp"], w["norm_g"], name=f"{tag}_ssd_b")
    dxbc, dconv_w, dconv_b = conv_bwd(dpre, proj, w["conv_w"], name=f"{tag}_conv_b")
    pad = jnp.zeros((duv.shape[0], HYB_PAD - HYB_MAIN - LANES), BF16)
    dproj = jnp.concatenate([duv, dz, dxbc, ddt, pad], axis=1)
    dhn = mm(dproj, w["hyb_in"], tb=True, name=f"{tag}_in_bx")
    dw_in = mm(hn, dproj, ta=True, out_dtype=BF16, name=f"{tag}_in_bw")
    g = dict(hyb_in=dw_in, hyb_out=dw_out, ln_g=dln_g, ln_b=dln_b, w_s=dw_s, b_st=db_st, conv_w=dconv_w,
             conv_b=dconv_b, dt_bias=ddt_bias, a_log=da_log, d_exp=dd_exp, norm_g=dnorm_g)
    return dhn, g


def _mla_fwd(hn, w, rope, tag, job=None):
    cs, sn, c256, s256 = rope
    c_all = mm(hn, w["mla_in"], name=f"{tag}_in")
    cqn, ckvn, kr = kprep_fwd(c_all, cs, sn, w["q_g"], w["kv_g"], name=f"{tag}_kprep")
    qb = mm(cqn, w["uq"], out_dtype=BF16, name=f"{tag}_uq")
    q = qrope_fwd(qb, c256, s256, name=f"{tag}_qrope")
    kv = mm(ckvn, w["ukv"], out_dtype=BF16, name=f"{tag}_ukv")
    o, lse, jouts = attn_fwd(q, kv, kr, name=f"{tag}_attn", job=job)
    mixed = mm(o, w["mla_out"], name=f"{tag}_out")
    return mixed, (c_all, cqn, ckvn, kr, q, kv, o, lse), jouts


def _mla_bwd(dmixed, hn, saved, w, rope, tag, job=None):
    cs, sn, c256, s256 = rope
    c_all, cqn, ckvn, kr, q, kv, o, lse = saved
    do = mm(dmixed, w["mla_out"], tb=True, out_dtype=BF16, name=f"{tag}_out_bx")
    dw_out = mm(o, dmixed, ta=True, out_dtype=BF16, name=f"{tag}_out_bw")
    stats = stats_fwd(do, o, lse, name=f"{tag}_stats")
    dq, dkn, dv, dkr_h, jouts = attn_bwd(q, kv, kr, do, stats, name=f"{tag}_attn_b", job=job)
    dkr = headsum(dkr_h, name=f"{tag}_dkr")
    dkv = jnp.concatenate([dkn, dv], axis=1)
    dckvn = mm(dkv, w["ukv"], tb=True, name=f"{tag}_ukv_bx")
    dw_ukv = mm(ckvn, dkv, ta=True, name=f"{tag}_ukv_bw")
    dqb = qrope_bwd(dq, c256, s256, name=f"{tag}_qrope_b")
    dcqn = mm(dqb, w["uq"], tb=True, name=f"{tag}_uq_bx")
    dw_uq = mm(cqn, dqb, ta=True, name=f"{tag}_uq_bw")
    dc_all, dq_g, dkv_g = kprep_bwd(c_all, cs, sn, dcqn, dckvn, dkr, w["q_g"], w["kv_g"], name=f"{tag}_kprep_b")
    dhn = mm(dc_all, w["mla_in"], tb=True, name=f"{tag}_in_bx")
    dw_in = mm(hn, dc_all, ta=True, name=f"{tag}_in_bw")
    g = dict(mla_in=dw_in, mla_out=dw_out, uq=dw_uq, ukv=dw_ukv, q_g=dq_g, kv_g=dkv_g)
    return dhn, g, jouts


class NoJobs:
    def fwd_job(self, i):
        return None

    def fwd_done(self, i, outs):
        pass

    def bwd_job(self, i, grads, partial):
        return None

    def bwd_done(self, i, outs):
        pass


def local_step(x, p, rope, target, weights_of, jobs=NoJobs()):
    h = x
    saved = []
    lw = []
    for i in range(DEPTH):
        w = weights_of(i)
        lw.append(w)
        t = f"l{i}"
        h, s1 = _ffn_fwd(h, w["ffn1_pre_g"], w["ffn1_w_in"], w["ffn1_w_down"], w["ffn1_post_g"], f"{t}_f1")
        h1 = h
        hn = rms_fwd(h1, w["mix_pre_g"], name=f"{t}_mixpre")
        if i % 2 == 0:
            mixed, sm = _hyb_fwd(hn, w, f"{t}_hyb")
        else:
            job = jobs.fwd_job(i)
            mixed, sm, jouts = _mla_fwd(hn, w, rope, f"{t}_mla", job)
            if job is not None:
                jobs.fwd_done(i, jouts)
                w = lw[i] = weights_of(i)
        h = post_fwd(h1, mixed, w["mix_post_g"], 1.0, name=f"{t}_mixpost")
        h, s2 = _ffn_fwd(h, w["ffn2_pre_g"], w["ffn2_w_in"], w["ffn2_w_down"], w["ffn2_post_g"], f"{t}_f2")
        h3 = h
        hn3 = rms_fwd(h3, w["ple_pre_g"], name=f"{t}_plepre")
        gl = mm(hn3, w["ple_w_gate"], name=f"{t}_plegate")
        pp = mm((p, i), w["ple_w_proj"], name=f"{t}_pleproj")
        h = ple_fwd(h3, gl, pp, w["ple_post_g"], name=f"{t}_plepost")
        saved.append((s1, h1, hn, sm, mixed, s2, h3, hn3, gl, pp))

    dh, sq = loss_fwd(h, target, name="loss")
    grads = [None] * DEPTH
    for i in reversed(range(DEPTH)):
        w = lw[i]
        t = f"l{i}"
        s1, h1, hn, sm, mixed, s2, h3, hn3, gl, pp = saved[i]
        g = {}
        dgl, dpp, g["ple_post_g"] = ple_bwd(gl, pp, dh, w["ple_post_g"], name=f"{t}_plepost_b")
        dhn3 = mm(dgl, w["ple_w_gate"], tb=True, name=f"{t}_plegate_bx")
        g["ple_w_gate"] = mm(hn3, dgl, ta=True, out_dtype=BF16, name=f"{t}_plegate_bw")
        g["ple_w_proj"] = mm((p, i), dpp, ta=True, out_dtype=BF16, name=f"{t}_pleproj_bw")
        dh, g["ple_pre_g"] = rms_bwd(h3, dhn3, dh, w["ple_pre_g"], name=f"{t}_plepre_b")
        dh, g["ffn2_pre_g"], g["ffn2_w_in"], g["ffn2_w_down"], g["ffn2_post_g"] = _ffn_bwd(
            dh, s2, w["ffn2_pre_g"], w["ffn2_w_in"], w["ffn2_w_down"], w["ffn2_post_g"], f"{t}_f2")
        dmixed, g["mix_post_g"] = post_bwd(mixed, dh, w["mix_post_g"], 1.0, name=f"{t}_mixpost_b")
        if i % 2 == 0:
            dhn, gm = _hyb_bwd(dmixed, hn, sm, w, f"{t}_hyb")
        else:
            job = jobs.bwd_job(i, grads, g)
            dhn, gm, jouts = _mla_bwd(dmixed, hn, sm, w, rope, f"{t}_mla", job)
            if job is not None:
                jobs.bwd_done(i, jouts)
        g.update(gm)
        dh, g["mix_pre_g"] = rms_bwd(h1, dhn, dh, w["mix_pre_g"], name=f"{t}_mixpre_b")
        dh, g["ffn1_pre_g"], g["ffn1_w_in"], g["ffn1_w_down"], g["ffn1_post_g"] = _ffn_bwd(
            dh, s1, w["ffn1_pre_g"], w["ffn1_w_in"], w["ffn1_w_down"], w["ffn1_post_g"], f"{t}_f1")
        grads[i] = g
    return sq, dh, grads


def _zeros_like_cols(a, n):
    return jnp.zeros(a.shape[:-1] + (n,), a.dtype)


def layer_weights(full, i):
    j = i // 2
    row = lambda v: v.reshape(1, -1)
    w = {k: row(full[k][i]) for k in ("ffn1_pre_g", "ffn1_post_g", "mix_pre_g", "mix_post_g", "ffn2_pre_g",
                                      "ffn2_post_g", "ple_pre_g", "ple_post_g")}
    for k in ("ffn1_w_in", "ffn1_w_down", "ffn2_w_in", "ffn2_w_down", "ple_w_gate", "ple_w_proj"):
        w[k] = full[k][i]
    if i % 2 == 0:
        hw = full["hyb_w_in"][j]
        w["hyb_in"] = jnp.concatenate([hw, _zeros_like_cols(hw, HYB_PAD - HYB_IN)], axis=1)
        w["hyb_out"] = full["hyb_w_out"][j]
        w["ln_g"], w["ln_b"] = row(full["gm_ln_g"][j]), row(full["gm_ln_b"][j])
        w["w_s"] = full["gm_w_s"][j].reshape(GM_HEADS * CHUNK, CHUNK)
        w["b_st"] = jnp.pad(full["gm_b_s"][j].T, ((0, 0), (0, LANES - GM_HEADS)))
        w["conv_w"], w["conv_b"] = full["ssd_conv_w"][j], row(full["ssd_conv_b"][j])
        pad16 = lambda v: jnp.pad(v.reshape(1, -1), ((0, 0), (0, LANES - SSD_HEADS)))
        w["dt_bias"], w["a_log"] = pad16(full["ssd_dt_bias"][j]), pad16(full["ssd_a_log"][j])
        w["d_exp"] = row(jnp.repeat(full["ssd_d"][j], SSD_HEAD_DIM))
        w["norm_g"] = row(full["ssd_norm_g"][j])
    else:
        wi = full["mla_w_in"][j]
        z64 = _zeros_like_cols(wi, 64)
        w["mla_in"] = jnp.concatenate([wi[:, :384], wi[:, 384:448], z64, -wi[:, 416:448], wi[:, 384:416], z64], axis=1)
        uq = full["mla_w_uq"][j].reshape(MLA_Q_LORA, MLA_HEADS, MLA_QK)
        zq = jnp.zeros((MLA_Q_LORA, MLA_HEADS, 64), uq.dtype)
        pad_part = jnp.concatenate([uq, zq], axis=2)
        swp_part = jnp.concatenate([jnp.zeros_like(uq[:, :, :128]), -uq[:, :, 160:192], uq[:, :, 128:160], zq], axis=2)
        w["uq"] = jnp.concatenate([pad_part.reshape(MLA_Q_LORA, -1), swp_part.reshape(MLA_Q_LORA, -1)], axis=1)
        ukv = full["mla_w_ukv"][j].reshape(MLA_KV_LORA, MLA_HEADS, 256)
        w["ukv"] = jnp.concatenate([ukv[:, :, :128].reshape(MLA_KV_LORA, -1), ukv[:, :, 128:].reshape(MLA_KV_LORA, -1)],
                                   axis=1)
        w["mla_out"] = full["mla_w_out"][j]
        w["q_g"], w["kv_g"] = row(full["mla_q_norm_g"][j]), row(full["mla_kv_norm_g"][j])
    return w


def full_grads(grads):
    out = {}
    stack = lambda k, idx: jnp.stack([grads[i][k] for i in idx])
    every, even, odd = range(DEPTH), range(0, DEPTH, 2), range(1, DEPTH, 2)
    for k in ("ffn1_pre_g", "ffn1_post_g", "mix_pre_g", "mix_post_g", "ffn2_pre_g", "ffn2_post_g", "ple_pre_g",
              "ple_post_g"):
        out[k] = stack(k, every).reshape(DEPTH, D_MODEL)
    for k in ("ffn1_w_in", "ffn1_w_down", "ffn2_w_in", "ffn2_w_down", "ple_w_gate", "ple_w_proj"):
        out[k] = stack(k, every)
    out["hyb_w_in"] = stack("hyb_in", even)[:, :, :HYB_IN]
    out["hyb_w_out"] = stack("hyb_out", even)
    out["gm_ln_g"] = stack("ln_g", even).reshape(2, 1024)
    out["gm_ln_b"] = stack("ln_b", even).reshape(2, 1024)
    out["gm_w_s"] = stack("w_s", even).reshape(2, GM_HEADS, CHUNK, CHUNK)
    out["gm_b_s"] = jnp.swapaxes(stack("b_st", even)[:, :, :GM_HEADS], 1, 2)
    out["ssd_conv_w"] = stack("conv_w", even)
    out["ssd_conv_b"] = stack("conv_b", even).reshape(2, SSD_CONV_CH)
    out["ssd_dt_bias"] = stack("dt_bias", even)[:, 0, :SSD_HEADS]
    out["ssd_a_log"] = stack("a_log", even)[:, 0, :SSD_HEADS]
    out["ssd_d"] = stack("d_exp", even).reshape(2, SSD_HEADS, SSD_HEAD_DIM).sum(axis=-1)
    out["ssd_norm_g"] = stack("norm_g", even).reshape(2, 1024)
    dwi = stack("mla_in", odd)
    out["mla_w_in"] = jnp.concatenate([dwi[:, :, :384], dwi[:, :, 384:416] + dwi[:, :, 544:576],
                                       dwi[:, :, 416:448] - dwi[:, :, 512:544]], axis=2)
    duq = stack("uq", odd)
    half = MLA_HEADS * MLA_QPAD
    dp = duq[:, :, :half].reshape(2, MLA_Q_LORA, MLA_HEADS, MLA_QPAD)
    ds = duq[:, :, half:].reshape(2, MLA_Q_LORA, MLA_HEADS, MLA_QPAD)
    out["mla_w_uq"] = jnp.concatenate([dp[..., :128], dp[..., 128:160] + ds[..., 160:192],
                                       dp[..., 160:192] - ds[..., 128:160]], axis=-1).reshape(2, MLA_Q_LORA, -1)
    dukv = stack("ukv", odd)
    dk = dukv[:, :, :2048].reshape(2, MLA_KV_LORA, MLA_HEADS, 128)
    dv = dukv[:, :, 2048:].reshape(2, MLA_KV_LORA, MLA_HEADS, 128)
    out["mla_w_ukv"] = jnp.concatenate([dk, dv], axis=-1).reshape(2, MLA_KV_LORA, -1)
    out["mla_w_out"] = stack("mla_out", odd)
    out["mla_q_norm_g"] = stack("q_g", odd).reshape(2, MLA_Q_LORA)
    out["mla_kv_norm_g"] = stack("kv_g", odd).reshape(2, MLA_KV_LORA)
    return out


def rope_tables(positions):
    T = positions.shape[0]
    inv = 1.0 / (ROPE_BASE ** (jnp.arange(0, MLA_ROPE, 2, dtype=F32) / MLA_ROPE))
    ang = positions.astype(F32)[:, None] * inv
    cos, sin = jnp.cos(ang), jnp.sin(ang)
    z64 = jnp.zeros((T, 64), F32)
    cs = jnp.concatenate([cos, cos, z64], axis=1)
    sn = jnp.concatenate([sin, sin, z64], axis=1)
    c256 = jnp.concatenate([jnp.ones((T, 128), F32), cs], axis=1)
    s256 = jnp.concatenate([jnp.zeros((T, 128), F32), sn], axis=1)
    return cs, sn, c256, s256


def _rows(n):
    return -(-n // LANES)


def _pack(pieces, dtype, row_multiple):
    flat = []
    total = 0
    for a in pieces:
        v = a.reshape(-1).astype(dtype)
        padn = _rows(v.shape[0]) * LANES - v.shape[0]
        if padn:
            v = jnp.concatenate([v, jnp.zeros((padn,), dtype)])
        flat.append(v)
        total += v.shape[0] // LANES
    tail = -total % row_multiple
    if tail:
        flat.append(jnp.zeros((tail * LANES,), dtype))
    return jnp.concatenate(flat).reshape(-1, LANES)


def _unpack(slab, shapes):
    out = []
    r = 0
    for s in shapes:
        n = int(np.prod(s))
        nr = _rows(n)
        out.append(slab[r:r + nr].reshape(-1)[:n].reshape(s))
        r += nr
    return out


def _shard_shape(shape, ax):
    if ax is None:
        return tuple(shape)
    s = list(shape)
    s[ax] //= N_CHIPS
    return tuple(s)


def _chip_slice(a, ax, k):
    if ax is None:
        return a
    n = a.shape[ax] // N_CHIPS
    return lax.slice_in_dim(a, k * n, (k + 1) * n, axis=ax)


def _plane_peers():
    x, y, c = lax.axis_index("x"), lax.axis_index("y"), lax.axis_index("c")
    return (x, y, c), [(1 - x, y, c), (x, 1 - y, c), (1 - x, 1 - y, c)]


ANY = pl.BlockSpec(memory_space=pl.ANY)


def plane_allgather(slab):
    R = slab.shape[0]

    def body(src, out, send_sems, recv_sems, local_sem):
        (x, y, c), peers = _plane_peers()
        me = 2 * x + y
        local = pltpu.make_async_copy(src, out.at[me], local_sem)
        local.start()
        copies = []
        for j, peer in enumerate(peers):
            cp = pltpu.make_async_remote_copy(src_ref=src, dst_ref=out.at[me], send_sem=send_sems.at[j],
                                              recv_sem=recv_sems.at[j], device_id=peer, device_id_type=MESH)
            cp.start()
            copies.append(cp)
        for cp in copies:
            cp.wait()
        local.wait()

    return pl.pallas_call(
        body, name="plane_allgather", out_shape=SDS((N_CHIPS, R, LANES), slab.dtype),
        in_specs=[ANY], out_specs=ANY,
        scratch_shapes=[pltpu.SemaphoreType.DMA((3,)), pltpu.SemaphoreType.DMA((3,)), pltpu.SemaphoreType.DMA],
    )(slab)


def plane_alltoall(buf):
    R = buf.shape[1]

    def body(src, out, send_sems, recv_sems, local_sem):
        (x, y, c), peers = _plane_peers()
        me = 2 * x + y
        local = pltpu.make_async_copy(src.at[me], out.at[me], local_sem)
        local.start()
        copies = []
        for j, peer in enumerate(peers):
            cp = pltpu.make_async_remote_copy(src_ref=src.at[2 * peer[0] + peer[1]], dst_ref=out.at[me],
                                              send_sem=send_sems.at[j], recv_sem=recv_sems.at[j], device_id=peer,
                                              device_id_type=MESH)
            cp.start()
            copies.append(cp)
        for cp in copies:
            cp.wait()
        local.wait()

    return pl.pallas_call(
        body, name="plane_alltoall", out_shape=SDS((N_CHIPS, R, LANES), buf.dtype),
        in_specs=[ANY], out_specs=ANY,
        scratch_shapes=[pltpu.SemaphoreType.DMA((3,)), pltpu.SemaphoreType.DMA((3,)), pltpu.SemaphoreType.DMA],
    )(buf)


def sibling_swap(buf):
    def body(src, out, send_sem, recv_sem):
        x, y, c = lax.axis_index("x"), lax.axis_index("y"), lax.axis_index("c")
        cp = pltpu.make_async_remote_copy(src_ref=src, dst_ref=out, send_sem=send_sem, recv_sem=recv_sem,
                                          device_id=(x, y, 1 - c), device_id_type=MESH)
        cp.start()
        cp.wait()

    return pl.pallas_call(
        body, name="sibling_swap", out_shape=SDS(buf.shape, buf.dtype), in_specs=[ANY], out_specs=ANY,
        scratch_shapes=[pltpu.SemaphoreType.DMA, pltpu.SemaphoreType.DMA],
    )(buf)


def _chip_block(ref, ax, k, n, stride=None):
    stride = n if stride is None else stride
    start = pl.multiple_of(k * stride, math.gcd(n, stride))
    return ref.at[:, pl.ds(start, n), :] if ax == 1 else ref.at[:, :, pl.ds(start, n)]


def gather_job(shards, axes):
    n = len(shards)
    fulls = []
    for s, ax in zip(shards, axes):
        if ax == "stack":
            fulls.append(SDS((N_CHIPS,) + tuple(s.shape), s.dtype))
            continue
        shape = list(s.shape)
        shape[ax] *= N_CHIPS
        fulls.append(SDS(tuple(shape), s.dtype))

    def copies(srcs, outs, sems):
        send_sems, recv_sems, local_sems = sems
        (x, y, c), peers = _plane_peers()
        me = 2 * x + y
        cps = []
        for t in range(n):
            if axes[t] == "stack":
                dst = outs[t].at[me]
            else:
                dst = _chip_block(outs[t], axes[t], me, srcs[t].shape[axes[t]])
            cps.append(pltpu.make_async_copy(srcs[t], dst, local_sems.at[t]))
            for j, peer in enumerate(peers):
                cps.append(pltpu.make_async_remote_copy(src_ref=srcs[t], dst_ref=dst, send_sem=send_sems.at[3 * t + j],
                                                        recv_sem=recv_sems.at[3 * t + j], device_id=peer,
                                                        device_id_type=MESH))
        return cps

    def start(srcs, outs, sems):
        for cp in copies(srcs, outs, sems):
            cp.start()

    def finish(srcs, outs, sems):
        for cp in copies(srcs, outs, sems):
            cp.wait()

    scratch = [pltpu.SemaphoreType.DMA((3 * n,)), pltpu.SemaphoreType.DMA((3 * n,)), pltpu.SemaphoreType.DMA((n,))]
    return dict(inputs=list(shards), out_shape=fulls, scratch=scratch, phases=(start, None, finish))


def split_gather_job(shards, axes):
    n = len(shards)
    plain = gather_job(shards, axes)

    def region(out, t, chip, shard_shape, half_of):
        ax = axes[t]
        ref = out.at[chip] if ax == "stack" else out
        starts, sizes = [0, 0, 0], list(shard_shape)
        if ax != "stack":
            starts[ax] = chip * shard_shape[ax]
        if half_of is not None:
            sa = 0 if shard_shape[0] % 2 == 0 else 1
            sizes[sa] = shard_shape[sa] // 2
            starts[sa] = starts[sa] + half_of * sizes[sa]
        starts = [s if isinstance(s, int) else pl.multiple_of(s, math.gcd(full, z))
                  for s, z, full in zip(starts, sizes, shard_shape)]
        return ref.at[tuple(pl.ds(s, z) for s, z in zip(starts, sizes))]

    def copies(srcs, outs, sems):
        send_sems, recv_sems, local_sems, fsend_sems, frecv_sems = sems
        (x, y, c), peers = _plane_peers()
        sibling = (x, y, 1 - c)
        me = 2 * x + y
        locals_, sends, forwards = [], [], []
        for t in range(n):
            shape = srcs[t].shape
            locals_.append(pltpu.make_async_copy(srcs[t], region(outs[t], t, me, shape, None), local_sems.at[t]))
            sa = 0 if shape[0] % 2 == 0 else 1
            hs = shape[sa] // 2
            mine = srcs[t].at[pl.ds(c * hs, hs)] if sa == 0 else srcs[t].at[:, pl.ds(c * hs, hs), :]
            for j, (px, py, _) in enumerate(peers):
                sends.append(pltpu.make_async_remote_copy(
                    src_ref=mine, dst_ref=region(outs[t], t, me, shape, c), send_sem=send_sems.at[3 * t + j],
                    recv_sem=recv_sems.at[3 * t + j], device_id=(px, py, c), device_id_type=MESH))
                landed = region(outs[t], t, 2 * px + py, shape, c)
                forwards.append(pltpu.make_async_remote_copy(
                    src_ref=landed, dst_ref=landed, send_sem=fsend_sems.at[3 * t + j],
                    recv_sem=frecv_sems.at[3 * t + j], device_id=sibling, device_id_type=MESH))
        return locals_, sends, forwards

    def start(srcs, outs, sems):
        locals_, sends, _ = copies(srcs, outs, sems)
        for cp in locals_ + sends:
            cp.start()

    def middle(srcs, outs, sems):
        _, sends, forwards = copies(srcs, outs, sems)
        for cp, fw in zip(sends, forwards):
            cp.wait_recv()
            fw.start()

    def finish(srcs, outs, sems):
        locals_, sends, forwards = copies(srcs, outs, sems)
        for cp in locals_:
            cp.wait()
        for cp in sends:
            cp.wait_send()
        for fw in forwards:
            fw.wait()

    scratch = [pltpu.SemaphoreType.DMA((3 * n,)) for _ in range(2)] + [pltpu.SemaphoreType.DMA((n,))] + \
              [pltpu.SemaphoreType.DMA((3 * n,)) for _ in range(2)]
    return dict(inputs=list(shards), out_shape=plain["out_shape"], scratch=scratch, phases=(start, middle, finish))


def exchange_job(grads, axes):
    n = len(grads)
    outs = []
    spans = []
    for g, ax in zip(grads, axes):
        ax, width, stride = ax if isinstance(ax, tuple) else (ax, g.shape[ax] // N_CHIPS, None)
        spans.append((ax, width, stride))
        shape = list(g.shape)
        shape[ax] = width
        outs.append(SDS((N_CHIPS,) + tuple(shape), g.dtype))

    def copies(srcs, res, sems):
        mine, theirs = res[:n], res[n:]
        send_sems, recv_sems, local_sems, fsend_sems, frecv_sems = sems
        (x, y, c), peers = _plane_peers()
        sibling = (x, y, 1 - c)
        me = 2 * x + y
        blocks = [me] + [2 * px + py for (px, py, _) in peers]
        locals_, sends, forwards = [], [], []
        for t in range(n):
            ax, width, stride = spans[t]
            locals_.append(pltpu.make_async_copy(_chip_block(srcs[t], ax, me, width, stride), mine[t].at[me],
                                                 local_sems.at[t]))
            for j, peer in enumerate(peers):
                sends.append(pltpu.make_async_remote_copy(
                    src_ref=_chip_block(srcs[t], ax, blocks[j + 1], width, stride), dst_ref=mine[t].at[me],
                    send_sem=send_sems.at[3 * t + j], recv_sem=recv_sems.at[3 * t + j], device_id=peer,
                    device_id_type=MESH))
            for q, blk in enumerate(blocks):
                forwards.append(pltpu.make_async_remote_copy(
                    src_ref=mine[t].at[blk], dst_ref=theirs[t].at[blk], send_sem=fsend_sems.at[4 * t + q],
                    recv_sem=frecv_sems.at[4 * t + q], device_id=sibling, device_id_type=MESH))
        return locals_, sends, forwards

    def start(srcs, res, sems):
        locals_, sends, _ = copies(srcs, res, sems)
        for cp in locals_ + sends:
            cp.start()

    def middle(srcs, res, sems):
        locals_, sends, forwards = copies(srcs, res, sems)
        for t in range(n):
            locals_[t].wait()
            for q in range(N_CHIPS):
                if q > 0:
                    sends[3 * t + q - 1].wait_recv()
                forwards[4 * t + q].start()

    def finish(srcs, res, sems):
        _, sends, forwards = copies(srcs, res, sems)
        for cp in sends:
            cp.wait_send()
        for fw in forwards:
            fw.wait()

    scratch = [pltpu.SemaphoreType.DMA((3 * n,)), pltpu.SemaphoreType.DMA((3 * n,)), pltpu.SemaphoreType.DMA((n,)),
               pltpu.SemaphoreType.DMA((4 * n,)), pltpu.SemaphoreType.DMA((4 * n,))]
    return dict(inputs=list(grads), out_shape=outs + outs, scratch=scratch, phases=(start, middle, finish))


def run_job(job, ins, outs, sems, first=None, mid=None, last=None):
    for phase, when in zip(job["phases"], (first, mid, last)):
        if phase is None:
            continue
        if when is None:
            phase(ins, outs, sems)
        else:
            pl.when(when)(functools.partial(phase, ins, outs, sems))


def job_call(job, *, name):
    ni, no = len(job["inputs"]), len(job["out_shape"])

    def body(*refs):
        run_job(job, refs[:ni], refs[ni:ni + no], refs[ni + no:])

    return pl.pallas_call(body, name=name, out_shape=job["out_shape"], in_specs=[ANY] * ni, out_specs=[ANY] * no,
                          scratch_shapes=job["scratch"])(*job["inputs"])


def _adam_update(g, w, m, v):
    mn = ADAM_B1 * m + (1.0 - ADAM_B1) * g
    vn = ADAM_B2 * v + (1.0 - ADAM_B2) * jnp.square(g)
    m_hat = mn / (1.0 - ADAM_B1 ** ADAM_STEP)
    v_hat = vn / (1.0 - ADAM_B2 ** ADAM_STEP)
    return -ADAM_LR * (m_hat / (jnp.sqrt(v_hat) + ADAM_EPS) + ADAM_WD * w), mn, vn


def _row_tile(rs):
    for cand in range(256, 15, -16):
        if rs % cand == 0:
            return cand
    return rs


def _group_sum(half, a0_r, b0_r, a1_r, b1_r):
    def plane_sums(a_r, b_r):
        pa = a_r[0].astype(F32)
        pb = b_r[0].astype(F32)
        for k in range(1, N_CHIPS):
            pa = pa + a_r[k].astype(F32)
            pb = pb + b_r[k].astype(F32)
        return pa + pb
    return jnp.where(pl.program_id(0) < half, plane_sums(a0_r, b0_r), plane_sums(a1_r, b1_r))


def _group_specs(half, tr, cs):
    first = pl.BlockSpec((N_CHIPS, None, tr, cs),
                         lambda l, i: (0, jnp.minimum(l, half - 1), jnp.where(l < half, i, 0), 0))
    second = pl.BlockSpec((N_CHIPS, None, tr, cs),
                          lambda l, i: (0, jnp.maximum(l - half, 0), jnp.where(l < half, 0, i), 0))
    return [first, first, second, second]


def adamw_reg(groups, w, m, v, *, name):
    L, rs, cs = w.shape
    tr = _row_tile(rs)
    early = groups[0][0].shape[1]

    def body(a0_r, b0_r, a1_r, b1_r, w_r, m_r, v_r, g_o, d_o, m_o, v_o):
        g = _group_sum(early, a0_r, b0_r, a1_r, b1_r)
        d, mn, vn = _adam_update(g, w_r[...], m_r[...], v_r[...])
        g_o[...] = g
        d_o[...] = d
        m_o[...] = mn
        v_o[...] = vn

    s1 = pl.BlockSpec((None, tr, cs), lambda l, i: (l, i, 0))
    (a0, b0), (a1, b1) = groups
    return pl.pallas_call(
        body, name=name, grid=(L, rs // tr), in_specs=_group_specs(early, tr, cs) + [s1, s1, s1], out_specs=[s1] * 4,
        out_shape=[SDS((L, rs, cs), F32)] * 4,
        compiler_params=pltpu.CompilerParams(dimension_semantics=("parallel", "parallel"),
                                             vmem_limit_bytes=VMEM_LIMIT),
    )(a0, b0, a1, b1, w, m, v)


def groups_sum(groups, *, name):
    (a0, b0), (a1, b1) = groups
    _, half, rs, cs = a0.shape
    L = half + a1.shape[1]
    tr = _row_tile(rs)

    def body(a0_r, b0_r, a1_r, b1_r, g_o):
        g_o[...] = _group_sum(half, a0_r, b0_r, a1_r, b1_r)

    return pl.pallas_call(
        body, name=name, grid=(L, rs // tr), in_specs=_group_specs(half, tr, cs),
        out_specs=pl.BlockSpec((None, tr, cs), lambda l, i: (l, i, 0)), out_shape=SDS((L, rs, cs), F32),
        compiler_params=pltpu.CompilerParams(dimension_semantics=("parallel", "parallel"),
                                             vmem_limit_bytes=VMEM_LIMIT),
    )(a0, b0, a1, b1)


def adamw_plain(g, w, m, v, *, name):
    L, rs, cs = w.shape
    tr = _row_tile(rs)

    def body(g_r, w_r, m_r, v_r, d_o, m_o, v_o):
        d, mn, vn = _adam_update(g_r[...], w_r[...], m_r[...], v_r[...])
        d_o[...] = d
        m_o[...] = mn
        v_o[...] = vn

    s1 = pl.BlockSpec((None, tr, cs), lambda l, i: (l, i, 0))
    return pl.pallas_call(
        body, name=name, grid=(L, rs // tr), in_specs=[s1] * 4, out_specs=[s1] * 3,
        out_shape=[SDS((L, rs, cs), F32)] * 3,
        compiler_params=pltpu.CompilerParams(dimension_semantics=("parallel", "parallel"),
                                             vmem_limit_bytes=VMEM_LIMIT),
    )(g, w, m, v)


def plane_sum(r4):
    R = r4.shape[1]

    def body(r_ref, o_ref):
        acc = r_ref[0].astype(F32)
        for k in range(1, N_CHIPS):
            acc = acc + r_ref[k].astype(F32)
        o_ref[...] = acc

    return pl.pallas_call(
        body, name="plane_sum", grid=(R // PACK_ROWS,),
        in_specs=[pl.BlockSpec((N_CHIPS, PACK_ROWS, LANES), lambda i: (0, i, 0))],
        out_specs=pl.BlockSpec((PACK_ROWS, LANES), lambda i: (i, 0)), out_shape=SDS((R, LANES), F32),
        compiler_params=pltpu.CompilerParams(dimension_semantics=("parallel",)),
    )(r4)


def adamw(pa, pb, w, m, v):
    R = w.shape[0]

    def body(pa_r, pb_r, w_r, m_r, v_r, g_o, d_o, m_o, v_o):
        g = pa_r[...] + pb_r[...]
        d, mn, vn = _adam_update(g, w_r[...], m_r[...], v_r[...])
        g_o[...] = g
        d_o[...] = d
        m_o[...] = mn
        v_o[...] = vn

    spec = pl.BlockSpec((PACK_ROWS, LANES), lambda i: (i, 0))
    return pl.pallas_call(
        body, name="adamw", grid=(R // PACK_ROWS,), in_specs=[spec] * 5, out_specs=[spec] * 4,
        out_shape=[SDS((R, LANES), F32)] * 4,
        compiler_params=pltpu.CompilerParams(dimension_semantics=("parallel",)),
    )(pa, pb, w, m, v)


GATHER_AXES = [WSPEC[n][2] for n in REG] + ["stack"]
EXCHANGE_AXES = [WSPEC[n][2] for n in REG] + [(2, WIN_WIDTH, WIN_STRIDE)]


EARLY = {n: WSPEC[n][1][0] // 2 for n in MOVED}
EARLY.update(ffn2_w_in=1, ffn2_w_down=1, ple_w_gate=1, ple_w_proj=1)


def _group_range(n, group):
    return range(EARLY[n]) if group == 0 else range(EARLY[n], WSPEC[n][1][0])


def moved_shards(wl, group):
    out = []
    for n in MOVED:
        r = _group_range(n, group)
        out.append(wl[n][r.start:r.stop].astype(BF16))
    return out


def moved_grads(grads, group, partial=None):
    out = []
    for n in MOVED:
        if WSPEC[n][1][0] == DEPTH:
            layer_of, key = (lambda s: s), n
        elif n == "mla_w_out":
            layer_of, key = (lambda s: 2 * s + 1), "mla_out"
        else:
            layer_of, key = (lambda s: 2 * s), {"hyb_w_out": "hyb_out", WIN: "hyb_in"}[n]
        per_layer = [grads[layer_of(s)] if grads[layer_of(s)] is not None else partial for s in _group_range(n, group)]
        out.append(jnp.stack([d[key] for d in per_layer]))
    return out


def gather_misc(wl):
    full = {}
    sharded = [n for n in MISC if WSPEC[n][2] is not None]
    pieces = [wl[n].astype(BF16) if WSPEC[n][3] else lax.bitcast_convert_type(wl[n], BF16) for n in sharded]
    got = plane_allgather(_pack(pieces, BF16, 16))
    shapes = [v.shape for v in pieces]
    per_chip = [_unpack(got[k], shapes) for k in range(N_CHIPS)]
    for idx, n in enumerate(sharded):
        parts = [per_chip[k][idx] for k in range(N_CHIPS)]
        if not WSPEC[n][3]:
            parts = [lax.bitcast_convert_type(v, F32) for v in parts]
        full[n] = jnp.concatenate(parts, axis=WSPEC[n][2])
    for n in MISC:
        if WSPEC[n][2] is None:
            full[n] = wl[n]
    return full


def kernel(x, p, positions, ffn1_pre_g, ffn1_w_in, ffn1_w_down, ffn1_post_g, mix_pre_g, mix_post_g, ffn2_pre_g, ffn2_w_in, ffn2_w_down, ffn2_post_g, ple_pre_g, ple_w_gate, ple_w_proj, ple_post_g, hyb_w_in, gm_ln_g, gm_ln_b, gm_w_s, gm_b_s, ssd_conv_w, ssd_conv_b, ssd_dt_bias, ssd_a_log, ssd_d, ssd_norm_g, hyb_w_out, mla_w_in, mla_q_norm_g, mla_kv_norm_g, mla_w_uq, mla_w_ukv, mla_w_out, loss_target, m_ffn1_pre_g, m_ffn1_w_in, m_ffn1_w_down, m_ffn1_post_g, m_mix_pre_g, m_mix_post_g, m_ffn2_pre_g, m_ffn2_w_in, m_ffn2_w_down, m_ffn2_post_g, m_ple_pre_g, m_ple_w_gate, m_ple_w_proj, m_ple_post_g, m_hyb_w_in, m_gm_ln_g, m_gm_ln_b, m_gm_w_s, m_gm_b_s, m_ssd_conv_w, m_ssd_conv_b, m_ssd_dt_bias, m_ssd_a_log, m_ssd_d, m_ssd_norm_g, m_hyb_w_out, m_mla_w_in, m_mla_q_norm_g, m_mla_kv_norm_g, m_mla_w_uq, m_mla_w_ukv, m_mla_w_out, v_ffn1_pre_g, v_ffn1_w_in, v_ffn1_w_down, v_ffn1_post_g, v_mix_pre_g, v_mix_post_g, v_ffn2_pre_g, v_ffn2_w_in, v_ffn2_w_down, v_ffn2_post_g, v_ple_pre_g, v_ple_w_gate, v_ple_w_proj, v_ple_post_g, v_hyb_w_in, v_gm_ln_g, v_gm_ln_b, v_gm_w_s, v_gm_b_s, v_ssd_conv_w, v_ssd_conv_b, v_ssd_dt_bias, v_ssd_a_log, v_ssd_d, v_ssd_norm_g, v_hyb_w_out, v_mla_w_in, v_mla_q_norm_g, v_mla_kv_norm_g, v_mla_w_uq, v_mla_w_ukv, v_mla_w_out):
    args = locals()
    wl = {n: args[n] for n in WNAMES}
    ml = {n: args["m_" + n] for n in WNAMES}
    vl = {n: args["v_" + n] for n in WNAMES}

    full = gather_misc(wl)
    nmoved = len(MOVED)
    for n in MOVED:
        full[n] = [None] * WSPEC[n][1][0]
    exchanged = [None, None]

    def gathered(group, outs):
        for t, n in enumerate(MOVED):
            for l, s in enumerate(_group_range(n, group)):
                if n == WIN:
                    full[n][s] = jnp.concatenate([outs[t][k, l] for k in range(N_CHIPS)], axis=-1)
                else:
                    full[n][s] = (outs[t], l)

    gathered(0, job_call(split_gather_job(moved_shards(wl, 0), GATHER_AXES), name="gather_first"))

    class Jobs(NoJobs):
        def fwd_job(self, i):
            return gather_job(moved_shards(wl, 1), GATHER_AXES) if i == 1 else None

        def fwd_done(self, i, outs):
            gathered(1, outs)

        def bwd_job(self, i, grads, partial):
            return exchange_job(moved_grads(grads, 1, partial), EXCHANGE_AXES) if i == 1 else None

        def bwd_done(self, i, outs):
            exchanged[1] = outs

    rope = rope_tables(positions[0])
    T = x.shape[1]
    sq, dx, grads = local_step(x[0], p.reshape(DEPTH, T, p.shape[-1]), rope, loss_target[0],
                               lambda i: layer_weights(full, i), Jobs())
    loss = lax.psum(0.5 * jnp.sum(sq) / D_MODEL, ("x", "y", "c"))

    res = {}
    exchanged[0] = job_call(exchange_job(moved_grads(grads, 0), EXCHANGE_AXES), name="exchange_first")
    for t, n in enumerate(MOVED):
        groups = [(e[t], e[nmoved + t]) for e in exchanged]
        if n == WIN:
            window = groups_sum(groups, name=f"sum_{n}")
            chip = 2 * lax.axis_index("x") + lax.axis_index("y")
            g = lax.dynamic_slice_in_dim(window, chip * (WIN_SHARD - WIN_STRIDE), WIN_SHARD, axis=2)
            res[n] = [g] + list(adamw_plain(g, wl[n], ml[n], vl[n], name=f"adamw_{n}"))
        else:
            res[n] = adamw_reg(groups, wl[n], ml[n], vl[n], name=f"adamw_{n}")
    fg = full_grads(grads)
    dest = [_pack([_chip_slice(fg[n], WSPEC[n][2], k) for n in MISC], BF16, PACK_ROWS) for k in range(N_CHIPS)]
    mine = plane_sum(plane_alltoall(jnp.stack(dest)))
    other = sibling_swap(mine)
    slabs = adamw(mine, other, *[_pack([d[n] for n in MISC], F32, PACK_ROWS) for d in (wl, ml, vl)])
    shapes = [wl[n].shape for n in MISC]
    unpacked = [_unpack(s, shapes) for s in slabs]
    for idx, n in enumerate(MISC):
        res[n] = [u[idx] for u in unpacked]
    return (loss, dx[None], *[res[n][k] for k in range(4) for n in WNAMES])
```

```python
import functools
import math

import jax
import jax.numpy as jnp
import numpy as np
from jax import lax
from jax.experimental import pallas as pl
from jax.experimental.pallas import tpu as pltpu

F32 = jnp.float32
BF16 = jnp.bfloat16
SDS = jax.ShapeDtypeStruct
MESH = pl.DeviceIdType.MESH
HIGHEST = lax.Precision.HIGHEST

D_MODEL = 1024
DEPTH = 4
D_FF = 2816
NORM_EPS = 1e-6
LN_EPS = 1e-5
GM_HEADS = 8
CHUNK = 128
SSD_HEADS = 16
SSD_HEAD_DIM = 64
SSD_INNER = 1024
SSD_STATE = 128
SSD_CONV = 4
SSD_CONV_CH = 1536
HYB_MAIN = 4608
HYB_IN = 4624
HYB_PAD = 5120
MLA_HEADS = 16
MLA_NOPE = 128
MLA_ROPE = 64
MLA_QK = 192
MLA_QPAD = 256
MLA_Q_LORA = 256
MLA_KV_LORA = 128
ROPE_BASE = 10000.0
ADAM_LR = 0.001
ADAM_B1 = 0.9
ADAM_B2 = 0.999
ADAM_EPS = 1e-08
ADAM_WD = 0.01
ADAM_STEP = 10

N_CHIPS = 4
LANES = 128
VMEM_LIMIT = 56 * 1024 * 1024
PACK_ROWS = 2048

WEIGHTS = [
    ("ffn1_pre_g", (4, 1024), None, False),
    ("ffn1_w_in", (4, 1024, 5632), 2, True),
    ("ffn1_w_down", (4, 2816, 1024), 1, True),
    ("ffn1_post_g", (4, 1024), None, False),
    ("mix_pre_g", (4, 1024), None, False),
    ("mix_post_g", (4, 1024), None, False),
    ("ffn2_pre_g", (4, 1024), None, False),
    ("ffn2_w_in", (4, 1024, 5632), 2, True),
    ("ffn2_w_down", (4, 2816, 1024), 1, True),
    ("ffn2_post_g", (4, 1024), None, False),
    ("ple_pre_g", (4, 1024), None, False),
    ("ple_w_gate", (4, 1024, 1024), 1, True),
    ("ple_w_proj", (4, 256, 1024), 2, True),
    ("ple_post_g", (4, 1024), None, False),
    ("hyb_w_in", (2, 1024, 4624), 2, True),
    ("gm_ln_g", (2, 1024), None, False),
    ("gm_ln_b", (2, 1024), None, False),
    ("gm_w_s", (2, 8, 128, 128), None, False),
    ("gm_b_s", (2, 8, 128), None, False),
    ("ssd_conv_w", (2, 4, 1536), 2, False),
    ("ssd_conv_b", (2, 1536), None, False),
    ("ssd_dt_bias", (2, 16), None, False),
    ("ssd_a_log", (2, 16), None, False),
    ("ssd_d", (2, 16), None, False),
    ("ssd_norm_g", (2, 1024), None, False),
    ("hyb_w_out", (2, 2048, 1024), 1, True),
    ("mla_w_in", (2, 1024, 448), 1, True),
    ("mla_q_norm_g", (2, 256), 1, False),
    ("mla_kv_norm_g", (2, 128), None, False),
    ("mla_w_uq", (2, 256, 3072), 2, True),
    ("mla_w_ukv", (2, 128, 4096), 2, True),
    ("mla_w_out", (2, 2048, 1024), 1, True),
]
WNAMES = [w[0] for w in WEIGHTS]
WSPEC = {w[0]: w for w in WEIGHTS}
REG = ["ffn1_w_in", "ffn1_w_down", "ffn2_w_in", "ffn2_w_down", "ple_w_gate", "ple_w_proj", "hyb_w_out", "mla_w_out"]
WIN = "hyb_w_in"
WIN_SHARD = 4624 // 4
WIN_STRIDE = (WIN_SHARD // 128) * 128
WIN_WIDTH = -(-(WIN_SHARD + 3 * (WIN_SHARD - WIN_STRIDE)) // 128) * 128
MOVED = REG + [WIN]
MISC = [n for n in WNAMES if n not in MOVED]


def _pick(dim, target):
    if dim <= target:
        return dim
    t = (target // LANES) * LANES
    while t >= LANES:
        if dim % t == 0:
            return t
        t -= LANES
    return dim


def mm(a, b, *, ta=False, tb=False, out_dtype=F32, name, tm=1024, tn=1024, tk=1024):
    a, la = a if isinstance(a, tuple) else (a, None)
    b, lb = b if isinstance(b, tuple) else (b, None)

    def dims(x, mode):
        r, c = x.shape[-2:]
        return (r, c * x.shape[0]) if mode == "planes" else (r, c)

    K, M = dims(a, la) if ta else dims(a, la)[::-1]
    N, K2 = dims(b, lb) if tb else dims(b, lb)[::-1]
    assert K == K2, (a.shape, b.shape, ta, tb)
    assert not (la == "planes" and ta) and not (lb == "planes" and tb)
    bm, bn = _pick(M, tm), _pick(b.shape[-1] if lb == "planes" else N, tn)
    bk = _pick(a.shape[-1] if la == "planes" else K, tk)
    nk = K // bk

    def spec(shape, idx, layer, x):
        if layer is None:
            return pl.BlockSpec(shape, idx)
        if layer == "planes":
            per = x.shape[-1] // shape[1]
            return pl.BlockSpec((None,) + shape, lambda i, j, k: (idx(i, j, k)[1] // per, idx(i, j, k)[0],
                                                                  idx(i, j, k)[1] % per))
        return pl.BlockSpec((None,) + shape, lambda i, j, k: (layer,) + idx(i, j, k))

    a_spec = spec((bk, bm), lambda i, j, k: (k, i), la, a) if ta else spec((bm, bk), lambda i, j, k: (i, k), la, a)
    b_spec = spec((bn, bk), lambda i, j, k: (j, k), lb, b) if tb else spec((bk, bn), lambda i, j, k: (k, j), lb, b)
    dn = (((0 if ta else 1,), (1 if tb else 0,)), ((), ()))

    def body(a_ref, b_ref, o_ref, acc_ref):
        k = pl.program_id(2)

        @pl.when(k == 0)
        def _():
            acc_ref[...] = jnp.zeros_like(acc_ref)

        acc_ref[...] += lax.dot_general(a_ref[...].astype(BF16), b_ref[...].astype(BF16), dn,
                                        preferred_element_type=F32)

        @pl.when(k == nk - 1)
        def _():
            o_ref[...] = acc_ref[...].astype(o_ref.dtype)

    return pl.pallas_call(
        body, name=name, grid=(M // bm, N // bn, nk),
        in_specs=[a_spec, b_spec], out_specs=pl.BlockSpec((bm, bn), lambda i, j, k: (i, j)),
        out_shape=SDS((M, N), out_dtype), scratch_shapes=[pltpu.VMEM((bm, bn), F32)],
        compiler_params=pltpu.CompilerParams(dimension_semantics=("parallel", "parallel", "arbitrary"),
                                             vmem_limit_bytes=VMEM_LIMIT),
    )(a, b)


def row_call(fn, xs, ps, outs, accs=(), *, tb, name, reverse=False):
    xs = [x if isinstance(x, tuple) else (x, x.shape[1], 0) for x in xs]
    T = xs[0][0].shape[0]
    tb = min(tb, T)
    n = T // tb
    assert n * tb == T
    nx, npar, no, na = len(xs), len(ps), len(outs), len(accs)

    def ridx(i):
        return n - 1 - i if reverse else i

    in_specs = [pl.BlockSpec((tb, w), functools.partial(lambda i, cb: (ridx(i), cb), cb=cb)) for (_, w, cb) in xs]
    in_specs += [pl.BlockSpec(p.shape, functools.partial(lambda i, nd: (0,) * nd, nd=p.ndim)) for p in ps]
    out_specs = [pl.BlockSpec((tb, c), lambda i: (ridx(i), 0)) for (c, _) in outs]
    out_specs += [pl.BlockSpec(s, functools.partial(lambda i, nd: (0,) * nd, nd=len(s))) for s in accs]
    out_shape = [SDS((T, c), dt) for (c, dt) in outs] + [SDS(s, F32) for s in accs]

    def body(*refs):
        xr, pr = refs[:nx], refs[nx:nx + npar]
        orf, ar = refs[nx + npar:nx + npar + no], refs[nx + npar + no:]
        res = fn(*[r[...] for r in xr], *[r[...] for r in pr])
        for r, v in zip(orf, res[:no]):
            r[...] = v.astype(r.dtype)
        if na:
            @pl.when(pl.program_id(0) == 0)
            def _():
                for r in ar:
                    r[...] = jnp.zeros_like(r)

            for r, v in zip(ar, res[no:]):
                r[...] += v.astype(F32)

    res = pl.pallas_call(
        body, name=name, grid=(n,), in_specs=in_specs, out_specs=out_specs, out_shape=out_shape,
        compiler_params=pltpu.CompilerParams(dimension_semantics=("arbitrary",), vmem_limit_bytes=VMEM_LIMIT),
    )(*[x[0] for x in xs], *ps)
    return res


def _f32(*a):
    return [v.astype(F32) for v in a]


def t_rms(x, g):
    return x * lax.rsqrt(jnp.mean(x * x, axis=-1, keepdims=True) + NORM_EPS) * g


def t_swiglu(gate, up):
    return jax.nn.silu(gate) * up


def t_ple(gl, pp, g):
    return t_rms(jax.nn.sigmoid(gl) * pp, g)


def _iota(shape, d):
    return lax.broadcasted_iota(jnp.int32, shape, d)


def _bdot(a, b, dn=(((1,), (0,)), ((), ()))):
    return lax.dot_general(a.astype(BF16), b.astype(BF16), dn, preferred_element_type=F32)


def _hdot(a, b):
    return jnp.dot(a, b, precision=HIGHEST, preferred_element_type=F32)


NT = (((1,), (1,)), ((), ()))
TN = (((0,), (0,)), ((), ()))


def t_gmlp(uv, ln_g, ln_b, w_s, b_st):
    tb = uv.shape[0]
    guv = jax.nn.gelu(uv)
    u, v = guv[:, :1024], guv[:, 1024:]
    tri = _iota((CHUNK, CHUNK), 1) <= _iota((CHUNK, CHUNK), 0)
    rows = []
    for c in range(tb // CHUNK):
        vc = v[c * CHUNK:(c + 1) * CHUNK]
        heads = []
        for h in range(GM_HEADS):
            sl = slice(h * 128, (h + 1) * 128)
            vh = vc[:, sl]
            xc = vh - jnp.mean(vh, axis=-1, keepdims=True)
            var = jnp.mean(xc * xc, axis=-1, keepdims=True)
            y = xc * lax.rsqrt(var + LN_EPS) * ln_g[:, sl] + ln_b[:, sl]
            wm = jnp.where(tri, w_s[sl, :], 0.0)
            heads.append(_bdot(wm, y) + b_st[:, h:h + 1])
        rows.append(jnp.concatenate(heads, axis=1))
    mixed = rows[0] if len(rows) == 1 else jnp.concatenate(rows, axis=0)
    return u * mixed


def t_ssd(pre, dtr, z, st, dt_bias, a_log, d_exp, norm_g):
    L = CHUNK
    xbc = jax.nn.silu(pre)
    xs, bm, cm = xbc[:, :1024], xbc[:, 1024:1280], xbc[:, 1280:1536]
    valid = _iota((1, LANES), 1) < SSD_HEADS
    dt16 = jnp.where(valid, jax.nn.softplus(dtr + dt_bias), 0.0)
    a16 = jnp.where(valid, -jnp.exp(a_log), 0.0)
    da16 = dt16 * a16
    tri = _iota((L, L), 1) <= _iota((L, L), 0)
    acs16 = _hdot(tri.astype(F32), da16)
    hh, cc = _iota((LANES, 1024), 0), _iota((LANES, 1024), 1)
    expand = ((cc >= hh * SSD_HEAD_DIM) & (cc < (hh + 1) * SSD_HEAD_DIM)).astype(F32)
    acs = _hdot(acs16, expand)
    dte = _hdot(dt16, expand)
    alast = jnp.sum(jnp.where(_iota((L, 1024), 0) == L - 1, acs, 0.0), axis=0, keepdims=True)
    xd = xs * dte
    groups = [slice(0, 512), slice(512, 1024)]
    bg = [bm[:, :128], bm[:, 128:]]
    cg = [cm[:, :128], cm[:, 128:]]
    yoff = jnp.concatenate([_bdot(cg[g], st[:, groups[g]]) for g in range(2)], axis=1) * jnp.exp(acs)
    xdw = xd * jnp.exp(alast - acs)
    s_t = jnp.concatenate([_bdot(bg[g], xdw[:, groups[g]], TN) for g in range(2)], axis=1)
    st_new = st * jnp.exp(alast) + s_t
    cb = [_bdot(cg[g], bg[g], NT) for g in range(2)]
    acs16_t = acs16.T
    lo = _iota((1, LANES), 1) < SSD_HEAD_DIM
    slabs = []
    for j in range(SSD_HEADS // 2):
        g = j // 4
        xslab = xd[:, j * 128:(j + 1) * 128]
        acc = None
        for half in range(2):
            h = 2 * j + half
            seg = acs16[:, h:h + 1] - acs16_t[h:h + 1, :]
            mmat = cb[g] * jnp.exp(jnp.where(tri, seg, -1e30))
            xm = jnp.where(lo if half == 0 else jnp.logical_not(lo), xslab, 0.0)
            term = _bdot(mmat, xm)
            acc = term if acc is None else acc + term
        slabs.append(acc)
    y = jnp.concatenate(slabs, axis=1) + yoff + d_exp * xs
    yg = y * jax.nn.silu(z)
    outs = []
    for g in range(2):
        t = yg[:, groups[g]]
        outs.append(t * lax.rsqrt(jnp.mean(t * t, axis=-1, keepdims=True) + NORM_EPS) * norm_g[:, groups[g]])
    return jnp.concatenate(outs, axis=1), st_new


def t_kprep(c_all, cs, sn, qg, kvg):
    cqn = t_rms(c_all[:, :256], qg)
    ckvn = t_rms(c_all[:, 256:384], kvg)
    kr = c_all[:, 384:512] * cs + c_all[:, 512:640] * sn
    return cqn, ckvn, kr


def t_qrope(qb, c256, s256):
    scale = MLA_QK ** -0.5
    half = MLA_HEADS * MLA_QPAD
    outs = []
    for h in range(MLA_HEADS):
        a = qb[:, h * MLA_QPAD:(h + 1) * MLA_QPAD]
        b = qb[:, half + h * MLA_QPAD:half + (h + 1) * MLA_QPAD]
        outs.append((a * c256 + b * s256) * scale)
    return jnp.concatenate(outs, axis=1)


def rms_fwd(h, g, *, name, tb=512):
    def fn(h, g):
        return (t_rms(h.astype(F32), g),)
    return row_call(fn, [h], [g], [(h.shape[1], BF16)], tb=tb, name=name)[0]


def rms_bwd(h, dhn, dres, g, *, name, tb=256):
    def fn(h, dhn, dres, g):
        h, dhn, dres = _f32(h, dhn, dres)
        _, vjp = jax.vjp(t_rms, h, g)
        dh, dg = vjp(dhn)
        return dres + dh, dg
    return row_call(fn, [h, dhn, dres], [g], [(h.shape[1], F32)], [g.shape], tb=tb, name=name)


def post_fwd(h, f, g, scale, *, name, tb=512):
    def fn(h, f, g):
        return (h + scale * t_rms(f.astype(F32), g),)
    return row_call(fn, [h, f], [g], [(h.shape[1], F32)], tb=tb, name=name)[0]


def post_bwd(f, dout, g, scale, *, name, tb=256):
    def fn(f, dout, g):
        f, dout = _f32(f, dout)
        _, vjp = jax.vjp(lambda f, g: scale * t_rms(f, g), f, g)
        return vjp(dout)
    return row_call(fn, [f, dout], [g], [(f.shape[1], BF16)], [g.shape], tb=tb, name=name)


FFN_TILE = D_FF // 2


def _stacked(w):
    return w if isinstance(w, tuple) else (w[None], 0)


def ffn_in_act(hn, w_in, *, name, tm=512):
    w, layer = _stacked(w_in)
    T, K = hn.shape
    bm, bn = _pick(T, tm), FFN_TILE
    nj = D_FF // bn

    def body(x_ref, wg_ref, wu_ref, gu_ref, act_ref):
        x = x_ref[...].astype(BF16)
        g = jnp.dot(x, wg_ref[...].astype(BF16), preferred_element_type=F32)
        u = jnp.dot(x, wu_ref[...].astype(BF16), preferred_element_type=F32)
        gu_ref[0] = g.astype(gu_ref.dtype)
        gu_ref[1] = u.astype(gu_ref.dtype)
        act_ref[...] = t_swiglu(g, u).astype(act_ref.dtype)

    return pl.pallas_call(
        body, name=name, grid=(T // bm, nj),
        in_specs=[pl.BlockSpec((bm, K), lambda i, j: (i, 0)),
                  pl.BlockSpec((None, K, bn), lambda i, j: (layer, 0, j)),
                  pl.BlockSpec((None, K, bn), lambda i, j: (layer, 0, nj + j))],
        out_specs=[pl.BlockSpec((2, bm, bn), lambda i, j: (0, i, j)), pl.BlockSpec((bm, bn), lambda i, j: (i, j))],
        out_shape=[SDS((2, T, D_FF), BF16), SDS((T, D_FF), BF16)],
        compiler_params=pltpu.CompilerParams(dimension_semantics=("parallel", "parallel"),
                                             vmem_limit_bytes=VMEM_LIMIT),
    )(hn, w, w)


def ffn_down_bx_act(df, w_down, gu, *, name, tm=512):
    w, layer = _stacked(w_down)
    T, K = df.shape
    bm, bn = _pick(T, tm), FFN_TILE

    def body(df_ref, wd_ref, gu_ref, dgu_ref):
        da = lax.dot_general(df_ref[...].astype(BF16), wd_ref[...].astype(BF16), NT, preferred_element_type=F32)
        _, vjp = jax.vjp(t_swiglu, gu_ref[0].astype(F32), gu_ref[1].astype(F32))
        dg, du = vjp(da)
        dgu_ref[0] = dg.astype(dgu_ref.dtype)
        dgu_ref[1] = du.astype(dgu_ref.dtype)

    return pl.pallas_call(
        body, name=name, grid=(T // bm, D_FF // bn),
        in_specs=[pl.BlockSpec((bm, K), lambda i, j: (i, 0)),
                  pl.BlockSpec((None, bn, K), lambda i, j: (layer, j, 0)),
                  pl.BlockSpec((2, bm, bn), lambda i, j: (0, i, j))],
        out_specs=pl.BlockSpec((2, bm, bn), lambda i, j: (0, i, j)), out_shape=SDS((2, T, D_FF), BF16),
        compiler_params=pltpu.CompilerParams(dimension_semantics=("parallel", "parallel"),
                                             vmem_limit_bytes=VMEM_LIMIT),
    )(df, w, gu)


def ple_fwd(h, gl, pp, g, *, name, tb=512):
    def fn(h, gl, pp, g):
        return (h + t_ple(gl, pp, g),)
    return row_call(fn, [h, gl, pp], [g], [(D_MODEL, F32)], tb=tb, name=name)[0]


def ple_bwd(gl, pp, dout, g, *, name, tb=256):
    def fn(gl, pp, dout, g):
        _, vjp = jax.vjp(t_ple, gl, pp, g)
        return vjp(dout)
    return row_call(fn, [gl, pp, dout], [g], [(D_MODEL, BF16), (D_MODEL, BF16)], [g.shape], tb=tb, name=name)


def gmlp_fwd(proj, ln_g, ln_b, w_s, b_st, *, name, tb=256):
    def fn(uv, ln_g, ln_b, w_s, b_st):
        return (t_gmlp(uv, ln_g, ln_b, w_s, b_st),)
    return row_call(fn, [(proj, 2048, 0)], [ln_g, ln_b, w_s, b_st], [(1024, BF16)], tb=tb, name=name)[0]


def gmlp_bwd(proj, dya, ln_g, ln_b, w_s, b_st, *, name, tb=128):
    def fn(uv, dya, ln_g, ln_b, w_s, b_st):
        _, vjp = jax.vjp(t_gmlp, uv, ln_g, ln_b, w_s, b_st)
        return vjp(dya.astype(F32))
    return row_call(fn, [(proj, 2048, 0), (dya, 1024, 0)], [ln_g, ln_b, w_s, b_st], [(2048, BF16)],
                    [ln_g.shape, ln_b.shape, w_s.shape, b_st.shape], tb=tb, name=name)


def kprep_fwd(c_all, cs, sn, qg, kvg, *, name, tb=512):
    return row_call(t_kprep, [c_all, cs, sn], [qg, kvg], [(256, BF16), (128, BF16), (128, BF16)], tb=tb, name=name)


def kprep_bwd(c_all, cs, sn, dcqn, dckvn, dkr, qg, kvg, *, name, tb=256):
    def fn(c_all, cs, sn, dcqn, dckvn, dkr, qg, kvg):
        dcqn, dckvn, dkr = _f32(dcqn, dckvn, dkr)
        _, vjp = jax.vjp(lambda c, qg, kvg: t_kprep(c, cs, sn, qg, kvg), c_all, qg, kvg)
        return vjp((dcqn, dckvn, dkr))
    return row_call(fn, [c_all, cs, sn, dcqn, dckvn, dkr], [qg, kvg], [(640, BF16)], [qg.shape, kvg.shape],
                    tb=tb, name=name)


def qrope_fwd(qb, c256, s256, *, name, tb=256):
    def fn(qb, c256, s256):
        return (t_qrope(qb, c256, s256),)
    return row_call(fn, [qb, c256, s256], [], [(MLA_HEADS * MLA_QPAD, BF16)], tb=tb, name=name)[0]


def qrope_bwd(dq, c256, s256, *, name, tb=256):
    def fn(dq, c256, s256):
        scale = MLA_QK ** -0.5
        a, b = [], []
        for h in range(MLA_HEADS):
            d = dq[:, h * MLA_QPAD:(h + 1) * MLA_QPAD] * scale
            a.append(d * c256)
            b.append(d * s256)
        return (jnp.concatenate(a + b, axis=1),)
    return row_call(fn, [dq, c256, s256], [], [(2 * MLA_HEADS * MLA_QPAD, BF16)], tb=tb, name=name)[0]


STAT_SPLIT = 64
ATTN_UNROLL = 4
ATTN_HEADS_PER_STEP = 4


def stats_fwd(do, o, lse, *, name, tb=512):
    def fn(do, o, lse):
        do, o = _f32(do, o)
        low = _iota((1, 128), 1) < STAT_SPLIT
        outs = []
        for h in range(MLA_HEADS):
            sl = slice(h * 128, (h + 1) * 128)
            dl = jnp.sum(do[:, sl] * o[:, sl], axis=-1, keepdims=True)
            outs.append(jnp.where(low, lse[:, sl], dl))
        return (jnp.concatenate(outs, axis=1),)
    return row_call(fn, [do, o, lse], [], [(2048, F32)], tb=tb, name=name)[0]


def headsum(dkr_h, *, name, tb=512):
    def fn(d):
        acc = d[:, :128]
        for h in range(1, MLA_HEADS):
            acc = acc + d[:, h * 128:(h + 1) * 128]
        return (acc,)
    return row_call(fn, [dkr_h], [], [(128, F32)], tb=tb, name=name)[0]


def loss_fwd(y, t, *, name, tb=512):
    def fn(y, t):
        e = y - t
        return e * (1.0 / D_MODEL), jnp.sum(e * e, axis=0, keepdims=True)
    return row_call(fn, [y, t], [], [(D_MODEL, F32)], [(1, D_MODEL)], tb=tb, name=name)


def conv_fwd(proj, w, b, *, name, tb=256):
    T = proj.shape[0]
    n = T // tb
    hb = tb // CHUNK
    C = SSD_CONV_CH

    def body(cur, prev, w_ref, b_ref, o_ref, scr):
        i = pl.program_id(0)
        scr[pl.ds(0, CHUNK), :] = jnp.where(i > 0, prev[...], 0.0)
        scr[pl.ds(CHUNK, tb), :] = cur[...]
        y = b_ref[...] + w_ref[3:4, :] * cur[...]
        for k in range(SSD_CONV - 1):
            y = y + w_ref[k:k + 1, :] * scr[pl.ds(CHUNK - (SSD_CONV - 1) + k, tb), :]
        o_ref[...] = y

    return pl.pallas_call(
        body, name=name, grid=(n,),
        in_specs=[pl.BlockSpec((tb, C), lambda i: (i, 2)),
                  pl.BlockSpec((CHUNK, C), lambda i: (jnp.maximum(i * hb - 1, 0), 2)),
                  pl.BlockSpec((SSD_CONV, C), lambda i: (0, 0)), pl.BlockSpec((1, C), lambda i: (0, 0))],
        out_specs=pl.BlockSpec((tb, C), lambda i: (i, 0)), out_shape=SDS((T, C), F32),
        scratch_shapes=[pltpu.VMEM((CHUNK + tb, C), F32)],
        compiler_params=pltpu.CompilerParams(dimension_semantics=("arbitrary",), vmem_limit_bytes=VMEM_LIMIT),
    )(proj, proj, w, b)


def conv_bwd(dpre, proj, w, *, name, tb=256):
    T = proj.shape[0]
    n = T // tb
    hb = tb // CHUNK
    nh = T // CHUNK
    C = SSD_CONV_CH

    def body(dcur, dnext, xcur, xprev, w_ref, dx_ref, dw_ref, db_ref, dscr, xscr):
        i = pl.program_id(0)

        @pl.when(i == 0)
        def _():
            dw_ref[...] = jnp.zeros_like(dw_ref)
            db_ref[...] = jnp.zeros_like(db_ref)

        d = dcur[...]
        dscr[pl.ds(0, tb), :] = d
        dscr[pl.ds(tb, CHUNK), :] = jnp.where(i < n - 1, dnext[...], 0.0)
        xscr[pl.ds(0, CHUNK), :] = jnp.where(i > 0, xprev[...], 0.0)
        xscr[pl.ds(CHUNK, tb), :] = xcur[...]
        dx = w_ref[3:4, :] * d
        for k in range(SSD_CONV - 1):
            dx = dx + w_ref[k:k + 1, :] * dscr[pl.ds(SSD_CONV - 1 - k, tb), :]
        dx_ref[...] = dx.astype(dx_ref.dtype)
        for k in range(SSD_CONV):
            xk = xscr[pl.ds(CHUNK - (SSD_CONV - 1) + k, tb), :]
            dw_ref[k:k + 1, :] += jnp.sum(d * xk, axis=0, keepdims=True)
        db_ref[...] += jnp.sum(d, axis=0, keepdims=True)

    return pl.pallas_call(
        body, name=name, grid=(n,),
        in_specs=[pl.BlockSpec((tb, C), lambda i: (i, 0)),
                  pl.BlockSpec((CHUNK, C), lambda i: (jnp.minimum((i + 1) * hb, nh - 1), 0)),
                  pl.BlockSpec((tb, C), lambda i: (i, 2)),
                  pl.BlockSpec((CHUNK, C), lambda i: (jnp.maximum(i * hb - 1, 0), 2)),
                  pl.BlockSpec((SSD_CONV, C), lambda i: (0, 0))],
        out_specs=[pl.BlockSpec((tb, C), lambda i: (i, 0)), pl.BlockSpec((SSD_CONV, C), lambda i: (0, 0)),
                   pl.BlockSpec((1, C), lambda i: (0, 0))],
        out_shape=[SDS((T, C), BF16), SDS((SSD_CONV, C), F32), SDS((1, C), F32)],
        scratch_shapes=[pltpu.VMEM((tb + CHUNK, C), F32), pltpu.VMEM((CHUNK + tb, C), F32)],
        compiler_params=pltpu.CompilerParams(dimension_semantics=("arbitrary",), vmem_limit_bytes=VMEM_LIMIT),
    )(dpre, dpre, proj, proj, w)


def _ssd_specs(nc, rev):
    def r(c):
        return nc - 1 - c if rev else c
    pre = pl.BlockSpec((CHUNK, SSD_CONV_CH), lambda c: (r(c), 0))
    dtr = pl.BlockSpec((CHUNK, LANES), lambda c: (r(c), HYB_MAIN // LANES))
    z = pl.BlockSpec((CHUNK, 1024), lambda c: (r(c), 2))
    row = pl.BlockSpec((CHUNK, 1024), lambda c: (r(c), 0))
    return pre, dtr, z, row


def _pspec(shape):
    return pl.BlockSpec(shape, lambda c: (0,) * len(shape))


def ssd_fwd(pre, proj, dt_bias, a_log, d_exp, norm_g, *, name):
    T = pre.shape[0]
    nc = T // CHUNK
    s_pre, s_dt, s_z, s_row = _ssd_specs(nc, False)

    def body(pre_r, dt_r, z_r, b_r, a_r, d_r, g_r, y_r, sv_r, st):
        @pl.when(pl.program_id(0) == 0)
        def _():
            st[...] = jnp.zeros_like(st)

        s0 = st[...]
        sv_r[...] = s0
        y, s1 = t_ssd(pre_r[...], dt_r[...], z_r[...], s0, b_r[...], a_r[...], d_r[...], g_r[...])
        y_r[...] = y.astype(y_r.dtype)
        st[...] = s1

    return pl.pallas_call(
        body, name=name, grid=(nc,),
        in_specs=[s_pre, s_dt, s_z, _pspec((1, LANES)), _pspec((1, LANES)), _pspec((1, 1024)), _pspec((1, 1024))],
        out_specs=[s_row, s_row], out_shape=[SDS((T, 1024), BF16), SDS((T, 1024), F32)],
        scratch_shapes=[pltpu.VMEM((SSD_STATE, 1024), F32)],
        compiler_params=pltpu.CompilerParams(dimension_semantics=("arbitrary",), vmem_limit_bytes=VMEM_LIMIT),
    )(pre, proj, proj, dt_bias, a_log, d_exp, norm_g)


def ssd_bwd(pre, proj, states, dyab, dt_bias, a_log, d_exp, norm_g, *, name):
    T = pre.shape[0]
    nc = T // CHUNK
    s_pre, s_dt, s_z, s_row = _ssd_specs(nc, True)
    s_dtout = pl.BlockSpec((CHUNK, LANES), lambda c: (nc - 1 - c, 0))
    s_dy = pl.BlockSpec((CHUNK, 1024), lambda c: (nc - 1 - c, 1))

    def body(pre_r, dt_r, z_r, sv_r, dy_r, b_r, a_r, d_r, g_r, dpre_r, ddt_r, dz_r, db_r, da_r, dd_r, dg_r, dst):
        @pl.when(pl.program_id(0) == 0)
        def _():
            dst[...] = jnp.zeros_like(dst)
            for r in (db_r, da_r, dd_r, dg_r):
                r[...] = jnp.zeros_like(r)

        _, vjp = jax.vjp(t_ssd, pre_r[...], dt_r[...], z_r[...], sv_r[...], b_r[...], a_r[...], d_r[...], g_r[...])
        dpre, ddt, dz, ds0, db, da, dd, dg = vjp((dy_r[...].astype(F32), dst[...]))
        dpre_r[...] = dpre
        ddt_r[...] = ddt.astype(ddt_r.dtype)
        dz_r[...] = dz.astype(dz_r.dtype)
        dst[...] = ds0
        db_r[...] += db
        da_r[...] += da
        dd_r[...] += dd
        dg_r[...] += dg

    return pl.pallas_call(
        body, name=name, grid=(nc,),
        in_specs=[s_pre, s_dt, s_z, s_row, s_dy, _pspec((1, LANES)), _pspec((1, LANES)), _pspec((1, 1024)),
                  _pspec((1, 1024))],
        out_specs=[s_pre, s_dtout, s_row, _pspec((1, LANES)), _pspec((1, LANES)), _pspec((1, 1024)), _pspec((1, 1024))],
        out_shape=[SDS((T, SSD_CONV_CH), F32), SDS((T, LANES), BF16), SDS((T, 1024), BF16),
                   SDS((1, LANES), F32), SDS((1, LANES), F32), SDS((1, 1024), F32), SDS((1, 1024), F32)],
        scratch_shapes=[pltpu.VMEM((SSD_STATE, 1024), F32)],
        compiler_params=pltpu.CompilerParams(dimension_semantics=("arbitrary",), vmem_limit_bytes=VMEM_LIMIT),
    )(pre, proj, proj, states, dyab, dt_bias, a_log, d_exp, norm_g)


def _attn_tile(T, target=512):
    return min(target, T // 2)


def _causal(tq):
    return _iota((tq, tq), 1) <= _iota((tq, tq), 0)


def _job_parts(job):
    if job is None:
        return 0, 0, [], [], []
    ni, no = len(job["inputs"]), len(job["out_shape"])
    return ni, no, list(job["inputs"]), list(job["out_shape"]), list(job["scratch"])


def _job_phase(job, which, refs, when):
    if job is not None and job["phases"][which] is not None:
        pl.when(when)(functools.partial(job["phases"][which], *refs))


def attn_fwd(q, kv, kr, *, name, job=None):
    T = q.shape[0]
    tq = _attn_tile(T)
    nq = T // tq
    hp = ATTN_HEADS_PER_STEP
    ng = MLA_HEADS // hp
    ni, no, jins, jouts, jscratch = _job_parts(job)

    def body(*refs):
        q_ref, kn_ref, v_ref, kr_ref = refs[:4]
        o_ref, lse_ref = refs[4 + ni:6 + ni]
        jrefs = (refs[4:4 + ni], refs[6 + ni:6 + ni + no], refs[6 + ni + no:])
        g = pl.program_id(0)
        qi = pl.program_id(1)
        _job_phase(job, 0, jrefs, (g == 0) & (qi == 0))
        qv = [q_ref[:, e * MLA_QPAD:(e + 1) * MLA_QPAD] for e in range(hp)]

        def blk(ki, masked, carry):
            rows = pl.ds(pl.multiple_of(ki * tq, tq), tq)
            kr = kr_ref[rows, :]
            out = []
            for e in range(hp):
                m, l, acc = carry[e]
                cols = slice(e * 128, (e + 1) * 128)
                k = jnp.concatenate([kn_ref[rows, cols], kr], axis=1)
                s = lax.dot_general(qv[e], k, NT, preferred_element_type=F32)
                if masked:
                    s = jnp.where(_causal(tq), s, -1e30)
                m_new = jnp.maximum(m, jnp.max(s, axis=-1, keepdims=True))
                p = jnp.exp(s - m_new)
                alpha = jnp.exp(m - m_new)
                l = alpha * l + jnp.sum(p, axis=-1, keepdims=True)
                acc = alpha * acc + jnp.dot(p.astype(BF16), v_ref[rows, cols], preferred_element_type=F32)
                out.append((m_new, l, acc))
            return tuple(out)

        one = (jnp.full((tq, 1), -1e30, F32), jnp.zeros((tq, 1), F32), jnp.zeros((tq, 128), F32))
        carry = lax.fori_loop(0, qi, lambda ki, c: blk(ki, False, c), (one,) * hp)
        carry = blk(qi, True, carry)
        for e in range(hp):
            m, l, acc = carry[e]
            cols = slice(e * 128, (e + 1) * 128)
            o_ref[:, cols] = (acc / l).astype(o_ref.dtype)
            lse_ref[:, cols] = jnp.broadcast_to(m + jnp.log(l), (tq, 128))
        _job_phase(job, 2, jrefs, (g == ng - 1) & (qi == nq - 1))

    res = pl.pallas_call(
        body, name=name, grid=(ng, nq),
        in_specs=[pl.BlockSpec((tq, hp * MLA_QPAD), lambda g, i: (i, g)),
                  pl.BlockSpec((T, hp * 128), lambda g, i: (0, g)),
                  pl.BlockSpec((T, hp * 128), lambda g, i: (0, ng + g)),
                  pl.BlockSpec((T, 128), lambda g, i: (0, 0))] + [ANY] * ni,
        out_specs=[pl.BlockSpec((tq, hp * 128), lambda g, i: (i, g)), pl.BlockSpec((tq, hp * 128), lambda g, i: (i, g))]
        + [ANY] * no,
        out_shape=[SDS((T, 2048), BF16), SDS((T, 2048), F32)] + jouts, scratch_shapes=jscratch,
        compiler_params=pltpu.CompilerParams(dimension_semantics=("arbitrary", "arbitrary"),
                                             vmem_limit_bytes=VMEM_LIMIT),
    )(q, kv, kv, kr, *jins)
    return res[0], res[1], list(res[2:])


def attn_bwd(q, kv, kr, do, stats, *, name, job=None):
    T = q.shape[0]
    tq = _attn_tile(T)
    nq = T // tq
    ni, no, jins, jouts, jscratch = _job_parts(job)

    def body(*refs):
        q_ref, do_ref, st_ref, kn_ref, v_ref, kr_ref = refs[:6]
        dq_ref, dkn_ref, dv_ref, dkr_ref = refs[6 + ni:10 + ni]
        jrefs = (refs[6:6 + ni], refs[10 + ni:10 + ni + no], refs[10 + ni + no:])
        h = pl.program_id(0)
        ki = pl.program_id(1)
        _job_phase(job, 0, jrefs, (h == 0) & (ki == 0))
        _job_phase(job, 1, jrefs, (h == MLA_HEADS // 2) & (ki == 0))

        @pl.when(ki == 0)
        def _():
            dq_ref[...] = jnp.zeros_like(dq_ref)

        k = jnp.concatenate([kn_ref[...], kr_ref[...]], axis=1)
        v = v_ref[...]

        def blk(qi, masked, carry):
            dk, dv = carry
            rows = pl.ds(pl.multiple_of(qi * tq, tq), tq)
            qv, dov = q_ref[rows, :], do_ref[rows, :]
            lse, dl = st_ref[rows, 0:1], st_ref[rows, STAT_SPLIT:STAT_SPLIT + 1]
            s = lax.dot_general(qv, k, NT, preferred_element_type=F32)
            if masked:
                s = jnp.where(_causal(tq), s, -1e30)
            p = jnp.exp(s - lse)
            dv = dv + lax.dot_general(p.astype(BF16), dov, TN, preferred_element_type=F32)
            dp = lax.dot_general(dov, v, NT, preferred_element_type=F32)
            ds = (p * (dp - dl)).astype(BF16)
            dk = dk + lax.dot_general(ds, qv, TN, preferred_element_type=F32)
            dq_ref[rows, :] += jnp.dot(ds, k, preferred_element_type=F32)
            return dk, dv

        carry = blk(ki, True, (jnp.zeros((tq, MLA_QPAD), F32), jnp.zeros((tq, 128), F32)))
        rest = nq - 1 - ki

        def group(j, c):
            for u in range(ATTN_UNROLL):
                c = blk(ki + 1 + ATTN_UNROLL * j + u, False, c)
            return c

        carry = lax.fori_loop(0, rest // ATTN_UNROLL, group, carry)
        for u in range(ATTN_UNROLL - 1):
            carry = lax.cond(rest % ATTN_UNROLL > u, functools.partial(lambda c, u: blk(nq - 1 - u, False, c), u=u),
                             lambda c: c, carry)
        dk, dv = carry
        dkn_ref[...] = dk[:, :128].astype(dkn_ref.dtype)
        dkr_ref[...] = dk[:, 128:]
        dv_ref[...] = dv.astype(dv_ref.dtype)
        _job_phase(job, 2, jrefs, (h == MLA_HEADS - 1) & (ki == nq - 1))

    res = pl.pallas_call(
        body, name=name, grid=(MLA_HEADS, nq),
        in_specs=[pl.BlockSpec((T, MLA_QPAD), lambda h, i: (0, h)),
                  pl.BlockSpec((T, 128), lambda h, i: (0, h)),
                  pl.BlockSpec((T, 128), lambda h, i: (0, h)),
                  pl.BlockSpec((tq, 128), lambda h, i: (i, h)),
                  pl.BlockSpec((tq, 128), lambda h, i: (i, MLA_HEADS + h)),
                  pl.BlockSpec((tq, 128), lambda h, i: (i, 0))] + [ANY] * ni,
        out_specs=[pl.BlockSpec((T, MLA_QPAD), lambda h, i: (0, h)),
                   pl.BlockSpec((tq, 128), lambda h, i: (i, h)), pl.BlockSpec((tq, 128), lambda h, i: (i, h)),
                   pl.BlockSpec((tq, 128), lambda h, i: (i, h))] + [ANY] * no,
        out_shape=[SDS((T, MLA_HEADS * MLA_QPAD), F32), SDS((T, 2048), BF16), SDS((T, 2048), BF16),
                   SDS((T, 2048), F32)] + jouts, scratch_shapes=jscratch,
        compiler_params=pltpu.CompilerParams(dimension_semantics=("arbitrary", "arbitrary"),
                                             vmem_limit_bytes=VMEM_LIMIT),
    )(q, do, stats, kv, kv, kr, *jins)
    return res[0], res[1], res[2], res[3], list(res[4:])


def _ffn_fwd(h, pre_g, w_in, w_down, post_g, tag):
    hn = rms_fwd(h, pre_g, name=f"{tag}_pre")
    gu, a = ffn_in_act(hn, w_in, name=f"{tag}_in")
    f = mm(a, w_down, name=f"{tag}_down", tk=1408)
    h2 = post_fwd(h, f, post_g, 0.5, name=f"{tag}_post")
    return h2, (h, hn, gu, a, f)


def _ffn_bwd(dh2, saved, pre_g, w_in, w_down, post_g, tag):
    h, hn, gu, a, f = saved
    df, dpost = post_bwd(f, dh2, post_g, 0.5, name=f"{tag}_post_b")
    dgu = ffn_down_bx_act(df, w_down, gu, name=f"{tag}_down_bx")
    dw_down = mm(a, df, ta=True, out_dtype=BF16, name=f"{tag}_down_bw", tm=1408)
    dhn = mm((dgu, "planes"), w_in, tb=True, name=f"{tag}_in_bx", tk=1408)
    dw_in = mm(hn, (dgu, "planes"), ta=True, out_dtype=BF16, name=f"{tag}_in_bw", tn=1408)
    dh, dpre = rms_bwd(h, dhn, dh2, pre_g, name=f"{tag}_pre_b")
    return dh, dpre, dw_in, dw_down, dpost


def _hyb_fwd(hn, w, tag):
    proj = mm(hn, w["hyb_in"], name=f"{tag}_in")
    ya = gmlp_fwd(proj, w["ln_g"], w["ln_b"], w["w_s"], w["b_st"], name=f"{tag}_gmlp")
    pre = conv_fwd(proj, w["conv_w"], w["conv_b"], name=f"{tag}_conv")
    yb, states = ssd_fwd(pre, proj, w["dt_bias"], w["a_log"], w["d_exp"], w["norm_g"], name=f"{tag}_ssd")
    yab = jnp.concatenate([ya, yb], axis=1)
    mixed = mm(yab, w["hyb_out"], name=f"{tag}_out")
    return mixed, (proj, pre, states, yab)


def _hyb_bwd(dmixed, hn, saved, w, tag):
    proj, pre, states, yab = saved
    dyab = mm(dmixed, w["hyb_out"], tb=True, name=f"{tag}_out_bx")
    dw_out = mm(yab, dmixed, ta=True, out_dtype=BF16, name=f"{tag}_out_bw")
    duv, dln_g, dln_b, dw_s, db_st = gmlp_bwd(proj, dyab, w["ln_g"], w["ln_b"], w["w_s"], w["b_st"],
                                              name=f"{tag}_gmlp_b")
    dpre, ddt, dz, ddt_bias, da_log, dd_exp, dnorm_g = ssd_bwd(
        pre, proj, states, dyab, w["dt_bias"], w["a_log"], w["d_exp"], w["norm_g"], name=f"{tag}_ssd_b")
    dxbc, dconv_w, dconv_b = conv_bwd(dpre, proj, w["conv_w"], name=f"{tag}_conv_b")
    pad = jnp.zeros((duv.shape[0], HYB_PAD - HYB_MAIN - LANES), BF16)
    dproj = jnp.concatenate([duv, dz, dxbc, ddt, pad], axis=1)
    dhn = mm(dproj, w["hyb_in"], tb=True, name=f"{tag}_in_bx")
    dw_in = mm(hn, dproj, ta=True, out_dtype=BF16, name=f"{tag}_in_bw")
    g = dict(hyb_in=dw_in, hyb_out=dw_out, ln_g=dln_g, ln_b=dln_b, w_s=dw_s, b_st=db_st, conv_w=dconv_w,
             conv_b=dconv_b, dt_bias=ddt_bias, a_log=da_log, d_exp=dd_exp, norm_g=dnorm_g)
    return dhn, g


def _mla_fwd(hn, w, rope, tag, job=None):
    cs, sn, c256, s256 = rope
    c_all = mm(hn, w["mla_in"], name=f"{tag}_in")
    cqn, ckvn, kr = kprep_fwd(c_all, cs, sn, w["q_g"], w["kv_g"], name=f"{tag}_kprep")
    qb = mm(cqn, w["uq"], out_dtype=BF16, name=f"{tag}_uq")
    q = qrope_fwd(qb, c256, s256, name=f"{tag}_qrope")
    kv = mm(ckvn, w["ukv"], out_dtype=BF16, name=f"{tag}_ukv")
    o, lse, jouts = attn_fwd(q, kv, kr, name=f"{tag}_attn", job=job)
    mixed = mm(o, w["mla_out"], name=f"{tag}_out")
    return mixed, (c_all, cqn, ckvn, kr, q, kv, o, lse), jouts


def _mla_bwd(dmixed, hn, saved, w, rope, tag, job=None):
    cs, sn, c256, s256 = rope
    c_all, cqn, ckvn, kr, q, kv, o, lse = saved
    do = mm(dmixed, w["mla_out"], tb=True, out_dtype=BF16, name=f"{tag}_out_bx")
    dw_out = mm(o, dmixed, ta=True, out_dtype=BF16, name=f"{tag}_out_bw")
    stats = stats_fwd(do, o, lse, name=f"{tag}_stats")
    dq, dkn, dv, dkr_h, jouts = attn_bwd(q, kv, kr, do, stats, name=f"{tag}_attn_b", job=job)
    dkr = headsum(dkr_h, name=f"{tag}_dkr")
    dkv = jnp.concatenate([dkn, dv], axis=1)
    dckvn = mm(dkv, w["ukv"], tb=True, name=f"{tag}_ukv_bx")
    dw_ukv = mm(ckvn, dkv, ta=True, name=f"{tag}_ukv_bw")
    dqb = qrope_bwd(dq, c256, s256, name=f"{tag}_qrope_b")
    dcqn = mm(dqb, w["uq"], tb=True, name=f"{tag}_uq_bx")
    dw_uq = mm(cqn, dqb, ta=True, name=f"{tag}_uq_bw")
    dc_all, dq_g, dkv_g = kprep_bwd(c_all, cs, sn, dcqn, dckvn, dkr, w["q_g"], w["kv_g"], name=f"{tag}_kprep_b")
    dhn = mm(dc_all, w["mla_in"], tb=True, name=f"{tag}_in_bx")
    dw_in = mm(hn, dc_all, ta=True, name=f"{tag}_in_bw")
    g = dict(mla_in=dw_in, mla_out=dw_out, uq=dw_uq, ukv=dw_ukv, q_g=dq_g, kv_g=dkv_g)
    return dhn, g, jouts


class NoJobs:
    def fwd_job(self, i):
        return None

    def fwd_done(self, i, outs):
        pass

    def bwd_job(self, i, grads, partial):
        return None

    def bwd_done(self, i, outs):
        pass


def local_step(x, p, rope, target, weights_of, jobs=NoJobs()):
    h = x
    saved = []
    lw = []
    for i in range(DEPTH):
        w = weights_of(i)
        lw.append(w)
        t = f"l{i}"
        h, s1 = _ffn_fwd(h, w["ffn1_pre_g"], w["ffn1_w_in"], w["ffn1_w_down"], w["ffn1_post_g"], f"{t}_f1")
        h1 = h
        hn = rms_fwd(h1, w["mix_pre_g"], name=f"{t}_mixpre")
        if i % 2 == 0:
            mixed, sm = _hyb_fwd(hn, w, f"{t}_hyb")
        else:
            job = jobs.fwd_job(i)
            mixed, sm, jouts = _mla_fwd(hn, w, rope, f"{t}_mla", job)
            if job is not None:
                jobs.fwd_done(i, jouts)
                w = lw[i] = weights_of(i)
        h = post_fwd(h1, mixed, w["mix_post_g"], 1.0, name=f"{t}_mixpost")
        h, s2 = _ffn_fwd(h, w["ffn2_pre_g"], w["ffn2_w_in"], w["ffn2_w_down"], w["ffn2_post_g"], f"{t}_f2")
        h3 = h
        hn3 = rms_fwd(h3, w["ple_pre_g"], name=f"{t}_plepre")
        gl = mm(hn3, w["ple_w_gate"], name=f"{t}_plegate")
        pp = mm((p, i), w["ple_w_proj"], name=f"{t}_pleproj")
        h = ple_fwd(h3, gl, pp, w["ple_post_g"], name=f"{t}_plepost")
        saved.append((s1, h1, hn, sm, mixed, s2, h3, hn3, gl, pp))

    dh, sq = loss_fwd(h, target, name="loss")
    grads = [None] * DEPTH
    for i in reversed(range(DEPTH)):
        w = lw[i]
        t = f"l{i}"
        s1, h1, hn, sm, mixed, s2, h3, hn3, gl, pp = saved[i]
        g = {}
        dgl, dpp, g["ple_post_g"] = ple_bwd(gl, pp, dh, w["ple_post_g"], name=f"{t}_plepost_b")
        dhn3 = mm(dgl, w["ple_w_gate"], tb=True, name=f"{t}_plegate_bx")
        g["ple_w_gate"] = mm(hn3, dgl, ta=True, out_dtype=BF16, name=f"{t}_plegate_bw")
        g["ple_w_proj"] = mm((p, i), dpp, ta=True, out_dtype=BF16, name=f"{t}_pleproj_bw")
        dh, g["ple_pre_g"] = rms_bwd(h3, dhn3, dh, w["ple_pre_g"], name=f"{t}_plepre_b")
        dh, g["ffn2_pre_g"], g["ffn2_w_in"], g["ffn2_w_down"], g["ffn2_post_g"] = _ffn_bwd(
            dh, s2, w["ffn2_pre_g"], w["ffn2_w_in"], w["ffn2_w_down"], w["ffn2_post_g"], f"{t}_f2")
        dmixed, g["mix_post_g"] = post_bwd(mixed, dh, w["mix_post_g"], 1.0, name=f"{t}_mixpost_b")
        if i % 2 == 0:
            dhn, gm = _hyb_bwd(dmixed, hn, sm, w, f"{t}_hyb")
        else:
            job = jobs.bwd_job(i, grads, g)
            dhn, gm, jouts = _mla_bwd(dmixed, hn, sm, w, rope, f"{t}_mla", job)
            if job is not None:
                jobs.bwd_done(i, jouts)
        g.update(gm)
        dh, g["mix_pre_g"] = rms_bwd(h1, dhn, dh, w["mix_pre_g"], name=f"{t}_mixpre_b")
        dh, g["ffn1_pre_g"], g["ffn1_w_in"], g["ffn1_w_down"], g["ffn1_post_g"] = _ffn_bwd(
            dh, s1, w["ffn1_pre_g"], w["ffn1_w_in"], w["ffn1_w_down"], w["ffn1_post_g"], f"{t}_f1")
        grads[i] = g
    return sq, dh, grads


def _zeros_like_cols(a, n):
    return jnp.zeros(a.shape[:-1] + (n,), a.dtype)


def layer_weights(full, i):
    j = i // 2
    row = lambda v: v.reshape(1, -1)
    w = {k: row(full[k][i]) for k in ("ffn1_pre_g", "ffn1_post_g", "mix_pre_g", "mix_post_g", "ffn2_pre_g",
                                      "ffn2_post_g", "ple_pre_g", "ple_post_g")}
    for k in ("ffn1_w_in", "ffn1_w_down", "ffn2_w_in", "ffn2_w_down", "ple_w_gate", "ple_w_proj"):
        w[k] = full[k][i]
    if i % 2 == 0:
        hw = full["hyb_w_in"][j]
        w["hyb_in"] = jnp.concatenate([hw, _zeros_like_cols(hw, HYB_PAD - HYB_IN)], axis=1)
        w["hyb_out"] = full["hyb_w_out"][j]
        w["ln_g"], w["ln_b"] = row(full["gm_ln_g"][j]), row(full["gm_ln_b"][j])
        w["w_s"] = full["gm_w_s"][j].reshape(GM_HEADS * CHUNK, CHUNK)
        w["b_st"] = jnp.pad(full["gm_b_s"][j].T, ((0, 0), (0, LANES - GM_HEADS)))
        w["conv_w"], w["conv_b"] = full["ssd_conv_w"][j], row(full["ssd_conv_b"][j])
        pad16 = lambda v: jnp.pad(v.reshape(1, -1), ((0, 0), (0, LANES - SSD_HEADS)))
        w["dt_bias"], w["a_log"] = pad16(full["ssd_dt_bias"][j]), pad16(full["ssd_a_log"][j])
        w["d_exp"] = row(jnp.repeat(full["ssd_d"][j], SSD_HEAD_DIM))
        w["norm_g"] = row(full["ssd_norm_g"][j])
    else:
        wi = full["mla_w_in"][j]
        z64 = _zeros_like_cols(wi, 64)
        w["mla_in"] = jnp.concatenate([wi[:, :384], wi[:, 384:448], z64, -wi[:, 416:448], wi[:, 384:416], z64], axis=1)
        uq = full["mla_w_uq"][j].reshape(MLA_Q_LORA, MLA_HEADS, MLA_QK)
        zq = jnp.zeros((MLA_Q_LORA, MLA_HEADS, 64), uq.dtype)
        pad_part = jnp.concatenate([uq, zq], axis=2)
        swp_part = jnp.concatenate([jnp.zeros_like(uq[:, :, :128]), -uq[:, :, 160:192], uq[:, :, 128:160], zq], axis=2)
        w["uq"] = jnp.concatenate([pad_part.reshape(MLA_Q_LORA, -1), swp_part.reshape(MLA_Q_LORA, -1)], axis=1)
        ukv = full["mla_w_ukv"][j].reshape(MLA_KV_LORA, MLA_HEADS, 256)
        w["ukv"] = jnp.concatenate([ukv[:, :, :128].reshape(MLA_KV_LORA, -1), ukv[:, :, 128:].reshape(MLA_KV_LORA, -1)],
                                   axis=1)
        w["mla_out"] = full["mla_w_out"][j]
        w["q_g"], w["kv_g"] = row(full["mla_q_norm_g"][j]), row(full["mla_kv_norm_g"][j])
    return w


def full_grads(grads):
    out = {}
    stack = lambda k, idx: jnp.stack([grads[i][k] for i in idx])
    every, even, odd = range(DEPTH), range(0, DEPTH, 2), range(1, DEPTH, 2)
    for k in ("ffn1_pre_g", "ffn1_post_g", "mix_pre_g", "mix_post_g", "ffn2_pre_g", "ffn2_post_g", "ple_pre_g",
              "ple_post_g"):
        out[k] = stack(k, every).reshape(DEPTH, D_MODEL)
    for k in ("ffn1_w_in", "ffn1_w_down", "ffn2_w_in", "ffn2_w_down", "ple_w_gate", "ple_w_proj"):
        out[k] = stack(k, every)
    out["hyb_w_in"] = stack("hyb_in", even)[:, :, :HYB_IN]
    out["hyb_w_out"] = stack("hyb_out", even)
    out["gm_ln_g"] = stack("ln_g", even).reshape(2, 1024)
    out["gm_ln_b"] = stack("ln_b", even).reshape(2, 1024)
    out["gm_w_s"] = stack("w_s", even).reshape(2, GM_HEADS, CHUNK, CHUNK)
    out["gm_b_s"] = jnp.swapaxes(stack("b_st", even)[:, :, :GM_HEADS], 1, 2)
    out["ssd_conv_w"] = stack("conv_w", even)
    out["ssd_conv_b"] = stack("conv_b", even).reshape(2, SSD_CONV_CH)
    out["ssd_dt_bias"] = stack("dt_bias", even)[:, 0, :SSD_HEADS]
    out["ssd_a_log"] = stack("a_log", even)[:, 0, :SSD_HEADS]
    out["ssd_d"] = stack("d_exp", even).reshape(2, SSD_HEADS, SSD_HEAD_DIM).sum(axis=-1)
    out["ssd_norm_g"] = stack("norm_g", even).reshape(2, 1024)
    dwi = stack("mla_in", odd)
    out["mla_w_in"] = jnp.concatenate([dwi[:, :, :384], dwi[:, :, 384:416] + dwi[:, :, 544:576],
                                       dwi[:, :, 416:448] - dwi[:, :, 512:544]], axis=2)
    duq = stack("uq", odd)
    half = MLA_HEADS * MLA_QPAD
    dp = duq[:, :, :half].reshape(2, MLA_Q_LORA, MLA_HEADS, MLA_QPAD)
    ds = duq[:, :, half:].reshape(2, MLA_Q_LORA, MLA_HEADS, MLA_QPAD)
    out["mla_w_uq"] = jnp.concatenate([dp[..., :128], dp[..., 128:160] + ds[..., 160:192],
                                       dp[..., 160:192] - ds[..., 128:160]], axis=-1).reshape(2, MLA_Q_LORA, -1)
    dukv = stack("ukv", odd)
    dk = dukv[:, :, :2048].reshape(2, MLA_KV_LORA, MLA_HEADS, 128)
    dv = dukv[:, :, 2048:].reshape(2, MLA_KV_LORA, MLA_HEADS, 128)
    out["mla_w_ukv"] = jnp.concatenate([dk, dv], axis=-1).reshape(2, MLA_KV_LORA, -1)
    out["mla_w_out"] = stack("mla_out", odd)
    out["mla_q_norm_g"] = stack("q_g", odd).reshape(2, MLA_Q_LORA)
    out["mla_kv_norm_g"] = stack("kv_g", odd).reshape(2, MLA_KV_LORA)
    return out


def rope_tables(positions):
    T = positions.shape[0]
    inv = 1.0 / (ROPE_BASE ** (jnp.arange(0, MLA_ROPE, 2, dtype=F32) / MLA_ROPE))
    ang = positions.astype(F32)[:, None] * inv
    cos, sin = jnp.cos(ang), jnp.sin(ang)
    z64 = jnp.zeros((T, 64), F32)
    cs = jnp.concatenate([cos, cos, z64], axis=1)
    sn = jnp.concatenate([sin, sin, z64], axis=1)
    c256 = jnp.concatenate([jnp.ones((T, 128), F32), cs], axis=1)
    s256 = jnp.concatenate([jnp.zeros((T, 128), F32), sn], axis=1)
    return cs, sn, c256, s256


def _rows(n):
    return -(-n // LANES)


def _pack(pieces, dtype, row_multiple):
    flat = []
    total = 0
    for a in pieces:
        v = a.reshape(-1).astype(dtype)
        padn = _rows(v.shape[0]) * LANES - v.shape[0]
        if padn:
            v = jnp.concatenate([v, jnp.zeros((padn,), dtype)])
        flat.append(v)
        total += v.shape[0] // LANES
    tail = -total % row_multiple
    if tail:
        flat.append(jnp.zeros((tail * LANES,), dtype))
    return jnp.concatenate(flat).reshape(-1, LANES)


def _unpack(slab, shapes):
    out = []
    r = 0
    for s in shapes:
        n = int(np.prod(s))
        nr = _rows(n)
        out.append(slab[r:r + nr].reshape(-1)[:n].reshape(s))
        r += nr
    return out


def _shard_shape(shape, ax):
    if ax is None:
        return tuple(shape)
    s = list(shape)
    s[ax] //= N_CHIPS
    return tuple(s)


def _chip_slice(a, ax, k):
    if ax is None:
        return a
    n = a.shape[ax] // N_CHIPS
    return lax.slice_in_dim(a, k * n, (k + 1) * n, axis=ax)


def _plane_peers():
    x, y, c = lax.axis_index("x"), lax.axis_index("y"), lax.axis_index("c")
    return (x, y, c), [(1 - x, y, c), (x, 1 - y, c), (1 - x, 1 - y, c)]


ANY = pl.BlockSpec(memory_space=pl.ANY)


def plane_allgather(slab):
    R = slab.shape[0]

    def body(src, out, send_sems, recv_sems, local_sem):
        (x, y, c), peers = _plane_peers()
        me = 2 * x + y
        local = pltpu.make_async_copy(src, out.at[me], local_sem)
        local.start()
        copies = []
        for j, peer in enumerate(peers):
            cp = pltpu.make_async_remote_copy(src_ref=src, dst_ref=out.at[me], send_sem=send_sems.at[j],
                                              recv_sem=recv_sems.at[j], device_id=peer, device_id_type=MESH)
            cp.start()
            copies.append(cp)
        for cp in copies:
            cp.wait()
        local.wait()

    return pl.pallas_call(
        body, name="plane_allgather", out_shape=SDS((N_CHIPS, R, LANES), slab.dtype),
        in_specs=[ANY], out_specs=ANY,
        scratch_shapes=[pltpu.SemaphoreType.DMA((3,)), pltpu.SemaphoreType.DMA((3,)), pltpu.SemaphoreType.DMA],
    )(slab)


def plane_alltoall(buf):
    R = buf.shape[1]

    def body(src, out, send_sems, recv_sems, local_sem):
        (x, y, c), peers = _plane_peers()
        me = 2 * x + y
        local = pltpu.make_async_copy(src.at[me], out.at[me], local_sem)
        local.start()
        copies = []
        for j, peer in enumerate(peers):
            cp = pltpu.make_async_remote_copy(src_ref=src.at[2 * peer[0] + peer[1]], dst_ref=out.at[me],
                                              send_sem=send_sems.at[j], recv_sem=recv_sems.at[j], device_id=peer,
                                              device_id_type=MESH)
            cp.start()
            copies.append(cp)
        for cp in copies:
            cp.wait()
        local.wait()

    return pl.pallas_call(
        body, name="plane_alltoall", out_shape=SDS((N_CHIPS, R, LANES), buf.dtype),
        in_specs=[ANY], out_specs=ANY,
        scratch_shapes=[pltpu.SemaphoreType.DMA((3,)), pltpu.SemaphoreType.DMA((3,)), pltpu.SemaphoreType.DMA],
    )(buf)


def sibling_swap(buf):
    def body(src, out, send_sem, recv_sem):
        x, y, c = lax.axis_index("x"), lax.axis_index("y"), lax.axis_index("c")
        cp = pltpu.make_async_remote_copy(src_ref=src, dst_ref=out, send_sem=send_sem, recv_sem=recv_sem,
                                          device_id=(x, y, 1 - c), device_id_type=MESH)
        cp.start()
        cp.wait()

    return pl.pallas_call(
        body, name="sibling_swap", out_shape=SDS(buf.shape, buf.dtype), in_specs=[ANY], out_specs=ANY,
        scratch_shapes=[pltpu.SemaphoreType.DMA, pltpu.SemaphoreType.DMA],
    )(buf)


def _chip_block(ref, ax, k, n, stride=None):
    stride = n if stride is None else stride
    start = pl.multiple_of(k * stride, math.gcd(n, stride))
    return ref.at[:, pl.ds(start, n), :] if ax == 1 else ref.at[:, :, pl.ds(start, n)]


def gather_job(shards, axes):
    n = len(shards)
    fulls = []
    for s, ax in zip(shards, axes):
        if ax == "stack":
            fulls.append(SDS((N_CHIPS,) + tuple(s.shape), s.dtype))
            continue
        shape = list(s.shape)
        shape[ax] *= N_CHIPS
        fulls.append(SDS(tuple(shape), s.dtype))

    def copies(srcs, outs, sems):
        send_sems, recv_sems, local_sems = sems
        (x, y, c), peers = _plane_peers()
        me = 2 * x + y
        cps = []
        for t in range(n):
            if axes[t] == "stack":
                dst = outs[t].at[me]
            else:
                dst = _chip_block(outs[t], axes[t], me, srcs[t].shape[axes[t]])
            cps.append(pltpu.make_async_copy(srcs[t], dst, local_sems.at[t]))
            for j, peer in enumerate(peers):
                cps.append(pltpu.make_async_remote_copy(src_ref=srcs[t], dst_ref=dst, send_sem=send_sems.at[3 * t + j],
                                                        recv_sem=recv_sems.at[3 * t + j], device_id=peer,
                                                        device_id_type=MESH))
        return cps

    def start(srcs, outs, sems):
        for cp in copies(srcs, outs, sems):
            cp.start()

    def finish(srcs, outs, sems):
        for cp in copies(srcs, outs, sems):
            cp.wait()

    scratch = [pltpu.SemaphoreType.DMA((3 * n,)), pltpu.SemaphoreType.DMA((3 * n,)), pltpu.SemaphoreType.DMA((n,))]
    return dict(inputs=list(shards), out_shape=fulls, scratch=scratch, phases=(start, None, finish))


def split_gather_job(shards, axes):
    n = len(shards)
    plain = gather_job(shards, axes)

    def region(out, t, chip, shard_shape, half_of):
        ax = axes[t]
        ref = out.at[chip] if ax == "stack" else out
        starts, sizes = [0, 0, 0], list(shard_shape)
        if ax != "stack":
            starts[ax] = chip * shard_shape[ax]
        if half_of is not None:
            sa = 0 if shard_shape[0] % 2 == 0 else 1
            sizes[sa] = shard_shape[sa] // 2
            starts[sa] = starts[sa] + half_of * sizes[sa]
        starts = [s if isinstance(s, int) else pl.multiple_of(s, math.gcd(full, z))
                  for s, z, full in zip(starts, sizes, shard_shape)]
        return ref.at[tuple(pl.ds(s, z) for s, z in zip(starts, sizes))]

    def copies(srcs, outs, sems):
        send_sems, recv_sems, local_sems, fsend_sems, frecv_sems = sems
        (x, y, c), peers = _plane_peers()
        sibling = (x, y, 1 - c)
        me = 2 * x + y
        locals_, sends, forwards = [], [], []
        for t in range(n):
            shape = srcs[t].shape
            locals_.append(pltpu.make_async_copy(srcs[t], region(outs[t], t, me, shape, None), local_sems.at[t]))
            sa = 0 if shape[0] % 2 == 0 else 1
            hs = shape[sa] // 2
            mine = srcs[t].at[pl.ds(c * hs, hs)] if sa == 0 else srcs[t].at[:, pl.ds(c * hs, hs), :]
            for j, (px, py, _) in enumerate(peers):
                sends.append(pltpu.make_async_remote_copy(
                    src_ref=mine, dst_ref=region(outs[t], t, me, shape, c), send_sem=send_sems.at[3 * t + j],
                    recv_sem=recv_sems.at[3 * t + j], device_id=(px, py, c), device_id_type=MESH))
                landed = region(outs[t], t, 2 * px + py, shape, c)
                forwards.append(pltpu.make_async_remote_copy(
                    src_ref=landed, dst_ref=landed, send_sem=fsend_sems.at[3 * t + j],
                    recv_sem=frecv_sems.at[3 * t + j], device_id=sibling, device_id_type=MESH))
        return locals_, sends, forwards

    def start(srcs, outs, sems):
        locals_, sends, _ = copies(srcs, outs, sems)
        for cp in locals_ + sends:
            cp.start()

    def middle(srcs, outs, sems):
        _, sends, forwards = copies(srcs, outs, sems)
        for cp, fw in zip(sends, forwards):
            cp.wait_recv()
            fw.start()

    def finish(srcs, outs, sems):
        locals_, sends, forwards = copies(srcs, outs, sems)
        for cp in locals_:
            cp.wait()
        for cp in sends:
            cp.wait_send()
        for fw in forwards:
            fw.wait()

    scratch = [pltpu.SemaphoreType.DMA((3 * n,)) for _ in range(2)] + [pltpu.SemaphoreType.DMA((n,))] + \
              [pltpu.SemaphoreType.DMA((3 * n,)) for _ in range(2)]
    return dict(inputs=list(shards), out_shape=plain["out_shape"], scratch=scratch, phases=(start, middle, finish))


def exchange_job(grads, axes):
    n = len(grads)
    outs = []
    spans = []
    for g, ax in zip(grads, axes):
        ax, width, stride = ax if isinstance(ax, tuple) else (ax, g.shape[ax] // N_CHIPS, None)
        spans.append((ax, width, stride))
        shape = list(g.shape)
        shape[ax] = width
        outs.append(SDS((N_CHIPS,) + tuple(shape), g.dtype))

    def copies(srcs, res, sems):
        mine, theirs = res[:n], res[n:]
        send_sems, recv_sems, local_sems, fsend_sems, frecv_sems = sems
        (x, y, c), peers = _plane_peers()
        sibling = (x, y, 1 - c)
        me = 2 * x + y
        blocks = [me] + [2 * px + py for (px, py, _) in peers]
        locals_, sends, forwards = [], [], []
        for t in range(n):
            ax, width, stride = spans[t]
            locals_.append(pltpu.make_async_copy(_chip_block(srcs[t], ax, me, width, stride), mine[t].at[me],
                                                 local_sems.at[t]))
            for j, peer in enumerate(peers):
                sends.append(pltpu.make_async_remote_copy(
                    src_ref=_chip_block(srcs[t], ax, blocks[j + 1], width, stride), dst_ref=mine[t].at[me],
                    send_sem=send_sems.at[3 * t + j], recv_sem=recv_sems.at[3 * t + j], device_id=peer,
                    device_id_type=MESH))
            for q, blk in enumerate(blocks):
                forwards.append(pltpu.make_async_remote_copy(
                    src_ref=mine[t].at[blk], dst_ref=theirs[t].at[blk], send_sem=fsend_sems.at[4 * t + q],
                    recv_sem=frecv_sems.at[4 * t + q], device_id=sibling, device_id_type=MESH))
        return locals_, sends, forwards

    def start(srcs, res, sems):
        locals_, sends, _ = copies(srcs, res, sems)
        for cp in locals_ + sends:
            cp.start()

    def middle(srcs, res, sems):
        locals_, sends, forwards = copies(srcs, res, sems)
        for t in range(n):
            locals_[t].wait()
            for q in range(N_CHIPS):
                if q > 0:
                    sends[3 * t + q - 1].wait_recv()
                forwards[4 * t + q].start()

    def finish(srcs, res, sems):
        _, sends, forwards = copies(srcs, res, sems)
        for cp in sends:
            cp.wait_send()
        for fw in forwards:
            fw.wait()

    scratch = [pltpu.SemaphoreType.DMA((3 * n,)), pltpu.SemaphoreType.DMA((3 * n,)), pltpu.SemaphoreType.DMA((n,)),
               pltpu.SemaphoreType.DMA((4 * n,)), pltpu.SemaphoreType.DMA((4 * n,))]
    return dict(inputs=list(grads), out_shape=outs + outs, scratch=scratch, phases=(start, middle, finish))


def run_job(job, ins, outs, sems, first=None, mid=None, last=None):
    for phase, when in zip(job["phases"], (first, mid, last)):
        if phase is None:
            continue
        if when is None:
            phase(ins, outs, sems)
        else:
            pl.when(when)(functools.partial(phase, ins, outs, sems))


def job_call(job, *, name):
    ni, no = len(job["inputs"]), len(job["out_shape"])

    def body(*refs):
        run_job(job, refs[:ni], refs[ni:ni + no], refs[ni + no:])

    return pl.pallas_call(body, name=name, out_shape=job["out_shape"], in_specs=[ANY] * ni, out_specs=[ANY] * no,
                          scratch_shapes=job["scratch"])(*job["inputs"])


def _adam_update(g, w, m, v):
    mn = ADAM_B1 * m + (1.0 - ADAM_B1) * g
    vn = ADAM_B2 * v + (1.0 - ADAM_B2) * jnp.square(g)
    m_hat = mn / (1.0 - ADAM_B1 ** ADAM_STEP)
    v_hat = vn / (1.0 - ADAM_B2 ** ADAM_STEP)
    return -ADAM_LR * (m_hat / (jnp.sqrt(v_hat) + ADAM_EPS) + ADAM_WD * w), mn, vn


def _row_tile(rs):
    for cand in range(256, 15, -16):
        if rs % cand == 0:
            return cand
    return rs


def _group_sum(half, a0_r, b0_r, a1_r, b1_r):
    def plane_sums(a_r, b_r):
        pa = a_r[0].astype(F32)
        pb = b_r[0].astype(F32)
        for k in range(1, N_CHIPS):
            pa = pa + a_r[k].astype(F32)
            pb = pb + b_r[k].astype(F32)
        return pa + pb
    return jnp.where(pl.program_id(0) < half, plane_sums(a0_r, b0_r), plane_sums(a1_r, b1_r))


def _group_specs(half, tr, cs):
    first = pl.BlockSpec((N_CHIPS, None, tr, cs),
                         lambda l, i: (0, jnp.minimum(l, half - 1), jnp.where(l < half, i, 0), 0))
    second = pl.BlockSpec((N_CHIPS, None, tr, cs),
                          lambda l, i: (0, jnp.maximum(l - half, 0), jnp.where(l < half, 0, i), 0))
    return [first, first, second, second]


def adamw_reg(groups, w, m, v, *, name):
    L, rs, cs = w.shape
    tr = _row_tile(rs)
    early = groups[0][0].shape[1]

    def body(a0_r, b0_r, a1_r, b1_r, w_r, m_r, v_r, g_o, d_o, m_o, v_o):
        g = _group_sum(early, a0_r, b0_r, a1_r, b1_r)
        d, mn, vn = _adam_update(g, w_r[...], m_r[...], v_r[...])
        g_o[...] = g
        d_o[...] = d
        m_o[...] = mn
        v_o[...] = vn

    s1 = pl.BlockSpec((None, tr, cs), lambda l, i: (l, i, 0))
    (a0, b0), (a1, b1) = groups
    return pl.pallas_call(
        body, name=name, grid=(L, rs // tr), in_specs=_group_specs(early, tr, cs) + [s1, s1, s1], out_specs=[s1] * 4,
        out_shape=[SDS((L, rs, cs), F32)] * 4,
        compiler_params=pltpu.CompilerParams(dimension_semantics=("parallel", "parallel"),
                                             vmem_limit_bytes=VMEM_LIMIT),
    )(a0, b0, a1, b1, w, m, v)


def groups_sum(groups, *, name):
    (a0, b0), (a1, b1) = groups
    _, half, rs, cs = a0.shape
    L = half + a1.shape[1]
    tr = _row_tile(rs)

    def body(a0_r, b0_r, a1_r, b1_r, g_o):
        g_o[...] = _group_sum(half, a0_r, b0_r, a1_r, b1_r)

    return pl.pallas_call(
        body, name=name, grid=(L, rs // tr), in_specs=_group_specs(half, tr, cs),
        out_specs=pl.BlockSpec((None, tr, cs), lambda l, i: (l, i, 0)), out_shape=SDS((L, rs, cs), F32),
        compiler_params=pltpu.CompilerParams(dimension_semantics=("parallel", "parallel"),
                                             vmem_limit_bytes=VMEM_LIMIT),
    )(a0, b0, a1, b1)


def adamw_plain(g, w, m, v, *, name):
    L, rs, cs = w.shape
    tr = _row_tile(rs)

    def body(g_r, w_r, m_r, v_r, d_o, m_o, v_o):
        d, mn, vn = _adam_update(g_r[...], w_r[...], m_r[...], v_r[...])
        d_o[...] = d
        m_o[...] = mn
        v_o[...] = vn

    s1 = pl.BlockSpec((None, tr, cs), lambda l, i: (l, i, 0))
    return pl.pallas_call(
        body, name=name, grid=(L, rs // tr), in_specs=[s1] * 4, out_specs=[s1] * 3,
        out_shape=[SDS((L, rs, cs), F32)] * 3,
        compiler_params=pltpu.CompilerParams(dimension_semantics=("parallel", "parallel"),
                                             vmem_limit_bytes=VMEM_LIMIT),
    )(g, w, m, v)


def plane_sum(r4):
    R = r4.shape[1]

    def body(r_ref, o_ref):
        acc = r_ref[0].astype(F32)
        for k in range(1, N_CHIPS):
            acc = acc + r_ref[k].astype(F32)
        o_ref[...] = acc

    return pl.pallas_call(
        body, name="plane_sum", grid=(R // PACK_ROWS,),
        in_specs=[pl.BlockSpec((N_CHIPS, PACK_ROWS, LANES), lambda i: (0, i, 0))],
        out_specs=pl.BlockSpec((PACK_ROWS, LANES), lambda i: (i, 0)), out_shape=SDS((R, LANES), F32),
        compiler_params=pltpu.CompilerParams(dimension_semantics=("parallel",)),
    )(r4)


def adamw(pa, pb, w, m, v):
    R = w.shape[0]

    def body(pa_r, pb_r, w_r, m_r, v_r, g_o, d_o, m_o, v_o):
        g = pa_r[...] + pb_r[...]
        d, mn, vn = _adam_update(g, w_r[...], m_r[...], v_r[...])
        g_o[...] = g
        d_o[...] = d
        m_o[...] = mn
        v_o[...] = vn

    spec = pl.BlockSpec((PACK_ROWS, LANES), lambda i: (i, 0))
    return pl.pallas_call(
        body, name="adamw", grid=(R // PACK_ROWS,), in_specs=[spec] * 5, out_specs=[spec] * 4,
        out_shape=[SDS((R, LANES), F32)] * 4,
        compiler_params=pltpu.CompilerParams(dimension_semantics=("parallel",)),
    )(pa, pb, w, m, v)


GATHER_AXES = [WSPEC[n][2] for n in REG] + ["stack"]
EXCHANGE_AXES = [WSPEC[n][2] for n in REG] + [(2, WIN_WIDTH, WIN_STRIDE)]


EARLY = {n: WSPEC[n][1][0] // 2 for n in MOVED}
EARLY.update(ffn2_w_in=1, ffn2_w_down=1, ple_w_gate=1, ple_w_proj=1)


def _group_range(n, group):
    return range(EARLY[n]) if group == 0 else range(EARLY[n], WSPEC[n][1][0])


def moved_shards(wl, group):
    out = []
    for n in MOVED:
        r = _group_range(n, group)
        out.append(wl[n][r.start:r.stop].astype(BF16))
    return out


def moved_grads(grads, group, partial=None):
    out = []
    for n in MOVED:
        if WSPEC[n][1][0] == DEPTH:
            layer_of, key = (lambda s: s), n
        elif n == "mla_w_out":
            layer_of, key = (lambda s: 2 * s + 1), "mla_out"
        else:
            layer_of, key = (lambda s: 2 * s), {"hyb_w_out": "hyb_out", WIN: "hyb_in"}[n]
        per_layer = [grads[layer_of(s)] if grads[layer_of(s)] is not None else partial for s in _group_range(n, group)]
        out.append(jnp.stack([d[key] for d in per_layer]))
    return out


def gather_misc(wl):
    full = {}
    sharded = [n for n in MISC if WSPEC[n][2] is not None]
    pieces = [wl[n].astype(BF16) if WSPEC[n][3] else lax.bitcast_convert_type(wl[n], BF16) for n in sharded]
    got = plane_allgather(_pack(pieces, BF16, 16))
    shapes = [v.shape for v in pieces]
    per_chip = [_unpack(got[k], shapes) for k in range(N_CHIPS)]
    for idx, n in enumerate(sharded):
        parts = [per_chip[k][idx] for k in range(N_CHIPS)]
        if not WSPEC[n][3]:
            parts = [lax.bitcast_convert_type(v, F32) for v in parts]
        full[n] = jnp.concatenate(parts, axis=WSPEC[n][2])
    for n in MISC:
        if WSPEC[n][2] is None:
            full[n] = wl[n]
    return full


def kernel(x, p, positions, ffn1_pre_g, ffn1_w_in, ffn1_w_down, ffn1_post_g, mix_pre_g, mix_post_g, ffn2_pre_g, ffn2_w_in, ffn2_w_down, ffn2_post_g, ple_pre_g, ple_w_gate, ple_w_proj, ple_post_g, hyb_w_in, gm_ln_g, gm_ln_b, gm_w_s, gm_b_s, ssd_conv_w, ssd_conv_b, ssd_dt_bias, ssd_a_log, ssd_d, ssd_norm_g, hyb_w_out, mla_w_in, mla_q_norm_g, mla_kv_norm_g, mla_w_uq, mla_w_ukv, mla_w_out, loss_target, m_ffn1_pre_g, m_ffn1_w_in, m_ffn1_w_down, m_ffn1_post_g, m_mix_pre_g, m_mix_post_g, m_ffn2_pre_g, m_ffn2_w_in, m_ffn2_w_down, m_ffn2_post_g, m_ple_pre_g, m_ple_w_gate, m_ple_w_proj, m_ple_post_g, m_hyb_w_in, m_gm_ln_g, m_gm_ln_b, m_gm_w_s, m_gm_b_s, m_ssd_conv_w, m_ssd_conv_b, m_ssd_dt_bias, m_ssd_a_log, m_ssd_d, m_ssd_norm_g, m_hyb_w_out, m_mla_w_in, m_mla_q_norm_g, m_mla_kv_norm_g, m_mla_w_uq, m_mla_w_ukv, m_mla_w_out, v_ffn1_pre_g, v_ffn1_w_in, v_ffn1_w_down, v_ffn1_post_g, v_mix_pre_g, v_mix_post_g, v_ffn2_pre_g, v_ffn2_w_in, v_ffn2_w_down, v_ffn2_post_g, v_ple_pre_g, v_ple_w_gate, v_ple_w_proj, v_ple_post_g, v_hyb_w_in, v_gm_ln_g, v_gm_ln_b, v_gm_w_s, v_gm_b_s, v_ssd_conv_w, v_ssd_conv_b, v_ssd_dt_bias, v_ssd_a_log, v_ssd_d, v_ssd_norm_g, v_hyb_w_out, v_mla_w_in, v_mla_q_norm_g, v_mla_kv_norm_g, v_mla_w_uq, v_mla_w_ukv, v_mla_w_out):
    args = locals()
    wl = {n: args[n] for n in WNAMES}
    ml = {n: args["m_" + n] for n in WNAMES}
    vl = {n: args["v_" + n] for n in WNAMES}

    full = gather_misc(wl)
    nmoved = len(MOVED)
    for n in MOVED:
        full[n] = [None] * WSPEC[n][1][0]
    exchanged = [None, None]

    def gathered(group, outs):
        for t, n in enumerate(MOVED):
            for l, s in enumerate(_group_range(n, group)):
                if n == WIN:
                    full[n][s] = jnp.concatenate([outs[t][k, l] for k in range(N_CHIPS)], axis=-1)
                else:
                    full[n][s] = (outs[t], l)

    gathered(0, job_call(split_gather_job(moved_shards(wl, 0), GATHER_AXES), name="gather_first"))

    class Jobs(NoJobs):
        def fwd_job(self, i):
            return gather_job(moved_shards(wl, 1), GATHER_AXES) if i == 1 else None

        def fwd_done(self, i, outs):
            gathered(1, outs)

        def bwd_job(self, i, grads, partial):
            return exchange_job(moved_grads(grads, 1, partial), EXCHANGE_AXES) if i == 1 else None

        def bwd_done(self, i, outs):
            exchanged[1] = outs

    rope = rope_tables(positions[0])
    T = x.shape[1]
    sq, dx, grads = local_step(x[0], p.reshape(DEPTH, T, p.shape[-1]), rope, loss_target[0],
                               lambda i: layer_weights(full, i), Jobs())
    loss = lax.psum(0.5 * jnp.sum(sq) / D_MODEL, ("x", "y", "c"))

    res = {}
    exchanged[0] = job_call(exchange_job(moved_grads(grads, 0), EXCHANGE_AXES), name="exchange_first")
    for t, n in enumerate(MOVED):
        groups = [(e[t], e[nmoved + t]) for e in exchanged]
        if n == WIN:
            window = groups_sum(groups, name=f"sum_{n}")
            chip = 2 * lax.axis_index("x") + lax.axis_index("y")
            g = lax.dynamic_slice_in_dim(window, chip * (WIN_SHARD - WIN_STRIDE), WIN_SHARD, axis=2)
            res[n] = [g] + list(adamw_plain(g, wl[n], ml[n], vl[n], name=f"adamw_{n}"))
        else:
            res[n] = adamw_reg(groups, wl[n], ml[n], vl[n], name=f"adamw_{n}")
    fg = full_grads(grads)
    dest = [_pack([_chip_slice(fg[n], WSPEC[n][2], k) for n in MISC], BF16, PACK_ROWS) for k in range(N_CHIPS)]
    mine = plane_sum(plane_alltoall(jnp.stack(dest)))
    other = sibling_swap(mine)
    slabs = adamw(mine, other, *[_pack([d[n] for n in MISC], F32, PACK_ROWS) for d in (wl, ml, vl)])
    shapes = [wl[n].shape for n in MISC]
    unpacked = [_unpack(s, shapes) for s in slabs]
    for idx, n in enumerate(MISC):
        res[n] = [u[idx] for u in unpacked]
    return (loss, dx[None], *[res[n][k] for k in range(4) for n in WNAMES])
```

```python
import functools
import math

import jax
import jax.numpy as jnp
import numpy as np
from jax import lax
from jax.experimental import pallas as pl
from jax.experimental.pallas import tpu as pltpu

F32 = jnp.float32
BF16 = jnp.bfloat16
SDS = jax.ShapeDtypeStruct
MESH = pl.DeviceIdType.MESH
HIGHEST = lax.Precision.HIGHEST

D_MODEL = 1024
DEPTH = 4
D_FF = 2816
NORM_EPS = 1e-6
LN_EPS = 1e-5
GM_HEADS = 8
CHUNK = 128
SSD_HEADS = 16
SSD_HEAD_DIM = 64
SSD_INNER = 1024
SSD_STATE = 128
SSD_CONV = 4
SSD_CONV_CH = 1536
HYB_MAIN = 4608
HYB_IN = 4624
HYB_PAD = 5120
MLA_HEADS = 16
MLA_NOPE = 128
MLA_ROPE = 64
MLA_QK = 192
MLA_QPAD = 256
MLA_Q_LORA = 256
MLA_KV_LORA = 128
ROPE_BASE = 10000.0
ADAM_LR = 0.001
ADAM_B1 = 0.9
ADAM_B2 = 0.999
ADAM_EPS = 1e-08
ADAM_WD = 0.01
ADAM_STEP = 10

N_CHIPS = 4
LANES = 128
VMEM_LIMIT = 56 * 1024 * 1024
PACK_ROWS = 2048

WEIGHTS = [
    ("ffn1_pre_g", (4, 1024), None, False),
    ("ffn1_w_in", (4, 1024, 5632), 2, True),
    ("ffn1_w_down", (4, 2816, 1024), 1, True),
    ("ffn1_post_g", (4, 1024), None, False),
    ("mix_pre_g", (4, 1024), None, False),
    ("mix_post_g", (4, 1024), None, False),
    ("ffn2_pre_g", (4, 1024), None, False),
    ("ffn2_w_in", (4, 1024, 5632), 2, True),
    ("ffn2_w_down", (4, 2816, 1024), 1, True),
    ("ffn2_post_g", (4, 1024), None, False),
    ("ple_pre_g", (4, 1024), None, False),
    ("ple_w_gate", (4, 1024, 1024), 1, True),
    ("ple_w_proj", (4, 256, 1024), 2, True),
    ("ple_post_g", (4, 1024), None, False),
    ("hyb_w_in", (2, 1024, 4624), 2, True),
    ("gm_ln_g", (2, 1024), None, False),
    ("gm_ln_b", (2, 1024), None, False),
    ("gm_w_s", (2, 8, 128, 128), None, False),
    ("gm_b_s", (2, 8, 128), None, False),
    ("ssd_conv_w", (2, 4, 1536), 2, False),
    ("ssd_conv_b", (2, 1536), None, False),
    ("ssd_dt_bias", (2, 16), None, False),
    ("ssd_a_log", (2, 16), None, False),
    ("ssd_d", (2, 16), None, False),
    ("ssd_norm_g", (2, 1024), None, False),
    ("hyb_w_out", (2, 2048, 1024), 1, True),
    ("mla_w_in", (2, 1024, 448), 1, True),
    ("mla_q_norm_g", (2, 256), 1, False),
    ("mla_kv_norm_g", (2, 128), None, False),
    ("mla_w_uq", (2, 256, 3072), 2, True),
    ("mla_w_ukv", (2, 128, 4096), 2, True),
    ("mla_w_out", (2, 2048, 1024), 1, True),
]
WNAMES = [w[0] for w in WEIGHTS]
WSPEC = {w[0]: w for w in WEIGHTS}
REG = ["ffn1_w_in", "ffn1_w_down", "ffn2_w_in", "ffn2_w_down", "ple_w_gate", "ple_w_proj", "hyb_w_out", "mla_w_out"]
WIN = "hyb_w_in"
WIN_SHARD = 4624 // 4
WIN_STRIDE = (WIN_SHARD // 128) * 128
WIN_WIDTH = -(-(WIN_SHARD + 3 * (WIN_SHARD - WIN_STRIDE)) // 128) * 128
MOVED = REG + [WIN]
MISC = [n for n in WNAMES if n not in MOVED]


def _pick(dim, target):
    if dim <= target:
        return dim
    t = (target // LANES) * LANES
    while t >= LANES:
        if dim % t == 0:
            return t
        t -= LANES
    return dim


def mm(a, b, *, ta=False, tb=False, out_dtype=F32, name, tm=1024, tn=1024, tk=1024):
    a, la = a if isinstance(a, tuple) else (a, None)
    b, lb = b if isinstance(b, tuple) else (b, None)

    def dims(x, mode):
        r, c = x.shape[-2:]
        return (r, c * x.shape[0]) if mode == "planes" else (r, c)

    K, M = dims(a, la) if ta else dims(a, la)[::-1]
    N, K2 = dims(b, lb) if tb else dims(b, lb)[::-1]
    assert K == K2, (a.shape, b.shape, ta, tb)
    assert not (la == "planes" and ta) and not (lb == "planes" and tb)
    bm, bn = _pick(M, tm), _pick(b.shape[-1] if lb == "planes" else N, tn)
    bk = _pick(a.shape[-1] if la == "planes" else K, tk)
    nk = K // bk

    def spec(shape, idx, layer, x):
        if layer is None:
            return pl.BlockSpec(shape, idx)
        if layer == "planes":
            per = x.shape[-1] // shape[1]
            return pl.BlockSpec((None,) + shape, lambda i, j, k: (idx(i, j, k)[1] // per, idx(i, j, k)[0],
                                                                  idx(i, j, k)[1] % per))
        return pl.BlockSpec((None,) + shape, lambda i, j, k: (layer,) + idx(i, j, k))

    a_spec = spec((bk, bm), lambda i, j, k: (k, i), la, a) if ta else spec((bm, bk), lambda i, j, k: (i, k), la, a)
    b_spec = spec((bn, bk), lambda i, j, k: (j, k), lb, b) if tb else spec((bk, bn), lambda i, j, k: (k, j), lb, b)
    dn = (((0 if ta else 1,), (1 if tb else 0,)), ((), ()))

    def body(a_ref, b_ref, o_ref, acc_ref):
        k = pl.program_id(2)

        @pl.when(k == 0)
        def _():
            acc_ref[...] = jnp.zeros_like(acc_ref)

        acc_ref[...] += lax.dot_general(a_ref[...].astype(BF16), b_ref[...].astype(BF16), dn,
                                        preferred_element_type=F32)

        @pl.when(k == nk - 1)
        def _():
            o_ref[...] = acc_ref[...].astype(o_ref.dtype)

    return pl.pallas_call(
        body, name=name, grid=(M // bm, N // bn, nk),
        in_specs=[a_spec, b_spec], out_specs=pl.BlockSpec((bm, bn), lambda i, j, k: (i, j)),
        out_shape=SDS((M, N), out_dtype), scratch_shapes=[pltpu.VMEM((bm, bn), F32)],
        compiler_params=pltpu.CompilerParams(dimension_semantics=("parallel", "parallel", "arbitrary"),
                                             vmem_limit_bytes=VMEM_LIMIT),
    )(a, b)


def row_call(fn, xs, ps, outs, accs=(), *, tb, name, reverse=False):
    xs = [x if isinstance(x, tuple) else (x, x.shape[1], 0) for x in xs]
    T = xs[0][0].shape[0]
    tb = min(tb, T)
    n = T // tb
    assert n * tb == T
    nx, npar, no, na = len(xs), len(ps), len(outs), len(accs)

    def ridx(i):
        return n - 1 - i if reverse else i

    in_specs = [pl.BlockSpec((tb, w), functools.partial(lambda i, cb: (ridx(i), cb), cb=cb)) for (_, w, cb) in xs]
    in_specs += [pl.BlockSpec(p.shape, functools.partial(lambda i, nd: (0,) * nd, nd=p.ndim)) for p in ps]
    out_specs = [pl.BlockSpec((tb, c), lambda i: (ridx(i), 0)) for (c, _) in outs]
    out_specs += [pl.BlockSpec(s, functools.partial(lambda i, nd: (0,) * nd, nd=len(s))) for s in accs]
    out_shape = [SDS((T, c), dt) for (c, dt) in outs] + [SDS(s, F32) for s in accs]

    def body(*refs):
        xr, pr = refs[:nx], refs[nx:nx + npar]
        orf, ar = refs[nx + npar:nx + npar + no], refs[nx + npar + no:]
        res = fn(*[r[...] for r in xr], *[r[...] for r in pr])
        for r, v in zip(orf, res[:no]):
            r[...] = v.astype(r.dtype)
        if na:
            @pl.when(pl.program_id(0) == 0)
            def _():
                for r in ar:
                    r[...] = jnp.zeros_like(r)

            for r, v in zip(ar, res[no:]):
                r[...] += v.astype(F32)

    res = pl.pallas_call(
        body, name=name, grid=(n,), in_specs=in_specs, out_specs=out_specs, out_shape=out_shape,
        compiler_params=pltpu.CompilerParams(dimension_semantics=("arbitrary",), vmem_limit_bytes=VMEM_LIMIT),
    )(*[x[0] for x in xs], *ps)
    return res


def _f32(*a):
    return [v.astype(F32) for v in a]


def t_rms(x, g):
    return x * lax.rsqrt(jnp.mean(x * x, axis=-1, keepdims=True) + NORM_EPS) * g


def t_swiglu(gate, up):
    return jax.nn.silu(gate) * up


def t_ple(gl, pp, g):
    return t_rms(jax.nn.sigmoid(gl) * pp, g)


def _iota(shape, d):
    return lax.broadcasted_iota(jnp.int32, shape, d)


def _bdot(a, b, dn=(((1,), (0,)), ((), ()))):
    return lax.dot_general(a.astype(BF16), b.astype(BF16), dn, preferred_element_type=F32)


def _hdot(a, b):
    return jnp.dot(a, b, precision=HIGHEST, preferred_element_type=F32)


NT = (((1,), (1,)), ((), ()))
TN = (((0,), (0,)), ((), ()))


def t_gmlp(uv, ln_g, ln_b, w_s, b_st):
    tb = uv.shape[0]
    guv = jax.nn.gelu(uv)
    u, v = guv[:, :1024], guv[:, 1024:]
    tri = _iota((CHUNK, CHUNK), 1) <= _iota((CHUNK, CHUNK), 0)
    rows = []
    for c in range(tb // CHUNK):
        vc = v[c * CHUNK:(c + 1) * CHUNK]
        heads = []
        for h in range(GM_HEADS):
            sl = slice(h * 128, (h + 1) * 128)
            vh = vc[:, sl]
            xc = vh - jnp.mean(vh, axis=-1, keepdims=True)
            var = jnp.mean(xc * xc, axis=-1, keepdims=True)
            y = xc * lax.rsqrt(var + LN_EPS) * ln_g[:, sl] + ln_b[:, sl]
            wm = jnp.where(tri, w_s[sl, :], 0.0)
            heads.append(_bdot(wm, y) + b_st[:, h:h + 1])
        rows.append(jnp.concatenate(heads, axis=1))
    mixed = rows[0] if len(rows) == 1 else jnp.concatenate(rows, axis=0)
    return u * mixed


def t_ssd(pre, dtr, z, st, dt_bias, a_log, d_exp, norm_g):
    L = CHUNK
    xbc = jax.nn.silu(pre)
    xs, bm, cm = xbc[:, :1024], xbc[:, 1024:1280], xbc[:, 1280:1536]
    valid = _iota((1, LANES), 1) < SSD_HEADS
    dt16 = jnp.where(valid, jax.nn.softplus(dtr + dt_bias), 0.0)
    a16 = jnp.where(valid, -jnp.exp(a_log), 0.0)
    da16 = dt16 * a16
    tri = _iota((L, L), 1) <= _iota((L, L), 0)
    acs16 = _hdot(tri.astype(F32), da16)
    hh, cc = _iota((LANES, 1024), 0), _iota((LANES, 1024), 1)
    expand = ((cc >= hh * SSD_HEAD_DIM) & (cc < (hh + 1) * SSD_HEAD_DIM)).astype(F32)
    acs = _hdot(acs16, expand)
    dte = _hdot(dt16, expand)
    alast = jnp.sum(jnp.where(_iota((L, 1024), 0) == L - 1, acs, 0.0), axis=0, keepdims=True)
    xd = xs * dte
    groups = [slice(0, 512), slice(512, 1024)]
    bg = [bm[:, :128], bm[:, 128:]]
    cg = [cm[:, :128], cm[:, 128:]]
    yoff = jnp.concatenate([_bdot(cg[g], st[:, groups[g]]) for g in range(2)], axis=1) * jnp.exp(acs)
    xdw = xd * jnp.exp(alast - acs)
    s_t = jnp.concatenate([_bdot(bg[g], xdw[:, groups[g]], TN) for g in range(2)], axis=1)
    st_new = st * jnp.exp(alast) + s_t
    cb = [_bdot(cg[g], bg[g], NT) for g in range(2)]
    acs16_t = acs16.T
    lo = _iota((1, LANES), 1) < SSD_HEAD_DIM
    slabs = []
    for j in range(SSD_HEADS // 2):
        g = j // 4
        xslab = xd[:, j * 128:(j + 1) * 128]
        acc = None
        for half in range(2):
            h = 2 * j + half
            seg = acs16[:, h:h + 1] - acs16_t[h:h + 1, :]
            mmat = cb[g] * jnp.exp(jnp.where(tri, seg, -1e30))
            xm = jnp.where(lo if half == 0 else jnp.logical_not(lo), xslab, 0.0)
            term = _bdot(mmat, xm)
            acc = term if acc is None else acc + term
        slabs.append(acc)
    y = jnp.concatenate(slabs, axis=1) + yoff + d_exp * xs
    yg = y * jax.nn.silu(z)
    outs = []
    for g in range(2):
        t = yg[:, groups[g]]
        outs.append(t * lax.rsqrt(jnp.mean(t * t, axis=-1, keepdims=True) + NORM_EPS) * norm_g[:, groups[g]])
    return jnp.concatenate(outs, axis=1), st_new


def t_kprep(c_all, cs, sn, qg, kvg):
    cqn = t_rms(c_all[:, :256], qg)
    ckvn = t_rms(c_all[:, 256:384], kvg)
    kr = c_all[:, 384:512] * cs + c_all[:, 512:640] * sn
    return cqn, ckvn, kr


def t_qrope(qb, c256, s256):
    scale = MLA_QK ** -0.5
    half = MLA_HEADS * MLA_QPAD
    outs = []
    for h in range(MLA_HEADS):
        a = qb[:, h * MLA_QPAD:(h + 1) * MLA_QPAD]
        b = qb[:, half + h * MLA_QPAD:half + (h + 1) * MLA_QPAD]
        outs.append((a * c256 + b * s256) * scale)
    return jnp.concatenate(outs, axis=1)


def rms_fwd(h, g, *, name, tb=512):
    def fn(h, g):
        return (t_rms(h.astype(F32), g),)
    return row_call(fn, [h], [g], [(h.shape[1], BF16)], tb=tb, name=name)[0]


def rms_bwd(h, dhn, dres, g, *, name, tb=256):
    def fn(h, dhn, dres, g):
        h, dhn, dres = _f32(h, dhn, dres)
        _, vjp = jax.vjp(t_rms, h, g)
        dh, dg = vjp(dhn)
        return dres + dh, dg
    return row_call(fn, [h, dhn, dres], [g], [(h.shape[1], F32)], [g.shape], tb=tb, name=name)


def post_fwd(h, f, g, scale, *, name, tb=512):
    def fn(h, f, g):
        return (h + scale * t_rms(f.astype(F32), g),)
    return row_call(fn, [h, f], [g], [(h.shape[1], F32)], tb=tb, name=name)[0]


def post_bwd(f, dout, g, scale, *, name, tb=256):
    def fn(f, dout, g):
        f, dout = _f32(f, dout)
        _, vjp = jax.vjp(lambda f, g: scale * t_rms(f, g), f, g)
        return vjp(dout)
    return row_call(fn, [f, dout], [g], [(f.shape[1], BF16)], [g.shape], tb=tb, name=name)


FFN_TILE = D_FF // 2


def _stacked(w):
    return w if isinstance(w, tuple) else (w[None], 0)


def ffn_in_act(hn, w_in, *, name, tm=512):
    w, layer = _stacked(w_in)
    T, K = hn.shape
    bm, bn = _pick(T, tm), FFN_TILE
    nj = D_FF // bn

    def body(x_ref, wg_ref, wu_ref, gu_ref, act_ref):
        x = x_ref[...].astype(BF16)
        g = jnp.dot(x, wg_ref[...].astype(BF16), preferred_element_type=F32)
        u = jnp.dot(x, wu_ref[...].astype(BF16), preferred_element_type=F32)
        gu_ref[0] = g.astype(gu_ref.dtype)
        gu_ref[1] = u.astype(gu_ref.dtype)
        act_ref[...] = t_swiglu(g, u).astype(act_ref.dtype)

    return pl.pallas_call(
        body, name=name, grid=(T // bm, nj),
        in_specs=[pl.BlockSpec((bm, K), lambda i, j: (i, 0)),
                  pl.BlockSpec((None, K, bn), lambda i, j: (layer, 0, j)),
                  pl.BlockSpec((None, K, bn), lambda i, j: (layer, 0, nj + j))],
        out_specs=[pl.BlockSpec((2, bm, bn), lambda i, j: (0, i, j)), pl.BlockSpec((bm, bn), lambda i, j: (i, j))],
        out_shape=[SDS((2, T, D_FF), BF16), SDS((T, D_FF), BF16)],
        compiler_params=pltpu.CompilerParams(dimension_semantics=("parallel", "parallel"),
                                             vmem_limit_bytes=VMEM_LIMIT),
    )(hn, w, w)


def ffn_down_bx_act(df, w_down, gu, *, name, tm=512):
    w, layer = _stacked(w_down)
    T, K = df.shape
    bm, bn = _pick(T, tm), FFN_TILE

    def body(df_ref, wd_ref, gu_ref, dgu_ref):
        da = lax.dot_general(df_ref[...].astype(BF16), wd_ref[...].astype(BF16), NT, preferred_element_type=F32)
        _, vjp = jax.vjp(t_swiglu, gu_ref[0].astype(F32), gu_ref[1].astype(F32))
        dg, du = vjp(da)
        dgu_ref[0] = dg.astype(dgu_ref.dtype)
        dgu_ref[1] = du.astype(dgu_ref.dtype)

    return pl.pallas_call(
        body, name=name, grid=(T // bm, D_FF // bn),
        in_specs=[pl.BlockSpec((bm, K), lambda i, j: (i, 0)),
                  pl.BlockSpec((None, bn, K), lambda i, j: (layer, j, 0)),
                  pl.BlockSpec((2, bm, bn), lambda i, j: (0, i, j))],
        out_specs=pl.BlockSpec((2, bm, bn), lambda i, j: (0, i, j)), out_shape=SDS((2, T, D_FF), BF16),
        compiler_params=pltpu.CompilerParams(dimension_semantics=("parallel", "parallel"),
                                             vmem_limit_bytes=VMEM_LIMIT),
    )(df, w, gu)


def ple_fwd(h, gl, pp, g, *, name, tb=512):
    def fn(h, gl, pp, g):
        return (h + t_ple(gl, pp, g),)
    return row_call(fn, [h, gl, pp], [g], [(D_MODEL, F32)], tb=tb, name=name)[0]


def ple_bwd(gl, pp, dout, g, *, name, tb=256):
    def fn(gl, pp, dout, g):
        _, vjp = jax.vjp(t_ple, gl, pp, g)
        return vjp(dout)
    return row_call(fn, [gl, pp, dout], [g], [(D_MODEL, BF16), (D_MODEL, BF16)], [g.shape], tb=tb, name=name)


def gmlp_fwd(proj, ln_g, ln_b, w_s, b_st, *, name, tb=256):
    def fn(uv, ln_g, ln_b, w_s, b_st):
        return (t_gmlp(uv, ln_g, ln_b, w_s, b_st),)
    return row_call(fn, [(proj, 2048, 0)], [ln_g, ln_b, w_s, b_st], [(1024, BF16)], tb=tb, name=name)[0]


def gmlp_bwd(proj, dya, ln_g, ln_b, w_s, b_st, *, name, tb=128):
    def fn(uv, dya, ln_g, ln_b, w_s, b_st):
        _, vjp = jax.vjp(t_gmlp, uv, ln_g, ln_b, w_s, b_st)
        return vjp(dya.astype(F32))
    return row_call(fn, [(proj, 2048, 0), (dya, 1024, 0)], [ln_g, ln_b, w_s, b_st], [(2048, BF16)],
                    [ln_g.shape, ln_b.shape, w_s.shape, b_st.shape], tb=tb, name=name)


def kprep_fwd(c_all, cs, sn, qg, kvg, *, name, tb=512):
    return row_call(t_kprep, [c_all, cs, sn], [qg, kvg], [(256, BF16), (128, BF16), (128, BF16)], tb=tb, name=name)


def kprep_bwd(c_all, cs, sn, dcqn, dckvn, dkr, qg, kvg, *, name, tb=256):
    def fn(c_all, cs, sn, dcqn, dckvn, dkr, qg, kvg):
        dcqn, dckvn, dkr = _f32(dcqn, dckvn, dkr)
        _, vjp = jax.vjp(lambda c, qg, kvg: t_kprep(c, cs, sn, qg, kvg), c_all, qg, kvg)
        return vjp((dcqn, dckvn, dkr))
    return row_call(fn, [c_all, cs, sn, dcqn, dckvn, dkr], [qg, kvg], [(640, BF16)], [qg.shape, kvg.shape],
                    tb=tb, name=name)


def qrope_fwd(qb, c256, s256, *, name, tb=256):
    def fn(qb, c256, s256):
        return (t_qrope(qb, c256, s256),)
    return row_call(fn, [qb, c256, s256], [], [(MLA_HEADS * MLA_QPAD, BF16)], tb=tb, name=name)[0]


def qrope_bwd(dq, c256, s256, *, name, tb=256):
    def fn(dq, c256, s256):
        scale = MLA_QK ** -0.5
        a, b = [], []
        for h in range(MLA_HEADS):
            d = dq[:, h * MLA_QPAD:(h + 1) * MLA_QPAD] * scale
            a.append(d * c256)
            b.append(d * s256)
        return (jnp.concatenate(a + b, axis=1),)
    return row_call(fn, [dq, c256, s256], [], [(2 * MLA_HEADS * MLA_QPAD, BF16)], tb=tb, name=name)[0]


STAT_SPLIT = 64
ATTN_UNROLL = 4
ATTN_HEADS_PER_STEP = 4


def stats_fwd(do, o, lse, *, name, tb=512):
    def fn(do, o, lse):
        do, o = _f32(do, o)
        low = _iota((1, 128), 1) < STAT_SPLIT
        outs = []
        for h in range(MLA_HEADS):
            sl = slice(h * 128, (h + 1) * 128)
            dl = jnp.sum(do[:, sl] * o[:, sl], axis=-1, keepdims=True)
            outs.append(jnp.where(low, lse[:, sl], dl))
        return (jnp.concatenate(outs, axis=1),)
    return row_call(fn, [do, o, lse], [], [(2048, F32)], tb=tb, name=name)[0]


def headsum(dkr_h, *, name, tb=512):
    def fn(d):
        acc = d[:, :128]
        for h in range(1, MLA_HEADS):
            acc = acc + d[:, h * 128:(h + 1) * 128]
        return (acc,)
    return row_call(fn, [dkr_h], [], [(128, F32)], tb=tb, name=name)[0]


def loss_fwd(y, t, *, name, tb=512):
    def fn(y, t):
        e = y - t
        return e * (1.0 / D_MODEL), jnp.sum(e * e, axis=0, keepdims=True)
    return row_call(fn, [y, t], [], [(D_MODEL, F32)], [(1, D_MODEL)], tb=tb, name=name)


def conv_fwd(proj, w, b, *, name, tb=256):
    T = proj.shape[0]
    n = T // tb
    hb = tb // CHUNK
    C = SSD_CONV_CH

    def body(cur, prev, w_ref, b_ref, o_ref, scr):
        i = pl.program_id(0)
        scr[pl.ds(0, CHUNK), :] = jnp.where(i > 0, prev[...], 0.0)
        scr[pl.ds(CHUNK, tb), :] = cur[...]
        y = b_ref[...] + w_ref[3:4, :] * cur[...]
        for k in range(SSD_CONV - 1):
            y = y + w_ref[k:k + 1, :] * scr[pl.ds(CHUNK - (SSD_CONV - 1) + k, tb), :]
        o_ref[...] = y

    return pl.pallas_call(
        body, name=name, grid=(n,),
        in_specs=[pl.BlockSpec((tb, C), lambda i: (i, 2)),
                  pl.BlockSpec((CHUNK, C), lambda i: (jnp.maximum(i * hb - 1, 0), 2)),
                  pl.BlockSpec((SSD_CONV, C), lambda i: (0, 0)), pl.BlockSpec((1, C), lambda i: (0, 0))],
        out_specs=pl.BlockSpec((tb, C), lambda i: (i, 0)), out_shape=SDS((T, C), F32),
        scratch_shapes=[pltpu.VMEM((CHUNK + tb, C), F32)],
        compiler_params=pltpu.CompilerParams(dimension_semantics=("arbitrary",), vmem_limit_bytes=VMEM_LIMIT),
    )(proj, proj, w, b)


def conv_bwd(dpre, proj, w, *, name, tb=256):
    T = proj.shape[0]
    n = T // tb
    hb = tb // CHUNK
    nh = T // CHUNK
    C = SSD_CONV_CH

    def body(dcur, dnext, xcur, xprev, w_ref, dx_ref, dw_ref, db_ref, dscr, xscr):
        i = pl.program_id(0)

        @pl.when(i == 0)
        def _():
            dw_ref[...] = jnp.zeros_like(dw_ref)
            db_ref[...] = jnp.zeros_like(db_ref)

        d = dcur[...]
        dscr[pl.ds(0, tb), :] = d
        dscr[pl.ds(tb, CHUNK), :] = jnp.where(i < n - 1, dnext[...], 0.0)
        xscr[pl.ds(0, CHUNK), :] = jnp.where(i > 0, xprev[...], 0.0)
        xscr[pl.ds(CHUNK, tb), :] = xcur[...]
        dx = w_ref[3:4, :] * d
        for k in range(SSD_CONV - 1):
            dx = dx + w_ref[k:k + 1, :] * dscr[pl.ds(SSD_CONV - 1 - k, tb), :]
        dx_ref[...] = dx.astype(dx_ref.dtype)
        for k in range(SSD_CONV):
            xk = xscr[pl.ds(CHUNK - (SSD_CONV - 1) + k, tb), :]
            dw_ref[k:k + 1, :] += jnp.sum(d * xk, axis=0, keepdims=True)
        db_ref[...] += jnp.sum(d, axis=0, keepdims=True)

    return pl.pallas_call(
        body, name=name, grid=(n,),
        in_specs=[pl.BlockSpec((tb, C), lambda i: (i, 0)),
                  pl.BlockSpec((CHUNK, C), lambda i: (jnp.minimum((i + 1) * hb, nh - 1), 0)),
                  pl.BlockSpec((tb, C), lambda i: (i, 2)),
                  pl.BlockSpec((CHUNK, C), lambda i: (jnp.maximum(i * hb - 1, 0), 2)),
                  pl.BlockSpec((SSD_CONV, C), lambda i: (0, 0))],
        out_specs=[pl.BlockSpec((tb, C), lambda i: (i, 0)), pl.BlockSpec((SSD_CONV, C), lambda i: (0, 0)),
                   pl.BlockSpec((1, C), lambda i: (0, 0))],
        out_shape=[SDS((T, C), BF16), SDS((SSD_CONV, C), F32), SDS((1, C), F32)],
        scratch_shapes=[pltpu.VMEM((tb + CHUNK, C), F32), pltpu.VMEM((CHUNK + tb, C), F32)],
        compiler_params=pltpu.CompilerParams(dimension_semantics=("arbitrary",), vmem_limit_bytes=VMEM_LIMIT),
    )(dpre, dpre, proj, proj, w)


def _ssd_specs(nc, rev):
    def r(c):
        return nc - 1 - c if rev else c
    pre = pl.BlockSpec((CHUNK, SSD_CONV_CH), lambda c: (r(c), 0))
    dtr = pl.BlockSpec((CHUNK, LANES), lambda c: (r(c), HYB_MAIN // LANES))
    z = pl.BlockSpec((CHUNK, 1024), lambda c: (r(c), 2))
    row = pl.BlockSpec((CHUNK, 1024), lambda c: (r(c), 0))
    return pre, dtr, z, row


def _pspec(shape):
    return pl.BlockSpec(shape, lambda c: (0,) * len(shape))


def ssd_fwd(pre, proj, dt_bias, a_log, d_exp, norm_g, *, name):
    T = pre.shape[0]
    nc = T // CHUNK
    s_pre, s_dt, s_z, s_row = _ssd_specs(nc, False)

    def body(pre_r, dt_r, z_r, b_r, a_r, d_r, g_r, y_r, sv_r, st):
        @pl.when(pl.program_id(0) == 0)
        def _():
            st[...] = jnp.zeros_like(st)

        s0 = st[...]
        sv_r[...] = s0
        y, s1 = t_ssd(pre_r[...], dt_r[...], z_r[...], s0, b_r[...], a_r[...], d_r[...], g_r[...])
        y_r[...] = y.astype(y_r.dtype)
        st[...] = s1

    return pl.pallas_call(
        body, name=name, grid=(nc,),
        in_specs=[s_pre, s_dt, s_z, _pspec((1, LANES)), _pspec((1, LANES)), _pspec((1, 1024)), _pspec((1, 1024))],
        out_specs=[s_row, s_row], out_shape=[SDS((T, 1024), BF16), SDS((T, 1024), F32)],
        scratch_shapes=[pltpu.VMEM((SSD_STATE, 1024), F32)],
        compiler_params=pltpu.CompilerParams(dimension_semantics=("arbitrary",), vmem_limit_bytes=VMEM_LIMIT),
    )(pre, proj, proj, dt_bias, a_log, d_exp, norm_g)


def ssd_bwd(pre, proj, states, dyab, dt_bias, a_log, d_exp, norm_g, *, name):
    T = pre.shape[0]
    nc = T // CHUNK
    s_pre, s_dt, s_z, s_row = _ssd_specs(nc, True)
    s_dtout = pl.BlockSpec((CHUNK, LANES), lambda c: (nc - 1 - c, 0))
    s_dy = pl.BlockSpec((CHUNK, 1024), lambda c: (nc - 1 - c, 1))

    def body(pre_r, dt_r, z_r, sv_r, dy_r, b_r, a_r, d_r, g_r, dpre_r, ddt_r, dz_r, db_r, da_r, dd_r, dg_r, dst):
        @pl.when(pl.program_id(0) == 0)
        def _():
            dst[...] = jnp.zeros_like(dst)
            for r in (db_r, da_r, dd_r, dg_r):
                r[...] = jnp.zeros_like(r)

        _, vjp = jax.vjp(t_ssd, pre_r[...], dt_r[...], z_r[...], sv_r[...], b_r[...], a_r[...], d_r[...], g_r[...])
        dpre, ddt, dz, ds0, db, da, dd, dg = vjp((dy_r[...].astype(F32), dst[...]))
        dpre_r[...] = dpre
        ddt_r[...] = ddt.astype(ddt_r.dtype)
        dz_r[...] = dz.astype(dz_r.dtype)
        dst[...] = ds0
        db_r[...] += db
        da_r[...] += da
        dd_r[...] += dd
        dg_r[...] += dg

    return pl.pallas_call(
        body, name=name, grid=(nc,),
        in_specs=[s_pre, s_dt, s_z, s_row, s_dy, _pspec((1, LANES)), _pspec((1, LANES)), _pspec((1, 1024)),
                  _pspec((1, 1024))],
        out_specs=[s_pre, s_dtout, s_row, _pspec((1, LANES)), _pspec((1, LANES)), _pspec((1, 1024)), _pspec((1, 1024))],
        out_shape=[SDS((T, SSD_CONV_CH), F32), SDS((T, LANES), BF16), SDS((T, 1024), BF16),
                   SDS((1, LANES), F32), SDS((1, LANES), F32), SDS((1, 1024), F32), SDS((1, 1024), F32)],
        scratch_shapes=[pltpu.VMEM((SSD_STATE, 1024), F32)],
        compiler_params=pltpu.CompilerParams(dimension_semantics=("arbitrary",), vmem_limit_bytes=VMEM_LIMIT),
    )(pre, proj, proj, states, dyab, dt_bias, a_log, d_exp, norm_g)


def _attn_tile(T, target=512):
    return min(target, T // 2)


def _causal(tq):
    return _iota((tq, tq), 1) <= _iota((tq, tq), 0)


def _job_parts(job):
    if job is None:
        return 0, 0, [], [], []
    ni, no = len(job["inputs"]), len(job["out_shape"])
    return ni, no, list(job["inputs"]), list(job["out_shape"]), list(job["scratch"])


def _job_phase(job, which, refs, when):
    if job is not None and job["phases"][which] is not None:
        pl.when(when)(functools.partial(job["phases"][which], *refs))


def attn_fwd(q, kv, kr, *, name, job=None):
    T = q.shape[0]
    tq = _attn_tile(T)
    nq = T // tq
    hp = ATTN_HEADS_PER_STEP
    ng = MLA_HEADS // hp
    ni, no, jins, jouts, jscratch = _job_parts(job)

    def body(*refs):
        q_ref, kn_ref, v_ref, kr_ref = refs[:4]
        o_ref, lse_ref = refs[4 + ni:6 + ni]
        jrefs = (refs[4:4 + ni], refs[6 + ni:6 + ni + no], refs[6 + ni + no:])
        g = pl.program_id(0)
        qi = pl.program_id(1)
        _job_phase(job, 0, jrefs, (g == 0) & (qi == 0))
        qv = [q_ref[:, e * MLA_QPAD:(e + 1) * MLA_QPAD] for e in range(hp)]

        def blk(ki, masked, carry):
            rows = pl.ds(pl.multiple_of(ki * tq, tq), tq)
            kr = kr_ref[rows, :]
            out = []
            for e in range(hp):
                m, l, acc = carry[e]
                cols = slice(e * 128, (e + 1) * 128)
                k = jnp.concatenate([kn_ref[rows, cols], kr], axis=1)
                s = lax.dot_general(qv[e], k, NT, preferred_element_type=F32)
                if masked:
                    s = jnp.where(_causal(tq), s, -1e30)
                m_new = jnp.maximum(m, jnp.max(s, axis=-1, keepdims=True))
                p = jnp.exp(s - m_new)
                alpha = jnp.exp(m - m_new)
                l = alpha * l + jnp.sum(p, axis=-1, keepdims=True)
                acc = alpha * acc + jnp.dot(p.astype(BF16), v_ref[rows, cols], preferred_element_type=F32)
                out.append((m_new, l, acc))
            return tuple(out)

        one = (jnp.full((tq, 1), -1e30, F32), jnp.zeros((tq, 1), F32), jnp.zeros((tq, 128), F32))
        carry = lax.fori_loop(0, qi, lambda ki, c: blk(ki, False, c), (one,) * hp)
        carry = blk(qi, True, carry)
        for e in range(hp):
            m, l, acc = carry[e]
            cols = slice(e * 128, (e + 1) * 128)
            o_ref[:, cols] = (acc / l).astype(o_ref.dtype)
            lse_ref[:, cols] = jnp.broadcast_to(m + jnp.log(l), (tq, 128))
        _job_phase(job, 2, jrefs, (g == ng - 1) & (qi == nq - 1))

    res = pl.pallas_call(
        body, name=name, grid=(ng, nq),
        in_specs=[pl.BlockSpec((tq, hp * MLA_QPAD), lambda g, i: (i, g)),
                  pl.BlockSpec((T, hp * 128), lambda g, i: (0, g)),
                  pl.BlockSpec((T, hp * 128), lambda g, i: (0, ng + g)),
                  pl.BlockSpec((T, 128), lambda g, i: (0, 0))] + [ANY] * ni,
        out_specs=[pl.BlockSpec((tq, hp * 128), lambda g, i: (i, g)), pl.BlockSpec((tq, hp * 128), lambda g, i: (i, g))]
        + [ANY] * no,
        out_shape=[SDS((T, 2048), BF16), SDS((T, 2048), F32)] + jouts, scratch_shapes=jscratch,
        compiler_params=pltpu.CompilerParams(dimension_semantics=("arbitrary", "arbitrary"),
                                             vmem_limit_bytes=VMEM_LIMIT),
    )(q, kv, kv, kr, *jins)
    return res[0], res[1], list(res[2:])


def attn_bwd(q, kv, kr, do, stats, *, name, job=None):
    T = q.shape[0]
    tq = _attn_tile(T)
    nq = T // tq
    ni, no, jins, jouts, jscratch = _job_parts(job)

    def body(*refs):
        q_ref, do_ref, st_ref, kn_ref, v_ref, kr_ref = refs[:6]
        dq_ref, dkn_ref, dv_ref, dkr_ref = refs[6 + ni:10 + ni]
        jrefs = (refs[6:6 + ni], refs[10 + ni:10 + ni + no], refs[10 + ni + no:])
        h = pl.program_id(0)
        ki = pl.program_id(1)
        _job_phase(job, 0, jrefs, (h == 0) & (ki == 0))
        _job_phase(job, 1, jrefs, (h == MLA_HEADS // 2) & (ki == 0))

        @pl.when(ki == 0)
        def _():
            dq_ref[...] = jnp.zeros_like(dq_ref)

        k = jnp.concatenate([kn_ref[...], kr_ref[...]], axis=1)
        v = v_ref[...]

        def blk(qi, masked, carry):
            dk, dv = carry
            rows = pl.ds(pl.multiple_of(qi * tq, tq), tq)
            qv, dov = q_ref[rows, :], do_ref[rows, :]
            lse, dl = st_ref[rows, 0:1], st_ref[rows, STAT_SPLIT:STAT_SPLIT + 1]
            s = lax.dot_general(qv, k, NT, preferred_element_type=F32)
            if masked:
                s = jnp.where(_causal(tq), s, -1e30)
            p = jnp.exp(s - lse)
            dv = dv + lax.dot_general(p.astype(BF16), dov, TN, preferred_element_type=F32)
            dp = lax.dot_general(dov, v, NT, preferred_element_type=F32)
            ds = (p * (dp - dl)).astype(BF16)
            dk = dk + lax.dot_general(ds, qv, TN, preferred_element_type=F32)
            dq_ref[rows, :] += jnp.dot(ds, k, preferred_element_type=F32)
            return dk, dv

        carry = blk(ki, True, (jnp.zeros((tq, MLA_QPAD), F32), jnp.zeros((tq, 128), F32)))
        rest = nq - 1 - ki

        def group(j, c):
            for u in range(ATTN_UNROLL):
                c = blk(ki + 1 + ATTN_UNROLL * j + u, False, c)
            return c

        carry = lax.fori_loop(0, rest // ATTN_UNROLL, group, carry)
        for u in range(ATTN_UNROLL - 1):
            carry = lax.cond(rest % ATTN_UNROLL > u, functools.partial(lambda c, u: blk(nq - 1 - u, False, c), u=u),
                             lambda c: c, carry)
        dk, dv = carry
        dkn_ref[...] = dk[:, :128].astype(dkn_ref.dtype)
        dkr_ref[...] = dk[:, 128:]
        dv_ref[...] = dv.astype(dv_ref.dtype)
        _job_phase(job, 2, jrefs, (h == MLA_HEADS - 1) & (ki == nq - 1))

    res = pl.pallas_call(
        body, name=name, grid=(MLA_HEADS, nq),
        in_specs=[pl.BlockSpec((T, MLA_QPAD), lambda h, i: (0, h)),
                  pl.BlockSpec((T, 128), lambda h, i: (0, h)),
                  pl.BlockSpec((T, 128), lambda h, i: (0, h)),
                  pl.BlockSpec((tq, 128), lambda h, i: (i, h)),
                  pl.BlockSpec((tq, 128), lambda h, i: (i, MLA_HEADS + h)),
                  pl.BlockSpec((tq, 128), lambda h, i: (i, 0))] + [ANY] * ni,
        out_specs=[pl.BlockSpec((T, MLA_QPAD), lambda h, i: (0, h)),
                   pl.BlockSpec((tq, 128), lambda h, i: (i, h)), pl.BlockSpec((tq, 128), lambda h, i: (i, h)),
                   pl.BlockSpec((tq, 128), lambda h, i: (i, h))] + [ANY] * no,
        out_shape=[SDS((T, MLA_HEADS * MLA_QPAD), F32), SDS((T, 2048), BF16), SDS((T, 2048), BF16),
                   SDS((T, 2048), F32)] + jouts, scratch_shapes=jscratch,
        compiler_params=pltpu.CompilerParams(dimension_semantics=("arbitrary", "arbitrary"),
                                             vmem_limit_bytes=VMEM_LIMIT),
    )(q, do, stats, kv, kv, kr, *jins)
    return res[0], res[1], res[2], res[3], list(res[4:])


def _ffn_fwd(h, pre_g, w_in, w_down, post_g, tag):
    hn = rms_fwd(h, pre_g, name=f"{tag}_pre")
    gu, a = ffn_in_act(hn, w_in, name=f"{tag}_in")
    f = mm(a, w_down, name=f"{tag}_down", tk=1408)
    h2 = post_fwd(h, f, post_g, 0.5, name=f"{tag}_post")
    return h2, (h, hn, gu, a, f)


def _ffn_bwd(dh2, saved, pre_g, w_in, w_down, post_g, tag):
    h, hn, gu, a, f = saved
    df, dpost = post_bwd(f, dh2, post_g, 0.5, name=f"{tag}_post_b")
    dgu = ffn_down_bx_act(df, w_down, gu, name=f"{tag}_down_bx")
    dw_down = mm(a, df, ta=True, out_dtype=BF16, name=f"{tag}_down_bw", tm=1408)
    dhn = mm((dgu, "planes"), w_in, tb=True, name=f"{tag}_in_bx", tk=1408)
    dw_in = mm(hn, (dgu, "planes"), ta=True, out_dtype=BF16, name=f"{tag}_in_bw", tn=1408)
    dh, dpre = rms_bwd(h, dhn, dh2, pre_g, name=f"{tag}_pre_b")
    return dh, dpre, dw_in, dw_down, dpost


def _hyb_fwd(hn, w, tag):
    proj = mm(hn, w["hyb_in"], name=f"{tag}_in")
    ya = gmlp_fwd(proj, w["ln_g"], w["ln_b"], w["w_s"], w["b_st"], name=f"{tag}_gmlp")
    pre = conv_fwd(proj, w["conv_w"], w["conv_b"], name=f"{tag}_conv")
    yb, states = ssd_fwd(pre, proj, w["dt_bias"], w["a_log"], w["d_exp"], w["norm_g"], name=f"{tag}_ssd")
    yab = jnp.concatenate([ya, yb], axis=1)
    mixed = mm(yab, w["hyb_out"], name=f"{tag}_out")
    return mixed, (proj, pre, states, yab)


def _hyb_bwd(dmixed, hn, saved, w, tag):
    proj, pre, states, yab = saved
    dyab = mm(dmixed, w["hyb_out"], tb=True, name=f"{tag}_out_bx")
    dw_out = mm(yab, dmixed, ta=True, out_dtype=BF16, name=f"{tag}_out_bw")
    duv, dln_g, dln_b, dw_s, db_st = gmlp_bwd(proj, dyab, w["ln_g"], w["ln_b"], w["w_s"], w["b_st"],
                                              name=f"{tag}_gmlp_b")
    dpre, ddt, dz, ddt_bias, da_log, dd_exp, dnorm_g = ssd_bwd(
        pre, proj, states, dyab, w["dt_bias"], w["a_log"], w["d_exp"], w["norm_g"], name=f"{tag}_ssd_b")
    dxbc, dconv_w, dconv_b = conv_bwd(dpre, proj, w["conv_w"], name=f"{tag}_conv_b")
    pad = jnp.zeros((duv.shape[0], HYB_PAD - HYB_MAIN - LANES), BF16)
    dproj = jnp.concatenate([duv, dz, dxbc, ddt, pad], axis=1)
    dhn = mm(dproj, w["hyb_in"], tb=True, name=f"{tag}_in_bx")
    dw_in = mm(hn, dproj, ta=True, out_dtype=BF16, name=f"{tag}_in_bw")
    g = dict(hyb_in=dw_in, hyb_out=dw_out, ln_g=dln_g, ln_b=dln_b, w_s=dw_s, b_st=db_st, conv_w=dconv_w,
             conv_b=dconv_b, dt_bias=ddt_bias, a_log=da_log, d_exp=dd_exp, norm_g=dnorm_g)
    return dhn, g


def _mla_fwd(hn, w, rope, tag, job=None):
    cs, sn, c256, s256 = rope
    c_all = mm(hn, w["mla_in"], name=f"{tag}_in")
    cqn, ckvn, kr = kprep_fwd(c_all, cs, sn, w["q_g"], w["kv_g"], name=f"{tag}_kprep")
    qb = mm(cqn, w["uq"], out_dtype=BF16, name=f"{tag}_uq")
    q = qrope_fwd(qb, c256, s256, name=f"{tag}_qrope")
    kv = mm(ckvn, w["ukv"], out_dtype=BF16, name=f"{tag}_ukv")
    o, lse, jouts = attn_fwd(q, kv, kr, name=f"{tag}_attn", job=job)
    mixed = mm(o, w["mla_out"], name=f"{tag}_out")
    return mixed, (c_all, cqn, ckvn, kr, q, kv, o, lse), jouts


def _mla_bwd(dmixed, hn, saved, w, rope, tag, job=None):
    cs, sn, c256, s256 = rope
    c_all, cqn, ckvn, kr, q, kv, o, lse = saved
    do = mm(dmixed, w["mla_out"], tb=True, out_dtype=BF16, name=f"{tag}_out_bx")
    dw_out = mm(o, dmixed, ta=True, out_dtype=BF16, name=f"{tag}_out_bw")
    stats = stats_fwd(do, o, lse, name=f"{tag}_stats")
    dq, dkn, dv, dkr_h, jouts = attn_bwd(q, kv, kr, do, stats, name=f"{tag}_attn_b", job=job)
    dkr = headsum(dkr_h, name=f"{tag}_dkr")
    dkv = jnp.concatenate([dkn, dv], axis=1)
    dckvn = mm(dkv, w["ukv"], tb=True, name=f"{tag}_ukv_bx")
    dw_ukv = mm(ckvn, dkv, ta=True, name=f"{tag}_ukv_bw")
    dqb = qrope_bwd(dq, c256, s256, name=f"{tag}_qrope_b")
    dcqn = mm(dqb, w["uq"], tb=True, name=f"{tag}_uq_bx")
    dw_uq = mm(cqn, dqb, ta=True, name=f"{tag}_uq_bw")
    dc_all, dq_g, dkv_g = kprep_bwd(c_all, cs, sn, dcqn, dckvn, dkr, w["q_g"], w["kv_g"], name=f"{tag}_kprep_b")
    dhn = mm(dc_all, w["mla_in"], tb=True, name=f"{tag}_in_bx")
    dw_in = mm(hn, dc_all, ta=True, name=f"{tag}_in_bw")
    g = dict(mla_in=dw_in, mla_out=dw_out, uq=dw_uq, ukv=dw_ukv, q_g=dq_g, kv_g=dkv_g)
    return dhn, g, jouts


class NoJobs:
    def fwd_job(self, i):
        return None

    def fwd_done(self, i, outs):
        pass

    def bwd_job(self, i, grads, partial):
        return None

    def bwd_done(self, i, outs):
        pass


def local_step(x, p, rope, target, weights_of, jobs=NoJobs()):
    h = x
    saved = []
    lw = []
    for i in range(DEPTH):
        w = weights_of(i)
        lw.append(w)
        t = f"l{i}"
        h, s1 = _ffn_fwd(h, w["ffn1_pre_g"], w["ffn1_w_in"], w["ffn1_w_down"], w["ffn1_post_g"], f"{t}_f1")
        h1 = h
        hn = rms_fwd(h1, w["mix_pre_g"], name=f"{t}_mixpre")
        if i % 2 == 0:
            mixed, sm = _hyb_fwd(hn, w, f"{t}_hyb")
        else:
            job = jobs.fwd_job(i)
            mixed, sm, jouts = _mla_fwd(hn, w, rope, f"{t}_mla", job)
            if job is not None:
                jobs.fwd_done(i, jouts)
                w = lw[i] = weights_of(i)
        h = post_fwd(h1, mixed, w["mix_post_g"], 1.0, name=f"{t}_mixpost")
        h, s2 = _ffn_fwd(h, w["ffn2_pre_g"], w["ffn2_w_in"], w["ffn2_w_down"], w["ffn2_post_g"], f"{t}_f2")
        h3 = h
        hn3 = rms_fwd(h3, w["ple_pre_g"], name=f"{t}_plepre")
        gl = mm(hn3, w["ple_w_gate"], name=f"{t}_plegate")
        pp = mm((p, i), w["ple_w_proj"], name=f"{t}_pleproj")
        h = ple_fwd(h3, gl, pp, w["ple_post_g"], name=f"{t}_plepost")
        saved.append((s1, h1, hn, sm, mixed, s2, h3, hn3, gl, pp))

    dh, sq = loss_fwd(h, target, name="loss")
    grads = [None] * DEPTH
    for i in reversed(range(DEPTH)):
        w = lw[i]
        t = f"l{i}"
        s1, h1, hn, sm, mixed, s2, h3, hn3, gl, pp = saved[i]
        g = {}
        dgl, dpp, g["ple_post_g"] = ple_bwd(gl, pp, dh, w["ple_post_g"], name=f"{t}_plepost_b")
        dhn3 = mm(dgl, w["ple_w_gate"], tb=True, name=f"{t}_plegate_bx")
        g["ple_w_gate"] = mm(hn3, dgl, ta=True, out_dtype=BF16, name=f"{t}_plegate_bw")
        g["ple_w_proj"] = mm((p, i), dpp, ta=True, out_dtype=BF16, name=f"{t}_pleproj_bw")
        dh, g["ple_pre_g"] = rms_bwd(h3, dhn3, dh, w["ple_pre_g"], name=f"{t}_plepre_b")
        dh, g["ffn2_pre_g"], g["ffn2_w_in"], g["ffn2_w_down"], g["ffn2_post_g"] = _ffn_bwd(
            dh, s2, w["ffn2_pre_g"], w["ffn2_w_in"], w["ffn2_w_down"], w["ffn2_post_g"], f"{t}_f2")
        dmixed, g["mix_post_g"] = post_bwd(mixed, dh, w["mix_post_g"], 1.0, name=f"{t}_mixpost_b")
        if i % 2 == 0:
            dhn, gm = _hyb_bwd(dmixed, hn, sm, w, f"{t}_hyb")
        else:
            job = jobs.bwd_job(i, grads, g)
            dhn, gm, jouts = _mla_bwd(dmixed, hn, sm, w, rope, f"{t}_mla", job)
            if job is not None:
                jobs.bwd_done(i, jouts)
        g.update(gm)
        dh, g["mix_pre_g"] = rms_bwd(h1, dhn, dh, w["mix_pre_g"], name=f"{t}_mixpre_b")
        dh, g["ffn1_pre_g"], g["ffn1_w_in"], g["ffn1_w_down"], g["ffn1_post_g"] = _ffn_bwd(
            dh, s1, w["ffn1_pre_g"], w["ffn1_w_in"], w["ffn1_w_down"], w["ffn1_post_g"], f"{t}_f1")
        grads[i] = g
    return sq, dh, grads


def _zeros_like_cols(a, n):
    return jnp.zeros(a.shape[:-1] + (n,), a.dtype)


def layer_weights(full, i):
    j = i // 2
    row = lambda v: v.reshape(1, -1)
    w = {k: row(full[k][i]) for k in ("ffn1_pre_g", "ffn1_post_g", "mix_pre_g", "mix_post_g", "ffn2_pre_g",
                                      "ffn2_post_g", "ple_pre_g", "ple_post_g")}
    for k in ("ffn1_w_in", "ffn1_w_down", "ffn2_w_in", "ffn2_w_down", "ple_w_gate", "ple_w_proj"):
        w[k] = full[k][i]
    if i % 2 == 0:
        hw = full["hyb_w_in"][j]
        w["hyb_in"] = jnp.concatenate([hw, _zeros_like_cols(hw, HYB_PAD - HYB_IN)], axis=1)
        w["hyb_out"] = full["hyb_w_out"][j]
        w["ln_g"], w["ln_b"] = row(full["gm_ln_g"][j]), row(full["gm_ln_b"][j])
        w["w_s"] = full["gm_w_s"][j].reshape(GM_HEADS * CHUNK, CHUNK)
        w["b_st"] = jnp.pad(full["gm_b_s"][j].T, ((0, 0), (0, LANES - GM_HEADS)))
        w["conv_w"], w["conv_b"] = full["ssd_conv_w"][j], row(full["ssd_conv_b"][j])
        pad16 = lambda v: jnp.pad(v.reshape(1, -1), ((0, 0), (0, LANES - SSD_HEADS)))
        w["dt_bias"], w["a_log"] = pad16(full["ssd_dt_bias"][j]), pad16(full["ssd_a_log"][j])
        w["d_exp"] = row(jnp.repeat(full["ssd_d"][j], SSD_HEAD_DIM))
        w["norm_g"] = row(full["ssd_norm_g"][j])
    else:
        wi = full["mla_w_in"][j]
        z64 = _zeros_like_cols(wi, 64)
        w["mla_in"] = jnp.concatenate([wi[:, :384], wi[:, 384:448], z64, -wi[:, 416:448], wi[:, 384:416], z64], axis=1)
        uq = full["mla_w_uq"][j].reshape(MLA_Q_LORA, MLA_HEADS, MLA_QK)
        zq = jnp.zeros((MLA_Q_LORA, MLA_HEADS, 64), uq.dtype)
        pad_part = jnp.concatenate([uq, zq], axis=2)
        swp_part = jnp.concatenate([jnp.zeros_like(uq[:, :, :128]), -uq[:, :, 160:192], uq[:, :, 128:160], zq], axis=2)
        w["uq"] = jnp.concatenate([pad_part.reshape(MLA_Q_LORA, -1), swp_part.reshape(MLA_Q_LORA, -1)], axis=1)
        ukv = full["mla_w_ukv"][j].reshape(MLA_KV_LORA, MLA_HEADS, 256)
        w["ukv"] = jnp.concatenate([ukv[:, :, :128].reshape(MLA_KV_LORA, -1), ukv[:, :, 128:].reshape(MLA_KV_LORA, -1)],
                                   axis=1)
        w["mla_out"] = full["mla_w_out"][j]
        w["q_g"], w["kv_g"] = row(full["mla_q_norm_g"][j]), row(full["mla_kv_norm_g"][j])
    return w


def full_grads(grads):
    out = {}
    stack = lambda k, idx: jnp.stack([grads[i][k] for i in idx])
    every, even, odd = range(DEPTH), range(0, DEPTH, 2), range(1, DEPTH, 2)
    for k in ("ffn1_pre_g", "ffn1_post_g", "mix_pre_g", "mix_post_g", "ffn2_pre_g", "ffn2_post_g", "ple_pre_g",
              "ple_post_g"):
        out[k] = stack(k, every).reshape(DEPTH, D_MODEL)
    for k in ("ffn1_w_in", "ffn1_w_down", "ffn2_w_in", "ffn2_w_down", "ple_w_gate", "ple_w_proj"):
        out[k] = stack(k, every)
    out["hyb_w_in"] = stack("hyb_in", even)[:, :, :HYB_IN]
    out["hyb_w_out"] = stack("hyb_out", even)
    out["gm_ln_g"] = stack("ln_g", even).reshape(2, 1024)
    out["gm_ln_b"] = stack("ln_b", even).reshape(2, 1024)
    out["gm_w_s"] = stack("w_s", even).reshape(2, GM_HEADS, CHUNK, CHUNK)
    out["gm_b_s"] = jnp.swapaxes(stack("b_st", even)[:, :, :GM_HEADS], 1, 2)
    out["ssd_conv_w"] = stack("conv_w", even)
    out["ssd_conv_b"] = stack("conv_b", even).reshape(2, SSD_CONV_CH)
    out["ssd_dt_bias"] = stack("dt_bias", even)[:, 0, :SSD_HEADS]
    out["ssd_a_log"] = stack("a_log", even)[:, 0, :SSD_HEADS]
    out["ssd_d"] = stack("d_exp", even).reshape(2, SSD_HEADS, SSD_HEAD_DIM).sum(axis=-1)
    out["ssd_norm_g"] = stack("norm_g", even).reshape(2, 1024)
    dwi = stack("mla_in", odd)
    out["mla_w_in"] = jnp.concatenate([dwi[:, :, :384], dwi[:, :, 384:416] + dwi[:, :, 544:576],
                                       dwi[:, :, 416:448] - dwi[:, :, 512:544]], axis=2)
    duq = stack("uq", odd)
    half = MLA_HEADS * MLA_QPAD
    dp = duq[:, :, :half].reshape(2, MLA_Q_LORA, MLA_HEADS, MLA_QPAD)
    ds = duq[:, :, half:].reshape(2, MLA_Q_LORA, MLA_HEADS, MLA_QPAD)
    out["mla_w_uq"] = jnp.concatenate([dp[..., :128], dp[..., 128:160] + ds[..., 160:192],
                                       dp[..., 160:192] - ds[..., 128:160]], axis=-1).reshape(2, MLA_Q_LORA, -1)
    dukv = stack("ukv", odd)
    dk = dukv[:, :, :2048].reshape(2, MLA_KV_LORA, MLA_HEADS, 128)
    dv = dukv[:, :, 2048:].reshape(2, MLA_KV_LORA, MLA_HEADS, 128)
    out["mla_w_ukv"] = jnp.concatenate([dk, dv], axis=-1).reshape(2, MLA_KV_LORA, -1)
    out["mla_w_out"] = stack("mla_out", odd)
    out["mla_q_norm_g"] = stack("q_g", odd).reshape(2, MLA_Q_LORA)
    out["mla_kv_norm_g"] = stack("kv_g", odd).reshape(2, MLA_KV_LORA)
    return out


def rope_tables(positions):
    T = positions.shape[0]
    inv = 1.0 / (ROPE_BASE ** (jnp.arange(0, MLA_ROPE, 2, dtype=F32) / MLA_ROPE))
    ang = positions.astype(F32)[:, None] * inv
    cos, sin = jnp.cos(ang), jnp.sin(ang)
    z64 = jnp.zeros((T, 64), F32)
    cs = jnp.concatenate([cos, cos, z64], axis=1)
    sn = jnp.concatenate([sin, sin, z64], axis=1)
    c256 = jnp.concatenate([jnp.ones((T, 128), F32), cs], axis=1)
    s256 = jnp.concatenate([jnp.zeros((T, 128), F32), sn], axis=1)
    return cs, sn, c256, s256


def _rows(n):
    return -(-n // LANES)


def _pack(pieces, dtype, row_multiple):
    flat = []
    total = 0
    for a in pieces:
        v = a.reshape(-1).astype(dtype)
        padn = _rows(v.shape[0]) * LANES - v.shape[0]
        if padn:
            v = jnp.concatenate([v, jnp.zeros((padn,), dtype)])
        flat.append(v)
        total += v.shape[0] // LANES
    tail = -total % row_multiple
    if tail:
        flat.append(jnp.zeros((tail * LANES,), dtype))
    return jnp.concatenate(flat).reshape(-1, LANES)


def _unpack(slab, shapes):
    out = []
    r = 0
    for s in shapes:
        n = int(np.prod(s))
        nr = _rows(n)
        out.append(slab[r:r + nr].reshape(-1)[:n].reshape(s))
        r += nr
    return out


def _shard_shape(shape, ax):
    if ax is None:
        return tuple(shape)
    s = list(shape)
    s[ax] //= N_CHIPS
    return tuple(s)


def _chip_slice(a, ax, k):
    if ax is None:
        return a
    n = a.shape[ax] // N_CHIPS
    return lax.slice_in_dim(a, k * n, (k + 1) * n, axis=ax)


def _plane_peers():
    x, y, c = lax.axis_index("x"), lax.axis_index("y"), lax.axis_index("c")
    return (x, y, c), [(1 - x, y, c), (x, 1 - y, c), (1 - x, 1 - y, c)]


ANY = pl.BlockSpec(memory_space=pl.ANY)


def plane_allgather(slab):
    R = slab.shape[0]

    def body(src, out, send_sems, recv_sems, local_sem):
        (x, y, c), peers = _plane_peers()
        me = 2 * x + y
        local = pltpu.make_async_copy(src, out.at[me], local_sem)
        local.start()
        copies = []
        for j, peer in enumerate(peers):
            cp = pltpu.make_async_remote_copy(src_ref=src, dst_ref=out.at[me], send_sem=send_sems.at[j],
                                              recv_sem=recv_sems.at[j], device_id=peer, device_id_type=MESH)
            cp.start()
            copies.append(cp)
        for cp in copies:
            cp.wait()
        local.wait()

    return pl.pallas_call(
        body, name="plane_allgather", out_shape=SDS((N_CHIPS, R, LANES), slab.dtype),
        in_specs=[ANY], out_specs=ANY,
        scratch_shapes=[pltpu.SemaphoreType.DMA((3,)), pltpu.SemaphoreType.DMA((3,)), pltpu.SemaphoreType.DMA],
    )(slab)


def plane_alltoall(buf):
    R = buf.shape[1]

    def body(src, out, send_sems, recv_sems, local_sem):
        (x, y, c), peers = _plane_peers()
        me = 2 * x + y
        local = pltpu.make_async_copy(src.at[me], out.at[me], local_sem)
        local.start()
        copies = []
        for j, peer in enumerate(peers):
            cp = pltpu.make_async_remote_copy(src_ref=src.at[2 * peer[0] + peer[1]], dst_ref=out.at[me],
                                              send_sem=send_sems.at[j], recv_sem=recv_sems.at[j], device_id=peer,
                                              device_id_type=MESH)
            cp.start()
            copies.append(cp)
        for cp in copies:
            cp.wait()
        local.wait()

    return pl.pallas_call(
        body, name="plane_alltoall", out_shape=SDS((N_CHIPS, R, LANES), buf.dtype),
        in_specs=[ANY], out_specs=ANY,
        scratch_shapes=[pltpu.SemaphoreType.DMA((3,)), pltpu.SemaphoreType.DMA((3,)), pltpu.SemaphoreType.DMA],
    )(buf)


def sibling_swap(buf):
    def body(src, out, send_sem, recv_sem):
        x, y, c = lax.axis_index("x"), lax.axis_index("y"), lax.axis_index("c")
        cp = pltpu.make_async_remote_copy(src_ref=src, dst_ref=out, send_sem=send_sem, recv_sem=recv_sem,
                                          device_id=(x, y, 1 - c), device_id_type=MESH)
        cp.start()
        cp.wait()

    return pl.pallas_call(
        body, name="sibling_swap", out_shape=SDS(buf.shape, buf.dtype), in_specs=[ANY], out_specs=ANY,
        scratch_shapes=[pltpu.SemaphoreType.DMA, pltpu.SemaphoreType.DMA],
    )(buf)


def _chip_block(ref, ax, k, n, stride=None):
    stride = n if stride is None else stride
    start = pl.multiple_of(k * stride, math.gcd(n, stride))
    return ref.at[:, pl.ds(start, n), :] if ax == 1 else ref.at[:, :, pl.ds(start, n)]


def gather_job(shards, axes):
    n = len(shards)
    fulls = []
    for s, ax in zip(shards, axes):
        if ax == "stack":
            fulls.append(SDS((N_CHIPS,) + tuple(s.shape), s.dtype))
            continue
        shape = list(s.shape)
        shape[ax] *= N_CHIPS
        fulls.append(SDS(tuple(shape), s.dtype))

    def copies(srcs, outs, sems):
        send_sems, recv_sems, local_sems = sems
        (x, y, c), peers = _plane_peers()
        me = 2 * x + y
        cps = []
        for t in range(n):
            if axes[t] == "stack":
                dst = outs[t].at[me]
            else:
                dst = _chip_block(outs[t], axes[t], me, srcs[t].shape[axes[t]])
            cps.append(pltpu.make_async_copy(srcs[t], dst, local_sems.at[t]))
            for j, peer in enumerate(peers):
                cps.append(pltpu.make_async_remote_copy(src_ref=srcs[t], dst_ref=dst, send_sem=send_sems.at[3 * t + j],
                                                        recv_sem=recv_sems.at[3 * t + j], device_id=peer,
                                                        device_id_type=MESH))
        return cps

    def start(srcs, outs, sems):
        for cp in copies(srcs, outs, sems):
            cp.start()

    def finish(srcs, outs, sems):
        for cp in copies(srcs, outs, sems):
            cp.wait()

    scratch = [pltpu.SemaphoreType.DMA((3 * n,)), pltpu.SemaphoreType.DMA((3 * n,)), pltpu.SemaphoreType.DMA((n,))]
    return dict(inputs=list(shards), out_shape=fulls, scratch=scratch, phases=(start, None, finish))


def split_gather_job(shards, axes):
    n = len(shards)
    plain = gather_job(shards, axes)

    def region(out, t, chip, shard_shape, half_of):
        ax = axes[t]
        ref = out.at[chip] if ax == "stack" else out
        starts, sizes = [0, 0, 0], list(shard_shape)
        if ax != "stack":
            starts[ax] = chip * shard_shape[ax]
        if half_of is not None:
            sa = 0 if shard_shape[0] % 2 == 0 else 1
            sizes[sa] = shard_shape[sa] // 2
            starts[sa] = starts[sa] + half_of * sizes[sa]
        starts = [s if isinstance(s, int) else pl.multiple_of(s, math.gcd(full, z))
                  for s, z, full in zip(starts, sizes, shard_shape)]
        return ref.at[tuple(pl.ds(s, z) for s, z in zip(starts, sizes))]

    def copies(srcs, outs, sems):
        send_sems, recv_sems, local_sems, fsend_sems, frecv_sems = sems
        (x, y, c), peers = _plane_peers()
        sibling = (x, y, 1 - c)
        me = 2 * x + y
        locals_, sends, forwards = [], [], []
        for t in range(n):
            shape = srcs[t].shape
            locals_.append(pltpu.make_async_copy(srcs[t], region(outs[t], t, me, shape, None), local_sems.at[t]))
            sa = 0 if shape[0] % 2 == 0 else 1
            hs = shape[sa] // 2
            mine = srcs[t].at[pl.ds(c * hs, hs)] if sa == 0 else srcs[t].at[:, pl.ds(c * hs, hs), :]
            for j, (px, py, _) in enumerate(peers):
                sends.append(pltpu.make_async_remote_copy(
                    src_ref=mine, dst_ref=region(outs[t], t, me, shape, c), send_sem=send_sems.at[3 * t + j],
                    recv_sem=recv_sems.at[3 * t + j], device_id=(px, py, c), device_id_type=MESH))
                landed = region(outs[t], t, 2 * px + py, shape, c)
                forwards.append(pltpu.make_async_remote_copy(
                    src_ref=landed, dst_ref=landed, send_sem=fsend_sems.at[3 * t + j],
                    recv_sem=frecv_sems.at[3 * t + j], device_id=sibling, device_id_type=MESH))
        return locals_, sends, forwards

    def start(srcs, outs, sems):
        locals_, sends, _ = copies(srcs, outs, sems)
        for cp in locals_ + sends:
            cp.start()

    def middle(srcs, outs, sems):
        _, sends, forwards = copies(srcs, outs, sems)
        for cp, fw in zip(sends, forwards):
            cp.wait_recv()
            fw.start()

    def finish(srcs, outs, sems):
        locals_, sends, forwards = copies(srcs, outs, sems)
        for cp in locals_:
            cp.wait()
        for cp in sends:
            cp.wait_send()
        for fw in forwards:
            fw.wait()

    scratch = [pltpu.SemaphoreType.DMA((3 * n,)) for _ in range(2)] + [pltpu.SemaphoreType.DMA((n,))] + \
              [pltpu.SemaphoreType.DMA((3 * n,)) for _ in range(2)]
    return dict(inputs=list(shards), out_shape=plain["out_shape"], scratch=scratch, phases=(start, middle, finish))


def exchange_job(grads, axes):
    n = len(grads)
    outs = []
    spans = []
    for g, ax in zip(grads, axes):
        if ax == "lead":
            spans.append(None)
            outs.append(SDS(tuple(g.shape), g.dtype))
            continue
        ax, width, stride = ax if isinstance(ax, tuple) else (ax, g.shape[ax] // N_CHIPS, None)
        spans.append((ax, width, stride))
        shape = list(g.shape)
        shape[ax] = width
        outs.append(SDS((N_CHIPS,) + tuple(shape), g.dtype))

    def block_for(src, t, chip):
        return src.at[chip] if spans[t] is None else _chip_block(src, spans[t][0], chip, spans[t][1], spans[t][2])

    def copies(srcs, res, sems):
        mine, theirs = res[:n], res[n:]
        send_sems, recv_sems, local_sems, fsend_sems, frecv_sems = sems
        (x, y, c), peers = _plane_peers()
        sibling = (x, y, 1 - c)
        me = 2 * x + y
        blocks = [me] + [2 * px + py for (px, py, _) in peers]
        locals_, sends, forwards = [], [], []
        for t in range(n):
            locals_.append(pltpu.make_async_copy(block_for(srcs[t], t, me), mine[t].at[me], local_sems.at[t]))
            for j, peer in enumerate(peers):
                sends.append(pltpu.make_async_remote_copy(
                    src_ref=block_for(srcs[t], t, blocks[j + 1]), dst_ref=mine[t].at[me],
                    send_sem=send_sems.at[3 * t + j], recv_sem=recv_sems.at[3 * t + j], device_id=peer,
                    device_id_type=MESH))
            for q, blk in enumerate(blocks):
                forwards.append(pltpu.make_async_remote_copy(
                    src_ref=mine[t].at[blk], dst_ref=theirs[t].at[blk], send_sem=fsend_sems.at[4 * t + q],
                    recv_sem=frecv_sems.at[4 * t + q], device_id=sibling, device_id_type=MESH))
        return locals_, sends, forwards

    def start(srcs, res, sems):
        locals_, sends, _ = copies(srcs, res, sems)
        for cp in locals_ + sends:
            cp.start()

    def middle(srcs, res, sems):
        locals_, sends, forwards = copies(srcs, res, sems)
        for t in range(n):
            locals_[t].wait()
            for q in range(N_CHIPS):
                if q > 0:
                    sends[3 * t + q - 1].wait_recv()
                forwards[4 * t + q].start()

    def finish(srcs, res, sems):
        _, sends, forwards = copies(srcs, res, sems)
        for cp in sends:
            cp.wait_send()
        for fw in forwards:
            fw.wait()

    scratch = [pltpu.SemaphoreType.DMA((3 * n,)), pltpu.SemaphoreType.DMA((3 * n,)), pltpu.SemaphoreType.DMA((n,)),
               pltpu.SemaphoreType.DMA((4 * n,)), pltpu.SemaphoreType.DMA((4 * n,))]
    return dict(inputs=list(grads), out_shape=outs + outs, scratch=scratch, phases=(start, middle, finish))


def run_job(job, ins, outs, sems, first=None, mid=None, last=None):
    for phase, when in zip(job["phases"], (first, mid, last)):
        if phase is None:
            continue
        if when is None:
            phase(ins, outs, sems)
        else:
            pl.when(when)(functools.partial(phase, ins, outs, sems))


def job_call(job, *, name):
    ni, no = len(job["inputs"]), len(job["out_shape"])

    def body(*refs):
        run_job(job, refs[:ni], refs[ni:ni + no], refs[ni + no:])

    return pl.pallas_call(body, name=name, out_shape=job["out_shape"], in_specs=[ANY] * ni, out_specs=[ANY] * no,
                          scratch_shapes=job["scratch"])(*job["inputs"])


def _adam_update(g, w, m, v):
    mn = ADAM_B1 * m + (1.0 - ADAM_B1) * g
    vn = ADAM_B2 * v + (1.0 - ADAM_B2) * jnp.square(g)
    m_hat = mn / (1.0 - ADAM_B1 ** ADAM_STEP)
    v_hat = vn / (1.0 - ADAM_B2 ** ADAM_STEP)
    return -ADAM_LR * (m_hat / (jnp.sqrt(v_hat) + ADAM_EPS) + ADAM_WD * w), mn, vn


def _row_tile(rs):
    for cand in range(256, 15, -16):
        if rs % cand == 0:
            return cand
    return rs


def _group_sum(half, a0_r, b0_r, a1_r, b1_r):
    def plane_sums(a_r, b_r):
        pa = a_r[0].astype(F32)
        pb = b_r[0].astype(F32)
        for k in range(1, N_CHIPS):
            pa = pa + a_r[k].astype(F32)
            pb = pb + b_r[k].astype(F32)
        return pa + pb
    return jnp.where(pl.program_id(0) < half, plane_sums(a0_r, b0_r), plane_sums(a1_r, b1_r))


def _group_specs(half, tr, cs):
    first = pl.BlockSpec((N_CHIPS, None, tr, cs),
                         lambda l, i: (0, jnp.minimum(l, half - 1), jnp.where(l < half, i, 0), 0))
    second = pl.BlockSpec((N_CHIPS, None, tr, cs),
                          lambda l, i: (0, jnp.maximum(l - half, 0), jnp.where(l < half, 0, i), 0))
    return [first, first, second, second]


def adamw_reg(groups, w, m, v, *, name):
    L, rs, cs = w.shape
    tr = _row_tile(rs)
    early = groups[0][0].shape[1]

    def body(a0_r, b0_r, a1_r, b1_r, w_r, m_r, v_r, g_o, d_o, m_o, v_o):
        g = _group_sum(early, a0_r, b0_r, a1_r, b1_r)
        d, mn, vn = _adam_update(g, w_r[...], m_r[...], v_r[...])
        g_o[...] = g
        d_o[...] = d
        m_o[...] = mn
        v_o[...] = vn

    s1 = pl.BlockSpec((None, tr, cs), lambda l, i: (l, i, 0))
    (a0, b0), (a1, b1) = groups
    return pl.pallas_call(
        body, name=name, grid=(L, rs // tr), in_specs=_group_specs(early, tr, cs) + [s1, s1, s1], out_specs=[s1] * 4,
        out_shape=[SDS((L, rs, cs), F32)] * 4,
        compiler_params=pltpu.CompilerParams(dimension_semantics=("parallel", "parallel"),
                                             vmem_limit_bytes=VMEM_LIMIT),
    )(a0, b0, a1, b1, w, m, v)


def groups_sum(groups, *, name):
    (a0, b0), (a1, b1) = groups
    _, half, rs, cs = a0.shape
    L = half + a1.shape[1]
    tr = _row_tile(rs)

    def body(a0_r, b0_r, a1_r, b1_r, g_o):
        g_o[...] = _group_sum(half, a0_r, b0_r, a1_r, b1_r)

    return pl.pallas_call(
        body, name=name, grid=(L, rs // tr), in_specs=_group_specs(half, tr, cs),
        out_specs=pl.BlockSpec((None, tr, cs), lambda l, i: (l, i, 0)), out_shape=SDS((L, rs, cs), F32),
        compiler_params=pltpu.CompilerParams(dimension_semantics=("parallel", "parallel"),
                                             vmem_limit_bytes=VMEM_LIMIT),
    )(a0, b0, a1, b1)


def adamw_plain(g, w, m, v, *, name):
    L, rs, cs = w.shape
    tr = _row_tile(rs)

    def body(g_r, w_r, m_r, v_r, d_o, m_o, v_o):
        d, mn, vn = _adam_update(g_r[...], w_r[...], m_r[...], v_r[...])
        d_o[...] = d
        m_o[...] = mn
        v_o[...] = vn

    s1 = pl.BlockSpec((None, tr, cs), lambda l, i: (l, i, 0))
    return pl.pallas_call(
        body, name=name, grid=(L, rs // tr), in_specs=[s1] * 4, out_specs=[s1] * 3,
        out_shape=[SDS((L, rs, cs), F32)] * 3,
        compiler_params=pltpu.CompilerParams(dimension_semantics=("parallel", "parallel"),
                                             vmem_limit_bytes=VMEM_LIMIT),
    )(g, w, m, v)


def plane_sum(r4):
    R = r4.shape[1]

    def body(r_ref, o_ref):
        acc = r_ref[0].astype(F32)
        for k in range(1, N_CHIPS):
            acc = acc + r_ref[k].astype(F32)
        o_ref[...] = acc

    return pl.pallas_call(
        body, name="plane_sum", grid=(R // PACK_ROWS,),
        in_specs=[pl.BlockSpec((N_CHIPS, PACK_ROWS, LANES), lambda i: (0, i, 0))],
        out_specs=pl.BlockSpec((PACK_ROWS, LANES), lambda i: (i, 0)), out_shape=SDS((R, LANES), F32),
        compiler_params=pltpu.CompilerParams(dimension_semantics=("parallel",)),
    )(r4)


def adamw(pa, pb, w, m, v):
    R = w.shape[0]

    def body(pa_r, pb_r, w_r, m_r, v_r, g_o, d_o, m_o, v_o):
        g = pa_r[...] + pb_r[...]
        d, mn, vn = _adam_update(g, w_r[...], m_r[...], v_r[...])
        g_o[...] = g
        d_o[...] = d
        m_o[...] = mn
        v_o[...] = vn

    spec = pl.BlockSpec((PACK_ROWS, LANES), lambda i: (i, 0))
    return pl.pallas_call(
        body, name="adamw", grid=(R // PACK_ROWS,), in_specs=[spec] * 5, out_specs=[spec] * 4,
        out_shape=[SDS((R, LANES), F32)] * 4,
        compiler_params=pltpu.CompilerParams(dimension_semantics=("parallel",)),
    )(pa, pb, w, m, v)


GATHER_AXES = [WSPEC[n][2] for n in REG] + ["stack"]
EXCHANGE_AXES = [WSPEC[n][2] for n in REG] + [(2, WIN_WIDTH, WIN_STRIDE)]


EARLY = {n: WSPEC[n][1][0] // 2 for n in MOVED}
EARLY.update(ffn2_w_in=1, ffn2_w_down=1, ple_w_gate=1, ple_w_proj=1)


def _group_range(n, group):
    return range(EARLY[n]) if group == 0 else range(EARLY[n], WSPEC[n][1][0])


def moved_shards(wl, group):
    out = []
    for n in MOVED:
        r = _group_range(n, group)
        out.append(wl[n][r.start:r.stop].astype(BF16))
    return out


def moved_grads(grads, group, partial=None):
    out = []
    for n in MOVED:
        if WSPEC[n][1][0] == DEPTH:
            layer_of, key = (lambda s: s), n
        elif n == "mla_w_out":
            layer_of, key = (lambda s: 2 * s + 1), "mla_out"
        else:
            layer_of, key = (lambda s: 2 * s), {"hyb_w_out": "hyb_out", WIN: "hyb_in"}[n]
        per_layer = [grads[layer_of(s)] if grads[layer_of(s)] is not None else partial for s in _group_range(n, group)]
        out.append(jnp.stack([d[key] for d in per_layer]))
    return out


def gather_misc(wl):
    full = {}
    sharded = [n for n in MISC if WSPEC[n][2] is not None]
    pieces = [wl[n].astype(BF16) if WSPEC[n][3] else lax.bitcast_convert_type(wl[n], BF16) for n in sharded]
    got = plane_allgather(_pack(pieces, BF16, 16))
    shapes = [v.shape for v in pieces]
    per_chip = [_unpack(got[k], shapes) for k in range(N_CHIPS)]
    for idx, n in enumerate(sharded):
        parts = [per_chip[k][idx] for k in range(N_CHIPS)]
        if not WSPEC[n][3]:
            parts = [lax.bitcast_convert_type(v, F32) for v in parts]
        full[n] = jnp.concatenate(parts, axis=WSPEC[n][2])
    for n in MISC:
        if WSPEC[n][2] is None:
            full[n] = wl[n]
    return full


def kernel(x, p, positions, ffn1_pre_g, ffn1_w_in, ffn1_w_down, ffn1_post_g, mix_pre_g, mix_post_g, ffn2_pre_g, ffn2_w_in, ffn2_w_down, ffn2_post_g, ple_pre_g, ple_w_gate, ple_w_proj, ple_post_g, hyb_w_in, gm_ln_g, gm_ln_b, gm_w_s, gm_b_s, ssd_conv_w, ssd_conv_b, ssd_dt_bias, ssd_a_log, ssd_d, ssd_norm_g, hyb_w_out, mla_w_in, mla_q_norm_g, mla_kv_norm_g, mla_w_uq, mla_w_ukv, mla_w_out, loss_target, m_ffn1_pre_g, m_ffn1_w_in, m_ffn1_w_down, m_ffn1_post_g, m_mix_pre_g, m_mix_post_g, m_ffn2_pre_g, m_ffn2_w_in, m_ffn2_w_down, m_ffn2_post_g, m_ple_pre_g, m_ple_w_gate, m_ple_w_proj, m_ple_post_g, m_hyb_w_in, m_gm_ln_g, m_gm_ln_b, m_gm_w_s, m_gm_b_s, m_ssd_conv_w, m_ssd_conv_b, m_ssd_dt_bias, m_ssd_a_log, m_ssd_d, m_ssd_norm_g, m_hyb_w_out, m_mla_w_in, m_mla_q_norm_g, m_mla_kv_norm_g, m_mla_w_uq, m_mla_w_ukv, m_mla_w_out, v_ffn1_pre_g, v_ffn1_w_in, v_ffn1_w_down, v_ffn1_post_g, v_mix_pre_g, v_mix_post_g, v_ffn2_pre_g, v_ffn2_w_in, v_ffn2_w_down, v_ffn2_post_g, v_ple_pre_g, v_ple_w_gate, v_ple_w_proj, v_ple_post_g, v_hyb_w_in, v_gm_ln_g, v_gm_ln_b, v_gm_w_s, v_gm_b_s, v_ssd_conv_w, v_ssd_conv_b, v_ssd_dt_bias, v_ssd_a_log, v_ssd_d, v_ssd_norm_g, v_hyb_w_out, v_mla_w_in, v_mla_q_norm_g, v_mla_kv_norm_g, v_mla_w_uq, v_mla_w_ukv, v_mla_w_out):
    args = locals()
    wl = {n: args[n] for n in WNAMES}
    ml = {n: args["m_" + n] for n in WNAMES}
    vl = {n: args["v_" + n] for n in WNAMES}

    full = gather_misc(wl)
    nmoved = len(MOVED)
    for n in MOVED:
        full[n] = [None] * WSPEC[n][1][0]
    exchanged = [None, None]

    def gathered(group, outs):
        for t, n in enumerate(MOVED):
            for l, s in enumerate(_group_range(n, group)):
                if n == WIN:
                    full[n][s] = jnp.concatenate([outs[t][k, l] for k in range(N_CHIPS)], axis=-1)
                else:
                    full[n][s] = (outs[t], l)

    gathered(0, job_call(split_gather_job(moved_shards(wl, 0), GATHER_AXES), name="gather_first"))

    class Jobs(NoJobs):
        def fwd_job(self, i):
            return gather_job(moved_shards(wl, 1), GATHER_AXES) if i == 1 else None

        def fwd_done(self, i, outs):
            gathered(1, outs)

        def bwd_job(self, i, grads, partial):
            return exchange_job(moved_grads(grads, 1, partial), EXCHANGE_AXES) if i == 1 else None

        def bwd_done(self, i, outs):
            exchanged[1] = outs

    rope = rope_tables(positions[0])
    T = x.shape[1]
    sq, dx, grads = local_step(x[0], p.reshape(DEPTH, T, p.shape[-1]), rope, loss_target[0],
                               lambda i: layer_weights(full, i), Jobs())
    loss = lax.psum(0.5 * jnp.sum(sq) / D_MODEL, ("x", "y", "c"))

    res = {}
    fg = full_grads(grads)
    dest = jnp.stack([_pack([_chip_slice(fg[n], WSPEC[n][2], k) for n in MISC], BF16, PACK_ROWS)[None]
                      for k in range(N_CHIPS)])
    early = job_call(exchange_job(moved_grads(grads, 0) + [dest], EXCHANGE_AXES + ["lead"]), name="exchange_first")
    packed = (early[nmoved], early[2 * nmoved + 1])
    exchanged[0] = early[:nmoved] + early[nmoved + 1:2 * nmoved + 1]
    for t, n in enumerate(MOVED):
        groups = [(e[t], e[nmoved + t]) for e in exchanged]
        if n == WIN:
            window = groups_sum(groups, name=f"sum_{n}")
            chip = 2 * lax.axis_index("x") + lax.axis_index("y")
            g = lax.dynamic_slice_in_dim(window, chip * (WIN_SHARD - WIN_STRIDE), WIN_SHARD, axis=2)
            res[n] = [g] + list(adamw_plain(g, wl[n], ml[n], vl[n], name=f"adamw_{n}"))
        else:
            res[n] = adamw_reg(groups, wl[n], ml[n], vl[n], name=f"adamw_{n}")
    slabs = adamw_reg([packed, packed], *[_pack([d[n] for n in MISC], F32, PACK_ROWS)[None] for d in (wl, ml, vl)],
                      name="adamw_packed")
    shapes = [wl[n].shape for n in MISC]
    unpacked = [_unpack(s[0], shapes) for s in slabs]
    for idx, n in enumerate(MISC):
        res[n] = [u[idx] for u in unpacked]
    return (loss, dx[None], *[res[n][k] for k in range(4) for n in WNAMES])
```

```python
import functools
import math

import jax
import jax.numpy as jnp
import numpy as np
from jax import lax
from jax.experimental import pallas as pl
from jax.experimental.pallas import tpu as pltpu

F32 = jnp.float32
BF16 = jnp.bfloat16
SDS = jax.ShapeDtypeStruct
MESH = pl.DeviceIdType.MESH
HIGHEST = lax.Precision.HIGHEST

D_MODEL = 1024
DEPTH = 4
D_FF = 2816
NORM_EPS = 1e-6
LN_EPS = 1e-5
GM_HEADS = 8
CHUNK = 128
SSD_HEADS = 16
SSD_HEAD_DIM = 64
SSD_INNER = 1024
SSD_STATE = 128
SSD_CONV = 4
SSD_CONV_CH = 1536
HYB_MAIN = 4608
HYB_IN = 4624
HYB_PAD = 5120
MLA_HEADS = 16
MLA_NOPE = 128
MLA_ROPE = 64
MLA_QK = 192
MLA_QPAD = 256
MLA_Q_LORA = 256
MLA_KV_LORA = 128
ROPE_BASE = 10000.0
ADAM_LR = 0.001
ADAM_B1 = 0.9
ADAM_B2 = 0.999
ADAM_EPS = 1e-08
ADAM_WD = 0.01
ADAM_STEP = 10

N_CHIPS = 4
LANES = 128
VMEM_LIMIT = 56 * 1024 * 1024
PACK_ROWS = 2048

WEIGHTS = [
    ("ffn1_pre_g", (4, 1024), None, False),
    ("ffn1_w_in", (4, 1024, 5632), 2, True),
    ("ffn1_w_down", (4, 2816, 1024), 1, True),
    ("ffn1_post_g", (4, 1024), None, False),
    ("mix_pre_g", (4, 1024), None, False),
    ("mix_post_g", (4, 1024), None, False),
    ("ffn2_pre_g", (4, 1024), None, False),
    ("ffn2_w_in", (4, 1024, 5632), 2, True),
    ("ffn2_w_down", (4, 2816, 1024), 1, True),
    ("ffn2_post_g", (4, 1024), None, False),
    ("ple_pre_g", (4, 1024), None, False),
    ("ple_w_gate", (4, 1024, 1024), 1, True),
    ("ple_w_proj", (4, 256, 1024), 2, True),
    ("ple_post_g", (4, 1024), None, False),
    ("hyb_w_in", (2, 1024, 4624), 2, True),
    ("gm_ln_g", (2, 1024), None, False),
    ("gm_ln_b", (2, 1024), None, False),
    ("gm_w_s", (2, 8, 128, 128), None, False),
    ("gm_b_s", (2, 8, 128), None, False),
    ("ssd_conv_w", (2, 4, 1536), 2, False),
    ("ssd_conv_b", (2, 1536), None, False),
    ("ssd_dt_bias", (2, 16), None, False),
    ("ssd_a_log", (2, 16), None, False),
    ("ssd_d", (2, 16), None, False),
    ("ssd_norm_g", (2, 1024), None, False),
    ("hyb_w_out", (2, 2048, 1024), 1, True),
    ("mla_w_in", (2, 1024, 448), 1, True),
    ("mla_q_norm_g", (2, 256), 1, False),
    ("mla_kv_norm_g", (2, 128), None, False),
    ("mla_w_uq", (2, 256, 3072), 2, True),
    ("mla_w_ukv", (2, 128, 4096), 2, True),
    ("mla_w_out", (2, 2048, 1024), 1, True),
]
WNAMES = [w[0] for w in WEIGHTS]
WSPEC = {w[0]: w for w in WEIGHTS}
REG = ["ffn1_w_in", "ffn1_w_down", "ffn2_w_in", "ffn2_w_down", "ple_w_gate", "ple_w_proj", "hyb_w_out", "mla_w_out"]
WIN = "hyb_w_in"
WIN_SHARD = 4624 // 4
WIN_STRIDE = (WIN_SHARD // 128) * 128
WIN_WIDTH = -(-(WIN_SHARD + 3 * (WIN_SHARD - WIN_STRIDE)) // 128) * 128
MOVED = REG + [WIN]
MISC = [n for n in WNAMES if n not in MOVED]


def _pick(dim, target):
    if dim <= target:
        return dim
    t = (target // LANES) * LANES
    while t >= LANES:
        if dim % t == 0:
            return t
        t -= LANES
    return dim


def mm(a, b, *, ta=False, tb=False, out_dtype=F32, name, tm=1024, tn=1024, tk=1024):
    a, la = a if isinstance(a, tuple) else (a, None)
    b, lb = b if isinstance(b, tuple) else (b, None)

    def dims(x, mode):
        r, c = x.shape[-2:]
        return (r, c * x.shape[0]) if mode == "planes" else (r, c)

    K, M = dims(a, la) if ta else dims(a, la)[::-1]
    N, K2 = dims(b, lb) if tb else dims(b, lb)[::-1]
    assert K == K2, (a.shape, b.shape, ta, tb)
    assert not (la == "planes" and ta) and not (lb == "planes" and tb)
    bm, bn = _pick(M, tm), _pick(b.shape[-1] if lb == "planes" else N, tn)
    bk = _pick(a.shape[-1] if la == "planes" else K, tk)
    nk = K // bk

    def spec(shape, idx, layer, x):
        if layer is None:
            return pl.BlockSpec(shape, idx)
        if layer == "planes":
            per = x.shape[-1] // shape[1]
            return pl.BlockSpec((None,) + shape, lambda i, j, k: (idx(i, j, k)[1] // per, idx(i, j, k)[0],
                                                                  idx(i, j, k)[1] % per))
        return pl.BlockSpec((None,) + shape, lambda i, j, k: (layer,) + idx(i, j, k))

    a_spec = spec((bk, bm), lambda i, j, k: (k, i), la, a) if ta else spec((bm, bk), lambda i, j, k: (i, k), la, a)
    b_spec = spec((bn, bk), lambda i, j, k: (j, k), lb, b) if tb else spec((bk, bn), lambda i, j, k: (k, j), lb, b)
    dn = (((0 if ta else 1,), (1 if tb else 0,)), ((), ()))

    def body(a_ref, b_ref, o_ref, acc_ref):
        k = pl.program_id(2)

        @pl.when(k == 0)
        def _():
            acc_ref[...] = jnp.zeros_like(acc_ref)

        acc_ref[...] += lax.dot_general(a_ref[...].astype(BF16), b_ref[...].astype(BF16), dn,
                                        preferred_element_type=F32)

        @pl.when(k == nk - 1)
        def _():
            o_ref[...] = acc_ref[...].astype(o_ref.dtype)

    return pl.pallas_call(
        body, name=name, grid=(M // bm, N // bn, nk),
        in_specs=[a_spec, b_spec], out_specs=pl.BlockSpec((bm, bn), lambda i, j, k: (i, j)),
        out_shape=SDS((M, N), out_dtype), scratch_shapes=[pltpu.VMEM((bm, bn), F32)],
        compiler_params=pltpu.CompilerParams(dimension_semantics=("parallel", "parallel", "arbitrary"),
                                             vmem_limit_bytes=VMEM_LIMIT),
    )(a, b)


def row_call(fn, xs, ps, outs, accs=(), *, tb, name, reverse=False):
    xs = [x if isinstance(x, tuple) else (x, x.shape[1], 0) for x in xs]
    T = xs[0][0].shape[0]
    tb = min(tb, T)
    n = T // tb
    assert n * tb == T
    nx, npar, no, na = len(xs), len(ps), len(outs), len(accs)

    def ridx(i):
        return n - 1 - i if reverse else i

    in_specs = [pl.BlockSpec((tb, w), functools.partial(lambda i, cb: (ridx(i), cb), cb=cb)) for (_, w, cb) in xs]
    in_specs += [pl.BlockSpec(p.shape, functools.partial(lambda i, nd: (0,) * nd, nd=p.ndim)) for p in ps]
    out_specs = [pl.BlockSpec((tb, c), lambda i: (ridx(i), 0)) for (c, _) in outs]
    out_specs += [pl.BlockSpec(s, functools.partial(lambda i, nd: (0,) * nd, nd=len(s))) for s in accs]
    out_shape = [SDS((T, c), dt) for (c, dt) in outs] + [SDS(s, F32) for s in accs]

    def body(*refs):
        xr, pr = refs[:nx], refs[nx:nx + npar]
        orf, ar = refs[nx + npar:nx + npar + no], refs[nx + npar + no:]
        res = fn(*[r[...] for r in xr], *[r[...] for r in pr])
        for r, v in zip(orf, res[:no]):
            r[...] = v.astype(r.dtype)
        if na:
            @pl.when(pl.program_id(0) == 0)
            def _():
                for r in ar:
                    r[...] = jnp.zeros_like(r)

            for r, v in zip(ar, res[no:]):
                r[...] += v.astype(F32)

    res = pl.pallas_call(
        body, name=name, grid=(n,), in_specs=in_specs, out_specs=out_specs, out_shape=out_shape,
        compiler_params=pltpu.CompilerParams(dimension_semantics=("arbitrary",), vmem_limit_bytes=VMEM_LIMIT),
    )(*[x[0] for x in xs], *ps)
    return res


def _f32(*a):
    return [v.astype(F32) for v in a]


def t_rms(x, g):
    return x * lax.rsqrt(jnp.mean(x * x, axis=-1, keepdims=True) + NORM_EPS) * g


def t_swiglu(gate, up):
    return jax.nn.silu(gate) * up


def t_ple(gl, pp, g):
    return t_rms(jax.nn.sigmoid(gl) * pp, g)


def _iota(shape, d):
    return lax.broadcasted_iota(jnp.int32, shape, d)


def _bdot(a, b, dn=(((1,), (0,)), ((), ()))):
    return lax.dot_general(a.astype(BF16), b.astype(BF16), dn, preferred_element_type=F32)


def _hdot(a, b):
    return jnp.dot(a, b, precision=HIGHEST, preferred_element_type=F32)


NT = (((1,), (1,)), ((), ()))
TN = (((0,), (0,)), ((), ()))


def t_gmlp(uv, ln_g, ln_b, w_s, b_st):
    tb = uv.shape[0]
    guv = jax.nn.gelu(uv)
    u, v = guv[:, :1024], guv[:, 1024:]
    tri = _iota((CHUNK, CHUNK), 1) <= _iota((CHUNK, CHUNK), 0)
    rows = []
    for c in range(tb // CHUNK):
        vc = v[c * CHUNK:(c + 1) * CHUNK]
        heads = []
        for h in range(GM_HEADS):
            sl = slice(h * 128, (h + 1) * 128)
            vh = vc[:, sl]
            xc = vh - jnp.mean(vh, axis=-1, keepdims=True)
            var = jnp.mean(xc * xc, axis=-1, keepdims=True)
            y = xc * lax.rsqrt(var + LN_EPS) * ln_g[:, sl] + ln_b[:, sl]
            wm = jnp.where(tri, w_s[sl, :], 0.0)
            heads.append(_bdot(wm, y) + b_st[:, h:h + 1])
        rows.append(jnp.concatenate(heads, axis=1))
    mixed = rows[0] if len(rows) == 1 else jnp.concatenate(rows, axis=0)
    return u * mixed


def t_ssd(pre, dtr, z, st, dt_bias, a_log, d_exp, norm_g):
    L = CHUNK
    xbc = jax.nn.silu(pre)
    xs, bm, cm = xbc[:, :1024], xbc[:, 1024:1280], xbc[:, 1280:1536]
    valid = _iota((1, LANES), 1) < SSD_HEADS
    dt16 = jnp.where(valid, jax.nn.softplus(dtr + dt_bias), 0.0)
    a16 = jnp.where(valid, -jnp.exp(a_log), 0.0)
    da16 = dt16 * a16
    tri = _iota((L, L), 1) <= _iota((L, L), 0)
    acs16 = _hdot(tri.astype(F32), da16)
    hh, cc = _iota((LANES, 1024), 0), _iota((LANES, 1024), 1)
    expand = ((cc >= hh * SSD_HEAD_DIM) & (cc < (hh + 1) * SSD_HEAD_DIM)).astype(F32)
    acs = _hdot(acs16, expand)
    dte = _hdot(dt16, expand)
    alast = jnp.sum(jnp.where(_iota((L, 1024), 0) == L - 1, acs, 0.0), axis=0, keepdims=True)
    xd = xs * dte
    groups = [slice(0, 512), slice(512, 1024)]
    bg = [bm[:, :128], bm[:, 128:]]
    cg = [cm[:, :128], cm[:, 128:]]
    yoff = jnp.concatenate([_bdot(cg[g], st[:, groups[g]]) for g in range(2)], axis=1) * jnp.exp(acs)
    xdw = xd * jnp.exp(alast - acs)
    s_t = jnp.concatenate([_bdot(bg[g], xdw[:, groups[g]], TN) for g in range(2)], axis=1)
    st_new = st * jnp.exp(alast) + s_t
    cb = [_bdot(cg[g], bg[g], NT) for g in range(2)]
    acs16_t = acs16.T
    lo = _iota((1, LANES), 1) < SSD_HEAD_DIM
    slabs = []
    for j in range(SSD_HEADS // 2):
        g = j // 4
        xslab = xd[:, j * 128:(j + 1) * 128]
        acc = None
        for half in range(2):
            h = 2 * j + half
            seg = acs16[:, h:h + 1] - acs16_t[h:h + 1, :]
            mmat = cb[g] * jnp.exp(jnp.where(tri, seg, -1e30))
            xm = jnp.where(lo if half == 0 else jnp.logical_not(lo), xslab, 0.0)
            term = _bdot(mmat, xm)
            acc = term if acc is None else acc + term
        slabs.append(acc)
    y = jnp.concatenate(slabs, axis=1) + yoff + d_exp * xs
    yg = y * jax.nn.silu(z)
    outs = []
    for g in range(2):
        t = yg[:, groups[g]]
        outs.append(t * lax.rsqrt(jnp.mean(t * t, axis=-1, keepdims=True) + NORM_EPS) * norm_g[:, groups[g]])
    return jnp.concatenate(outs, axis=1), st_new


def t_kprep(c_all, cs, sn, qg, kvg):
    cqn = t_rms(c_all[:, :256], qg)
    ckvn = t_rms(c_all[:, 256:384], kvg)
    kr = c_all[:, 384:512] * cs + c_all[:, 512:640] * sn
    return cqn, ckvn, kr


def t_qrope(qb, c256, s256):
    scale = MLA_QK ** -0.5
    half = MLA_HEADS * MLA_QPAD
    outs = []
    for h in range(MLA_HEADS):
        a = qb[:, h * MLA_QPAD:(h + 1) * MLA_QPAD]
        b = qb[:, half + h * MLA_QPAD:half + (h + 1) * MLA_QPAD]
        outs.append((a * c256 + b * s256) * scale)
    return jnp.concatenate(outs, axis=1)


def rms_fwd(h, g, *, name, tb=512):
    def fn(h, g):
        return (t_rms(h.astype(F32), g),)
    return row_call(fn, [h], [g], [(h.shape[1], BF16)], tb=tb, name=name)[0]


def rms_bwd(h, dhn, dres, g, *, name, tb=512):
    def fn(h, dhn, dres, g):
        h, dhn, dres = _f32(h, dhn, dres)
        _, vjp = jax.vjp(t_rms, h, g)
        dh, dg = vjp(dhn)
        return dres + dh, dg
    return row_call(fn, [h, dhn, dres], [g], [(h.shape[1], F32)], [g.shape], tb=tb, name=name)


def post_fwd(h, f, g, scale, *, name, tb=512):
    def fn(h, f, g):
        return (h + scale * t_rms(f.astype(F32), g),)
    return row_call(fn, [h, f], [g], [(h.shape[1], F32)], tb=tb, name=name)[0]


def post_bwd(f, dout, g, scale, *, name, tb=512):
    def fn(f, dout, g):
        f, dout = _f32(f, dout)
        _, vjp = jax.vjp(lambda f, g: scale * t_rms(f, g), f, g)
        return vjp(dout)
    return row_call(fn, [f, dout], [g], [(f.shape[1], BF16)], [g.shape], tb=tb, name=name)


FFN_TILE = D_FF // 2


def _stacked(w):
    return w if isinstance(w, tuple) else (w[None], 0)


def ffn_in_act(hn, w_in, *, name, tm=512):
    w, layer = _stacked(w_in)
    T, K = hn.shape
    bm, bn = _pick(T, tm), FFN_TILE
    nj = D_FF // bn

    def body(x_ref, wg_ref, wu_ref, gu_ref, act_ref):
        x = x_ref[...].astype(BF16)
        g = jnp.dot(x, wg_ref[...].astype(BF16), preferred_element_type=F32)
        u = jnp.dot(x, wu_ref[...].astype(BF16), preferred_element_type=F32)
        gu_ref[0] = g.astype(gu_ref.dtype)
        gu_ref[1] = u.astype(gu_ref.dtype)
        act_ref[...] = t_swiglu(g, u).astype(act_ref.dtype)

    return pl.pallas_call(
        body, name=name, grid=(T // bm, nj),
        in_specs=[pl.BlockSpec((bm, K), lambda i, j: (i, 0)),
                  pl.BlockSpec((None, K, bn), lambda i, j: (layer, 0, j)),
                  pl.BlockSpec((None, K, bn), lambda i, j: (layer, 0, nj + j))],
        out_specs=[pl.BlockSpec((2, bm, bn), lambda i, j: (0, i, j)), pl.BlockSpec((bm, bn), lambda i, j: (i, j))],
        out_shape=[SDS((2, T, D_FF), BF16), SDS((T, D_FF), BF16)],
        compiler_params=pltpu.CompilerParams(dimension_semantics=("parallel", "parallel"),
                                             vmem_limit_bytes=VMEM_LIMIT),
    )(hn, w, w)


def ffn_down_bx_act(df, w_down, gu, *, name, tm=512):
    w, layer = _stacked(w_down)
    T, K = df.shape
    bm, bn = _pick(T, tm), FFN_TILE

    def body(df_ref, wd_ref, gu_ref, dgu_ref):
        da = lax.dot_general(df_ref[...].astype(BF16), wd_ref[...].astype(BF16), NT, preferred_element_type=F32)
        _, vjp = jax.vjp(t_swiglu, gu_ref[0].astype(F32), gu_ref[1].astype(F32))
        dg, du = vjp(da)
        dgu_ref[0] = dg.astype(dgu_ref.dtype)
        dgu_ref[1] = du.astype(dgu_ref.dtype)

    return pl.pallas_call(
        body, name=name, grid=(T // bm, D_FF // bn),
        in_specs=[pl.BlockSpec((bm, K), lambda i, j: (i, 0)),
                  pl.BlockSpec((None, bn, K), lambda i, j: (layer, j, 0)),
                  pl.BlockSpec((2, bm, bn), lambda i, j: (0, i, j))],
        out_specs=pl.BlockSpec((2, bm, bn), lambda i, j: (0, i, j)), out_shape=SDS((2, T, D_FF), BF16),
        compiler_params=pltpu.CompilerParams(dimension_semantics=("parallel", "parallel"),
                                             vmem_limit_bytes=VMEM_LIMIT),
    )(df, w, gu)


def ple_fwd(h, gl, pp, g, *, name, tb=512):
    def fn(h, gl, pp, g):
        return (h + t_ple(gl, pp, g),)
    return row_call(fn, [h, gl, pp], [g], [(D_MODEL, F32)], tb=tb, name=name)[0]


def ple_bwd(gl, pp, dout, g, *, name, tb=512):
    def fn(gl, pp, dout, g):
        _, vjp = jax.vjp(t_ple, gl, pp, g)
        return vjp(dout)
    return row_call(fn, [gl, pp, dout], [g], [(D_MODEL, BF16), (D_MODEL, BF16)], [g.shape], tb=tb, name=name)


def gmlp_fwd(proj, ln_g, ln_b, w_s, b_st, *, name, tb=256):
    def fn(uv, ln_g, ln_b, w_s, b_st):
        return (t_gmlp(uv, ln_g, ln_b, w_s, b_st),)
    return row_call(fn, [(proj, 2048, 0)], [ln_g, ln_b, w_s, b_st], [(1024, BF16)], tb=tb, name=name)[0]


def gmlp_bwd(proj, dya, ln_g, ln_b, w_s, b_st, *, name, tb=128):
    def fn(uv, dya, ln_g, ln_b, w_s, b_st):
        _, vjp = jax.vjp(t_gmlp, uv, ln_g, ln_b, w_s, b_st)
        return vjp(dya.astype(F32))
    return row_call(fn, [(proj, 2048, 0), (dya, 1024, 0)], [ln_g, ln_b, w_s, b_st], [(2048, BF16)],
                    [ln_g.shape, ln_b.shape, w_s.shape, b_st.shape], tb=tb, name=name)


def kprep_fwd(c_all, cs, sn, qg, kvg, *, name, tb=512):
    return row_call(t_kprep, [c_all, cs, sn], [qg, kvg], [(256, BF16), (128, BF16), (128, BF16)], tb=tb, name=name)


def kprep_bwd(c_all, cs, sn, dcqn, dckvn, dkr, qg, kvg, *, name, tb=256):
    def fn(c_all, cs, sn, dcqn, dckvn, dkr, qg, kvg):
        dcqn, dckvn, dkr = _f32(dcqn, dckvn, dkr)
        _, vjp = jax.vjp(lambda c, qg, kvg: t_kprep(c, cs, sn, qg, kvg), c_all, qg, kvg)
        return vjp((dcqn, dckvn, dkr))
    return row_call(fn, [c_all, cs, sn, dcqn, dckvn, dkr], [qg, kvg], [(640, BF16)], [qg.shape, kvg.shape],
                    tb=tb, name=name)


def qrope_fwd(qb, c256, s256, *, name, tb=256):
    def fn(qb, c256, s256):
        return (t_qrope(qb, c256, s256),)
    return row_call(fn, [qb, c256, s256], [], [(MLA_HEADS * MLA_QPAD, BF16)], tb=tb, name=name)[0]


def qrope_bwd(dq, c256, s256, *, name, tb=256):
    def fn(dq, c256, s256):
        scale = MLA_QK ** -0.5
        a, b = [], []
        for h in range(MLA_HEADS):
            d = dq[:, h * MLA_QPAD:(h + 1) * MLA_QPAD] * scale
            a.append(d * c256)
            b.append(d * s256)
        return (jnp.concatenate(a + b, axis=1),)
    return row_call(fn, [dq, c256, s256], [], [(2 * MLA_HEADS * MLA_QPAD, BF16)], tb=tb, name=name)[0]


STAT_SPLIT = 64
ATTN_UNROLL = 4
ATTN_HEADS_PER_STEP = 4


def stats_fwd(do, o, lse, *, name, tb=512):
    def fn(do, o, lse):
        do, o = _f32(do, o)
        low = _iota((1, 128), 1) < STAT_SPLIT
        outs = []
        for h in range(MLA_HEADS):
            sl = slice(h * 128, (h + 1) * 128)
            dl = jnp.sum(do[:, sl] * o[:, sl], axis=-1, keepdims=True)
            outs.append(jnp.where(low, lse[:, sl], dl))
        return (jnp.concatenate(outs, axis=1),)
    return row_call(fn, [do, o, lse], [], [(2048, F32)], tb=tb, name=name)[0]


def headsum(dkr_h, *, name, tb=512):
    def fn(d):
        acc = d[:, :128]
        for h in range(1, MLA_HEADS):
            acc = acc + d[:, h * 128:(h + 1) * 128]
        return (acc,)
    return row_call(fn, [dkr_h], [], [(128, F32)], tb=tb, name=name)[0]


def loss_fwd(y, t, *, name, tb=512):
    def fn(y, t):
        e = y - t
        return e * (1.0 / D_MODEL), jnp.sum(e * e, axis=0, keepdims=True)
    return row_call(fn, [y, t], [], [(D_MODEL, F32)], [(1, D_MODEL)], tb=tb, name=name)


def conv_fwd(proj, w, b, *, name, tb=256):
    T = proj.shape[0]
    n = T // tb
    hb = tb // CHUNK
    C = SSD_CONV_CH

    def body(cur, prev, w_ref, b_ref, o_ref, scr):
        i = pl.program_id(0)
        scr[pl.ds(0, CHUNK), :] = jnp.where(i > 0, prev[...], 0.0)
        scr[pl.ds(CHUNK, tb), :] = cur[...]
        y = b_ref[...] + w_ref[3:4, :] * cur[...]
        for k in range(SSD_CONV - 1):
            y = y + w_ref[k:k + 1, :] * scr[pl.ds(CHUNK - (SSD_CONV - 1) + k, tb), :]
        o_ref[...] = y

    return pl.pallas_call(
        body, name=name, grid=(n,),
        in_specs=[pl.BlockSpec((tb, C), lambda i: (i, 2)),
                  pl.BlockSpec((CHUNK, C), lambda i: (jnp.maximum(i * hb - 1, 0), 2)),
                  pl.BlockSpec((SSD_CONV, C), lambda i: (0, 0)), pl.BlockSpec((1, C), lambda i: (0, 0))],
        out_specs=pl.BlockSpec((tb, C), lambda i: (i, 0)), out_shape=SDS((T, C), F32),
        scratch_shapes=[pltpu.VMEM((CHUNK + tb, C), F32)],
        compiler_params=pltpu.CompilerParams(dimension_semantics=("arbitrary",), vmem_limit_bytes=VMEM_LIMIT),
    )(proj, proj, w, b)


def conv_bwd(dpre, proj, w, *, name, tb=256):
    T = proj.shape[0]
    n = T // tb
    hb = tb // CHUNK
    nh = T // CHUNK
    C = SSD_CONV_CH

    def body(dcur, dnext, xcur, xprev, w_ref, dx_ref, dw_ref, db_ref, dscr, xscr):
        i = pl.program_id(0)

        @pl.when(i == 0)
        def _():
            dw_ref[...] = jnp.zeros_like(dw_ref)
            db_ref[...] = jnp.zeros_like(db_ref)

        d = dcur[...]
        dscr[pl.ds(0, tb), :] = d
        dscr[pl.ds(tb, CHUNK), :] = jnp.where(i < n - 1, dnext[...], 0.0)
        xscr[pl.ds(0, CHUNK), :] = jnp.where(i > 0, xprev[...], 0.0)
        xscr[pl.ds(CHUNK, tb), :] = xcur[...]
        dx = w_ref[3:4, :] * d
        for k in range(SSD_CONV - 1):
            dx = dx + w_ref[k:k + 1, :] * dscr[pl.ds(SSD_CONV - 1 - k, tb), :]
        dx_ref[...] = dx.astype(dx_ref.dtype)
        for k in range(SSD_CONV):
            xk = xscr[pl.ds(CHUNK - (SSD_CONV - 1) + k, tb), :]
            dw_ref[k:k + 1, :] += jnp.sum(d * xk, axis=0, keepdims=True)
        db_ref[...] += jnp.sum(d, axis=0, keepdims=True)

    return pl.pallas_call(
        body, name=name, grid=(n,),
        in_specs=[pl.BlockSpec((tb, C), lambda i: (i, 0)),
                  pl.BlockSpec((CHUNK, C), lambda i: (jnp.minimum((i + 1) * hb, nh - 1), 0)),
                  pl.BlockSpec((tb, C), lambda i: (i, 2)),
                  pl.BlockSpec((CHUNK, C), lambda i: (jnp.maximum(i * hb - 1, 0), 2)),
                  pl.BlockSpec((SSD_CONV, C), lambda i: (0, 0))],
        out_specs=[pl.BlockSpec((tb, C), lambda i: (i, 0)), pl.BlockSpec((SSD_CONV, C), lambda i: (0, 0)),
                   pl.BlockSpec((1, C), lambda i: (0, 0))],
        out_shape=[SDS((T, C), BF16), SDS((SSD_CONV, C), F32), SDS((1, C), F32)],
        scratch_shapes=[pltpu.VMEM((tb + CHUNK, C), F32), pltpu.VMEM((CHUNK + tb, C), F32)],
        compiler_params=pltpu.CompilerParams(dimension_semantics=("arbitrary",), vmem_limit_bytes=VMEM_LIMIT),
    )(dpre, dpre, proj, proj, w)


def _ssd_specs(nc, rev):
    def r(c):
        return nc - 1 - c if rev else c
    pre = pl.BlockSpec((CHUNK, SSD_CONV_CH), lambda c: (r(c), 0))
    dtr = pl.BlockSpec((CHUNK, LANES), lambda c: (r(c), HYB_MAIN // LANES))
    z = pl.BlockSpec((CHUNK, 1024), lambda c: (r(c), 2))
    row = pl.BlockSpec((CHUNK, 1024), lambda c: (r(c), 0))
    return pre, dtr, z, row


def _pspec(shape):
    return pl.BlockSpec(shape, lambda c: (0,) * len(shape))


def ssd_fwd(pre, proj, dt_bias, a_log, d_exp, norm_g, *, name):
    T = pre.shape[0]
    nc = T // CHUNK
    s_pre, s_dt, s_z, s_row = _ssd_specs(nc, False)

    def body(pre_r, dt_r, z_r, b_r, a_r, d_r, g_r, y_r, sv_r, st):
        @pl.when(pl.program_id(0) == 0)
        def _():
            st[...] = jnp.zeros_like(st)

        s0 = st[...]
        sv_r[...] = s0
        y, s1 = t_ssd(pre_r[...], dt_r[...], z_r[...], s0, b_r[...], a_r[...], d_r[...], g_r[...])
        y_r[...] = y.astype(y_r.dtype)
        st[...] = s1

    return pl.pallas_call(
        body, name=name, grid=(nc,),
        in_specs=[s_pre, s_dt, s_z, _pspec((1, LANES)), _pspec((1, LANES)), _pspec((1, 1024)), _pspec((1, 1024))],
        out_specs=[s_row, s_row], out_shape=[SDS((T, 1024), BF16), SDS((T, 1024), F32)],
        scratch_shapes=[pltpu.VMEM((SSD_STATE, 1024), F32)],
        compiler_params=pltpu.CompilerParams(dimension_semantics=("arbitrary",), vmem_limit_bytes=VMEM_LIMIT),
    )(pre, proj, proj, dt_bias, a_log, d_exp, norm_g)


def ssd_bwd(pre, proj, states, dyab, dt_bias, a_log, d_exp, norm_g, *, name):
    T = pre.shape[0]
    nc = T // CHUNK
    s_pre, s_dt, s_z, s_row = _ssd_specs(nc, True)
    s_dtout = pl.BlockSpec((CHUNK, LANES), lambda c: (nc - 1 - c, 0))
    s_dy = pl.BlockSpec((CHUNK, 1024), lambda c: (nc - 1 - c, 1))

    def body(pre_r, dt_r, z_r, sv_r, dy_r, b_r, a_r, d_r, g_r, dpre_r, ddt_r, dz_r, db_r, da_r, dd_r, dg_r, dst):
        @pl.when(pl.program_id(0) == 0)
        def _():
            dst[...] = jnp.zeros_like(dst)
            for r in (db_r, da_r, dd_r, dg_r):
                r[...] = jnp.zeros_like(r)

        _, vjp = jax.vjp(t_ssd, pre_r[...], dt_r[...], z_r[...], sv_r[...], b_r[...], a_r[...], d_r[...], g_r[...])
        dpre, ddt, dz, ds0, db, da, dd, dg = vjp((dy_r[...].astype(F32), dst[...]))
        dpre_r[...] = dpre
        ddt_r[...] = ddt.astype(ddt_r.dtype)
        dz_r[...] = dz.astype(dz_r.dtype)
        dst[...] = ds0
        db_r[...] += db
        da_r[...] += da
        dd_r[...] += dd
        dg_r[...] += dg

    return pl.pallas_call(
        body, name=name, grid=(nc,),
        in_specs=[s_pre, s_dt, s_z, s_row, s_dy, _pspec((1, LANES)), _pspec((1, LANES)), _pspec((1, 1024)),
                  _pspec((1, 1024))],
        out_specs=[s_pre, s_dtout, s_row, _pspec((1, LANES)), _pspec((1, LANES)), _pspec((1, 1024)), _pspec((1, 1024))],
        out_shape=[SDS((T, SSD_CONV_CH), F32), SDS((T, LANES), BF16), SDS((T, 1024), BF16),
                   SDS((1, LANES), F32), SDS((1, LANES), F32), SDS((1, 1024), F32), SDS((1, 1024), F32)],
        scratch_shapes=[pltpu.VMEM((SSD_STATE, 1024), F32)],
        compiler_params=pltpu.CompilerParams(dimension_semantics=("arbitrary",), vmem_limit_bytes=VMEM_LIMIT),
    )(pre, proj, proj, states, dyab, dt_bias, a_log, d_exp, norm_g)


def _attn_tile(T, target=512):
    return min(target, T // 2)


def _causal(tq):
    return _iota((tq, tq), 1) <= _iota((tq, tq), 0)


def _job_parts(job):
    if job is None:
        return 0, 0, [], [], []
    ni, no = len(job["inputs"]), len(job["out_shape"])
    return ni, no, list(job["inputs"]), list(job["out_shape"]), list(job["scratch"])


def _job_phase(job, which, refs, when):
    if job is not None and job["phases"][which] is not None:
        pl.when(when)(functools.partial(job["phases"][which], *refs))


def attn_fwd(q, kv, kr, *, name, job=None):
    T = q.shape[0]
    tq = _attn_tile(T)
    nq = T // tq
    hp = ATTN_HEADS_PER_STEP
    ng = MLA_HEADS // hp
    ni, no, jins, jouts, jscratch = _job_parts(job)

    def body(*refs):
        q_ref, kn_ref, v_ref, kr_ref = refs[:4]
        o_ref, lse_ref = refs[4 + ni:6 + ni]
        jrefs = (refs[4:4 + ni], refs[6 + ni:6 + ni + no], refs[6 + ni + no:])
        g = pl.program_id(0)
        qi = pl.program_id(1)
        _job_phase(job, 0, jrefs, (g == 0) & (qi == 0))
        qv = [q_ref[:, e * MLA_QPAD:(e + 1) * MLA_QPAD] for e in range(hp)]

        def blk(ki, masked, carry):
            rows = pl.ds(pl.multiple_of(ki * tq, tq), tq)
            kr = kr_ref[rows, :]
            out = []
            for e in range(hp):
                m, l, acc = carry[e]
                cols = slice(e * 128, (e + 1) * 128)
                k = jnp.concatenate([kn_ref[rows, cols], kr], axis=1)
                s = lax.dot_general(qv[e], k, NT, preferred_element_type=F32)
                if masked:
                    s = jnp.where(_causal(tq), s, -1e30)
                m_new = jnp.maximum(m, jnp.max(s, axis=-1, keepdims=True))
                p = jnp.exp(s - m_new)
                alpha = jnp.exp(m - m_new)
                l = alpha * l + jnp.sum(p, axis=-1, keepdims=True)
                acc = alpha * acc + jnp.dot(p.astype(BF16), v_ref[rows, cols], preferred_element_type=F32)
                out.append((m_new, l, acc))
            return tuple(out)

        one = (jnp.full((tq, 1), -1e30, F32), jnp.zeros((tq, 1), F32), jnp.zeros((tq, 128), F32))
        carry = lax.fori_loop(0, qi, lambda ki, c: blk(ki, False, c), (one,) * hp)
        carry = blk(qi, True, carry)
        for e in range(hp):
            m, l, acc = carry[e]
            cols = slice(e * 128, (e + 1) * 128)
            o_ref[:, cols] = (acc / l).astype(o_ref.dtype)
            lse_ref[:, cols] = jnp.broadcast_to(m + jnp.log(l), (tq, 128))
        _job_phase(job, 2, jrefs, (g == ng - 1) & (qi == nq - 1))

    res = pl.pallas_call(
        body, name=name, grid=(ng, nq),
        in_specs=[pl.BlockSpec((tq, hp * MLA_QPAD), lambda g, i: (i, g)),
                  pl.BlockSpec((T, hp * 128), lambda g, i: (0, g)),
                  pl.BlockSpec((T, hp * 128), lambda g, i: (0, ng + g)),
                  pl.BlockSpec((T, 128), lambda g, i: (0, 0))] + [ANY] * ni,
        out_specs=[pl.BlockSpec((tq, hp * 128), lambda g, i: (i, g)), pl.BlockSpec((tq, hp * 128), lambda g, i: (i, g))]
        + [ANY] * no,
        out_shape=[SDS((T, 2048), BF16), SDS((T, 2048), F32)] + jouts, scratch_shapes=jscratch,
        compiler_params=pltpu.CompilerParams(dimension_semantics=("arbitrary", "arbitrary"),
                                             vmem_limit_bytes=VMEM_LIMIT),
    )(q, kv, kv, kr, *jins)
    return res[0], res[1], list(res[2:])


def attn_bwd(q, kv, kr, do, stats, *, name, job=None):
    T = q.shape[0]
    tq = _attn_tile(T)
    nq = T // tq
    ni, no, jins, jouts, jscratch = _job_parts(job)

    def body(*refs):
        q_ref, do_ref, st_ref, kn_ref, v_ref, kr_ref = refs[:6]
        dq_ref, dkn_ref, dv_ref, dkr_ref = refs[6 + ni:10 + ni]
        jrefs = (refs[6:6 + ni], refs[10 + ni:10 + ni + no], refs[10 + ni + no:])
        h = pl.program_id(0)
        ki = pl.program_id(1)
        _job_phase(job, 0, jrefs, (h == 0) & (ki == 0))
        _job_phase(job, 1, jrefs, (h == MLA_HEADS // 2) & (ki == 0))

        @pl.when(ki == 0)
        def _():
            dq_ref[...] = jnp.zeros_like(dq_ref)

        k = jnp.concatenate([kn_ref[...], kr_ref[...]], axis=1)
        v = v_ref[...]

        def blk(qi, masked, carry):
            dk, dv = carry
            rows = pl.ds(pl.multiple_of(qi * tq, tq), tq)
            qv, dov = q_ref[rows, :], do_ref[rows, :]
            lse, dl = st_ref[rows, 0:1], st_ref[rows, STAT_SPLIT:STAT_SPLIT + 1]
            s = lax.dot_general(qv, k, NT, preferred_element_type=F32)
            if masked:
                s = jnp.where(_causal(tq), s, -1e30)
            p = jnp.exp(s - lse)
            dv = dv + lax.dot_general(p.astype(BF16), dov, TN, preferred_element_type=F32)
            dp = lax.dot_general(dov, v, NT, preferred_element_type=F32)
            ds = (p * (dp - dl)).astype(BF16)
            dk = dk + lax.dot_general(ds, qv, TN, preferred_element_type=F32)
            dq_ref[rows, :] += jnp.dot(ds, k, preferred_element_type=F32)
            return dk, dv

        carry = blk(ki, True, (jnp.zeros((tq, MLA_QPAD), F32), jnp.zeros((tq, 128), F32)))
        rest = nq - 1 - ki

        def group(j, c):
            for u in range(ATTN_UNROLL):
                c = blk(ki + 1 + ATTN_UNROLL * j + u, False, c)
            return c

        carry = lax.fori_loop(0, rest // ATTN_UNROLL, group, carry)
        for u in range(ATTN_UNROLL - 1):
            carry = lax.cond(rest % ATTN_UNROLL > u, functools.partial(lambda c, u: blk(nq - 1 - u, False, c), u=u),
                             lambda c: c, carry)
        dk, dv = carry
        dkn_ref[...] = dk[:, :128].astype(dkn_ref.dtype)
        dkr_ref[...] = dk[:, 128:]
        dv_ref[...] = dv.astype(dv_ref.dtype)
        _job_phase(job, 2, jrefs, (h == MLA_HEADS - 1) & (ki == nq - 1))

    res = pl.pallas_call(
        body, name=name, grid=(MLA_HEADS, nq),
        in_specs=[pl.BlockSpec((T, MLA_QPAD), lambda h, i: (0, h)),
                  pl.BlockSpec((T, 128), lambda h, i: (0, h)),
                  pl.BlockSpec((T, 128), lambda h, i: (0, h)),
                  pl.BlockSpec((tq, 128), lambda h, i: (i, h)),
                  pl.BlockSpec((tq, 128), lambda h, i: (i, MLA_HEADS + h)),
                  pl.BlockSpec((tq, 128), lambda h, i: (i, 0))] + [ANY] * ni,
        out_specs=[pl.BlockSpec((T, MLA_QPAD), lambda h, i: (0, h)),
                   pl.BlockSpec((tq, 128), lambda h, i: (i, h)), pl.BlockSpec((tq, 128), lambda h, i: (i, h)),
                   pl.BlockSpec((tq, 128), lambda h, i: (i, h))] + [ANY] * no,
        out_shape=[SDS((T, MLA_HEADS * MLA_QPAD), F32), SDS((T, 2048), BF16), SDS((T, 2048), BF16),
                   SDS((T, 2048), F32)] + jouts, scratch_shapes=jscratch,
        compiler_params=pltpu.CompilerParams(dimension_semantics=("arbitrary", "arbitrary"),
                                             vmem_limit_bytes=VMEM_LIMIT),
    )(q, do, stats, kv, kv, kr, *jins)
    return res[0], res[1], res[2], res[3], list(res[4:])


def _ffn_fwd(h, pre_g, w_in, w_down, post_g, tag):
    hn = rms_fwd(h, pre_g, name=f"{tag}_pre")
    gu, a = ffn_in_act(hn, w_in, name=f"{tag}_in")
    f = mm(a, w_down, name=f"{tag}_down", tk=1408)
    h2 = post_fwd(h, f, post_g, 0.5, name=f"{tag}_post")
    return h2, (h, hn, gu, a, f)


def _ffn_bwd(dh2, saved, pre_g, w_in, w_down, post_g, tag):
    h, hn, gu, a, f = saved
    df, dpost = post_bwd(f, dh2, post_g, 0.5, name=f"{tag}_post_b")
    dgu = ffn_down_bx_act(df, w_down, gu, name=f"{tag}_down_bx")
    dw_down = mm(a, df, ta=True, out_dtype=BF16, name=f"{tag}_down_bw", tm=1408)
    dhn = mm((dgu, "planes"), w_in, tb=True, name=f"{tag}_in_bx", tk=1408)
    dw_in = mm(hn, (dgu, "planes"), ta=True, out_dtype=BF16, name=f"{tag}_in_bw", tn=1408)
    dh, dpre = rms_bwd(h, dhn, dh2, pre_g, name=f"{tag}_pre_b")
    return dh, dpre, dw_in, dw_down, dpost


def _hyb_fwd(hn, w, tag):
    proj = mm(hn, w["hyb_in"], name=f"{tag}_in")
    ya = gmlp_fwd(proj, w["ln_g"], w["ln_b"], w["w_s"], w["b_st"], name=f"{tag}_gmlp")
    pre = conv_fwd(proj, w["conv_w"], w["conv_b"], name=f"{tag}_conv")
    yb, states = ssd_fwd(pre, proj, w["dt_bias"], w["a_log"], w["d_exp"], w["norm_g"], name=f"{tag}_ssd")
    yab = jnp.concatenate([ya, yb], axis=1)
    mixed = mm(yab, w["hyb_out"], name=f"{tag}_out")
    return mixed, (proj, pre, states, yab)


def _hyb_bwd(dmixed, hn, saved, w, tag):
    proj, pre, states, yab = saved
    dyab = mm(dmixed, w["hyb_out"], tb=True, name=f"{tag}_out_bx")
    dw_out = mm(yab, dmixed, ta=True, out_dtype=BF16, name=f"{tag}_out_bw")
    duv, dln_g, dln_b, dw_s, db_st = gmlp_bwd(proj, dyab, w["ln_g"], w["ln_b"], w["w_s"], w["b_st"],
                                              name=f"{tag}_gmlp_b")
    dpre, ddt, dz, ddt_bias, da_log, dd_exp, dnorm_g = ssd_bwd(
        pre, proj, states, dyab, w["dt_bias"], w["a_log"], w["d_exp"], w["norm_g"], name=f"{tag}_ssd_b")
    dxbc, dconv_w, dconv_b = conv_bwd(dpre, proj, w["conv_w"], name=f"{tag}_conv_b")
    pad = jnp.zeros((duv.shape[0], HYB_PAD - HYB_MAIN - LANES), BF16)
    dproj = jnp.concatenate([duv, dz, dxbc, ddt, pad], axis=1)
    dhn = mm(dproj, w["hyb_in"], tb=True, name=f"{tag}_in_bx")
    dw_in = mm(hn, dproj, ta=True, out_dtype=BF16, name=f"{tag}_in_bw")
    g = dict(hyb_in=dw_in, hyb_out=dw_out, ln_g=dln_g, ln_b=dln_b, w_s=dw_s, b_st=db_st, conv_w=dconv_w,
             conv_b=dconv_b, dt_bias=ddt_bias, a_log=da_log, d_exp=dd_exp, norm_g=dnorm_g)
    return dhn, g


def _mla_fwd(hn, w, rope, tag, job=None):
    cs, sn, c256, s256 = rope
    c_all = mm(hn, w["mla_in"], name=f"{tag}_in")
    cqn, ckvn, kr = kprep_fwd(c_all, cs, sn, w["q_g"], w["kv_g"], name=f"{tag}_kprep")
    qb = mm(cqn, w["uq"], out_dtype=BF16, name=f"{tag}_uq")
    q = qrope_fwd(qb, c256, s256, name=f"{tag}_qrope")
    kv = mm(ckvn, w["ukv"], out_dtype=BF16, name=f"{tag}_ukv")
    o, lse, jouts = attn_fwd(q, kv, kr, name=f"{tag}_attn", job=job)
    mixed = mm(o, w["mla_out"], name=f"{tag}_out")
    return mixed, (c_all, cqn, ckvn, kr, q, kv, o, lse), jouts


def _mla_bwd(dmixed, hn, saved, w, rope, tag, job=None):
    cs, sn, c256, s256 = rope
    c_all, cqn, ckvn, kr, q, kv, o, lse = saved
    do = mm(dmixed, w["mla_out"], tb=True, out_dtype=BF16, name=f"{tag}_out_bx")
    dw_out = mm(o, dmixed, ta=True, out_dtype=BF16, name=f"{tag}_out_bw")
    stats = stats_fwd(do, o, lse, name=f"{tag}_stats")
    dq, dkn, dv, dkr_h, jouts = attn_bwd(q, kv, kr, do, stats, name=f"{tag}_attn_b", job=job)
    dkr = headsum(dkr_h, name=f"{tag}_dkr")
    dkv = jnp.concatenate([dkn, dv], axis=1)
    dckvn = mm(dkv, w["ukv"], tb=True, name=f"{tag}_ukv_bx")
    dw_ukv = mm(ckvn, dkv, ta=True, name=f"{tag}_ukv_bw")
    dqb = qrope_bwd(dq, c256, s256, name=f"{tag}_qrope_b")
    dcqn = mm(dqb, w["uq"], tb=True, name=f"{tag}_uq_bx")
    dw_uq = mm(cqn, dqb, ta=True, name=f"{tag}_uq_bw")
    dc_all, dq_g, dkv_g = kprep_bwd(c_all, cs, sn, dcqn, dckvn, dkr, w["q_g"], w["kv_g"], name=f"{tag}_kprep_b")
    dhn = mm(dc_all, w["mla_in"], tb=True, name=f"{tag}_in_bx")
    dw_in = mm(hn, dc_all, ta=True, name=f"{tag}_in_bw")
    g = dict(mla_in=dw_in, mla_out=dw_out, uq=dw_uq, ukv=dw_ukv, q_g=dq_g, kv_g=dkv_g)
    return dhn, g, jouts


class NoJobs:
    def fwd_job(self, i):
        return None

    def fwd_done(self, i, outs):
        pass

    def bwd_job(self, i, grads, partial):
        return None

    def bwd_done(self, i, outs):
        pass


def local_step(x, p, rope, target, weights_of, jobs=NoJobs()):
    h = x
    saved = []
    lw = []
    for i in range(DEPTH):
        w = weights_of(i)
        lw.append(w)
        t = f"l{i}"
        h, s1 = _ffn_fwd(h, w["ffn1_pre_g"], w["ffn1_w_in"], w["ffn1_w_down"], w["ffn1_post_g"], f"{t}_f1")
        h1 = h
        hn = rms_fwd(h1, w["mix_pre_g"], name=f"{t}_mixpre")
        if i % 2 == 0:
            mixed, sm = _hyb_fwd(hn, w, f"{t}_hyb")
        else:
            job = jobs.fwd_job(i)
            mixed, sm, jouts = _mla_fwd(hn, w, rope, f"{t}_mla", job)
            if job is not None:
                jobs.fwd_done(i, jouts)
                w = lw[i] = weights_of(i)
        h = post_fwd(h1, mixed, w["mix_post_g"], 1.0, name=f"{t}_mixpost")
        h, s2 = _ffn_fwd(h, w["ffn2_pre_g"], w["ffn2_w_in"], w["ffn2_w_down"], w["ffn2_post_g"], f"{t}_f2")
        h3 = h
        hn3 = rms_fwd(h3, w["ple_pre_g"], name=f"{t}_plepre")
        gl = mm(hn3, w["ple_w_gate"], name=f"{t}_plegate")
        pp = mm((p, i), w["ple_w_proj"], name=f"{t}_pleproj")
        h = ple_fwd(h3, gl, pp, w["ple_post_g"], name=f"{t}_plepost")
        saved.append((s1, h1, hn, sm, mixed, s2, h3, hn3, gl, pp))

    dh, sq = loss_fwd(h, target, name="loss")
    grads = [None] * DEPTH
    for i in reversed(range(DEPTH)):
        w = lw[i]
        t = f"l{i}"
        s1, h1, hn, sm, mixed, s2, h3, hn3, gl, pp = saved[i]
        g = {}
        dgl, dpp, g["ple_post_g"] = ple_bwd(gl, pp, dh, w["ple_post_g"], name=f"{t}_plepost_b")
        dhn3 = mm(dgl, w["ple_w_gate"], tb=True, name=f"{t}_plegate_bx")
        g["ple_w_gate"] = mm(hn3, dgl, ta=True, out_dtype=BF16, name=f"{t}_plegate_bw")
        g["ple_w_proj"] = mm((p, i), dpp, ta=True, out_dtype=BF16, name=f"{t}_pleproj_bw")
        dh, g["ple_pre_g"] = rms_bwd(h3, dhn3, dh, w["ple_pre_g"], name=f"{t}_plepre_b")
        dh, g["ffn2_pre_g"], g["ffn2_w_in"], g["ffn2_w_down"], g["ffn2_post_g"] = _ffn_bwd(
            dh, s2, w["ffn2_pre_g"], w["ffn2_w_in"], w["ffn2_w_down"], w["ffn2_post_g"], f"{t}_f2")
        dmixed, g["mix_post_g"] = post_bwd(mixed, dh, w["mix_post_g"], 1.0, name=f"{t}_mixpost_b")
        if i % 2 == 0:
            dhn, gm = _hyb_bwd(dmixed, hn, sm, w, f"{t}_hyb")
        else:
            job = jobs.bwd_job(i, grads, g)
            dhn, gm, jouts = _mla_bwd(dmixed, hn, sm, w, rope, f"{t}_mla", job)
            if job is not None:
                jobs.bwd_done(i, jouts)
        g.update(gm)
        dh, g["mix_pre_g"] = rms_bwd(h1, dhn, dh, w["mix_pre_g"], name=f"{t}_mixpre_b")
        dh, g["ffn1_pre_g"], g["ffn1_w_in"], g["ffn1_w_down"], g["ffn1_post_g"] = _ffn_bwd(
            dh, s1, w["ffn1_pre_g"], w["ffn1_w_in"], w["ffn1_w_down"], w["ffn1_post_g"], f"{t}_f1")
        grads[i] = g
    return sq, dh, grads


def _zeros_like_cols(a, n):
    return jnp.zeros(a.shape[:-1] + (n,), a.dtype)


def layer_weights(full, i):
    j = i // 2
    row = lambda v: v.reshape(1, -1)
    w = {k: row(full[k][i]) for k in ("ffn1_pre_g", "ffn1_post_g", "mix_pre_g", "mix_post_g", "ffn2_pre_g",
                                      "ffn2_post_g", "ple_pre_g", "ple_post_g")}
    for k in ("ffn1_w_in", "ffn1_w_down", "ffn2_w_in", "ffn2_w_down", "ple_w_gate", "ple_w_proj"):
        w[k] = full[k][i]
    if i % 2 == 0:
        hw = full["hyb_w_in"][j]
        w["hyb_in"] = jnp.concatenate([hw, _zeros_like_cols(hw, HYB_PAD - HYB_IN)], axis=1)
        w["hyb_out"] = full["hyb_w_out"][j]
        w["ln_g"], w["ln_b"] = row(full["gm_ln_g"][j]), row(full["gm_ln_b"][j])
        w["w_s"] = full["gm_w_s"][j].reshape(GM_HEADS * CHUNK, CHUNK)
        w["b_st"] = jnp.pad(full["gm_b_s"][j].T, ((0, 0), (0, LANES - GM_HEADS)))
        w["conv_w"], w["conv_b"] = full["ssd_conv_w"][j], row(full["ssd_conv_b"][j])
        pad16 = lambda v: jnp.pad(v.reshape(1, -1), ((0, 0), (0, LANES - SSD_HEADS)))
        w["dt_bias"], w["a_log"] = pad16(full["ssd_dt_bias"][j]), pad16(full["ssd_a_log"][j])
        w["d_exp"] = row(jnp.repeat(full["ssd_d"][j], SSD_HEAD_DIM))
        w["norm_g"] = row(full["ssd_norm_g"][j])
    else:
        wi = full["mla_w_in"][j]
        z64 = _zeros_like_cols(wi, 64)
        w["mla_in"] = jnp.concatenate([wi[:, :384], wi[:, 384:448], z64, -wi[:, 416:448], wi[:, 384:416], z64], axis=1)
        uq = full["mla_w_uq"][j].reshape(MLA_Q_LORA, MLA_HEADS, MLA_QK)
        zq = jnp.zeros((MLA_Q_LORA, MLA_HEADS, 64), uq.dtype)
        pad_part = jnp.concatenate([uq, zq], axis=2)
        swp_part = jnp.concatenate([jnp.zeros_like(uq[:, :, :128]), -uq[:, :, 160:192], uq[:, :, 128:160], zq], axis=2)
        w["uq"] = jnp.concatenate([pad_part.reshape(MLA_Q_LORA, -1), swp_part.reshape(MLA_Q_LORA, -1)], axis=1)
        ukv = full["mla_w_ukv"][j].reshape(MLA_KV_LORA, MLA_HEADS, 256)
        w["ukv"] = jnp.concatenate([ukv[:, :, :128].reshape(MLA_KV_LORA, -1), ukv[:, :, 128:].reshape(MLA_KV_LORA, -1)],
                                   axis=1)
        w["mla_out"] = full["mla_w_out"][j]
        w["q_g"], w["kv_g"] = row(full["mla_q_norm_g"][j]), row(full["mla_kv_norm_g"][j])
    return w


def full_grads(grads):
    out = {}
    stack = lambda k, idx: jnp.stack([grads[i][k] for i in idx])
    every, even, odd = range(DEPTH), range(0, DEPTH, 2), range(1, DEPTH, 2)
    for k in ("ffn1_pre_g", "ffn1_post_g", "mix_pre_g", "mix_post_g", "ffn2_pre_g", "ffn2_post_g", "ple_pre_g",
              "ple_post_g"):
        out[k] = stack(k, every).reshape(DEPTH, D_MODEL)
    for k in ("ffn1_w_in", "ffn1_w_down", "ffn2_w_in", "ffn2_w_down", "ple_w_gate", "ple_w_proj"):
        out[k] = stack(k, every)
    out["hyb_w_in"] = stack("hyb_in", even)[:, :, :HYB_IN]
    out["hyb_w_out"] = stack("hyb_out", even)
    out["gm_ln_g"] = stack("ln_g", even).reshape(2, 1024)
    out["gm_ln_b"] = stack("ln_b", even).reshape(2, 1024)
    out["gm_w_s"] = stack("w_s", even).reshape(2, GM_HEADS, CHUNK, CHUNK)
    out["gm_b_s"] = jnp.swapaxes(stack("b_st", even)[:, :, :GM_HEADS], 1, 2)
    out["ssd_conv_w"] = stack("conv_w", even)
    out["ssd_conv_b"] = stack("conv_b", even).reshape(2, SSD_CONV_CH)
    out["ssd_dt_bias"] = stack("dt_bias", even)[:, 0, :SSD_HEADS]
    out["ssd_a_log"] = stack("a_log", even)[:, 0, :SSD_HEADS]
    out["ssd_d"] = stack("d_exp", even).reshape(2, SSD_HEADS, SSD_HEAD_DIM).sum(axis=-1)
    out["ssd_norm_g"] = stack("norm_g", even).reshape(2, 1024)
    dwi = stack("mla_in", odd)
    out["mla_w_in"] = jnp.concatenate([dwi[:, :, :384], dwi[:, :, 384:416] + dwi[:, :, 544:576],
                                       dwi[:, :, 416:448] - dwi[:, :, 512:544]], axis=2)
    duq = stack("uq", odd)
    half = MLA_HEADS * MLA_QPAD
    dp = duq[:, :, :half].reshape(2, MLA_Q_LORA, MLA_HEADS, MLA_QPAD)
    ds = duq[:, :, half:].reshape(2, MLA_Q_LORA, MLA_HEADS, MLA_QPAD)
    out["mla_w_uq"] = jnp.concatenate([dp[..., :128], dp[..., 128:160] + ds[..., 160:192],
                                       dp[..., 160:192] - ds[..., 128:160]], axis=-1).reshape(2, MLA_Q_LORA, -1)
    dukv = stack("ukv", odd)
    dk = dukv[:, :, :2048].reshape(2, MLA_KV_LORA, MLA_HEADS, 128)
    dv = dukv[:, :, 2048:].reshape(2, MLA_KV_LORA, MLA_HEADS, 128)
    out["mla_w_ukv"] = jnp.concatenate([dk, dv], axis=-1).reshape(2, MLA_KV_LORA, -1)
    out["mla_w_out"] = stack("mla_out", odd)
    out["mla_q_norm_g"] = stack("q_g", odd).reshape(2, MLA_Q_LORA)
    out["mla_kv_norm_g"] = stack("kv_g", odd).reshape(2, MLA_KV_LORA)
    return out


def rope_tables(positions):
    T = positions.shape[0]
    inv = 1.0 / (ROPE_BASE ** (jnp.arange(0, MLA_ROPE, 2, dtype=F32) / MLA_ROPE))
    ang = positions.astype(F32)[:, None] * inv
    cos, sin = jnp.cos(ang), jnp.sin(ang)
    z64 = jnp.zeros((T, 64), F32)
    cs = jnp.concatenate([cos, cos, z64], axis=1)
    sn = jnp.concatenate([sin, sin, z64], axis=1)
    c256 = jnp.concatenate([jnp.ones((T, 128), F32), cs], axis=1)
    s256 = jnp.concatenate([jnp.zeros((T, 128), F32), sn], axis=1)
    return cs, sn, c256, s256


def _rows(n):
    return -(-n // LANES)


def _pack(pieces, dtype, row_multiple):
    flat = []
    total = 0
    for a in pieces:
        v = a.reshape(-1).astype(dtype)
        padn = _rows(v.shape[0]) * LANES - v.shape[0]
        if padn:
            v = jnp.concatenate([v, jnp.zeros((padn,), dtype)])
        flat.append(v)
        total += v.shape[0] // LANES
    tail = -total % row_multiple
    if tail:
        flat.append(jnp.zeros((tail * LANES,), dtype))
    return jnp.concatenate(flat).reshape(-1, LANES)


def _unpack(slab, shapes):
    out = []
    r = 0
    for s in shapes:
        n = int(np.prod(s))
        nr = _rows(n)
        out.append(slab[r:r + nr].reshape(-1)[:n].reshape(s))
        r += nr
    return out


def _shard_shape(shape, ax):
    if ax is None:
        return tuple(shape)
    s = list(shape)
    s[ax] //= N_CHIPS
    return tuple(s)


def _chip_slice(a, ax, k):
    if ax is None:
        return a
    n = a.shape[ax] // N_CHIPS
    return lax.slice_in_dim(a, k * n, (k + 1) * n, axis=ax)


def _plane_peers():
    x, y, c = lax.axis_index("x"), lax.axis_index("y"), lax.axis_index("c")
    return (x, y, c), [(1 - x, y, c), (x, 1 - y, c), (1 - x, 1 - y, c)]


ANY = pl.BlockSpec(memory_space=pl.ANY)


def plane_allgather(slab):
    R = slab.shape[0]

    def body(src, out, send_sems, recv_sems, local_sem):
        (x, y, c), peers = _plane_peers()
        me = 2 * x + y
        local = pltpu.make_async_copy(src, out.at[me], local_sem)
        local.start()
        copies = []
        for j, peer in enumerate(peers):
            cp = pltpu.make_async_remote_copy(src_ref=src, dst_ref=out.at[me], send_sem=send_sems.at[j],
                                              recv_sem=recv_sems.at[j], device_id=peer, device_id_type=MESH)
            cp.start()
            copies.append(cp)
        for cp in copies:
            cp.wait()
        local.wait()

    return pl.pallas_call(
        body, name="plane_allgather", out_shape=SDS((N_CHIPS, R, LANES), slab.dtype),
        in_specs=[ANY], out_specs=ANY,
        scratch_shapes=[pltpu.SemaphoreType.DMA((3,)), pltpu.SemaphoreType.DMA((3,)), pltpu.SemaphoreType.DMA],
    )(slab)


def plane_alltoall(buf):
    R = buf.shape[1]

    def body(src, out, send_sems, recv_sems, local_sem):
        (x, y, c), peers = _plane_peers()
        me = 2 * x + y
        local = pltpu.make_async_copy(src.at[me], out.at[me], local_sem)
        local.start()
        copies = []
        for j, peer in enumerate(peers):
            cp = pltpu.make_async_remote_copy(src_ref=src.at[2 * peer[0] + peer[1]], dst_ref=out.at[me],
                                              send_sem=send_sems.at[j], recv_sem=recv_sems.at[j], device_id=peer,
                                              device_id_type=MESH)
            cp.start()
            copies.append(cp)
        for cp in copies:
            cp.wait()
        local.wait()

    return pl.pallas_call(
        body, name="plane_alltoall", out_shape=SDS((N_CHIPS, R, LANES), buf.dtype),
        in_specs=[ANY], out_specs=ANY,
        scratch_shapes=[pltpu.SemaphoreType.DMA((3,)), pltpu.SemaphoreType.DMA((3,)), pltpu.SemaphoreType.DMA],
    )(buf)


def sibling_swap(buf):
    def body(src, out, send_sem, recv_sem):
        x, y, c = lax.axis_index("x"), lax.axis_index("y"), lax.axis_index("c")
        cp = pltpu.make_async_remote_copy(src_ref=src, dst_ref=out, send_sem=send_sem, recv_sem=recv_sem,
                                          device_id=(x, y, 1 - c), device_id_type=MESH)
        cp.start()
        cp.wait()

    return pl.pallas_call(
        body, name="sibling_swap", out_shape=SDS(buf.shape, buf.dtype), in_specs=[ANY], out_specs=ANY,
        scratch_shapes=[pltpu.SemaphoreType.DMA, pltpu.SemaphoreType.DMA],
    )(buf)


def _chip_block(ref, ax, k, n, stride=None):
    stride = n if stride is None else stride
    start = pl.multiple_of(k * stride, math.gcd(n, stride))
    return ref.at[:, pl.ds(start, n), :] if ax == 1 else ref.at[:, :, pl.ds(start, n)]


def gather_job(shards, axes):
    n = len(shards)
    fulls = []
    for s, ax in zip(shards, axes):
        if ax == "stack":
            fulls.append(SDS((N_CHIPS,) + tuple(s.shape), s.dtype))
            continue
        shape = list(s.shape)
        shape[ax] *= N_CHIPS
        fulls.append(SDS(tuple(shape), s.dtype))

    def copies(srcs, outs, sems):
        send_sems, recv_sems, local_sems = sems
        (x, y, c), peers = _plane_peers()
        me = 2 * x + y
        cps = []
        for t in range(n):
            if axes[t] == "stack":
                dst = outs[t].at[me]
            else:
                dst = _chip_block(outs[t], axes[t], me, srcs[t].shape[axes[t]])
            cps.append(pltpu.make_async_copy(srcs[t], dst, local_sems.at[t]))
            for j, peer in enumerate(peers):
                cps.append(pltpu.make_async_remote_copy(src_ref=srcs[t], dst_ref=dst, send_sem=send_sems.at[3 * t + j],
                                                        recv_sem=recv_sems.at[3 * t + j], device_id=peer,
                                                        device_id_type=MESH))
        return cps

    def start(srcs, outs, sems):
        for cp in copies(srcs, outs, sems):
            cp.start()

    def finish(srcs, outs, sems):
        for cp in copies(srcs, outs, sems):
            cp.wait()

    scratch = [pltpu.SemaphoreType.DMA((3 * n,)), pltpu.SemaphoreType.DMA((3 * n,)), pltpu.SemaphoreType.DMA((n,))]
    return dict(inputs=list(shards), out_shape=fulls, scratch=scratch, phases=(start, None, finish))


def split_gather_job(shards, axes):
    n = len(shards)
    plain = gather_job(shards, axes)

    def region(out, t, chip, shard_shape, half_of):
        ax = axes[t]
        ref = out.at[chip] if ax == "stack" else out
        starts, sizes = [0, 0, 0], list(shard_shape)
        if ax != "stack":
            starts[ax] = chip * shard_shape[ax]
        if half_of is not None:
            sa = 0 if shard_shape[0] % 2 == 0 else 1
            sizes[sa] = shard_shape[sa] // 2
            starts[sa] = starts[sa] + half_of * sizes[sa]
        starts = [s if isinstance(s, int) else pl.multiple_of(s, math.gcd(full, z))
                  for s, z, full in zip(starts, sizes, shard_shape)]
        return ref.at[tuple(pl.ds(s, z) for s, z in zip(starts, sizes))]

    def copies(srcs, outs, sems):
        send_sems, recv_sems, local_sems, fsend_sems, frecv_sems = sems
        (x, y, c), peers = _plane_peers()
        sibling = (x, y, 1 - c)
        me = 2 * x + y
        locals_, sends, forwards = [], [], []
        for t in range(n):
            shape = srcs[t].shape
            locals_.append(pltpu.make_async_copy(srcs[t], region(outs[t], t, me, shape, None), local_sems.at[t]))
            sa = 0 if shape[0] % 2 == 0 else 1
            hs = shape[sa] // 2
            mine = srcs[t].at[pl.ds(c * hs, hs)] if sa == 0 else srcs[t].at[:, pl.ds(c * hs, hs), :]
            for j, (px, py, _) in enumerate(peers):
                sends.append(pltpu.make_async_remote_copy(
                    src_ref=mine, dst_ref=region(outs[t], t, me, shape, c), send_sem=send_sems.at[3 * t + j],
                    recv_sem=recv_sems.at[3 * t + j], device_id=(px, py, c), device_id_type=MESH))
                landed = region(outs[t], t, 2 * px + py, shape, c)
                forwards.append(pltpu.make_async_remote_copy(
                    src_ref=landed, dst_ref=landed, send_sem=fsend_sems.at[3 * t + j],
                    recv_sem=frecv_sems.at[3 * t + j], device_id=sibling, device_id_type=MESH))
        return locals_, sends, forwards

    def start(srcs, outs, sems):
        locals_, sends, _ = copies(srcs, outs, sems)
        for cp in locals_ + sends:
            cp.start()

    def middle(srcs, outs, sems):
        _, sends, forwards = copies(srcs, outs, sems)
        for cp, fw in zip(sends, forwards):
            cp.wait_recv()
            fw.start()

    def finish(srcs, outs, sems):
        locals_, sends, forwards = copies(srcs, outs, sems)
        for cp in locals_:
            cp.wait()
        for cp in sends:
            cp.wait_send()
        for fw in forwards:
            fw.wait()

    scratch = [pltpu.SemaphoreType.DMA((3 * n,)) for _ in range(2)] + [pltpu.SemaphoreType.DMA((n,))] + \
              [pltpu.SemaphoreType.DMA((3 * n,)) for _ in range(2)]
    return dict(inputs=list(shards), out_shape=plain["out_shape"], scratch=scratch, phases=(start, middle, finish))


def exchange_job(grads, axes):
    n = len(grads)
    outs = []
    spans = []
    for g, ax in zip(grads, axes):
        if ax == "lead":
            spans.append(None)
            outs.append(SDS(tuple(g.shape), g.dtype))
            continue
        ax, width, stride = ax if isinstance(ax, tuple) else (ax, g.shape[ax] // N_CHIPS, None)
        spans.append((ax, width, stride))
        shape = list(g.shape)
        shape[ax] = width
        outs.append(SDS((N_CHIPS,) + tuple(shape), g.dtype))

    def block_for(src, t, chip):
        return src.at[chip] if spans[t] is None else _chip_block(src, spans[t][0], chip, spans[t][1], spans[t][2])

    def copies(srcs, res, sems):
        mine, theirs = res[:n], res[n:]
        send_sems, recv_sems, local_sems, fsend_sems, frecv_sems = sems
        (x, y, c), peers = _plane_peers()
        sibling = (x, y, 1 - c)
        me = 2 * x + y
        blocks = [me] + [2 * px + py for (px, py, _) in peers]
        locals_, sends, forwards = [], [], []
        for t in range(n):
            locals_.append(pltpu.make_async_copy(block_for(srcs[t], t, me), mine[t].at[me], local_sems.at[t]))
            for j, peer in enumerate(peers):
                sends.append(pltpu.make_async_remote_copy(
                    src_ref=block_for(srcs[t], t, blocks[j + 1]), dst_ref=mine[t].at[me],
                    send_sem=send_sems.at[3 * t + j], recv_sem=recv_sems.at[3 * t + j], device_id=peer,
                    device_id_type=MESH))
            for q, blk in enumerate(blocks):
                forwards.append(pltpu.make_async_remote_copy(
                    src_ref=mine[t].at[blk], dst_ref=theirs[t].at[blk], send_sem=fsend_sems.at[4 * t + q],
                    recv_sem=frecv_sems.at[4 * t + q], device_id=sibling, device_id_type=MESH))
        return locals_, sends, forwards

    def start(srcs, res, sems):
        locals_, sends, _ = copies(srcs, res, sems)
        for cp in locals_ + sends:
            cp.start()

    def middle(srcs, res, sems):
        locals_, sends, forwards = copies(srcs, res, sems)
        for t in range(n):
            locals_[t].wait()
            for q in range(N_CHIPS):
                if q > 0:
                    sends[3 * t + q - 1].wait_recv()
                forwards[4 * t + q].start()

    def finish(srcs, res, sems):
        _, sends, forwards = copies(srcs, res, sems)
        for cp in sends:
            cp.wait_send()
        for fw in forwards:
            fw.wait()

    scratch = [pltpu.SemaphoreType.DMA((3 * n,)), pltpu.SemaphoreType.DMA((3 * n,)), pltpu.SemaphoreType.DMA((n,)),
               pltpu.SemaphoreType.DMA((4 * n,)), pltpu.SemaphoreType.DMA((4 * n,))]
    return dict(inputs=list(grads), out_shape=outs + outs, scratch=scratch, phases=(start, middle, finish))


def run_job(job, ins, outs, sems, first=None, mid=None, last=None):
    for phase, when in zip(job["phases"], (first, mid, last)):
        if phase is None:
            continue
        if when is None:
            phase(ins, outs, sems)
        else:
            pl.when(when)(functools.partial(phase, ins, outs, sems))


def job_call(job, *, name):
    ni, no = len(job["inputs"]), len(job["out_shape"])

    def body(*refs):
        run_job(job, refs[:ni], refs[ni:ni + no], refs[ni + no:])

    return pl.pallas_call(body, name=name, out_shape=job["out_shape"], in_specs=[ANY] * ni, out_specs=[ANY] * no,
                          scratch_shapes=job["scratch"])(*job["inputs"])


def _adam_update(g, w, m, v):
    mn = ADAM_B1 * m + (1.0 - ADAM_B1) * g
    vn = ADAM_B2 * v + (1.0 - ADAM_B2) * jnp.square(g)
    m_hat = mn / (1.0 - ADAM_B1 ** ADAM_STEP)
    v_hat = vn / (1.0 - ADAM_B2 ** ADAM_STEP)
    return -ADAM_LR * (m_hat / (jnp.sqrt(v_hat) + ADAM_EPS) + ADAM_WD * w), mn, vn


def _row_tile(rs):
    for cand in range(256, 15, -16):
        if rs % cand == 0:
            return cand
    return rs


def _group_sum(half, a0_r, b0_r, a1_r, b1_r):
    def plane_sums(a_r, b_r):
        pa = a_r[0].astype(F32)
        pb = b_r[0].astype(F32)
        for k in range(1, N_CHIPS):
            pa = pa + a_r[k].astype(F32)
            pb = pb + b_r[k].astype(F32)
        return pa + pb
    return jnp.where(pl.program_id(0) < half, plane_sums(a0_r, b0_r), plane_sums(a1_r, b1_r))


def _group_specs(half, tr, cs):
    first = pl.BlockSpec((N_CHIPS, None, tr, cs),
                         lambda l, i: (0, jnp.minimum(l, half - 1), jnp.where(l < half, i, 0), 0))
    second = pl.BlockSpec((N_CHIPS, None, tr, cs),
                          lambda l, i: (0, jnp.maximum(l - half, 0), jnp.where(l < half, 0, i), 0))
    return [first, first, second, second]


def adamw_reg(groups, w, m, v, *, name):
    L, rs, cs = w.shape
    tr = _row_tile(rs)
    early = groups[0][0].shape[1]

    def body(a0_r, b0_r, a1_r, b1_r, w_r, m_r, v_r, g_o, d_o, m_o, v_o):
        g = _group_sum(early, a0_r, b0_r, a1_r, b1_r)
        d, mn, vn = _adam_update(g, w_r[...], m_r[...], v_r[...])
        g_o[...] = g
        d_o[...] = d
        m_o[...] = mn
        v_o[...] = vn

    s1 = pl.BlockSpec((None, tr, cs), lambda l, i: (l, i, 0))
    (a0, b0), (a1, b1) = groups
    return pl.pallas_call(
        body, name=name, grid=(L, rs // tr), in_specs=_group_specs(early, tr, cs) + [s1, s1, s1], out_specs=[s1] * 4,
        out_shape=[SDS((L, rs, cs), F32)] * 4,
        compiler_params=pltpu.CompilerParams(dimension_semantics=("parallel", "parallel"),
                                             vmem_limit_bytes=VMEM_LIMIT),
    )(a0, b0, a1, b1, w, m, v)


def groups_sum(groups, *, name):
    (a0, b0), (a1, b1) = groups
    _, half, rs, cs = a0.shape
    L = half + a1.shape[1]
    tr = _row_tile(rs)

    def body(a0_r, b0_r, a1_r, b1_r, g_o):
        g_o[...] = _group_sum(half, a0_r, b0_r, a1_r, b1_r)

    return pl.pallas_call(
        body, name=name, grid=(L, rs // tr), in_specs=_group_specs(half, tr, cs),
        out_specs=pl.BlockSpec((None, tr, cs), lambda l, i: (l, i, 0)), out_shape=SDS((L, rs, cs), F32),
        compiler_params=pltpu.CompilerParams(dimension_semantics=("parallel", "parallel"),
                                             vmem_limit_bytes=VMEM_LIMIT),
    )(a0, b0, a1, b1)


def adamw_plain(g, w, m, v, *, name):
    L, rs, cs = w.shape
    tr = _row_tile(rs)

    def body(g_r, w_r, m_r, v_r, d_o, m_o, v_o):
        d, mn, vn = _adam_update(g_r[...], w_r[...], m_r[...], v_r[...])
        d_o[...] = d
        m_o[...] = mn
        v_o[...] = vn

    s1 = pl.BlockSpec((None, tr, cs), lambda l, i: (l, i, 0))
    return pl.pallas_call(
        body, name=name, grid=(L, rs // tr), in_specs=[s1] * 4, out_specs=[s1] * 3,
        out_shape=[SDS((L, rs, cs), F32)] * 3,
        compiler_params=pltpu.CompilerParams(dimension_semantics=("parallel", "parallel"),
                                             vmem_limit_bytes=VMEM_LIMIT),
    )(g, w, m, v)


def plane_sum(r4):
    R = r4.shape[1]

    def body(r_ref, o_ref):
        acc = r_ref[0].astype(F32)
        for k in range(1, N_CHIPS):
            acc = acc + r_ref[k].astype(F32)
        o_ref[...] = acc

    return pl.pallas_call(
        body, name="plane_sum", grid=(R // PACK_ROWS,),
        in_specs=[pl.BlockSpec((N_CHIPS, PACK_ROWS, LANES), lambda i: (0, i, 0))],
        out_specs=pl.BlockSpec((PACK_ROWS, LANES), lambda i: (i, 0)), out_shape=SDS((R, LANES), F32),
        compiler_params=pltpu.CompilerParams(dimension_semantics=("parallel",)),
    )(r4)


def adamw(pa, pb, w, m, v):
    R = w.shape[0]

    def body(pa_r, pb_r, w_r, m_r, v_r, g_o, d_o, m_o, v_o):
        g = pa_r[...] + pb_r[...]
        d, mn, vn = _adam_update(g, w_r[...], m_r[...], v_r[...])
        g_o[...] = g
        d_o[...] = d
        m_o[...] = mn
        v_o[...] = vn

    spec = pl.BlockSpec((PACK_ROWS, LANES), lambda i: (i, 0))
    return pl.pallas_call(
        body, name="adamw", grid=(R // PACK_ROWS,), in_specs=[spec] * 5, out_specs=[spec] * 4,
        out_shape=[SDS((R, LANES), F32)] * 4,
        compiler_params=pltpu.CompilerParams(dimension_semantics=("parallel",)),
    )(pa, pb, w, m, v)


GATHER_AXES = [WSPEC[n][2] for n in REG] + ["stack"]
EXCHANGE_AXES = [WSPEC[n][2] for n in REG] + [(2, WIN_WIDTH, WIN_STRIDE)]


EARLY = {n: WSPEC[n][1][0] // 2 for n in MOVED}
EARLY.update(ffn2_w_in=1, ffn2_w_down=1, ple_w_gate=1, ple_w_proj=1)


def _group_range(n, group):
    return range(EARLY[n]) if group == 0 else range(EARLY[n], WSPEC[n][1][0])


def moved_shards(wl, group):
    out = []
    for n in MOVED:
        r = _group_range(n, group)
        out.append(wl[n][r.start:r.stop].astype(BF16))
    return out


def moved_grads(grads, group, partial=None):
    out = []
    for n in MOVED:
        if WSPEC[n][1][0] == DEPTH:
            layer_of, key = (lambda s: s), n
        elif n == "mla_w_out":
            layer_of, key = (lambda s: 2 * s + 1), "mla_out"
        else:
            layer_of, key = (lambda s: 2 * s), {"hyb_w_out": "hyb_out", WIN: "hyb_in"}[n]
        per_layer = [grads[layer_of(s)] if grads[layer_of(s)] is not None else partial for s in _group_range(n, group)]
        out.append(jnp.stack([d[key] for d in per_layer]))
    return out


def gather_misc(wl):
    full = {}
    sharded = [n for n in MISC if WSPEC[n][2] is not None]
    pieces = [wl[n].astype(BF16) if WSPEC[n][3] else lax.bitcast_convert_type(wl[n], BF16) for n in sharded]
    got = plane_allgather(_pack(pieces, BF16, 16))
    shapes = [v.shape for v in pieces]
    per_chip = [_unpack(got[k], shapes) for k in range(N_CHIPS)]
    for idx, n in enumerate(sharded):
        parts = [per_chip[k][idx] for k in range(N_CHIPS)]
        if not WSPEC[n][3]:
            parts = [lax.bitcast_convert_type(v, F32) for v in parts]
        full[n] = jnp.concatenate(parts, axis=WSPEC[n][2])
    for n in MISC:
        if WSPEC[n][2] is None:
            full[n] = wl[n]
    return full


def kernel(x, p, positions, ffn1_pre_g, ffn1_w_in, ffn1_w_down, ffn1_post_g, mix_pre_g, mix_post_g, ffn2_pre_g, ffn2_w_in, ffn2_w_down, ffn2_post_g, ple_pre_g, ple_w_gate, ple_w_proj, ple_post_g, hyb_w_in, gm_ln_g, gm_ln_b, gm_w_s, gm_b_s, ssd_conv_w, ssd_conv_b, ssd_dt_bias, ssd_a_log, ssd_d, ssd_norm_g, hyb_w_out, mla_w_in, mla_q_norm_g, mla_kv_norm_g, mla_w_uq, mla_w_ukv, mla_w_out, loss_target, m_ffn1_pre_g, m_ffn1_w_in, m_ffn1_w_down, m_ffn1_post_g, m_mix_pre_g, m_mix_post_g, m_ffn2_pre_g, m_ffn2_w_in, m_ffn2_w_down, m_ffn2_post_g, m_ple_pre_g, m_ple_w_gate, m_ple_w_proj, m_ple_post_g, m_hyb_w_in, m_gm_ln_g, m_gm_ln_b, m_gm_w_s, m_gm_b_s, m_ssd_conv_w, m_ssd_conv_b, m_ssd_dt_bias, m_ssd_a_log, m_ssd_d, m_ssd_norm_g, m_hyb_w_out, m_mla_w_in, m_mla_q_norm_g, m_mla_kv_norm_g, m_mla_w_uq, m_mla_w_ukv, m_mla_w_out, v_ffn1_pre_g, v_ffn1_w_in, v_ffn1_w_down, v_ffn1_post_g, v_mix_pre_g, v_mix_post_g, v_ffn2_pre_g, v_ffn2_w_in, v_ffn2_w_down, v_ffn2_post_g, v_ple_pre_g, v_ple_w_gate, v_ple_w_proj, v_ple_post_g, v_hyb_w_in, v_gm_ln_g, v_gm_ln_b, v_gm_w_s, v_gm_b_s, v_ssd_conv_w, v_ssd_conv_b, v_ssd_dt_bias, v_ssd_a_log, v_ssd_d, v_ssd_norm_g, v_hyb_w_out, v_mla_w_in, v_mla_q_norm_g, v_mla_kv_norm_g, v_mla_w_uq, v_mla_w_ukv, v_mla_w_out):
    args = locals()
    wl = {n: args[n] for n in WNAMES}
    ml = {n: args["m_" + n] for n in WNAMES}
    vl = {n: args["v_" + n] for n in WNAMES}

    full = gather_misc(wl)
    nmoved = len(MOVED)
    for n in MOVED:
        full[n] = [None] * WSPEC[n][1][0]
    exchanged = [None, None]

    def gathered(group, outs):
        for t, n in enumerate(MOVED):
            for l, s in enumerate(_group_range(n, group)):
                if n == WIN:
                    full[n][s] = jnp.concatenate([outs[t][k, l] for k in range(N_CHIPS)], axis=-1)
                else:
                    full[n][s] = (outs[t], l)

    gathered(0, job_call(split_gather_job(moved_shards(wl, 0), GATHER_AXES), name="gather_first"))

    class Jobs(NoJobs):
        def fwd_job(self, i):
            return gather_job(moved_shards(wl, 1), GATHER_AXES) if i == 1 else None

        def fwd_done(self, i, outs):
            gathered(1, outs)

        def bwd_job(self, i, grads, partial):
            return exchange_job(moved_grads(grads, 1, partial), EXCHANGE_AXES) if i == 1 else None

        def bwd_done(self, i, outs):
            exchanged[1] = outs

    rope = rope_tables(positions[0])
    T = x.shape[1]
    sq, dx, grads = local_step(x[0], p.reshape(DEPTH, T, p.shape[-1]), rope, loss_target[0],
                               lambda i: layer_weights(full, i), Jobs())
    loss = lax.psum(0.5 * jnp.sum(sq) / D_MODEL, ("x", "y", "c"))

    res = {}
    fg = full_grads(grads)
    dest = jnp.stack([_pack([_chip_slice(fg[n], WSPEC[n][2], k) for n in MISC], BF16, PACK_ROWS)[None]
                      for k in range(N_CHIPS)])
    early = job_call(exchange_job(moved_grads(grads, 0) + [dest], EXCHANGE_AXES + ["lead"]), name="exchange_first")
    packed = (early[nmoved], early[2 * nmoved + 1])
    exchanged[0] = early[:nmoved] + early[nmoved + 1:2 * nmoved + 1]
    for t, n in enumerate(MOVED):
        groups = [(e[t], e[nmoved + t]) for e in exchanged]
        if n == WIN:
            window = groups_sum(groups, name=f"sum_{n}")
            chip = 2 * lax.axis_index("x") + lax.axis_index("y")
            g = lax.dynamic_slice_in_dim(window, chip * (WIN_SHARD - WIN_STRIDE), WIN_SHARD, axis=2)
            res[n] = [g] + list(adamw_plain(g, wl[n], ml[n], vl[n], name=f"adamw_{n}"))
        else:
            res[n] = adamw_reg(groups, wl[n], ml[n], vl[n], name=f"adamw_{n}")
    slabs = adamw_reg([packed, packed], *[_pack([d[n] for n in MISC], F32, PACK_ROWS)[None] for d in (wl, ml, vl)],
                      name="adamw_packed")
    shapes = [wl[n].shape for n in MISC]
    unpacked = [_unpack(s[0], shapes) for s in slabs]
    for idx, n in enumerate(MISC):
        res[n] = [u[idx] for u in unpacked]
    return (loss, dx[None], *[res[n][k] for k in range(4) for n in WNAMES])
```
